```python
import math
import jax, jax.numpy as jnp
from jax import lax
import numpy as np

D_MODEL = 1024
BATCH = 2
SEQ = 8192
DEPTH = 2
DEC_BATCH = 128
DEC_SEQ = 4
PAST_LEN = 2048
PAGE_SIZE = 128

H_A = 4
DK_A = 128
DV_A = 128
CONV_A = 4
DELTA_CHUNK = 64
QKV_A = 2 * H_A * DK_A + H_A * DV_A
H_B = 4
DH_B = 64
DV_B = 2 * DH_B
Q_BLOCK = 128
NUM_BUCKETS = 32
MAX_DISTANCE = 128
NEG_INF = -1e30
BRANCH_W = H_A * DV_A
D_FF = 2816
CONV_F = 3
EPS = 1e-6
SPLIT_SIZES = (QKV_A, H_A * DV_A, H_A, H_A, H_B * 2 * DH_B, H_B * 2 * DH_B, H_B * DV_B, D_MODEL, D_MODEL)
N_IN = QKV_A + H_A * DV_A + 2 * H_A + 2 * H_B * 2 * DH_B + H_B * DV_B + 2 * D_MODEL

kernel_name = 'hybrid_gdn_diffattn_convffn_step'


def split_points():
    return [int(s) for s in np.cumsum(np.array(SPLIT_SIZES))[:-1]]


def rmsnorm(x, w):
    xf = x.astype(jnp.float32)
    y = xf * lax.rsqrt(jnp.mean(xf * xf, axis=-1, keepdims=True) + EPS)
    return (y * w.astype(jnp.float32)).astype(x.dtype)


def l2norm(x):
    xf = x.astype(jnp.float32)
    return (xf * lax.rsqrt(jnp.sum(xf * xf, axis=-1, keepdims=True) + EPS)).astype(x.dtype)


def causal_dwconv(x, buf, w):
    width = w.shape[0]
    t = x.shape[1]
    xp = jnp.concatenate([buf.astype(x.dtype), x], axis=1)
    y = sum(xp[:, i:i + t] * w[i] for i in range(width))
    return y, xp[:, t:]


def t5_bucket(rel):
    n = jnp.maximum(rel, 0)
    max_exact = NUM_BUCKETS // 2
    nf = jnp.maximum(n, 1).astype(jnp.float32)
    large = max_exact + (jnp.log(nf / max_exact) / math.log(MAX_DISTANCE / max_exact)
                         * (NUM_BUCKETS - max_exact)).astype(jnp.int32)
    large = jnp.minimum(large, NUM_BUCKETS - 1)
    return jnp.where(n < max_exact, n, large)


def gated_delta_chunked(q, k, v, beta, g, s0):
    b, t, h, _ = q.shape
    dv = v.shape[-1]
    c = DELTA_CHUNK if t >= DELTA_CHUNK else t
    pad = (-t) % c
    n = (t + pad) // c
    f32 = jnp.float32

    def prep(a):
        a = jnp.pad(a.astype(f32), [(0, 0), (0, pad)] + [(0, 0)] * (a.ndim - 2))
        a = jnp.moveaxis(a, 2, 1)
        return a.reshape(a.shape[:2] + (n, c) + a.shape[3:])

    q, k, v, beta, g = prep(q), prep(k), prep(v), prep(beta), prep(g)
    gc = jnp.cumsum(g, axis=-1)
    incl = jnp.tril(jnp.ones((c, c), bool))
    strict = jnp.tril(jnp.ones((c, c), bool), -1)
    diff = gc[..., :, None] - gc[..., None, :]
    decay = jnp.where(incl, jnp.exp(jnp.where(incl, diff, 0.0)), 0.0)
    kb = k * beta[..., None]
    vb = v * beta[..., None]
    m = jnp.where(strict, jnp.einsum('bhnid,bhnjd->bhnij', kb, k) * decay, 0.0)
    eye = jnp.eye(c, dtype=f32)
    t_mat = lax.linalg.triangular_solve(eye + m, jnp.broadcast_to(eye, m.shape), left_side=True, lower=True)
    u = jnp.einsum('bhnij,bhnjv->bhniv', t_mat, vb)
    w = jnp.einsum('bhnij,bhnjd->bhnid', t_mat, kb * jnp.exp(gc)[..., None])
    a_intra = jnp.where(incl, jnp.einsum('bhnid,bhnjd->bhnij', q, k) * decay, 0.0)
    q_g = q * jnp.exp(gc)[..., None]
    k_d = k * jnp.exp(gc[..., -1:] - gc)[..., None]
    g_last = jnp.exp(gc[..., -1])

    def step(s, xs):
        w_c, u_c, q_c, k_c, a_c, gl = xs
        v_new = u_c - jnp.einsum('bhcd,bhdv->bhcv', w_c, s)
        o = jnp.einsum('bhcd,bhdv->bhcv', q_c, s) + jnp.einsum('bhij,bhjv->bhiv', a_c, v_new)
        s = s * gl[..., None, None] + jnp.einsum('bhcd,bhcv->bhdv', k_c, v_new)
        return s, o

    xs = tuple(jnp.moveaxis(a, 2, 0) for a in (w, u, q_g, k_d, a_intra, g_last))
    s_fin, o = lax.scan(step, s0.astype(f32), xs)
    o = jnp.moveaxis(o, 0, 2).reshape(b, h, t + pad, dv)[:, :, :t]
    return jnp.moveaxis(o, 1, 2), s_fin.astype(s0.dtype)


def diff_attention(q1, q2, k1, k2, v, q_pos, k_pos, rel_bias, lam):
    b, tq, h, _ = q1.shape
    qb = Q_BLOCK if tq % Q_BLOCK == 0 else tq
    nb = tq // qb
    scale = DH_B ** -0.5

    def to_blocks(a):
        return jnp.moveaxis(a.reshape((b, nb, qb) + a.shape[2:]), 1, 0)

    def block(args):
        q1b, q2b, qp = args
        rel = qp[:, None] - k_pos[None, :]
        bias = jnp.moveaxis(rel_bias[t5_bucket(rel)], -1, 0).astype(jnp.float32)
        visible = rel >= 0

        def probs(qx, kx):
            s = jnp.einsum('bqhd,bkhd->bhqk', qx, kx).astype(jnp.float32) * scale + bias
            return jax.nn.softmax(jnp.where(visible, s, NEG_INF), axis=-1)

        p = probs(q1b, k1) - lam * probs(q2b, k2)
        return jnp.einsum('bhqk,bkhv->bqhv', p.astype(v.dtype), v)

    o = lax.map(block, (to_blocks(q1), to_blocks(q2), q_pos.reshape(nb, qb)))
    return jnp.moveaxis(o, 0, 1).reshape(b, tq, h, v.shape[-1])


def trunk_layer(x, q_pos, k_past, v_past, conv_a_buf, s0, conv_f_buf, li, rel_bias,
                norm_mix, w_in, conv_a, a_log, dt_bias, onorm_a, qnorm_b, knorm_b,
                lam_q1, lam_k1, lam_q2, lam_k2, subln_b, w_branch, w_o,
                norm_ffn, w_up, conv_f, conv_f_b, w_down):
    b, t, _ = x.shape
    f32 = jnp.float32
    h = rmsnorm(x, norm_mix)
    proj = h @ w_in
    qkv_a, z_a, b_a, a_a, q_b, k_b, v_b, g_a, g_b = jnp.split(proj, split_points(), axis=-1)

    qkv_a, conv_a_new = causal_dwconv(qkv_a, conv_a_buf, conv_a)
    qkv_a = jax.nn.silu(qkv_a)
    q_a, k_a, v_a = jnp.split(qkv_a, [H_A * DK_A, 2 * H_A * DK_A], axis=-1)
    q_a = l2norm(q_a.reshape(b, t, H_A, DK_A)) * (DK_A ** -0.5)
    k_a = l2norm(k_a.reshape(b, t, H_A, DK_A))
    v_a = v_a.reshape(b, t, H_A, DV_A)
    beta = jax.nn.sigmoid(b_a.astype(f32))
    g = -jnp.exp(a_log.astype(f32)) * jax.nn.softplus(a_a.astype(f32) + dt_bias.astype(f32))
    o_a, s_new = gated_delta_chunked(q_a, k_a, v_a, beta, g, s0)
    o_a = rmsnorm(o_a.astype(x.dtype), onorm_a) * jax.nn.silu(z_a.reshape(b, t, H_A, DV_A))
    o_a = o_a.reshape(b, t, H_A * DV_A)

    lam_init = 0.8 - 0.6 * math.exp(-0.3 * li)
    q_b = rmsnorm(q_b.reshape(b, t, H_B, 2, DH_B), qnorm_b)
    k_b = rmsnorm(k_b.reshape(b, t, H_B, 2, DH_B), knorm_b)
    v_b = v_b.reshape(b, t, H_B, DV_B)
    k_rows = k_b.reshape(b, t, H_B, 2 * DH_B)
    if k_past is None:
        k_all, v_all = k_rows, v_b
    else:
        k_all = jnp.concatenate([k_past.astype(k_rows.dtype), k_rows], axis=1)
        v_all = jnp.concatenate([v_past.astype(v_b.dtype), v_b], axis=1)
    tk = k_all.shape[1]
    k_all = k_all.reshape(b, tk, H_B, 2, DH_B)
    k_pos = jnp.arange(tk, dtype=jnp.int32)
    lam = (jnp.exp(jnp.sum(lam_q1.astype(f32) * lam_k1.astype(f32)))
           - jnp.exp(jnp.sum(lam_q2.astype(f32) * lam_k2.astype(f32))) + lam_init)
    o_b = diff_attention(q_b[..., 0, :], q_b[..., 1, :], k_all[..., 0, :], k_all[..., 1, :],
                         v_all, q_pos, k_pos, rel_bias, lam)
    o_b = (rmsnorm(o_b, subln_b) * (1.0 - lam_init)).reshape(b, t, H_B * DV_B)

    mixed = jax.nn.sigmoid(g_a) * (o_a @ w_branch[0]) + jax.nn.sigmoid(g_b) * (o_b @ w_branch[1])
    x = x + mixed @ w_o

    hf = rmsnorm(x, norm_ffn)
    u, conv_f_new = causal_dwconv(hf @ w_up, conv_f_buf, conv_f)
    u = u + conv_f_b
    gate, val = jnp.split(u, 2, axis=-1)
    x = x + (jax.nn.silu(gate) * val) @ w_down
    return x, k_rows, v_b, s_new, conv_a_new, conv_f_new


def setup_inputs(seed: int = 0) -> dict:
    key = jax.random.key(seed)
    nxt = iter(list(jax.random.split(key, 40)))
    f32 = jnp.float32

    def nrm(shape, scale):
        return jax.random.normal(next(nxt), shape, f32) * scale

    def gain(shape):
        return 1.0 + nrm(shape, 0.02)

    n_pages = PAST_LEN // PAGE_SIZE
    n_used = DEC_BATCH * n_pages
    n_pool = n_used + n_used // 4
    perm = jax.random.permutation(next(nxt), n_pool)
    page_table = perm[:n_used].reshape(DEC_BATCH, n_pages).astype(jnp.int32)

    x_prompt = nrm((BATCH, SEQ, D_MODEL), 1.0)
    x_sample = nrm((DEC_BATCH, DEC_SEQ, D_MODEL), 1.0)
    cache_k = nrm((DEPTH, n_pool, PAGE_SIZE, H_B, 2 * DH_B), 1.0)
    cache_v = nrm((DEPTH, n_pool, PAGE_SIZE, H_B, DV_B), 1.0)
    state_delta = nrm((DEPTH, DEC_BATCH, H_A, DK_A, DV_A), 0.1)
    state_conv_a = nrm((DEPTH, DEC_BATCH, CONV_A - 1, QKV_A), 1.0)
    state_conv_ffn = nrm((DEPTH, DEC_BATCH, CONV_F - 1, 2 * D_FF), 1.0)

    rel_bias = nrm((NUM_BUCKETS, H_B), 0.5)
    norm_mix = gain((DEPTH, D_MODEL))
    w_in = nrm((DEPTH, D_MODEL, N_IN), D_MODEL ** -0.5)
    conv_a = nrm((DEPTH, CONV_A, QKV_A), CONV_A ** -0.5)
    a_log = jnp.log(jax.random.uniform(next(nxt), (DEPTH, H_A), f32, 1.0, 16.0))
    dt = jnp.exp(jax.random.uniform(next(nxt), (DEPTH, H_A), f32, math.log(0.001), math.log(0.1)))
    dt_bias = dt + jnp.log(-jnp.expm1(-dt))
    onorm_a = gain((DEPTH, DV_A))
    qnorm_b = gain((DEPTH, DH_B))
    knorm_b = gain((DEPTH, DH_B))
    lam_q1 = nrm((DEPTH, DH_B), 0.1)
    lam_k1 = nrm((DEPTH, DH_B), 0.1)
    lam_q2 = nrm((DEPTH, DH_B), 0.1)
    lam_k2 = nrm((DEPTH, DH_B), 0.1)
    subln_b = gain((DEPTH, DV_B))
    w_branch = nrm((DEPTH, 2, BRANCH_W, D_MODEL), BRANCH_W ** -0.5)
    w_o = nrm((DEPTH, D_MODEL, D_MODEL), D_MODEL ** -0.5)
    norm_ffn = gain((DEPTH, D_MODEL))
    w_up = nrm((DEPTH, D_MODEL, 2 * D_FF), D_MODEL ** -0.5)
    conv_f = nrm((DEPTH, CONV_F, 2 * D_FF), CONV_F ** -0.5)
    conv_f_b = nrm((DEPTH, 2 * D_FF), 0.01)
    w_down = nrm((DEPTH, D_FF, D_MODEL), D_FF ** -0.5)
    return {'x_prompt': x_prompt, 'x_sample': x_sample, 'cache_k': cache_k, 'cache_v': cache_v,
            'state_delta': state_delta, 'state_conv_a': state_conv_a, 'state_conv_ffn': state_conv_ffn,
            'page_table': page_table, 'rel_bias': rel_bias, 'norm_mix': norm_mix, 'w_in': w_in,
            'conv_a': conv_a, 'a_log': a_log, 'dt_bias': dt_bias, 'onorm_a': onorm_a,
            'qnorm_b': qnorm_b, 'knorm_b': knorm_b, 'lam_q1': lam_q1, 'lam_k1': lam_k1,
            'lam_q2': lam_q2, 'lam_k2': lam_k2, 'subln_b': subln_b, 'w_branch': w_branch, 'w_o': w_o,
            'norm_ffn': norm_ffn, 'w_up': w_up, 'conv_f': conv_f, 'conv_f_b': conv_f_b, 'w_down': w_down}


def reference(x_prompt, x_sample, cache_k, cache_v, state_delta, state_conv_a, state_conv_ffn,
              page_table, rel_bias, norm_mix, w_in, conv_a, a_log, dt_bias, onorm_a, qnorm_b, knorm_b,
              lam_q1, lam_k1, lam_q2, lam_k2, subln_b, w_branch, w_o, norm_ffn, w_up, conv_f,
              conv_f_b, w_down):
    n_pages = page_table.shape[1]
    pos_prompt = jnp.arange(SEQ, dtype=jnp.int32)
    pos_sample = PAST_LEN + jnp.arange(DEC_SEQ, dtype=jnp.int32)
    xp, xs = x_prompt, x_sample
    kp_l, vp_l, sp_l, cap_l, cfp_l = [], [], [], [], []
    ks_l, vs_l, ss_l, cas_l, cfs_l = [], [], [], [], []
    for li in range(DEPTH):
        weights = (norm_mix[li], w_in[li], conv_a[li], a_log[li], dt_bias[li], onorm_a[li],
                   qnorm_b[li], knorm_b[li], lam_q1[li], lam_k1[li], lam_q2[li], lam_k2[li],
                   subln_b[li], w_branch[li], w_o[li], norm_ffn[li], w_up[li], conv_f[li],
                   conv_f_b[li], w_down[li])
        zero_conv_a = jnp.zeros((BATCH, CONV_A - 1, QKV_A), xp.dtype)
        zero_delta = jnp.zeros((BATCH, H_A, DK_A, DV_A), state_delta.dtype)
        zero_conv_f = jnp.zeros((BATCH, CONV_F - 1, 2 * D_FF), xp.dtype)
        xp, kp, vp, sp, cap, cfp = trunk_layer(xp, pos_prompt, None, None, zero_conv_a, zero_delta,
                                               zero_conv_f, li, rel_bias, *weights)
        k_past = cache_k[li][page_table].reshape(DEC_BATCH, n_pages * PAGE_SIZE, H_B, 2 * DH_B)
        v_past = cache_v[li][page_table].reshape(DEC_BATCH, n_pages * PAGE_SIZE, H_B, DV_B)
        xs, ks, vs, ss, cas, cfs = trunk_layer(xs, pos_sample, k_past, v_past, state_conv_a[li],
                                               state_delta[li], state_conv_ffn[li], li, rel_bias, *weights)
        kp_l.append(kp); vp_l.append(vp); sp_l.append(sp); cap_l.append(cap); cfp_l.append(cfp)
        ks_l.append(ks); vs_l.append(vs); ss_l.append(ss); cas_l.append(cas); cfs_l.append(cfs)
    k_prompt = jnp.stack(kp_l)
    v_prompt = jnp.stack(vp_l)
    delta_prompt = jnp.stack(sp_l)
    conv_a_prompt = jnp.stack(cap_l)
    conv_ffn_prompt = jnp.stack(cfp_l)
    k_sample = jnp.stack(ks_l)
    v_sample = jnp.stack(vs_l)
    delta_sample = jnp.stack(ss_l)
    conv_a_sample = jnp.stack(cas_l)
    conv_ffn_sample = jnp.stack(cfs_l)
    return (xp, xs, k_prompt, v_prompt, delta_prompt, conv_a_prompt, conv_ffn_prompt,
            k_sample, v_sample, delta_sample, conv_a_sample, conv_ffn_sample)
```

```python
import functools
import math

import jax
import jax.numpy as jnp
from jax import lax
from jax.experimental import pallas as pl
from jax.experimental.pallas import tpu as pltpu

F32 = jnp.float32
BF16 = jnp.bfloat16

D_MODEL = 1024
H_A, DK_A, DV_A, CONV_A, CHUNK = 4, 128, 128, 4, 64
QKV_A = 2 * H_A * DK_A + H_A * DV_A
H_B, DH_B, DV_B = 4, 64, 128
HB_W = H_B * 2 * DH_B
PAGE = 128
D_FF, CONV_F = 2816, 3
NUM_BUCKETS, MAX_DISTANCE = 32, 128
NEG_INF = -1e30
EPS = 1e-6
ATT_SCALE = DH_B ** -0.5

V7X_VMEM_BYTES = 64 * 1024 * 1024
LANES = 128
SUBLANES = 8
MXU_DIM = 256

P1_QKV, P1_Q, P1_K, P1_V, P1_BA, P1_END = 0, 1536, 2048, 2560, 3072, 3200
P2_Z, P2_GA, P2_GB, P2_END = 0, 512, 1536, 2560


def _cparams(n_axes, vmem_mib):
    return pltpu.CompilerParams(
        dimension_semantics=("arbitrary",) * n_axes,
        vmem_limit_bytes=min(vmem_mib * 1024 * 1024, V7X_VMEM_BYTES - 8 * 1024 * 1024),
    )


def _bdot(a, b):
    return jnp.dot(a.astype(BF16), b.astype(BF16), preferred_element_type=F32)


def _rms(x, w):
    return x * lax.rsqrt(jnp.mean(x * x, axis=-1, keepdims=True) + EPS) * w


def _sigmoid(x):
    return 1.0 / (1.0 + jnp.exp(-x))


def _silu(x):
    return x * _sigmoid(x)


def _softplus(x):
    return jnp.maximum(x, 0.0) + jnp.log1p(jnp.exp(-jnp.abs(x)))


def _inproj_kernel(x_ref, nw_ref, w_ref, qw_ref, kw_ref, g_ref,
                   qkv_ref, ba_ref, qn_ref, kn_ref, kf_ref, vf_ref, vb_ref):
    h = _rms(x_ref[...], nw_ref[...]).astype(BF16)

    def proj(lo, hi):
        return jnp.dot(h, w_ref[:, lo:hi], preferred_element_type=F32)

    qkv_ref[...] = proj(P1_QKV, P1_Q)
    ba_ref[...] = proj(P1_BA, P1_END)
    v = proj(P1_V, P1_BA)
    vf_ref[...] = v
    vb_ref[...] = v.astype(BF16)

    g = g_ref[...]

    def qk_norm(y, w):
        sq = y * y
        hi = sq.astype(BF16)
        lo = (sq - hi.astype(F32)).astype(BF16)
        outs = []
        for c in range(HB_W // MXU_DIM):
            sl = slice(c * MXU_DIM, (c + 1) * MXU_DIM)
            ss = (jnp.dot(hi[:, sl], g, preferred_element_type=F32)
                  + jnp.dot(lo[:, sl], g, preferred_element_type=F32))
            outs.append(y[:, sl] * lax.rsqrt(ss * (1.0 / DH_B) + EPS) * w[:, sl])
        return outs

    qn = qk_norm(proj(P1_Q, P1_K), qw_ref[...])
    kn = qk_norm(proj(P1_K, P1_V), kw_ref[...])
    for c in range(HB_W // MXU_DIM):
        sl = slice(c * MXU_DIM, (c + 1) * MXU_DIM)
        qn_ref[:, sl] = (qn[c] * ATT_SCALE).astype(BF16)
        kf_ref[:, sl] = kn[c]
        kn_ref[:, sl] = kn[c].astype(BF16)


def _inproj(x, nw, w1, qw, kw, g, tm):
    r = x.shape[0]
    row = lambda w: pl.BlockSpec((tm, w), lambda i: (i, 0))
    full = lambda a: pl.BlockSpec(a.shape, lambda i: (0,) * a.ndim)
    outs = [(QKV_A, F32), (LANES, F32), (HB_W, BF16), (HB_W, BF16), (HB_W, F32), (HB_W, F32), (HB_W, BF16)]
    return pl.pallas_call(
        _inproj_kernel,
        grid=(r // tm,),
        in_specs=[row(D_MODEL), full(nw), full(w1), full(qw), full(kw), full(g)],
        out_specs=[row(w) for w, _ in outs],
        out_shape=[jax.ShapeDtypeStruct((r, w), dt) for w, dt in outs],
        compiler_params=_cparams(1, 52),
        name="inproj",
    )(x, nw, w1, qw, kw, g)


def _bucket_bias(d, table):
    n = jnp.maximum(d, 0)
    max_exact = NUM_BUCKETS // 2
    nf = jnp.maximum(n, 1).astype(F32)
    large = max_exact + (jnp.log(nf / max_exact) / math.log(MAX_DISTANCE / max_exact)
                         * (NUM_BUCKETS - max_exact)).astype(jnp.int32)
    large = jnp.minimum(large, NUM_BUCKETS - 1)
    bucket = jnp.where(n < max_exact, n, large)
    val = jnp.zeros(d.shape, F32)
    for b in range(NUM_BUCKETS):
        val = jnp.where(bucket == b, table(b), val)
    return val


def _prompt_bias_kernel(rb_ref, o_ref, *, blk):
    h = pl.program_id(0)
    table = lambda b: rb_ref[h * NUM_BUCKETS + b]
    far = table(NUM_BUCKETS - 1)
    i = lax.broadcasted_iota(jnp.int32, (blk, blk), 0)
    j = lax.broadcasted_iota(jnp.int32, (blk, blk), 1)
    d0 = i - j
    o_ref[0, 0] = jnp.where(d0 >= 0, _bucket_bias(d0, table) - far, NEG_INF)
    o_ref[0, 1] = _bucket_bias(d0 + blk, table) - far


def _prompt_bias(rb_flat, blk):
    return pl.pallas_call(
        functools.partial(_prompt_bias_kernel, blk=blk),
        grid=(H_B,),
        in_specs=[pl.BlockSpec(memory_space=pltpu.SMEM)],
        out_specs=pl.BlockSpec((1, 2, blk, blk), lambda h: (h, 0, 0, 0)),
        out_shape=jax.ShapeDtypeStruct((H_B, 2, blk, blk), F32),
        compiler_params=_cparams(1, 32),
        name="prompt_bias",
    )(rb_flat)


def _sample_bias_kernel(rb_ref, o_ref, *, t_new):
    rows, cols = o_ref.shape
    r = lax.broadcasted_iota(jnp.int32, (rows, cols), 0)
    c = lax.broadcasted_iota(jnp.int32, (rows, cols), 1)
    t = r % t_new
    hd = r // (2 * t_new)
    is_new = c >= PAGE
    d = jnp.where(is_new, t - (c - PAGE), t + PAGE - c)
    val = jnp.zeros((rows, cols), F32)
    for h in range(H_B):
        table = lambda b, h=h: rb_ref[h * NUM_BUCKETS + b]
        vh = _bucket_bias(d, table) - table(NUM_BUCKETS - 1)
        val = jnp.where(hd == h, vh, val)
    visible = jnp.logical_and(d >= 0, jnp.logical_or(~is_new, (c - PAGE) < t_new))
    o_ref[...] = jnp.where(visible, val, NEG_INF)


def _sample_bias(rb_flat, t_new):
    rows = H_B * 2 * t_new
    return pl.pallas_call(
        functools.partial(_sample_bias_kernel, t_new=t_new),
        in_specs=[pl.BlockSpec(memory_space=pltpu.SMEM)],
        out_shape=jax.ShapeDtypeStruct((rows, 2 * PAGE), F32),
        name="sample_bias",
    )(rb_flat)


def _lam_value(lv, lam_init):
    s1 = jnp.sum(lv[0:1] * lv[1:2], axis=-1, keepdims=True)
    s2 = jnp.sum(lv[2:3] * lv[3:4], axis=-1, keepdims=True)
    return jnp.exp(s1) - jnp.exp(s2) + lam_init


def _attn_kernel(lam_ref, q_ref, k_ref, v_ref, bias_ref, o_ref, m_ref, l_ref, acc_ref, *, blk, lam_init):
    qi = pl.program_id(2)
    q = q_ref[...]
    lane = lax.broadcasted_iota(jnp.int32, (1, 2 * DH_B), 1)
    zero = jnp.zeros_like(q)
    qs = (jnp.where(lane < DH_B, q, zero), jnp.where(lane >= DH_B, q, zero))
    m_ref[...] = jnp.full(m_ref.shape, NEG_INF, F32)
    l_ref[...] = jnp.zeros(l_ref.shape, F32)
    acc_ref[...] = jnp.zeros(acc_ref.shape, F32)
    reps = blk // LANES

    def update(off, bias):
        kb = k_ref[pl.ds(off, blk), :]
        vb = v_ref[pl.ds(off, blk), :]
        for mp in range(2):
            s = lax.dot_general(qs[mp], kb, (((1,), (1,)), ((), ())), preferred_element_type=F32)
            if bias is not None:
                s = s + bias
            m_old = m_ref[mp]
            m_new = jnp.maximum(m_old, jnp.max(s, axis=-1, keepdims=True))
            alpha = jnp.exp(m_old - m_new)
            p = jnp.exp(s - pltpu.repeat(m_new, reps, axis=1))
            l_ref[mp] = alpha * l_ref[mp] + jnp.sum(p, axis=-1, keepdims=True)
            acc_ref[mp] = alpha * acc_ref[mp] + jnp.dot(p.astype(BF16), vb, preferred_element_type=F32)
            m_ref[mp] = m_new

    def far(kj, carry):
        update(pl.multiple_of(kj * blk, blk), None)
        return carry

    lax.fori_loop(0, jnp.maximum(qi - 1, 0), far, 0)

    @pl.when(qi >= 1)
    def _():
        update(pl.multiple_of((qi - 1) * blk, blk), bias_ref[0, 1])

    update(pl.multiple_of(qi * blk, blk), bias_ref[0, 0])

    lam = _lam_value(lam_ref[...], lam_init)
    o_ref[...] = acc_ref[0] / l_ref[0] - lam * (acc_ref[1] / l_ref[1])


def _prompt_attn(lamv, qn, kn, vb, bias, n_seq, seq, blk, lam_init):
    nq = seq // blk
    r = n_seq * seq
    return pl.pallas_call(
        functools.partial(_attn_kernel, blk=blk, lam_init=lam_init),
        grid=(n_seq, H_B, nq),
        in_specs=[
            pl.BlockSpec(lamv.shape, lambda b, h, i: (0, 0)),
            pl.BlockSpec((blk, LANES), lambda b, h, i: (b * nq + i, h)),
            pl.BlockSpec((seq, LANES), lambda b, h, i: (b, h)),
            pl.BlockSpec((seq, LANES), lambda b, h, i: (b, h)),
            pl.BlockSpec((1, 2, blk, blk), lambda b, h, i: (h, 0, 0, 0)),
        ],
        out_specs=pl.BlockSpec((blk, LANES), lambda b, h, i: (b * nq + i, h)),
        out_shape=jax.ShapeDtypeStruct((r, HB_W), F32),
        scratch_shapes=[pltpu.VMEM((2, blk, LANES), F32), pltpu.VMEM((2, blk, LANES), F32),
                        pltpu.VMEM((2, blk, DV_B), F32)],
        compiler_params=_cparams(3, 48),
        name="prompt_attn",
    )(lamv, qn, kn, vb, bias)


def _tri_masks():
    i = lax.broadcasted_iota(jnp.int32, (CHUNK, CHUNK), 0)
    j = lax.broadcasted_iota(jnp.int32, (CHUNK, CHUNK), 1)
    incl = i >= j
    strict = i > j
    eye = (i == j).astype(F32)
    base = jnp.logical_and(strict, (i // SUBLANES) == (j // SUBLANES))
    levels = []
    s = SUBLANES
    while s < CHUNK:
        levels.append(jnp.logical_and((i // (2 * s)) == (j // (2 * s)), (i // s) > (j // s)))
        s *= 2
    return incl, strict, eye, base, levels


def _unit_lower_inverse(m, eye, base, levels):
    d = jnp.where(base, m, 0.0)
    d2 = _bdot(d, d)
    d4 = _bdot(d2, d2)
    x = _bdot(_bdot(eye - d, eye + d2), eye + d4)
    for lvl in levels:
        c = jnp.where(lvl, m, 0.0)
        x = x - _bdot(_bdot(x, c), x)
    return x


def _gdn_kernel(qkv_ref, ba_ref, cw_ref, prm_ref, o_ref, sout_ref, cst_ref, s_ref, ext_ref, *, tb):
    j = pl.program_id(1)
    nchunk = tb // CHUNK
    halo = SUBLANES

    @pl.when(j == 0)
    def _():
        s_ref[...] = jnp.zeros(s_ref.shape, F32)
        ext_ref[0:halo, :] = jnp.zeros((halo, QKV_A), F32)

    x = qkv_ref[...]
    ext_ref[halo:halo + tb, :] = x
    cw = cw_ref[...]
    y = cw[3:4] * x
    for i in range(CONV_A - 1):
        sh = CONV_A - 1 - i
        y = y + cw[i:i + 1] * ext_ref[halo - sh:halo - sh + tb, :]
    tail = x[tb - halo:tb, :]
    ext_ref[0:halo, :] = tail
    cst_ref[0] = tail
    y = _silu(y)

    bg = ba_ref[...]
    prm = prm_ref[...]
    beta_all = _sigmoid(bg)
    g_all = -jnp.exp(prm[0:1]) * _softplus(bg + prm[1:2])

    incl, strict, eye, base, levels = _tri_masks()
    tril = incl.astype(F32)

    pre = []
    for c in range(nchunk):
        rs = slice(c * CHUNK, (c + 1) * CHUNK)
        gc = jnp.dot(tril, g_all[rs], precision=lax.Precision.HIGHEST, preferred_element_type=F32)
        gct = gc.T
        heads = []
        for h in range(H_A):
            q = y[rs, h * DK_A:(h + 1) * DK_A]
            k = y[rs, H_A * DK_A + h * DK_A:H_A * DK_A + (h + 1) * DK_A]
            v = y[rs, 2 * H_A * DK_A + h * DV_A:2 * H_A * DK_A + (h + 1) * DV_A]
            q = q * lax.rsqrt(jnp.sum(q * q, axis=-1, keepdims=True) + EPS) * (DK_A ** -0.5)
            k = k * lax.rsqrt(jnp.sum(k * k, axis=-1, keepdims=True) + EPS)
            beta = beta_all[rs, h:h + 1]
            gcol = gc[:, H_A + h:H_A + h + 1]
            grow = gct[H_A + h:H_A + h + 1, :]
            glast = gc[CHUNK - 1:CHUNK, H_A + h:H_A + h + 1]
            decay = jnp.where(incl, jnp.exp(jnp.where(incl, gcol - grow, 0.0)), 0.0)
            eg = jnp.exp(gcol)
            kb = k * beta
            vb = v * beta
            kk_qk = lax.dot_general(jnp.concatenate([kb, q], axis=0).astype(BF16), k.astype(BF16),
                                    (((1,), (1,)), ((), ())), preferred_element_type=F32)
            m = jnp.where(strict, kk_qk[0:CHUNK] * decay, 0.0)
            a = kk_qk[CHUNK:2 * CHUNK] * decay
            t = _unit_lower_inverse(m, eye, base, levels)
            uw = _bdot(t, jnp.concatenate([vb, kb * eg], axis=1))
            u = uw[:, 0:DV_A]
            w = uw[:, DV_A:DV_A + DK_A]
            wq = jnp.concatenate([w, q * eg], axis=0).astype(BF16)
            kd = (k * jnp.exp(glast - gcol)).astype(BF16)
            heads.append((u, wq, a.astype(BF16), kd, jnp.exp(glast)))
        pre.append(heads)

    for h in range(H_A):
        s = s_ref[h]
        for c in range(nchunk):
            u, wq, a, kd, gl = pre[c][h]
            ws = jnp.dot(wq, s.astype(BF16), preferred_element_type=F32)
            v_new = u - ws[0:CHUNK]
            v_new_b = v_new.astype(BF16)
            o = ws[CHUNK:2 * CHUNK] + jnp.dot(a, v_new_b, preferred_element_type=F32)
            s = s * gl + lax.dot_general(kd, v_new_b, (((0,), (0,)), ((), ())), preferred_element_type=F32)
            o_ref[c * CHUNK:(c + 1) * CHUNK, h * DV_A:(h + 1) * DV_A] = o
        s_ref[h] = s
        sout_ref[0, h] = s


def _prompt_gdn(qkv, ba, cw, prm, n_seq, seq, tb):
    nb = seq // tb
    r = n_seq * seq
    return pl.pallas_call(
        functools.partial(_gdn_kernel, tb=tb),
        grid=(n_seq, nb),
        in_specs=[
            pl.BlockSpec((tb, QKV_A), lambda b, j: (b * nb + j, 0)),
            pl.BlockSpec((tb, LANES), lambda b, j: (b * nb + j, 0)),
            pl.BlockSpec(cw.shape, lambda b, j: (0, 0)),
            pl.BlockSpec(prm.shape, lambda b, j: (0, 0)),
        ],
        out_specs=[
            pl.BlockSpec((tb, H_A * DV_A), lambda b, j: (b * nb + j, 0)),
            pl.BlockSpec((1, H_A, DK_A, DV_A), lambda b, j: (b, 0, 0, 0)),
            pl.BlockSpec((1, SUBLANES, QKV_A), lambda b, j: (b, 0, 0)),
        ],
        out_shape=[
            jax.ShapeDtypeStruct((r, H_A * DV_A), F32),
            jax.ShapeDtypeStruct((n_seq, H_A, DK_A, DV_A), F32),
            jax.ShapeDtypeStruct((n_seq, SUBLANES, QKV_A), F32),
        ],
        scratch_shapes=[pltpu.VMEM((H_A, DK_A, DV_A), F32), pltpu.VMEM((tb + SUBLANES, QKV_A), F32)],
        compiler_params=_cparams(2, 48),
        name="prompt_gdn",
    )(qkv, ba, cw, prm)


def _merge_kernel(x_ref, oa_ref, ob_ref, nw_ref, w2_ref, onw_ref, sbw_ref, wb_ref, wo_ref, out_ref, *, ob_scale):
    x = x_ref[...]
    h = _rms(x, nw_ref[...]).astype(BF16)
    z = jnp.dot(h, w2_ref[:, P2_Z:P2_GA], preferred_element_type=F32)
    oa = oa_ref[...]
    ob = ob_ref[...]
    onw = onw_ref[...]
    sbw = sbw_ref[...]
    oa_n, ob_n = [], []
    for hd in range(H_A):
        sl = slice(hd * DV_A, (hd + 1) * DV_A)
        oa_n.append((_rms(oa[:, sl], onw) * _silu(z[:, sl])).astype(BF16))
    for hd in range(H_B):
        sl = slice(hd * DV_B, (hd + 1) * DV_B)
        ob_n.append((_rms(ob[:, sl], sbw) * ob_scale).astype(BF16))
    pa = jnp.dot(jnp.concatenate(oa_n, axis=1), wb_ref[0], preferred_element_type=F32)
    pb = jnp.dot(jnp.concatenate(ob_n, axis=1), wb_ref[1], preferred_element_type=F32)
    ga = jnp.dot(h, w2_ref[:, P2_GA:P2_GB], preferred_element_type=F32)
    gb = jnp.dot(h, w2_ref[:, P2_GB:P2_END], preferred_element_type=F32)
    mixed = (_sigmoid(ga) * pa + _sigmoid(gb) * pb).astype(BF16)
    out_ref[...] = x + jnp.dot(mixed, wo_ref[...], preferred_element_type=F32)


def _merge(x, oa, ob, nw, w2, onw, sbw, wb, wo, tm, ob_scale):
    r = x.shape[0]
    row = lambda w: pl.BlockSpec((tm, w), lambda i: (i, 0))
    full = lambda a: pl.BlockSpec(a.shape, lambda i: (0,) * a.ndim)
    return pl.pallas_call(
        functools.partial(_merge_kernel, ob_scale=ob_scale),
        grid=(r // tm,),
        in_specs=[row(D_MODEL), row(H_A * DV_A), row(H_B * DV_B), full(nw), full(w2), full(onw), full(sbw),
                  full(wb), full(wo)],
        out_specs=row(D_MODEL),
        out_shape=jax.ShapeDtypeStruct((r, D_MODEL), F32),
        compiler_params=_cparams(1, 52),
        name="merge",
    )(x, oa, ob, nw, w2, onw, sbw, wb, wo)


FF_CHUNK = 256


def _ffn_kernel(x_ref, st_ref, nw_ref, wup_ref, cw_ref, cb_ref, wdn_ref, out_ref, stout_ref,
                carry_ref, ext_ref, *, tm, shift, halo):
    i = pl.program_id(1)

    @pl.when(i == 0)
    def _():
        carry_ref[...] = st_ref[0]

    x = x_ref[...]
    h = _rms(x, nw_ref[...]).astype(BF16)
    cw = cw_ref[...]
    cb = cb_ref[...]
    acc = x
    for c in range(D_FF // FF_CHUNK):
        parts = []
        for part in range(2):
            lo = part * D_FF + c * FF_CHUNK
            sl = slice(lo, lo + FF_CHUNK)
            es = slice(part * FF_CHUNK, (part + 1) * FF_CHUNK)
            up = jnp.dot(h, wup_ref[:, sl], preferred_element_type=F32)
            ext_ref[0:halo, es] = carry_ref[:, sl]
            ext_ref[halo:halo + tm, es] = up
            u = cw[2:3, sl] * up + cb[:, sl]
            u = u + cw[1:2, sl] * ext_ref[halo - shift:halo - shift + tm, es]
            u = u + cw[0:1, sl] * ext_ref[halo - 2 * shift:halo - 2 * shift + tm, es]
            carry_ref[:, sl] = ext_ref[tm:tm + halo, es]
            parts.append(u)
        act = (_silu(parts[0]) * parts[1]).astype(BF16)
        acc = acc + jnp.dot(act, wdn_ref[c * FF_CHUNK:(c + 1) * FF_CHUNK, :], preferred_element_type=F32)
    out_ref[...] = acc
    stout_ref[0] = carry_ref[...]


def _ffn(x, st, nw, wup, cw, cb, wdn, groups, tm, shift, halo):
    r = x.shape[0]
    tiles = r // groups // tm
    full = lambda a: pl.BlockSpec(a.shape, lambda g, i: (0,) * a.ndim)
    return pl.pallas_call(
        functools.partial(_ffn_kernel, tm=tm, shift=shift, halo=halo),
        grid=(groups, tiles),
        in_specs=[
            pl.BlockSpec((tm, D_MODEL), lambda g, i: (g * tiles + i, 0)),
            pl.BlockSpec((1, halo, 2 * D_FF), lambda g, i: (g, 0, 0)),
            full(nw), full(wup), full(cw), full(cb), full(wdn),
        ],
        out_specs=[
            pl.BlockSpec((tm, D_MODEL), lambda g, i: (g * tiles + i, 0)),
            pl.BlockSpec((1, halo, 2 * D_FF), lambda g, i: (g, 0, 0)),
        ],
        out_shape=[jax.ShapeDtypeStruct((r, D_MODEL), F32),
                   jax.ShapeDtypeStruct((groups, halo, 2 * D_FF), F32)],
        scratch_shapes=[pltpu.VMEM((halo, 2 * D_FF), F32), pltpu.VMEM((halo + tm, 2 * FF_CHUNK), F32)],
        compiler_params=_cparams(2, 56),
        name="ffn",
    )(x, st, nw, wup, cw, cb, wdn)


def _sgdn_pre_kernel(qkv_ref, st_ref, ba_ref, cw_ref, prm_ref,
                     wq_ref, u_ref, kd_ref, a_ref, gl_ref, cst_ref, *, t_new, nb):
    cw = cw_ref[...]
    xp = [st_ref[i] for i in range(CONV_A - 1)] + [qkv_ref[t] for t in range(t_new)]
    for i in range(CONV_A - 1):
        cst_ref[i] = xp[t_new + i]
    prm = prm_ref[...]
    lane = lax.broadcasted_iota(jnp.int32, (nb, LANES), 1)

    ys, betas, gs = [], [], []
    for t in range(t_new):
        y = cw[0:1] * xp[t]
        for i in range(1, CONV_A):
            y = y + cw[i:i + 1] * xp[t + i]
        ys.append(_silu(y))
        bg = ba_ref[t]
        betas.append(_sigmoid(bg))
        gs.append(-jnp.exp(prm[0:1]) * _softplus(bg + prm[1:2]))
    gcs = [gs[0]]
    for t in range(1, t_new):
        gcs.append(gcs[-1] + gs[t])

    a_out = [jnp.zeros((nb, LANES), F32) for _ in range(t_new)]
    for h in range(H_A):
        q, k, v, beta, gc = [], [], [], [], []
        for t in range(t_new):
            qt = ys[t][:, h * DK_A:(h + 1) * DK_A]
            kt = ys[t][:, H_A * DK_A + h * DK_A:H_A * DK_A + (h + 1) * DK_A]
            q.append(qt * lax.rsqrt(jnp.sum(qt * qt, axis=-1, keepdims=True) + EPS) * (DK_A ** -0.5))
            k.append(kt * lax.rsqrt(jnp.sum(kt * kt, axis=-1, keepdims=True) + EPS))
            v.append(ys[t][:, 2 * H_A * DK_A + h * DV_A:2 * H_A * DK_A + (h + 1) * DV_A])
            beta.append(betas[t][:, h:h + 1])
            gc.append(gcs[t][:, H_A + h:H_A + h + 1])
        m = [[None] * t_new for _ in range(t_new)]
        for i in range(t_new):
            for jj in range(i + 1):
                dec = jnp.exp(gc[i] - gc[jj])
                if jj < i:
                    m[i][jj] = beta[i] * jnp.sum(k[i] * k[jj], axis=-1, keepdims=True) * dec
                aij = jnp.sum(q[i] * k[jj], axis=-1, keepdims=True) * dec
                a_out[i] = jnp.where(lane == h * SUBLANES + jj, aij, a_out[i])
        tm_ = [[None] * t_new for _ in range(t_new)]
        for i in range(t_new):
            for jj in range(i):
                acc = m[i][jj]
                for l in range(jj + 1, i):
                    acc = acc + m[i][l] * tm_[l][jj]
                tm_[i][jj] = -acc
        vb = [v[t] * beta[t] for t in range(t_new)]
        kbg = [k[t] * (beta[t] * jnp.exp(gc[t])) for t in range(t_new)]
        hs = slice(h * DK_A, (h + 1) * DK_A)
        for i in range(t_new):
            u = vb[i]
            w = kbg[i]
            for jj in range(i):
                u = u + tm_[i][jj] * vb[jj]
                w = w + tm_[i][jj] * kbg[jj]
            u_ref[i, :, hs] = u
            wq_ref[i, :, hs] = w
            wq_ref[t_new + i, :, hs] = q[i] * jnp.exp(gc[i])
            kd_ref[i, :, hs] = k[i] * jnp.exp(gc[t_new - 1] - gc[i])
        gl_ref[:, hs] = jnp.broadcast_to(jnp.exp(gc[t_new - 1]), (nb, DK_A))
    for i in range(t_new):
        a_ref[i] = a_out[i]


def _sgdn_pre(qkv_t, st_t, ba_t, cw, prm, t_new, nb):
    wide = H_A * DK_A
    return pl.pallas_call(
        functools.partial(_sgdn_pre_kernel, t_new=t_new, nb=nb),
        out_shape=[
            jax.ShapeDtypeStruct((2 * t_new, nb, wide), F32),
            jax.ShapeDtypeStruct((t_new, nb, wide), F32),
            jax.ShapeDtypeStruct((t_new, nb, wide), F32),
            jax.ShapeDtypeStruct((t_new, nb, LANES), F32),
            jax.ShapeDtypeStruct((nb, wide), F32),
            jax.ShapeDtypeStruct((CONV_A - 1, nb, QKV_A), F32),
        ],
        compiler_params=pltpu.CompilerParams(vmem_limit_bytes=48 * 1024 * 1024),
        name="sample_gdn_pre",
    )(qkv_t, st_t, ba_t, cw, prm)


SGDN_BB = 8


def _sgdn_state_kernel(wq_ref, u_ref, kd_ref, a_ref, gl_ref, s0_ref, o_ref, s1_ref, *, t_new):
    rows = 2 * t_new
    rid = lax.broadcasted_iota(jnp.int32, (rows, DK_A), 0)
    zpad = jnp.zeros((rows - t_new, DK_A), F32)
    for bi in range(SGDN_BB):
        amat = a_ref[:, bi, :]
        glrow = gl_ref[bi:bi + 1, :]
        for h in range(H_A):
            hs = slice(h * DK_A, (h + 1) * DK_A)
            s0 = s0_ref[bi, h]
            r = jnp.dot(wq_ref[:, bi, hs].astype(BF16), s0.astype(BF16), preferred_element_type=F32)
            u8 = jnp.concatenate([u_ref[:, bi, hs], zpad], axis=0)
            v_new = jnp.where(rid < t_new, u8 - r, 0.0)
            o = r[t_new:rows]
            for jj in range(t_new):
                col = amat[:, h * SUBLANES + jj:h * SUBLANES + jj + 1]
                o = o + col * v_new[jj:jj + 1, :]
            kd8 = jnp.concatenate([kd_ref[:, bi, hs], zpad], axis=0)
            s1 = s0 * glrow[:, hs] + lax.dot_general(kd8.astype(BF16), v_new.astype(BF16),
                                                     (((0,), (0,)), ((), ())), preferred_element_type=F32)
            s1_ref[bi, h] = s1
            o_ref[:, bi, hs] = o


def _sgdn_state(wq, u, kd, a, gl, s0, t_new, nb):
    wide = H_A * DK_A
    bb = SGDN_BB
    return pl.pallas_call(
        functools.partial(_sgdn_state_kernel, t_new=t_new),
        grid=(nb // bb,),
        in_specs=[
            pl.BlockSpec((2 * t_new, bb, wide), lambda i: (0, i, 0)),
            pl.BlockSpec((t_new, bb, wide), lambda i: (0, i, 0)),
            pl.BlockSpec((t_new, bb, wide), lambda i: (0, i, 0)),
            pl.BlockSpec((t_new, bb, LANES), lambda i: (0, i, 0)),
            pl.BlockSpec((bb, wide), lambda i: (i, 0)),
            pl.BlockSpec((bb, H_A, DK_A, DV_A), lambda i: (i, 0, 0, 0)),
        ],
        out_specs=[
            pl.BlockSpec((t_new, bb, wide), lambda i: (0, i, 0)),
            pl.BlockSpec((bb, H_A, DK_A, DV_A), lambda i: (i, 0, 0, 0)),
        ],
        out_shape=[jax.ShapeDtypeStruct((t_new, nb, wide), F32),
                   jax.ShapeDtypeStruct((nb, H_A, DK_A, DV_A), F32)],
        compiler_params=_cparams(1, 32),
        name="sample_gdn_state",
    )(wq, u, kd, a, gl, s0)


NEW_ROWS = 16


def _sattn_kernel(pt_ref, lam_ref, q_ref, kn_ref, vn_ref, bias_ref, *rest, n_pages, t_new, lam_init):
    del pt_ref
    k_pages = rest[:n_pages]
    v_pages = rest[n_pages:2 * n_pages]
    o_ref = rest[2 * n_pages]
    rows = H_B * 2 * t_new
    q = q_ref[0]
    qt = jnp.concatenate([q] * (2 * H_B), axis=0)
    r = lax.broadcasted_iota(jnp.int32, (rows, HB_W), 0)
    c = lax.broadcasted_iota(jnp.int32, (rows, HB_W), 1)
    qx = jnp.where((c // DH_B) == (r // t_new), qt, jnp.zeros_like(qt)).astype(BF16)
    bias = bias_ref[...]

    nt = (((1,), (1,)), ((), ()))
    s_parts = []
    for p in range(n_pages):
        s = lax.dot_general(qx, k_pages[p][0].astype(BF16), nt, preferred_element_type=F32)
        if p == n_pages - 1:
            s = s + bias[:, 0:PAGE]
        s_parts.append(s)
    pad = jnp.zeros((PAGE - NEW_ROWS, HB_W), BF16)
    kn = jnp.concatenate([kn_ref[0], pad], axis=0)
    s_new = lax.dot_general(qx, kn, nt, preferred_element_type=F32) + bias[:, PAGE:2 * PAGE]
    s_parts.append(s_new)

    m = s_parts[0].max(axis=-1, keepdims=True)
    for s in s_parts[1:]:
        m = jnp.maximum(m, s.max(axis=-1, keepdims=True))
    l = jnp.zeros((rows, 1), F32)
    acc = jnp.zeros((rows, HB_W), F32)
    for p in range(n_pages + 1):
        pr = jnp.exp(s_parts[p] - m)
        l = l + jnp.sum(pr, axis=-1, keepdims=True)
        if p < n_pages:
            vv = v_pages[p][0].astype(BF16)
        else:
            vv = jnp.concatenate([vn_ref[0], pad], axis=0)
        acc = acc + jnp.dot(pr.astype(BF16), vv, preferred_element_type=F32)
    acc = acc / l
    lam = _lam_value(lam_ref[...], lam_init)
    for h in range(H_B):
        r0 = h * 2 * t_new
        ls = slice(h * DV_B, (h + 1) * DV_B)
        o_ref[0, :, ls] = acc[r0:r0 + t_new, ls] - lam * acc[r0 + t_new:r0 + 2 * t_new, ls]


def _sample_attn(page_table, lamv, q_b, kn_b, vn_b, bias, ck, cv, t_new, lam_init):
    nb, n_pages = page_table.shape
    page_spec = lambda p: pl.BlockSpec((1, PAGE, HB_W), lambda b, pt, p=p: (pt[b, p], 0, 0))
    tok_spec = pl.BlockSpec((1, t_new, HB_W), lambda b, pt: (b, 0, 0))
    new_spec = pl.BlockSpec((1, NEW_ROWS, HB_W), lambda b, pt: (b, 0, 0))
    grid_spec = pltpu.PrefetchScalarGridSpec(
        num_scalar_prefetch=1,
        grid=(nb,),
        in_specs=[pl.BlockSpec(lamv.shape, lambda b, pt: (0, 0)), tok_spec, new_spec, new_spec,
                  pl.BlockSpec(bias.shape, lambda b, pt: (0, 0))]
                 + [page_spec(p) for p in range(n_pages)] * 2,
        out_specs=pl.BlockSpec((1, t_new, HB_W), lambda b, pt: (b, 0, 0)),
    )
    return pl.pallas_call(
        functools.partial(_sattn_kernel, n_pages=n_pages, t_new=t_new, lam_init=lam_init),
        grid_spec=grid_spec,
        out_shape=jax.ShapeDtypeStruct((nb, t_new, HB_W), F32),
        compiler_params=_cparams(1, 40),
        name="sample_attn",
    )(page_table, lamv, q_b, kn_b, vn_b, bias, *([ck] * n_pages), *([cv] * n_pages))


def _pick(n, pref):
    return pref if n % pref == 0 else n


def _to_bmajor(a_t, t_new, nb):
    return jnp.swapaxes(a_t.reshape(t_new, nb, -1), 0, 1)


def _layer_weights(li, norm_mix, w_in, conv_a, a_log, dt_bias, onorm_a, qnorm_b, knorm_b,
                   lam_q1, lam_k1, lam_q2, lam_k2, subln_b, w_branch, w_o, norm_ffn, w_up, conv_f,
                   conv_f_b, w_down):
    w = w_in[li]
    o_z = QKV_A
    o_b = o_z + H_A * DV_A
    o_qb = o_b + 2 * H_A
    o_kb = o_qb + HB_W
    o_vb = o_kb + HB_W
    o_ga = o_vb + H_B * DV_B
    o_gb = o_ga + D_MODEL
    ba_cols = jnp.pad(w[:, o_b:o_qb], ((0, 0), (0, LANES - 2 * H_A)))
    w1 = jnp.concatenate([w[:, 0:o_z], w[:, o_qb:o_ga], ba_cols], axis=1).astype(BF16)
    w2 = jnp.concatenate([w[:, o_z:o_b], w[:, o_ga:]], axis=1).astype(BF16)
    prm = jnp.zeros((2, LANES), F32)
    prm = prm.at[0, H_A:2 * H_A].set(a_log[li]).at[1, H_A:2 * H_A].set(dt_bias[li])
    return dict(
        nw=norm_mix[li].reshape(1, D_MODEL), w1=w1, w2=w2,
        qw=jnp.tile(qnorm_b[li], 2 * H_B).reshape(1, HB_W),
        kw=jnp.tile(knorm_b[li], 2 * H_B).reshape(1, HB_W),
        cw_a=conv_a[li], prm=prm,
        onw=onorm_a[li].reshape(1, DV_A), sbw=subln_b[li].reshape(1, DV_B),
        lamv=jnp.stack([lam_q1[li], lam_k1[li], lam_q2[li], lam_k2[li]]),
        wb=w_branch[li].astype(BF16), wo=w_o[li].astype(BF16),
        nwf=norm_ffn[li].reshape(1, D_MODEL), wup=w_up[li].astype(BF16),
        cw_f=conv_f[li], cb_f=conv_f_b[li].reshape(1, 2 * D_FF), wdn=w_down[li].astype(BF16),
    )


def _group_ones():
    i = jnp.arange(MXU_DIM)
    return ((i[:, None] // DH_B) == (i[None, :] // DH_B)).astype(BF16)


def _prompt_layer(x, wt, bias_p, g, n_seq, seq, lam_init):
    tm = _pick(seq, 512)
    blk = bias_p.shape[-1]
    qkv, ba, qn, kn, kf, vf, vb = _inproj(x, wt["nw"], wt["w1"], wt["qw"], wt["kw"], g, tm)
    o_a, s_fin, cst_a = _prompt_gdn(qkv, ba, wt["cw_a"], wt["prm"], n_seq, seq, _pick(seq, 256))
    o_b = _prompt_attn(wt["lamv"], qn, kn, vb, bias_p, n_seq, seq, blk, lam_init)
    x = _merge(x, o_a, o_b, wt["nw"], wt["w2"], wt["onw"], wt["sbw"], wt["wb"], wt["wo"], tm, 1.0 - lam_init)
    zero_st = jnp.zeros((n_seq, SUBLANES, 2 * D_FF), F32)
    x, cst_f = _ffn(x, zero_st, wt["nwf"], wt["wup"], wt["cw_f"], wt["cb_f"], wt["wdn"],
                    n_seq, tm, 1, SUBLANES)
    return (x, kf.reshape(n_seq, seq, H_B, 2 * DH_B), vf.reshape(n_seq, seq, H_B, DV_B), s_fin,
            cst_a[:, SUBLANES - (CONV_A - 1):], cst_f[:, SUBLANES - (CONV_F - 1):])


def _sample_layer(x_t, wt, bias_s, g, page_table, ck, cv, s0, cst_a, cst_f, nb, t_new, lam_init):
    r = nb * t_new
    qkv, ba, qn, kn, kf, vf, vb = _inproj(x_t, wt["nw"], wt["w1"], wt["qw"], wt["kw"], g, r)
    st_a_t = jnp.swapaxes(cst_a, 0, 1)
    wq, u, kd, a, gl, cst_a_t = _sgdn_pre(qkv.reshape(t_new, nb, QKV_A), st_a_t,
                                          ba.reshape(t_new, nb, LANES), wt["cw_a"], wt["prm"], t_new, nb)
    o_a_t, s1 = _sgdn_state(wq, u, kd, a, gl, s0, t_new, nb)
    pad_new = lambda a: jnp.pad(_to_bmajor(a, t_new, nb), ((0, 0), (0, NEW_ROWS - t_new), (0, 0)))
    o_b = _sample_attn(page_table, wt["lamv"], _to_bmajor(qn, t_new, nb).astype(F32), pad_new(kn),
                       pad_new(vb), bias_s, ck, cv, t_new, lam_init)
    o_b_t = jnp.swapaxes(o_b, 0, 1).reshape(r, HB_W)
    x_t = _merge(x_t, o_a_t.reshape(r, H_A * DV_A), o_b_t, wt["nw"], wt["w2"], wt["onw"], wt["sbw"],
                 wt["wb"], wt["wo"], r, 1.0 - lam_init)
    halo = (CONV_F - 1) * nb
    st_f_t = jnp.swapaxes(cst_f, 0, 1).reshape(1, halo, 2 * D_FF)
    x_t, cst_f_t = _ffn(x_t, st_f_t, wt["nwf"], wt["wup"], wt["cw_f"], wt["cb_f"], wt["wdn"],
                        1, r, nb, halo)
    return (x_t, _to_bmajor(kf, t_new, nb).reshape(nb, t_new, H_B, 2 * DH_B),
            _to_bmajor(vf, t_new, nb).reshape(nb, t_new, H_B, DV_B), s1,
            jnp.swapaxes(cst_a_t, 0, 1), jnp.swapaxes(cst_f_t.reshape(CONV_F - 1, nb, 2 * D_FF), 0, 1))


def kernel(x_prompt, x_sample, cache_k, cache_v, state_delta, state_conv_a, state_conv_ffn, page_table,
           rel_bias, norm_mix, w_in, conv_a, a_log, dt_bias, onorm_a, qnorm_b, knorm_b, lam_q1, lam_k1,
           lam_q2, lam_k2, subln_b, w_branch, w_o, norm_ffn, w_up, conv_f, conv_f_b, w_down):
    n_seq, seq, _ = x_prompt.shape
    nb, t_new, _ = x_sample.shape
    depth = w_in.shape[0]
    n_pool = cache_k.shape[1]

    rb_flat = rel_bias.T.reshape(-1)
    bias_p = _prompt_bias(rb_flat, _pick(seq, 512))
    bias_s = _sample_bias(rb_flat, t_new)
    g = _group_ones()

    xp = x_prompt.reshape(n_seq * seq, D_MODEL)
    xs = jnp.swapaxes(x_sample, 0, 1).reshape(t_new * nb, D_MODEL)
    outs_p, outs_s = [], []
    for li in range(depth):
        wt = _layer_weights(li, norm_mix, w_in, conv_a, a_log, dt_bias, onorm_a, qnorm_b, knorm_b,
                            lam_q1, lam_k1, lam_q2, lam_k2, subln_b, w_branch, w_o, norm_ffn, w_up,
                            conv_f, conv_f_b, w_down)
        lam_init = 0.8 - 0.6 * math.exp(-0.3 * li)
        xp, *rest_p = _prompt_layer(xp, wt, bias_p, g, n_seq, seq, lam_init)
        ck = cache_k[li].reshape(n_pool, PAGE, HB_W)
        cv = cache_v[li].reshape(n_pool, PAGE, H_B * DV_B)
        xs, *rest_s = _sample_layer(xs, wt, bias_s, g, page_table, ck, cv, state_delta[li],
                                    state_conv_a[li], state_conv_ffn[li], nb, t_new, lam_init)
        outs_p.append(rest_p)
        outs_s.append(rest_s)

    stack = lambda outs, i: jnp.stack([o[i] for o in outs])
    y_prompt = xp.reshape(n_seq, seq, D_MODEL)
    y_sample = jnp.swapaxes(xs.reshape(t_new, nb, D_MODEL), 0, 1)
    return (y_prompt, y_sample,
            stack(outs_p, 0), stack(outs_p, 1), stack(outs_p, 2), stack(outs_p, 3), stack(outs_p, 4),
            stack(outs_s, 0), stack(outs_s, 1), stack(outs_s, 2), stack(outs_s, 3), stack(outs_s, 4))
```

```python
import functools
import math

import jax
import jax.numpy as jnp
from jax import lax
from jax.experimental import pallas as pl
from jax.experimental.pallas import tpu as pltpu

F32 = jnp.float32
BF16 = jnp.bfloat16

D_MODEL = 1024
H_A, DK_A, DV_A, CONV_A, CHUNK = 4, 128, 128, 4, 64
QKV_A = 2 * H_A * DK_A + H_A * DV_A
H_B, DH_B, DV_B = 4, 64, 128
HB_W = H_B * 2 * DH_B
PAGE = 128
D_FF, CONV_F = 2816, 3
NUM_BUCKETS, MAX_DISTANCE = 32, 128
NEG_INF = -1e30
EPS = 1e-6
ATT_SCALE = DH_B ** -0.5

V7X_VMEM_BYTES = 64 * 1024 * 1024
LANES = 128
SUBLANES = 8
MXU_DIM = 256

P1_QKV, P1_Q, P1_K, P1_V, P1_BA, P1_END = 0, 1536, 2048, 2560, 3072, 3200
P2_Z, P2_GA, P2_GB, P2_END = 0, 512, 1536, 2560


def _cparams(n_axes, vmem_mib):
    return pltpu.CompilerParams(
        dimension_semantics=("arbitrary",) * n_axes,
        vmem_limit_bytes=min(vmem_mib * 1024 * 1024, V7X_VMEM_BYTES - 8 * 1024 * 1024),
    )


def _bdot(a, b):
    return jnp.dot(a.astype(BF16), b.astype(BF16), preferred_element_type=F32)


def _rms(x, w):
    return x * lax.rsqrt(jnp.mean(x * x, axis=-1, keepdims=True) + EPS) * w


def _sigmoid(x):
    return 1.0 / (1.0 + jnp.exp(-x))


def _silu(x):
    return x * _sigmoid(x)


def _softplus(x):
    return jnp.maximum(x, 0.0) + jnp.log1p(jnp.exp(-jnp.abs(x)))


def _inproj_kernel(x_ref, nw_ref, w_ref, qw_ref, kw_ref, g_ref,
                   qkv_ref, ba_ref, qn_ref, kn_ref, kf_ref, vf_ref, vb_ref):
    h = _rms(x_ref[...], nw_ref[...]).astype(BF16)

    def proj(lo, hi):
        return jnp.dot(h, w_ref[:, lo:hi], preferred_element_type=F32)

    qkv_ref[...] = proj(P1_QKV, P1_Q)
    ba_ref[...] = proj(P1_BA, P1_END)
    v = proj(P1_V, P1_BA)
    vf_ref[...] = v
    vb_ref[...] = v.astype(BF16)

    g = g_ref[...]

    def qk_norm(y, w):
        sq = y * y
        hi = sq.astype(BF16)
        lo = (sq - hi.astype(F32)).astype(BF16)
        outs = []
        for c in range(HB_W // MXU_DIM):
            sl = slice(c * MXU_DIM, (c + 1) * MXU_DIM)
            ss = (jnp.dot(hi[:, sl], g, preferred_element_type=F32)
                  + jnp.dot(lo[:, sl], g, preferred_element_type=F32))
            outs.append(y[:, sl] * lax.rsqrt(ss * (1.0 / DH_B) + EPS) * w[:, sl])
        return outs

    qn = qk_norm(proj(P1_Q, P1_K), qw_ref[...])
    kn = qk_norm(proj(P1_K, P1_V), kw_ref[...])
    for c in range(HB_W // MXU_DIM):
        sl = slice(c * MXU_DIM, (c + 1) * MXU_DIM)
        qn_ref[:, sl] = (qn[c] * ATT_SCALE).astype(BF16)
        kf_ref[:, sl] = kn[c]
        kn_ref[:, sl] = kn[c].astype(BF16)


def _inproj(x, nw, w1, qw, kw, g, tm):
    r = x.shape[0]
    row = lambda w: pl.BlockSpec((tm, w), lambda i: (i, 0))
    full = lambda a: pl.BlockSpec(a.shape, lambda i: (0,) * a.ndim)
    outs = [(QKV_A, F32), (LANES, F32), (HB_W, BF16), (HB_W, BF16), (HB_W, F32), (HB_W, F32), (HB_W, BF16)]
    return pl.pallas_call(
        _inproj_kernel,
        grid=(r // tm,),
        in_specs=[row(D_MODEL), full(nw), full(w1), full(qw), full(kw), full(g)],
        out_specs=[row(w) for w, _ in outs],
        out_shape=[jax.ShapeDtypeStruct((r, w), dt) for w, dt in outs],
        compiler_params=_cparams(1, 52),
        name="inproj",
    )(x, nw, w1, qw, kw, g)


def _bucket_bias(d, table):
    n = jnp.maximum(d, 0)
    max_exact = NUM_BUCKETS // 2
    nf = jnp.maximum(n, 1).astype(F32)
    large = max_exact + (jnp.log(nf / max_exact) / math.log(MAX_DISTANCE / max_exact)
                         * (NUM_BUCKETS - max_exact)).astype(jnp.int32)
    large = jnp.minimum(large, NUM_BUCKETS - 1)
    bucket = jnp.where(n < max_exact, n, large)
    val = jnp.zeros(d.shape, F32)
    for b in range(NUM_BUCKETS):
        val = jnp.where(bucket == b, table(b), val)
    return val


def _prompt_bias_kernel(rb_ref, o_ref, *, blk):
    h = pl.program_id(0)
    table = lambda b: rb_ref[h * NUM_BUCKETS + b]
    far = table(NUM_BUCKETS - 1)
    i = lax.broadcasted_iota(jnp.int32, (blk, blk), 0)
    j = lax.broadcasted_iota(jnp.int32, (blk, blk), 1)
    d0 = i - j
    o_ref[0, 0] = jnp.where(d0 >= 0, _bucket_bias(d0, table) - far, NEG_INF)
    o_ref[0, 1] = _bucket_bias(d0 + blk, table) - far


def _prompt_bias(rb_flat, blk):
    return pl.pallas_call(
        functools.partial(_prompt_bias_kernel, blk=blk),
        grid=(H_B,),
        in_specs=[pl.BlockSpec(memory_space=pltpu.SMEM)],
        out_specs=pl.BlockSpec((1, 2, blk, blk), lambda h: (h, 0, 0, 0)),
        out_shape=jax.ShapeDtypeStruct((H_B, 2, blk, blk), F32),
        compiler_params=_cparams(1, 32),
        name="prompt_bias",
    )(rb_flat)


def _sample_bias_kernel(rb_ref, o_ref, *, t_new):
    rows, cols = o_ref.shape
    r = lax.broadcasted_iota(jnp.int32, (rows, cols), 0)
    c = lax.broadcasted_iota(jnp.int32, (rows, cols), 1)
    t = r % t_new
    hd = r // (2 * t_new)
    is_new = c >= PAGE
    d = jnp.where(is_new, t - (c - PAGE), t + PAGE - c)
    val = jnp.zeros((rows, cols), F32)
    for h in range(H_B):
        table = lambda b, h=h: rb_ref[h * NUM_BUCKETS + b]
        vh = _bucket_bias(d, table) - table(NUM_BUCKETS - 1)
        val = jnp.where(hd == h, vh, val)
    visible = jnp.logical_and(d >= 0, jnp.logical_or(~is_new, (c - PAGE) < t_new))
    o_ref[...] = jnp.where(visible, val, NEG_INF)


def _sample_bias(rb_flat, t_new):
    rows = H_B * 2 * t_new
    return pl.pallas_call(
        functools.partial(_sample_bias_kernel, t_new=t_new),
        in_specs=[pl.BlockSpec(memory_space=pltpu.SMEM)],
        out_shape=jax.ShapeDtypeStruct((rows, 2 * PAGE), F32),
        name="sample_bias",
    )(rb_flat)


def _lam_value(lv, lam_init):
    s1 = jnp.sum(lv[0:1] * lv[1:2], axis=-1, keepdims=True)
    s2 = jnp.sum(lv[2:3] * lv[3:4], axis=-1, keepdims=True)
    return jnp.exp(s1) - jnp.exp(s2) + lam_init


def _attn_kernel(lam_ref, q_ref, k_ref, v_ref, bias_ref, o_ref, m_ref, l_ref, acc_ref, *, blk, lam_init):
    qi = pl.program_id(2)
    q = q_ref[...]
    lane = lax.broadcasted_iota(jnp.int32, (1, 2 * DH_B), 1)
    zero = jnp.zeros_like(q)
    qs = (jnp.where(lane < DH_B, q, zero), jnp.where(lane >= DH_B, q, zero))
    m_ref[...] = jnp.full(m_ref.shape, NEG_INF, F32)
    l_ref[...] = jnp.zeros(l_ref.shape, F32)
    acc_ref[...] = jnp.zeros(acc_ref.shape, F32)
    reps = blk // LANES

    def update(off, bias):
        kb = k_ref[pl.ds(off, blk), :]
        vb = v_ref[pl.ds(off, blk), :]
        for mp in range(2):
            s = lax.dot_general(qs[mp], kb, (((1,), (1,)), ((), ())), preferred_element_type=F32)
            if bias is not None:
                s = s + bias
            m_old = m_ref[mp]
            m_new = jnp.maximum(m_old, jnp.max(s, axis=-1, keepdims=True))
            alpha = jnp.exp(m_old - m_new)
            p = jnp.exp(s - pltpu.repeat(m_new, reps, axis=1))
            l_ref[mp] = alpha * l_ref[mp] + jnp.sum(p, axis=-1, keepdims=True)
            acc_ref[mp] = alpha * acc_ref[mp] + jnp.dot(p.astype(BF16), vb, preferred_element_type=F32)
            m_ref[mp] = m_new

    def far(kj, carry):
        update(pl.multiple_of(kj * blk, blk), None)
        return carry

    lax.fori_loop(0, jnp.maximum(qi - 1, 0), far, 0)

    @pl.when(qi >= 1)
    def _():
        update(pl.multiple_of((qi - 1) * blk, blk), bias_ref[0, 1])

    update(pl.multiple_of(qi * blk, blk), bias_ref[0, 0])

    lam = _lam_value(lam_ref[...], lam_init)
    o_ref[...] = acc_ref[0] / l_ref[0] - lam * (acc_ref[1] / l_ref[1])


def _prompt_attn(lamv, qn, kn, vb, bias, n_seq, seq, blk, lam_init):
    nq = seq // blk
    r = n_seq * seq
    return pl.pallas_call(
        functools.partial(_attn_kernel, blk=blk, lam_init=lam_init),
        grid=(n_seq, H_B, nq),
        in_specs=[
            pl.BlockSpec(lamv.shape, lambda b, h, i: (0, 0)),
            pl.BlockSpec((blk, LANES), lambda b, h, i: (b * nq + i, h)),
            pl.BlockSpec((seq, LANES), lambda b, h, i: (b, h)),
            pl.BlockSpec((seq, LANES), lambda b, h, i: (b, h)),
            pl.BlockSpec((1, 2, blk, blk), lambda b, h, i: (h, 0, 0, 0)),
        ],
        out_specs=pl.BlockSpec((blk, LANES), lambda b, h, i: (b * nq + i, h)),
        out_shape=jax.ShapeDtypeStruct((r, HB_W), F32),
        scratch_shapes=[pltpu.VMEM((2, blk, LANES), F32), pltpu.VMEM((2, blk, LANES), F32),
                        pltpu.VMEM((2, blk, DV_B), F32)],
        compiler_params=_cparams(3, 48),
        name="prompt_attn",
    )(lamv, qn, kn, vb, bias)


def _tri_masks():
    i = lax.broadcasted_iota(jnp.int32, (CHUNK, CHUNK), 0)
    j = lax.broadcasted_iota(jnp.int32, (CHUNK, CHUNK), 1)
    incl = i >= j
    strict = i > j
    eye = (i == j).astype(F32)
    base = jnp.logical_and(strict, (i // SUBLANES) == (j // SUBLANES))
    levels = []
    s = SUBLANES
    while s < CHUNK:
        levels.append(jnp.logical_and((i // (2 * s)) == (j // (2 * s)), (i // s) > (j // s)))
        s *= 2
    return incl, strict, eye, base, levels


def _unit_lower_inverse(ms, eye, base, levels):
    d = [jnp.where(base, m, 0.0) for m in ms]
    d2 = [_bdot(a, a) for a in d]
    d4 = [_bdot(a, a) for a in d2]
    x = [_bdot(eye - a, eye + b) for a, b in zip(d, d2)]
    x = [_bdot(a, eye + b) for a, b in zip(x, d4)]
    for lvl in levels:
        c = [jnp.where(lvl, m, 0.0) for m in ms]
        xc = [_bdot(a, b) for a, b in zip(x, c)]
        xcx = [_bdot(a, b) for a, b in zip(xc, x)]
        x = [a - b for a, b in zip(x, xcx)]
    return x


def _gdn_kernel(qkv_ref, ba_ref, cw_ref, prm_ref, o_ref, sout_ref, cst_ref, s_ref, ext_ref, *, tb, n_seq):
    j = pl.program_id(0)
    nchunk = tb // CHUNK
    halo = SUBLANES

    @pl.when(j == 0)
    def _():
        s_ref[...] = jnp.zeros(s_ref.shape, F32)
        ext_ref[:, 0:halo, :] = jnp.zeros((n_seq, halo, QKV_A), F32)

    cw = cw_ref[...]
    prm = prm_ref[...]
    incl, strict, eye, base, levels = _tri_masks()
    tril = incl.astype(F32)
    nt = (((1,), (1,)), ((), ()))

    pairs = [(b, c) for b in range(n_seq) for c in range(nchunk)]
    ys, betas, gcs = {}, {}, {}
    for b in range(n_seq):
        x = qkv_ref[b]
        ext_ref[b, halo:halo + tb, :] = x
        y = cw[3:4] * x
        for i in range(CONV_A - 1):
            sh = CONV_A - 1 - i
            y = y + cw[i:i + 1] * ext_ref[b, halo - sh:halo - sh + tb, :]
        tail = x[tb - halo:tb, :]
        ext_ref[b, 0:halo, :] = tail
        cst_ref[b] = tail
        ys[b] = _silu(y)
        bg = ba_ref[b]
        betas[b] = _sigmoid(bg)
        g_all = -jnp.exp(prm[0:1]) * _softplus(bg + prm[1:2])
        for c in range(nchunk):
            gcs[b, c] = jnp.dot(tril, g_all[c * CHUNK:(c + 1) * CHUNK],
                                precision=lax.Precision.HIGHEST, preferred_element_type=F32)

    grp = [(b, c, h) for (b, c) in pairs for h in range(H_A)]
    q_, k_, kb_, vb_, dec_, eg_, ekd_, gl_ = [], [], [], [], [], [], [], []
    gct = {bc: gcs[bc].T for bc in pairs}
    for (b, c, h) in grp:
        rs = slice(c * CHUNK, (c + 1) * CHUNK)
        y = ys[b]
        q = y[rs, h * DK_A:(h + 1) * DK_A]
        k = y[rs, H_A * DK_A + h * DK_A:H_A * DK_A + (h + 1) * DK_A]
        v = y[rs, 2 * H_A * DK_A + h * DV_A:2 * H_A * DK_A + (h + 1) * DV_A]
        q = q * lax.rsqrt(jnp.sum(q * q, axis=-1, keepdims=True) + EPS) * (DK_A ** -0.5)
        k = k * lax.rsqrt(jnp.sum(k * k, axis=-1, keepdims=True) + EPS)
        beta = betas[b][rs, h:h + 1]
        gc = gcs[b, c]
        gcol = gc[:, H_A + h:H_A + h + 1]
        grow = gct[b, c][H_A + h:H_A + h + 1, :]
        glast = gc[CHUNK - 1:CHUNK, H_A + h:H_A + h + 1]
        dec_.append(jnp.where(incl, jnp.exp(jnp.where(incl, gcol - grow, 0.0)), 0.0))
        eg_.append(jnp.exp(gcol))
        ekd_.append(jnp.exp(glast - gcol))
        gl_.append(jnp.exp(glast))
        q_.append(q)
        k_.append(k)
        kb_.append(k * beta)
        vb_.append(v * beta)

    n = len(grp)
    kk_qk = [lax.dot_general(jnp.concatenate([kb_[g], q_[g]], axis=0).astype(BF16), k_[g].astype(BF16),
                             nt, preferred_element_type=F32) for g in range(n)]
    ms = [jnp.where(strict, kk_qk[g][0:CHUNK] * dec_[g], 0.0) for g in range(n)]
    a_ = [(kk_qk[g][CHUNK:2 * CHUNK] * dec_[g]).astype(BF16) for g in range(n)]
    t_ = _unit_lower_inverse(ms, eye, base, levels)
    uw = [_bdot(t_[g], jnp.concatenate([vb_[g], kb_[g] * eg_[g]], axis=1)) for g in range(n)]
    wq_ = [jnp.concatenate([uw[g][:, DV_A:DV_A + DK_A], q_[g] * eg_[g]], axis=0).astype(BF16)
           for g in range(n)]
    kd_ = [(k_[g] * ekd_[g]).astype(BF16) for g in range(n)]

    lanes = [(b, h) for b in range(n_seq) for h in range(H_A)]
    s = {bh: s_ref[bh[0], bh[1]] for bh in lanes}
    for c in range(nchunk):
        gi = {(b, h): grp.index((b, c, h)) for (b, h) in lanes}
        ws = {bh: jnp.dot(wq_[gi[bh]], s[bh].astype(BF16), preferred_element_type=F32) for bh in lanes}
        vn = {bh: (uw[gi[bh]][:, 0:DV_A] - ws[bh][0:CHUNK]).astype(BF16) for bh in lanes}
        av = {bh: jnp.dot(a_[gi[bh]], vn[bh], preferred_element_type=F32) for bh in lanes}
        kv = {bh: lax.dot_general(kd_[gi[bh]], vn[bh], (((0,), (0,)), ((), ())), preferred_element_type=F32)
              for bh in lanes}
        for (b, h) in lanes:
            o_ref[b, c * CHUNK:(c + 1) * CHUNK, h * DV_A:(h + 1) * DV_A] = ws[b, h][CHUNK:2 * CHUNK] + av[b, h]
            s[b, h] = s[b, h] * gl_[gi[b, h]] + kv[b, h]
    for (b, h) in lanes:
        s_ref[b, h] = s[b, h]
        sout_ref[b, h] = s[b, h]


def _prompt_gdn(qkv, ba, cw, prm, n_seq, seq, tb):
    nb = seq // tb
    return pl.pallas_call(
        functools.partial(_gdn_kernel, tb=tb, n_seq=n_seq),
        grid=(nb,),
        in_specs=[
            pl.BlockSpec((n_seq, tb, QKV_A), lambda j: (0, j, 0)),
            pl.BlockSpec((n_seq, tb, LANES), lambda j: (0, j, 0)),
            pl.BlockSpec(cw.shape, lambda j: (0, 0)),
            pl.BlockSpec(prm.shape, lambda j: (0, 0)),
        ],
        out_specs=[
            pl.BlockSpec((n_seq, tb, H_A * DV_A), lambda j: (0, j, 0)),
            pl.BlockSpec((n_seq, H_A, DK_A, DV_A), lambda j: (0, 0, 0, 0)),
            pl.BlockSpec((n_seq, SUBLANES, QKV_A), lambda j: (0, 0, 0)),
        ],
        out_shape=[
            jax.ShapeDtypeStruct((n_seq, seq, H_A * DV_A), F32),
            jax.ShapeDtypeStruct((n_seq, H_A, DK_A, DV_A), F32),
            jax.ShapeDtypeStruct((n_seq, SUBLANES, QKV_A), F32),
        ],
        scratch_shapes=[pltpu.VMEM((n_seq, H_A, DK_A, DV_A), F32),
                        pltpu.VMEM((n_seq, tb + SUBLANES, QKV_A), F32)],
        compiler_params=_cparams(1, 48),
        name="prompt_gdn",
    )(qkv, ba, cw, prm)


def _merge_kernel(x_ref, oa_ref, ob_ref, nw_ref, w2_ref, onw_ref, sbw_ref, wb_ref, wo_ref, out_ref, *, ob_scale):
    x = x_ref[...]
    h = _rms(x, nw_ref[...]).astype(BF16)
    z = jnp.dot(h, w2_ref[:, P2_Z:P2_GA], preferred_element_type=F32)
    oa = oa_ref[...]
    ob = ob_ref[...]
    onw = onw_ref[...]
    sbw = sbw_ref[...]
    oa_n, ob_n = [], []
    for hd in range(H_A):
        sl = slice(hd * DV_A, (hd + 1) * DV_A)
        oa_n.append((_rms(oa[:, sl], onw) * _silu(z[:, sl])).astype(BF16))
    for hd in range(H_B):
        sl = slice(hd * DV_B, (hd + 1) * DV_B)
        ob_n.append((_rms(ob[:, sl], sbw) * ob_scale).astype(BF16))
    pa = jnp.dot(jnp.concatenate(oa_n, axis=1), wb_ref[0], preferred_element_type=F32)
    pb = jnp.dot(jnp.concatenate(ob_n, axis=1), wb_ref[1], preferred_element_type=F32)
    ga = jnp.dot(h, w2_ref[:, P2_GA:P2_GB], preferred_element_type=F32)
    gb = jnp.dot(h, w2_ref[:, P2_GB:P2_END], preferred_element_type=F32)
    mixed = (_sigmoid(ga) * pa + _sigmoid(gb) * pb).astype(BF16)
    out_ref[...] = x + jnp.dot(mixed, wo_ref[...], preferred_element_type=F32)


def _merge(x, oa, ob, nw, w2, onw, sbw, wb, wo, tm, ob_scale):
    r = x.shape[0]
    row = lambda w: pl.BlockSpec((tm, w), lambda i: (i, 0))
    full = lambda a: pl.BlockSpec(a.shape, lambda i: (0,) * a.ndim)
    return pl.pallas_call(
        functools.partial(_merge_kernel, ob_scale=ob_scale),
        grid=(r // tm,),
        in_specs=[row(D_MODEL), row(H_A * DV_A), row(H_B * DV_B), full(nw), full(w2), full(onw), full(sbw),
                  full(wb), full(wo)],
        out_specs=row(D_MODEL),
        out_shape=jax.ShapeDtypeStruct((r, D_MODEL), F32),
        compiler_params=_cparams(1, 52),
        name="merge",
    )(x, oa, ob, nw, w2, onw, sbw, wb, wo)


FF_CHUNK = 256


def _ffn_kernel(x_ref, st_ref, nw_ref, wup_ref, cw_ref, cb_ref, wdn_ref, out_ref, stout_ref,
                carry_ref, ext_ref, *, tm, shift, halo):
    i = pl.program_id(1)

    @pl.when(i == 0)
    def _():
        carry_ref[...] = st_ref[0]

    x = x_ref[...]
    h = _rms(x, nw_ref[...]).astype(BF16)
    cw = cw_ref[...]
    cb = cb_ref[...]
    acc = x
    for c in range(D_FF // FF_CHUNK):
        parts = []
        for part in range(2):
            lo = part * D_FF + c * FF_CHUNK
            sl = slice(lo, lo + FF_CHUNK)
            es = slice(part * FF_CHUNK, (part + 1) * FF_CHUNK)
            up = jnp.dot(h, wup_ref[:, sl], preferred_element_type=F32)
            ext_ref[0:halo, es] = carry_ref[:, sl]
            ext_ref[halo:halo + tm, es] = up
            u = cw[2:3, sl] * up + cb[:, sl]
            u = u + cw[1:2, sl] * ext_ref[halo - shift:halo - shift + tm, es]
            u = u + cw[0:1, sl] * ext_ref[halo - 2 * shift:halo - 2 * shift + tm, es]
            carry_ref[:, sl] = ext_ref[tm:tm + halo, es]
            parts.append(u)
        act = (_silu(parts[0]) * parts[1]).astype(BF16)
        acc = acc + jnp.dot(act, wdn_ref[c * FF_CHUNK:(c + 1) * FF_CHUNK, :], preferred_element_type=F32)
    out_ref[...] = acc
    stout_ref[0] = carry_ref[...]


def _ffn(x, st, nw, wup, cw, cb, wdn, groups, tm, shift, halo):
    r = x.shape[0]
    tiles = r // groups // tm
    full = lambda a: pl.BlockSpec(a.shape, lambda g, i: (0,) * a.ndim)
    return pl.pallas_call(
        functools.partial(_ffn_kernel, tm=tm, shift=shift, halo=halo),
        grid=(groups, tiles),
        in_specs=[
            pl.BlockSpec((tm, D_MODEL), lambda g, i: (g * tiles + i, 0)),
            pl.BlockSpec((1, halo, 2 * D_FF), lambda g, i: (g, 0, 0)),
            full(nw), full(wup), full(cw), full(cb), full(wdn),
        ],
        out_specs=[
            pl.BlockSpec((tm, D_MODEL), lambda g, i: (g * tiles + i, 0)),
            pl.BlockSpec((1, halo, 2 * D_FF), lambda g, i: (g, 0, 0)),
        ],
        out_shape=[jax.ShapeDtypeStruct((r, D_MODEL), F32),
                   jax.ShapeDtypeStruct((groups, halo, 2 * D_FF), F32)],
        scratch_shapes=[pltpu.VMEM((halo, 2 * D_FF), F32), pltpu.VMEM((halo + tm, 2 * FF_CHUNK), F32)],
        compiler_params=_cparams(2, 56),
        name="ffn",
    )(x, st, nw, wup, cw, cb, wdn)


def _sgdn_pre_kernel(qkv_ref, st_ref, ba_ref, cw_ref, prm_ref,
                     wq_ref, u_ref, kd_ref, a_ref, gl_ref, cst_ref, *, t_new, nb):
    cw = cw_ref[...]
    xp = [st_ref[i] for i in range(CONV_A - 1)] + [qkv_ref[t] for t in range(t_new)]
    for i in range(CONV_A - 1):
        cst_ref[i] = xp[t_new + i]
    prm = prm_ref[...]
    lane = lax.broadcasted_iota(jnp.int32, (nb, LANES), 1)

    ys, betas, gs = [], [], []
    for t in range(t_new):
        y = cw[0:1] * xp[t]
        for i in range(1, CONV_A):
            y = y + cw[i:i + 1] * xp[t + i]
        ys.append(_silu(y))
        bg = ba_ref[t]
        betas.append(_sigmoid(bg))
        gs.append(-jnp.exp(prm[0:1]) * _softplus(bg + prm[1:2]))
    gcs = [gs[0]]
    for t in range(1, t_new):
        gcs.append(gcs[-1] + gs[t])

    a_out = [jnp.zeros((nb, LANES), F32) for _ in range(t_new)]
    for h in range(H_A):
        q, k, v, beta, gc = [], [], [], [], []
        for t in range(t_new):
            qt = ys[t][:, h * DK_A:(h + 1) * DK_A]
            kt = ys[t][:, H_A * DK_A + h * DK_A:H_A * DK_A + (h + 1) * DK_A]
            q.append(qt * lax.rsqrt(jnp.sum(qt * qt, axis=-1, keepdims=True) + EPS) * (DK_A ** -0.5))
            k.append(kt * lax.rsqrt(jnp.sum(kt * kt, axis=-1, keepdims=True) + EPS))
            v.append(ys[t][:, 2 * H_A * DK_A + h * DV_A:2 * H_A * DK_A + (h + 1) * DV_A])
            beta.append(betas[t][:, h:h + 1])
            gc.append(gcs[t][:, H_A + h:H_A + h + 1])
        m = [[None] * t_new for _ in range(t_new)]
        for i in range(t_new):
            for jj in range(i + 1):
                dec = jnp.exp(gc[i] - gc[jj])
                if jj < i:
                    m[i][jj] = beta[i] * jnp.sum(k[i] * k[jj], axis=-1, keepdims=True) * dec
                aij = jnp.sum(q[i] * k[jj], axis=-1, keepdims=True) * dec
                a_out[i] = jnp.where(lane == h * SUBLANES + jj, aij, a_out[i])
        tm_ = [[None] * t_new for _ in range(t_new)]
        for i in range(t_new):
            for jj in range(i):
                acc = m[i][jj]
                for l in range(jj + 1, i):
                    acc = acc + m[i][l] * tm_[l][jj]
                tm_[i][jj] = -acc
        vb = [v[t] * beta[t] for t in range(t_new)]
        kbg = [k[t] * (beta[t] * jnp.exp(gc[t])) for t in range(t_new)]
        hs = slice(h * DK_A, (h + 1) * DK_A)
        for i in range(t_new):
            u = vb[i]
            w = kbg[i]
            for jj in range(i):
                u = u + tm_[i][jj] * vb[jj]
                w = w + tm_[i][jj] * kbg[jj]
            u_ref[i, :, hs] = u
            wq_ref[i, :, hs] = w
            wq_ref[t_new + i, :, hs] = q[i] * jnp.exp(gc[i])
            kd_ref[i, :, hs] = k[i] * jnp.exp(gc[t_new - 1] - gc[i])
        gl_ref[:, hs] = jnp.broadcast_to(jnp.exp(gc[t_new - 1]), (nb, DK_A))
    for i in range(t_new):
        a_ref[i] = a_out[i]


def _sgdn_pre(qkv_t, st_t, ba_t, cw, prm, t_new, nb):
    wide = H_A * DK_A
    return pl.pallas_call(
        functools.partial(_sgdn_pre_kernel, t_new=t_new, nb=nb),
        out_shape=[
            jax.ShapeDtypeStruct((2 * t_new, nb, wide), F32),
            jax.ShapeDtypeStruct((t_new, nb, wide), F32),
            jax.ShapeDtypeStruct((t_new, nb, wide), F32),
            jax.ShapeDtypeStruct((t_new, nb, LANES), F32),
            jax.ShapeDtypeStruct((nb, wide), F32),
            jax.ShapeDtypeStruct((CONV_A - 1, nb, QKV_A), F32),
        ],
        compiler_params=pltpu.CompilerParams(vmem_limit_bytes=48 * 1024 * 1024),
        name="sample_gdn_pre",
    )(qkv_t, st_t, ba_t, cw, prm)


SGDN_BB = 8


def _sgdn_state_kernel(wq_ref, u_ref, kd_ref, a_ref, gl_ref, s0_ref, o_ref, s1_ref, *, t_new):
    rows = 2 * t_new
    rid = lax.broadcasted_iota(jnp.int32, (rows, DK_A), 0)
    zpad = jnp.zeros((rows - t_new, DK_A), F32)
    for bi in range(SGDN_BB):
        amat = a_ref[:, bi, :]
        glrow = gl_ref[bi:bi + 1, :]
        for h in range(H_A):
            hs = slice(h * DK_A, (h + 1) * DK_A)
            s0 = s0_ref[bi, h]
            r = jnp.dot(wq_ref[:, bi, hs].astype(BF16), s0.astype(BF16), preferred_element_type=F32)
            u8 = jnp.concatenate([u_ref[:, bi, hs], zpad], axis=0)
            v_new = jnp.where(rid < t_new, u8 - r, 0.0)
            o = r[t_new:rows]
            for jj in range(t_new):
                col = amat[:, h * SUBLANES + jj:h * SUBLANES + jj + 1]
                o = o + col * v_new[jj:jj + 1, :]
            kd8 = jnp.concatenate([kd_ref[:, bi, hs], zpad], axis=0)
            s1 = s0 * glrow[:, hs] + lax.dot_general(kd8.astype(BF16), v_new.astype(BF16),
                                                     (((0,), (0,)), ((), ())), preferred_element_type=F32)
            s1_ref[bi, h] = s1
            o_ref[:, bi, hs] = o


def _sgdn_state(wq, u, kd, a, gl, s0, t_new, nb):
    wide = H_A * DK_A
    bb = SGDN_BB
    return pl.pallas_call(
        functools.partial(_sgdn_state_kernel, t_new=t_new),
        grid=(nb // bb,),
        in_specs=[
            pl.BlockSpec((2 * t_new, bb, wide), lambda i: (0, i, 0)),
            pl.BlockSpec((t_new, bb, wide), lambda i: (0, i, 0)),
            pl.BlockSpec((t_new, bb, wide), lambda i: (0, i, 0)),
            pl.BlockSpec((t_new, bb, LANES), lambda i: (0, i, 0)),
            pl.BlockSpec((bb, wide), lambda i: (i, 0)),
            pl.BlockSpec((bb, H_A, DK_A, DV_A), lambda i: (i, 0, 0, 0)),
        ],
        out_specs=[
            pl.BlockSpec((t_new, bb, wide), lambda i: (0, i, 0)),
            pl.BlockSpec((bb, H_A, DK_A, DV_A), lambda i: (i, 0, 0, 0)),
        ],
        out_shape=[jax.ShapeDtypeStruct((t_new, nb, wide), F32),
                   jax.ShapeDtypeStruct((nb, H_A, DK_A, DV_A), F32)],
        compiler_params=_cparams(1, 32),
        name="sample_gdn_state",
    )(wq, u, kd, a, gl, s0)


NEW_ROWS = 16


def _sattn_kernel(pt_ref, lam_ref, q_ref, kn_ref, vn_ref, bias_ref, *rest, n_pages, t_new, lam_init):
    del pt_ref
    k_pages = rest[:n_pages]
    v_pages = rest[n_pages:2 * n_pages]
    o_ref = rest[2 * n_pages]
    rows = 2 * t_new
    q = q_ref[0]
    bias = bias_ref[...]
    lam = _lam_value(lam_ref[...], lam_init)
    r = lax.broadcasted_iota(jnp.int32, (rows, 2 * DH_B), 0)
    c = lax.broadcasted_iota(jnp.int32, (rows, 2 * DH_B), 1)
    map_mask = (c // DH_B) == (r // t_new)
    pad = jnp.zeros((PAGE - NEW_ROWS, LANES), BF16)
    nt = (((1,), (1,)), ((), ()))

    heads = range(H_B)
    lanes = [slice(h * LANES, (h + 1) * LANES) for h in heads]
    head_rows = [pl.ds(h, PAGE, stride=H_B) for h in heads]
    qx = [jnp.where(map_mask, jnp.concatenate([q[:, lanes[h]]] * 2, axis=0), 0.0).astype(BF16) for h in heads]
    bh = [bias[h * rows:(h + 1) * rows] for h in heads]

    s_parts = [[] for _ in heads]
    for p in range(n_pages):
        for h in heads:
            kh = k_pages[p][0, 0, head_rows[h], :].astype(BF16)
            s = lax.dot_general(qx[h], kh, nt, preferred_element_type=F32)
            if p == n_pages - 1:
                s = s + bh[h][:, 0:PAGE]
            s_parts[h].append(s)
    for h in heads:
        kn = jnp.concatenate([kn_ref[0, :, lanes[h]], pad], axis=0)
        s_parts[h].append(lax.dot_general(qx[h], kn, nt, preferred_element_type=F32) + bh[h][:, PAGE:2 * PAGE])

    m = []
    for h in heads:
        mh = s_parts[h][0].max(axis=-1, keepdims=True)
        for s in s_parts[h][1:]:
            mh = jnp.maximum(mh, s.max(axis=-1, keepdims=True))
        m.append(mh)
    l = [jnp.zeros((rows, 1), F32) for _ in heads]
    acc = [jnp.zeros((rows, DV_B), F32) for _ in heads]
    for p in range(n_pages + 1):
        for h in heads:
            pr = jnp.exp(s_parts[h][p] - m[h])
            l[h] = l[h] + jnp.sum(pr, axis=-1, keepdims=True)
            if p < n_pages:
                vv = v_pages[p][0, 0, head_rows[h], :].astype(BF16)
            else:
                vv = jnp.concatenate([vn_ref[0, :, lanes[h]], pad], axis=0)
            acc[h] = acc[h] + jnp.dot(pr.astype(BF16), vv, preferred_element_type=F32)
    for h in heads:
        a = acc[h] / l[h]
        o_ref[0, :, lanes[h]] = a[0:t_new] - lam * a[t_new:rows]


def _sample_attn(page_table, lamv, q_b, kn_b, vn_b, bias, ck, cv, li, t_new, lam_init):
    nb, n_pages = page_table.shape
    page_spec = lambda p: pl.BlockSpec((1, 1, PAGE * H_B, LANES), lambda b, pt, p=p: (li, pt[b, p], 0, 0))
    tok_spec = pl.BlockSpec((1, t_new, HB_W), lambda b, pt: (b, 0, 0))
    new_spec = pl.BlockSpec((1, NEW_ROWS, HB_W), lambda b, pt: (b, 0, 0))
    grid_spec = pltpu.PrefetchScalarGridSpec(
        num_scalar_prefetch=1,
        grid=(nb,),
        in_specs=[pl.BlockSpec(lamv.shape, lambda b, pt: (0, 0)), tok_spec, new_spec, new_spec,
                  pl.BlockSpec(bias.shape, lambda b, pt: (0, 0))]
                 + [page_spec(p) for p in range(n_pages)] * 2,
        out_specs=pl.BlockSpec((1, t_new, HB_W), lambda b, pt: (b, 0, 0)),
    )
    return pl.pallas_call(
        functools.partial(_sattn_kernel, n_pages=n_pages, t_new=t_new, lam_init=lam_init),
        grid_spec=grid_spec,
        out_shape=jax.ShapeDtypeStruct((nb, t_new, HB_W), F32),
        compiler_params=_cparams(1, 40),
        name="sample_attn",
    )(page_table, lamv, q_b, kn_b, vn_b, bias, *([ck] * n_pages), *([cv] * n_pages))


def _pick(n, pref):
    return pref if n % pref == 0 else n


def _to_bmajor(a_t, t_new, nb):
    return jnp.swapaxes(a_t.reshape(t_new, nb, -1), 0, 1)


def _layer_weights(li, norm_mix, w_in, conv_a, a_log, dt_bias, onorm_a, qnorm_b, knorm_b,
                   lam_q1, lam_k1, lam_q2, lam_k2, subln_b, w_branch, w_o, norm_ffn, w_up, conv_f,
                   conv_f_b, w_down):
    w = w_in[li]
    o_z = QKV_A
    o_b = o_z + H_A * DV_A
    o_qb = o_b + 2 * H_A
    o_kb = o_qb + HB_W
    o_vb = o_kb + HB_W
    o_ga = o_vb + H_B * DV_B
    o_gb = o_ga + D_MODEL
    ba_cols = jnp.pad(w[:, o_b:o_qb], ((0, 0), (0, LANES - 2 * H_A)))
    w1 = jnp.concatenate([w[:, 0:o_z], w[:, o_qb:o_ga], ba_cols], axis=1).astype(BF16)
    w2 = jnp.concatenate([w[:, o_z:o_b], w[:, o_ga:]], axis=1).astype(BF16)
    prm = jnp.zeros((2, LANES), F32)
    prm = prm.at[0, H_A:2 * H_A].set(a_log[li]).at[1, H_A:2 * H_A].set(dt_bias[li])
    return dict(
        nw=norm_mix[li].reshape(1, D_MODEL), w1=w1, w2=w2,
        qw=jnp.tile(qnorm_b[li], 2 * H_B).reshape(1, HB_W),
        kw=jnp.tile(knorm_b[li], 2 * H_B).reshape(1, HB_W),
        cw_a=conv_a[li], prm=prm,
        onw=onorm_a[li].reshape(1, DV_A), sbw=subln_b[li].reshape(1, DV_B),
        lamv=jnp.stack([lam_q1[li], lam_k1[li], lam_q2[li], lam_k2[li]]),
        wb=w_branch[li].astype(BF16), wo=w_o[li].astype(BF16),
        nwf=norm_ffn[li].reshape(1, D_MODEL), wup=w_up[li].astype(BF16),
        cw_f=conv_f[li], cb_f=conv_f_b[li].reshape(1, 2 * D_FF), wdn=w_down[li].astype(BF16),
    )


def _group_ones():
    i = jnp.arange(MXU_DIM)
    return ((i[:, None] // DH_B) == (i[None, :] // DH_B)).astype(BF16)


def _prompt_layer(x, wt, bias_p, g, n_seq, seq, lam_init):
    tm = _pick(seq, 512)
    blk = bias_p.shape[-1]
    qkv, ba, qn, kn, kf, vf, vb = _inproj(x, wt["nw"], wt["w1"], wt["qw"], wt["kw"], g, tm)
    o_a, s_fin, cst_a = _prompt_gdn(qkv.reshape(n_seq, seq, QKV_A), ba.reshape(n_seq, seq, LANES),
                                    wt["cw_a"], wt["prm"], n_seq, seq, _pick(seq, 256))
    o_a = o_a.reshape(n_seq * seq, H_A * DV_A)
    o_b = _prompt_attn(wt["lamv"], qn, kn, vb, bias_p, n_seq, seq, blk, lam_init)
    x = _merge(x, o_a, o_b, wt["nw"], wt["w2"], wt["onw"], wt["sbw"], wt["wb"], wt["wo"], tm, 1.0 - lam_init)
    zero_st = jnp.zeros((n_seq, SUBLANES, 2 * D_FF), F32)
    x, cst_f = _ffn(x, zero_st, wt["nwf"], wt["wup"], wt["cw_f"], wt["cb_f"], wt["wdn"],
                    n_seq, tm, 1, SUBLANES)
    return (x, kf.reshape(n_seq, seq, H_B, 2 * DH_B), vf.reshape(n_seq, seq, H_B, DV_B), s_fin,
            cst_a[:, SUBLANES - (CONV_A - 1):], cst_f[:, SUBLANES - (CONV_F - 1):])


def _sample_layer(x_t, wt, bias_s, g, page_table, ck, cv, li, s0, cst_a, cst_f, nb, t_new, lam_init):
    r = nb * t_new
    qkv, ba, qn, kn, kf, vf, vb = _inproj(x_t, wt["nw"], wt["w1"], wt["qw"], wt["kw"], g, r)
    st_a_t = jnp.swapaxes(cst_a, 0, 1)
    wq, u, kd, a, gl, cst_a_t = _sgdn_pre(qkv.reshape(t_new, nb, QKV_A), st_a_t,
                                          ba.reshape(t_new, nb, LANES), wt["cw_a"], wt["prm"], t_new, nb)
    o_a_t, s1 = _sgdn_state(wq, u, kd, a, gl, s0, t_new, nb)
    pad_new = lambda a: jnp.pad(_to_bmajor(a, t_new, nb), ((0, 0), (0, NEW_ROWS - t_new), (0, 0)))
    o_b = _sample_attn(page_table, wt["lamv"], _to_bmajor(qn, t_new, nb).astype(F32), pad_new(kn),
                       pad_new(vb), bias_s, ck, cv, li, t_new, lam_init)
    o_b_t = jnp.swapaxes(o_b, 0, 1).reshape(r, HB_W)
    x_t = _merge(x_t, o_a_t.reshape(r, H_A * DV_A), o_b_t, wt["nw"], wt["w2"], wt["onw"], wt["sbw"],
                 wt["wb"], wt["wo"], r, 1.0 - lam_init)
    halo = (CONV_F - 1) * nb
    st_f_t = jnp.swapaxes(cst_f, 0, 1).reshape(1, halo, 2 * D_FF)
    x_t, cst_f_t = _ffn(x_t, st_f_t, wt["nwf"], wt["wup"], wt["cw_f"], wt["cb_f"], wt["wdn"],
                        1, r, nb, halo)
    return (x_t, _to_bmajor(kf, t_new, nb).reshape(nb, t_new, H_B, 2 * DH_B),
            _to_bmajor(vf, t_new, nb).reshape(nb, t_new, H_B, DV_B), s1,
            jnp.swapaxes(cst_a_t, 0, 1), jnp.swapaxes(cst_f_t.reshape(CONV_F - 1, nb, 2 * D_FF), 0, 1))


def kernel(x_prompt, x_sample, cache_k, cache_v, state_delta, state_conv_a, state_conv_ffn, page_table,
           rel_bias, norm_mix, w_in, conv_a, a_log, dt_bias, onorm_a, qnorm_b, knorm_b, lam_q1, lam_k1,
           lam_q2, lam_k2, subln_b, w_branch, w_o, norm_ffn, w_up, conv_f, conv_f_b, w_down):
    n_seq, seq, _ = x_prompt.shape
    nb, t_new, _ = x_sample.shape
    depth = w_in.shape[0]
    n_pool = cache_k.shape[1]

    rb_flat = rel_bias.T.reshape(-1)
    bias_p = _prompt_bias(rb_flat, _pick(seq, 512))
    bias_s = _sample_bias(rb_flat, t_new)
    g = _group_ones()
    ck = cache_k.reshape(depth, n_pool, PAGE * H_B, 2 * DH_B)
    cv = cache_v.reshape(depth, n_pool, PAGE * H_B, DV_B)

    xp = x_prompt.reshape(n_seq * seq, D_MODEL)
    xs = jnp.swapaxes(x_sample, 0, 1).reshape(t_new * nb, D_MODEL)
    outs_p, outs_s = [], []
    for li in range(depth):
        wt = _layer_weights(li, norm_mix, w_in, conv_a, a_log, dt_bias, onorm_a, qnorm_b, knorm_b,
                            lam_q1, lam_k1, lam_q2, lam_k2, subln_b, w_branch, w_o, norm_ffn, w_up,
                            conv_f, conv_f_b, w_down)
        lam_init = 0.8 - 0.6 * math.exp(-0.3 * li)
        xp, *rest_p = _prompt_layer(xp, wt, bias_p, g, n_seq, seq, lam_init)
        xs, *rest_s = _sample_layer(xs, wt, bias_s, g, page_table, ck, cv, li, state_delta[li],
                                    state_conv_a[li], state_conv_ffn[li], nb, t_new, lam_init)
        outs_p.append(rest_p)
        outs_s.append(rest_s)

    stack = lambda outs, i: jnp.stack([o[i] for o in outs])
    y_prompt = xp.reshape(n_seq, seq, D_MODEL)
    y_sample = jnp.swapaxes(xs.reshape(t_new, nb, D_MODEL), 0, 1)
    return (y_prompt, y_sample,
            stack(outs_p, 0), stack(outs_p, 1), stack(outs_p, 2), stack(outs_p, 3), stack(outs_p, 4),
            stack(outs_s, 0), stack(outs_s, 1), stack(outs_s, 2), stack(outs_s, 3), stack(outs_s, 4))
```

```python
import functools
import math

import jax
import jax.numpy as jnp
from jax import lax
from jax.experimental import pallas as pl
from jax.experimental.pallas import tpu as pltpu

F32 = jnp.float32
BF16 = jnp.bfloat16

D_MODEL = 1024
H_A, DK_A, DV_A, CONV_A, CHUNK = 4, 128, 128, 4, 64
QKV_A = 2 * H_A * DK_A + H_A * DV_A
H_B, DH_B, DV_B = 4, 64, 128
HB_W = H_B * 2 * DH_B
PAGE = 128
D_FF, CONV_F = 2816, 3
NUM_BUCKETS, MAX_DISTANCE = 32, 128
NEG_INF = -1e30
EPS = 1e-6
ATT_SCALE = DH_B ** -0.5
LOG2E = math.log2(math.e)
Q_SCALE = ATT_SCALE * LOG2E
SAFE_SPREAD_BITS = 100.0

V7X_VMEM_BYTES = 64 * 1024 * 1024
LANES = 128
SUBLANES = 8
MXU_DIM = 256

P1_QKV, P1_Q, P1_K, P1_V, P1_BA, P1_END = 0, 1536, 2048, 2560, 3072, 3200
P2_Z, P2_GA, P2_GB, P2_END = 0, 512, 1536, 2560


def _cparams(n_axes, vmem_mib):
    return pltpu.CompilerParams(
        dimension_semantics=("arbitrary",) * n_axes,
        vmem_limit_bytes=min(vmem_mib * 1024 * 1024, V7X_VMEM_BYTES - 8 * 1024 * 1024),
    )


def _bdot(a, b):
    return jnp.dot(a.astype(BF16), b.astype(BF16), preferred_element_type=F32)


def _rms(x, w):
    return x * lax.rsqrt(jnp.mean(x * x, axis=-1, keepdims=True) + EPS) * w


def _sigmoid(x):
    return 1.0 / (1.0 + jnp.exp(-x))


def _silu(x):
    return x * _sigmoid(x)


def _softplus(x):
    return jnp.maximum(x, 0.0) + jnp.log1p(jnp.exp(-jnp.abs(x)))


def _inproj_kernel(x_ref, nw_ref, w_ref, qw_ref, kw_ref, g_ref,
                   qkv_ref, ba_ref, qn_ref, kn_ref, kf_ref, vf_ref, vb_ref):
    h = _rms(x_ref[...], nw_ref[...]).astype(BF16)

    def proj(lo, hi):
        return jnp.dot(h, w_ref[:, lo:hi], preferred_element_type=F32)

    tm = x_ref.shape[0]
    head_rows = [pl.ds(hd, tm, stride=H_B) for hd in range(H_B)]

    qkv_ref[...] = proj(P1_QKV, P1_Q)
    ba_ref[...] = proj(P1_BA, P1_END)
    v = proj(P1_V, P1_BA)
    for hd in range(H_B):
        vf_ref[head_rows[hd], :] = v[:, hd * DV_B:(hd + 1) * DV_B]
    vb_ref[...] = v.astype(BF16)

    g = g_ref[...]

    def qk_norm(y, w):
        sq = y * y
        hi = sq.astype(BF16)
        lo = (sq - hi.astype(F32)).astype(BF16)
        outs = []
        for c in range(HB_W // MXU_DIM):
            sl = slice(c * MXU_DIM, (c + 1) * MXU_DIM)
            ss = (jnp.dot(hi[:, sl], g, preferred_element_type=F32)
                  + jnp.dot(lo[:, sl], g, preferred_element_type=F32))
            outs.append(y[:, sl] * lax.rsqrt(ss * (1.0 / DH_B) + EPS) * w[:, sl])
        return outs

    qn = qk_norm(proj(P1_Q, P1_K), qw_ref[...])
    kn = qk_norm(proj(P1_K, P1_V), kw_ref[...])
    for c in range(HB_W // MXU_DIM):
        sl = slice(c * MXU_DIM, (c + 1) * MXU_DIM)
        qn_ref[:, sl] = (qn[c] * Q_SCALE).astype(BF16)
        kn_ref[:, sl] = kn[c].astype(BF16)
        for i in range(MXU_DIM // LANES):
            kf_ref[head_rows[c * (MXU_DIM // LANES) + i], :] = kn[c][:, i * LANES:(i + 1) * LANES]


def _inproj(x, nw, w1, qw, kw, g, tm):
    r = x.shape[0]
    row = lambda w: pl.BlockSpec((tm, w), lambda i: (i, 0))
    full = lambda a: pl.BlockSpec(a.shape, lambda i: (0,) * a.ndim)
    outs = [(1, QKV_A, F32), (1, LANES, F32), (1, HB_W, BF16), (1, HB_W, BF16), (H_B, LANES, F32),
            (H_B, LANES, F32), (1, HB_W, BF16)]
    return pl.pallas_call(
        _inproj_kernel,
        grid=(r // tm,),
        in_specs=[row(D_MODEL), full(nw), full(w1), full(qw), full(kw), full(g)],
        out_specs=[pl.BlockSpec((tm * m, w), lambda i: (i, 0)) for m, w, _ in outs],
        out_shape=[jax.ShapeDtypeStruct((r * m, w), dt) for m, w, dt in outs],
        compiler_params=_cparams(1, 52),
        name="inproj",
    )(x, nw, w1, qw, kw, g)


def _bucket_bias(d, table):
    n = jnp.maximum(d, 0)
    max_exact = NUM_BUCKETS // 2
    nf = jnp.maximum(n, 1).astype(F32)
    large = max_exact + (jnp.log(nf / max_exact) / math.log(MAX_DISTANCE / max_exact)
                         * (NUM_BUCKETS - max_exact)).astype(jnp.int32)
    large = jnp.minimum(large, NUM_BUCKETS - 1)
    bucket = jnp.where(n < max_exact, n, large)
    val = jnp.zeros(d.shape, F32)
    for b in range(NUM_BUCKETS):
        val = jnp.where(bucket == b, table(b), val)
    return val


def _prompt_bias_kernel(rb_ref, o_ref, *, blk):
    h = pl.program_id(0)
    table = lambda b: rb_ref[h * NUM_BUCKETS + b]
    far = table(NUM_BUCKETS - 1)
    i = lax.broadcasted_iota(jnp.int32, (blk, blk), 0)
    j = lax.broadcasted_iota(jnp.int32, (blk, blk), 1)
    d0 = i - j
    o_ref[0, 0] = jnp.where(d0 >= 0, (_bucket_bias(d0, table) - far) * LOG2E, NEG_INF)
    o_ref[0, 1] = (_bucket_bias(d0 + blk, table) - far) * LOG2E


def _prompt_bias(rb_flat, blk):
    return pl.pallas_call(
        functools.partial(_prompt_bias_kernel, blk=blk),
        grid=(H_B,),
        in_specs=[pl.BlockSpec(memory_space=pltpu.SMEM)],
        out_specs=pl.BlockSpec((1, 2, blk, blk), lambda h: (h, 0, 0, 0)),
        out_shape=jax.ShapeDtypeStruct((H_B, 2, blk, blk), F32),
        compiler_params=_cparams(1, 32),
        name="prompt_bias",
    )(rb_flat)


def _sample_bias_kernel(rb_ref, o_ref, *, t_new):
    rows, cols = o_ref.shape
    r = lax.broadcasted_iota(jnp.int32, (rows, cols), 0)
    c = lax.broadcasted_iota(jnp.int32, (rows, cols), 1)
    t = r % t_new
    hd = r // (2 * t_new)
    is_new = c >= PAGE
    d = jnp.where(is_new, t - (c - PAGE), t + PAGE - c)
    val = jnp.zeros((rows, cols), F32)
    for h in range(H_B):
        table = lambda b, h=h: rb_ref[h * NUM_BUCKETS + b]
        vh = _bucket_bias(d, table) - table(NUM_BUCKETS - 1)
        val = jnp.where(hd == h, vh, val)
    visible = jnp.logical_and(d >= 0, jnp.logical_or(~is_new, (c - PAGE) < t_new))
    o_ref[...] = jnp.where(visible, val * LOG2E, NEG_INF)


def _sample_bias(rb_flat, t_new):
    rows = H_B * 2 * t_new
    return pl.pallas_call(
        functools.partial(_sample_bias_kernel, t_new=t_new),
        in_specs=[pl.BlockSpec(memory_space=pltpu.SMEM)],
        out_shape=jax.ShapeDtypeStruct((rows, 2 * PAGE), F32),
        name="sample_bias",
    )(rb_flat)


def _lam_value(lv, lam_init):
    s1 = jnp.sum(lv[0:1] * lv[1:2], axis=-1, keepdims=True)
    s2 = jnp.sum(lv[2:3] * lv[3:4], axis=-1, keepdims=True)
    return jnp.exp(s1) - jnp.exp(s2) + lam_init


FAR_BLOCKS = 2


def _causal_sweep(qi, blk, bias_ref, update):
    n_far = jnp.maximum(qi - 1, 0)
    span = FAR_BLOCKS * blk

    def far(kk, carry):
        update(pl.multiple_of(kk * span, span), FAR_BLOCKS, None)
        return carry

    lax.fori_loop(0, n_far // FAR_BLOCKS, far, 0)
    for r in range(1, FAR_BLOCKS):
        @pl.when(n_far % FAR_BLOCKS >= r)
        def _():
            update(pl.multiple_of((n_far - r) * blk, blk), 1, None)

    @pl.when(qi >= 1)
    def _():
        update(pl.multiple_of((qi - 1) * blk, blk), 1, bias_ref[0, 1])

    update(pl.multiple_of(qi * blk, blk), 1, bias_ref[0, 0])


def _attn_kernel(scal_ref, lam_ref, q_ref, k_ref, v_ref, bias_ref, o_ref, m_ref, l_ref, acc_ref, accx_ref,
                 *, blk, lam_init):
    hd = pl.program_id(1)
    qi = pl.program_id(2)
    q = q_ref[...]
    lane = lax.broadcasted_iota(jnp.int32, (1, 2 * DH_B), 1)
    zero = jnp.zeros_like(q)
    qs = (jnp.where(lane < DH_B, q, zero), jnp.where(lane >= DH_B, q, zero))
    lam = _lam_value(lam_ref[...], lam_init)
    nt = (((1,), (1,)), ((), ()))

    @pl.when(scal_ref[0] > 0.5)
    def _():
        one_col = jnp.where(lane == 0, 1.0, 0.0).astype(BF16)
        ext_k = {n: jnp.broadcast_to(one_col, (n * blk, 2 * DH_B)) for n in (1, FAR_BLOCKS)}
        qe = []
        for mp in range(2):
            qf = qs[mp].astype(F32)
            shift = jnp.sqrt(jnp.sum(qf * qf, axis=-1, keepdims=True)) * scal_ref[1] + scal_ref[2 + hd]
            qe.append(jnp.concatenate([qs[mp], jnp.where(lane == 0, -shift, 0.0).astype(BF16)], axis=1))
        accx_ref[...] = jnp.zeros(accx_ref.shape, F32)

        def update(off, nblk, bias):
            rows = nblk * blk
            kb = jnp.concatenate([k_ref[pl.ds(off, rows), :], ext_k[nblk]], axis=1)
            vb = jnp.concatenate([v_ref[pl.ds(off, rows), :], ext_k[nblk]], axis=1)
            for mp in range(2):
                s = lax.dot_general(qe[mp], kb, nt, preferred_element_type=F32)
                if bias is not None:
                    s = s + bias
                accx_ref[mp] += jnp.dot(jnp.exp2(s).astype(BF16), vb, preferred_element_type=F32)

        _causal_sweep(qi, blk, bias_ref, update)
        a1 = accx_ref[0]
        a2 = accx_ref[1]
        o_ref[...] = (a1[:, 0:DV_B] / a1[:, DV_B:DV_B + 1]
                      - lam * (a2[:, 0:DV_B] / a2[:, DV_B:DV_B + 1]))

    @pl.when(scal_ref[0] <= 0.5)
    def _():
        m_ref[...] = jnp.full(m_ref.shape, NEG_INF, F32)
        l_ref[...] = jnp.zeros(l_ref.shape, F32)
        acc_ref[...] = jnp.zeros(acc_ref.shape, F32)

        def update(off, nblk, bias):
            for i in range(nblk):
                update_one(pl.multiple_of(off + i * blk, blk), bias)

        def update_one(off, bias):
            kb = k_ref[pl.ds(off, blk), :]
            vb = v_ref[pl.ds(off, blk), :]
            for mp in range(2):
                s = lax.dot_general(qs[mp], kb, nt, preferred_element_type=F32)
                if bias is not None:
                    s = s + bias
                m_old = m_ref[mp]
                m_new = jnp.maximum(m_old, jnp.max(s, axis=-1, keepdims=True))
                alpha = jnp.exp2(m_old - m_new)
                p = jnp.exp2(s - jnp.concatenate([m_new] * (blk // LANES), axis=1))
                l_ref[mp] = alpha * l_ref[mp] + jnp.sum(p, axis=-1, keepdims=True)
                acc_ref[mp] = alpha * acc_ref[mp] + jnp.dot(p.astype(BF16), vb, preferred_element_type=F32)
                m_ref[mp] = m_new

        _causal_sweep(qi, blk, bias_ref, update)
        o_ref[...] = acc_ref[0] / l_ref[0] - lam * (acc_ref[1] / l_ref[1])


def _prompt_attn(scal, lamv, qn, kn, vb, bias, n_seq, seq, blk, lam_init):
    nq = seq // blk
    r = n_seq * seq
    return pl.pallas_call(
        functools.partial(_attn_kernel, blk=blk, lam_init=lam_init),
        grid=(n_seq, H_B, nq),
        in_specs=[
            pl.BlockSpec(memory_space=pltpu.SMEM),
            pl.BlockSpec(lamv.shape, lambda b, h, i: (0, 0)),
            pl.BlockSpec((blk, LANES), lambda b, h, i: (b * nq + i, h)),
            pl.BlockSpec((seq, LANES), lambda b, h, i: (b, h)),
            pl.BlockSpec((seq, LANES), lambda b, h, i: (b, h)),
            pl.BlockSpec((1, 2, blk, blk), lambda b, h, i: (h, 0, 0, 0)),
        ],
        out_specs=pl.BlockSpec((blk, LANES), lambda b, h, i: (b * nq + i, h)),
        out_shape=jax.ShapeDtypeStruct((r, HB_W), F32),
        scratch_shapes=[pltpu.VMEM((2, blk, LANES), F32), pltpu.VMEM((2, blk, LANES), F32),
                        pltpu.VMEM((2, blk, DV_B), F32), pltpu.VMEM((2, blk, 2 * DV_B), F32)],
        compiler_params=_cparams(3, 48),
        name="prompt_attn",
    )(scal, lamv, qn, kn, vb, bias)


def _tri_masks():
    i = lax.broadcasted_iota(jnp.int32, (CHUNK, CHUNK), 0)
    j = lax.broadcasted_iota(jnp.int32, (CHUNK, CHUNK), 1)
    incl = i >= j
    strict = i > j
    eye = (i == j).astype(F32)
    base = jnp.logical_and(strict, (i // SUBLANES) == (j // SUBLANES))
    levels = []
    s = SUBLANES
    while s < CHUNK:
        levels.append(jnp.logical_and((i // (2 * s)) == (j // (2 * s)), (i // s) > (j // s)))
        s *= 2
    return incl, strict, eye, base, levels


def _unit_lower_inverse(ms, eye, base, levels):
    d = [jnp.where(base, m, 0.0) for m in ms]
    d2 = [_bdot(a, a) for a in d]
    d4 = [_bdot(a, a) for a in d2]
    x = [_bdot(eye - a, eye + b) for a, b in zip(d, d2)]
    x = [_bdot(a, eye + b) for a, b in zip(x, d4)]
    for lvl in levels:
        c = [jnp.where(lvl, m, 0.0) for m in ms]
        xc = [_bdot(a, b) for a, b in zip(x, c)]
        xcx = [_bdot(a, b) for a, b in zip(xc, x)]
        x = [a - b for a, b in zip(x, xcx)]
    return x


def _gdn_kernel(qkv_ref, ba_ref, cw_ref, prm_ref, o_ref, sout_ref, cst_ref, s_ref, ext_ref, *, tb, n_seq):
    j = pl.program_id(0)
    nchunk = tb // CHUNK
    halo = SUBLANES

    @pl.when(j == 0)
    def _():
        s_ref[...] = jnp.zeros(s_ref.shape, F32)
        ext_ref[:, 0:halo, :] = jnp.zeros((n_seq, halo, QKV_A), F32)

    cw = cw_ref[...]
    prm = prm_ref[...]
    incl, strict, eye, base, levels = _tri_masks()
    tril = incl.astype(F32)
    nt = (((1,), (1,)), ((), ()))

    pairs = [(b, c) for b in range(n_seq) for c in range(nchunk)]
    ys, betas, gcs = {}, {}, {}
    for b in range(n_seq):
        x = qkv_ref[b]
        ext_ref[b, halo:halo + tb, :] = x
        y = cw[3:4] * x
        for i in range(CONV_A - 1):
            sh = CONV_A - 1 - i
            y = y + cw[i:i + 1] * ext_ref[b, halo - sh:halo - sh + tb, :]
        tail = x[tb - halo:tb, :]
        ext_ref[b, 0:halo, :] = tail
        cst_ref[b] = tail
        ys[b] = _silu(y)
        bg = ba_ref[b]
        betas[b] = _sigmoid(bg)
        g_all = -jnp.exp(prm[0:1]) * _softplus(bg + prm[1:2])
        for c in range(nchunk):
            gcs[b, c] = jnp.dot(tril, g_all[c * CHUNK:(c + 1) * CHUNK],
                                precision=lax.Precision.HIGHEST, preferred_element_type=F32)

    grp = [(b, c, h) for (b, c) in pairs for h in range(H_A)]
    q_, k_, kb_, vb_, dec_, eg_, ekd_, gl_ = [], [], [], [], [], [], [], []
    gct = {bc: gcs[bc].T for bc in pairs}
    for (b, c, h) in grp:
        rs = slice(c * CHUNK, (c + 1) * CHUNK)
        y = ys[b]
        q = y[rs, h * DK_A:(h + 1) * DK_A]
        k = y[rs, H_A * DK_A + h * DK_A:H_A * DK_A + (h + 1) * DK_A]
        v = y[rs, 2 * H_A * DK_A + h * DV_A:2 * H_A * DK_A + (h + 1) * DV_A]
        q = q * lax.rsqrt(jnp.sum(q * q, axis=-1, keepdims=True) + EPS) * (DK_A ** -0.5)
        k = k * lax.rsqrt(jnp.sum(k * k, axis=-1, keepdims=True) + EPS)
        beta = betas[b][rs, h:h + 1]
        gc = gcs[b, c]
        gcol = gc[:, H_A + h:H_A + h + 1]
        grow = gct[b, c][H_A + h:H_A + h + 1, :]
        glast = gc[CHUNK - 1:CHUNK, H_A + h:H_A + h + 1]
        dec_.append(jnp.where(incl, jnp.exp(jnp.where(incl, gcol - grow, 0.0)), 0.0))
        eg_.append(jnp.exp(gcol))
        ekd_.append(jnp.exp(glast - gcol))
        gl_.append(jnp.exp(glast))
        q_.append(q)
        k_.append(k)
        kb_.append(k * beta)
        vb_.append(v * beta)

    n = len(grp)
    kk_qk = [lax.dot_general(jnp.concatenate([kb_[g], q_[g]], axis=0).astype(BF16), k_[g].astype(BF16),
                             nt, preferred_element_type=F32) for g in range(n)]
    ms = [jnp.where(strict, kk_qk[g][0:CHUNK] * dec_[g], 0.0) for g in range(n)]
    a_ = [(kk_qk[g][CHUNK:2 * CHUNK] * dec_[g]).astype(BF16) for g in range(n)]
    t_ = _unit_lower_inverse(ms, eye, base, levels)
    uw = [_bdot(t_[g], jnp.concatenate([vb_[g], kb_[g] * eg_[g]], axis=1)) for g in range(n)]
    wq_ = [jnp.concatenate([uw[g][:, DV_A:DV_A + DK_A], q_[g] * eg_[g]], axis=0).astype(BF16)
           for g in range(n)]
    kd_ = [(k_[g] * ekd_[g]).astype(BF16) for g in range(n)]

    lanes = [(b, h) for b in range(n_seq) for h in range(H_A)]
    s = {bh: s_ref[bh[0], bh[1]] for bh in lanes}
    for c in range(nchunk):
        gi = {(b, h): grp.index((b, c, h)) for (b, h) in lanes}
        ws = {bh: jnp.dot(wq_[gi[bh]], s[bh].astype(BF16), preferred_element_type=F32) for bh in lanes}
        vn = {bh: (uw[gi[bh]][:, 0:DV_A] - ws[bh][0:CHUNK]).astype(BF16) for bh in lanes}
        av = {bh: jnp.dot(a_[gi[bh]], vn[bh], preferred_element_type=F32) for bh in lanes}
        kv = {bh: lax.dot_general(kd_[gi[bh]], vn[bh], (((0,), (0,)), ((), ())), preferred_element_type=F32)
              for bh in lanes}
        for (b, h) in lanes:
            o_ref[b, c * CHUNK:(c + 1) * CHUNK, h * DV_A:(h + 1) * DV_A] = ws[b, h][CHUNK:2 * CHUNK] + av[b, h]
            s[b, h] = s[b, h] * gl_[gi[b, h]] + kv[b, h]
    for (b, h) in lanes:
        s_ref[b, h] = s[b, h]
        sout_ref[b, h] = s[b, h]


def _prompt_gdn(qkv, ba, cw, prm, n_seq, seq, tb):
    nb = seq // tb
    return pl.pallas_call(
        functools.partial(_gdn_kernel, tb=tb, n_seq=n_seq),
        grid=(nb,),
        in_specs=[
            pl.BlockSpec((n_seq, tb, QKV_A), lambda j: (0, j, 0)),
            pl.BlockSpec((n_seq, tb, LANES), lambda j: (0, j, 0)),
            pl.BlockSpec(cw.shape, lambda j: (0, 0)),
            pl.BlockSpec(prm.shape, lambda j: (0, 0)),
        ],
        out_specs=[
            pl.BlockSpec((n_seq, tb, H_A * DV_A), lambda j: (0, j, 0)),
            pl.BlockSpec((n_seq, H_A, DK_A, DV_A), lambda j: (0, 0, 0, 0)),
            pl.BlockSpec((n_seq, SUBLANES, QKV_A), lambda j: (0, 0, 0)),
        ],
        out_shape=[
            jax.ShapeDtypeStruct((n_seq, seq, H_A * DV_A), F32),
            jax.ShapeDtypeStruct((n_seq, H_A, DK_A, DV_A), F32),
            jax.ShapeDtypeStruct((n_seq, SUBLANES, QKV_A), F32),
        ],
        scratch_shapes=[pltpu.VMEM((n_seq, H_A, DK_A, DV_A), F32),
                        pltpu.VMEM((n_seq, tb + SUBLANES, QKV_A), F32)],
        compiler_params=_cparams(1, 48),
        name="prompt_gdn",
    )(qkv, ba, cw, prm)


def _merge_kernel(x_ref, oa_ref, ob_ref, nw_ref, w2_ref, onw_ref, sbw_ref, wb_ref, wo_ref, out_ref, *, ob_scale):
    x = x_ref[...]
    h = _rms(x, nw_ref[...]).astype(BF16)
    z = jnp.dot(h, w2_ref[:, P2_Z:P2_GA], preferred_element_type=F32)
    oa = oa_ref[...]
    ob = ob_ref[...]
    onw = onw_ref[...]
    sbw = sbw_ref[...]
    oa_n, ob_n = [], []
    for hd in range(H_A):
        sl = slice(hd * DV_A, (hd + 1) * DV_A)
        oa_n.append((_rms(oa[:, sl], onw) * _silu(z[:, sl])).astype(BF16))
    for hd in range(H_B):
        sl = slice(hd * DV_B, (hd + 1) * DV_B)
        ob_n.append((_rms(ob[:, sl], sbw) * ob_scale).astype(BF16))
    pa = jnp.dot(jnp.concatenate(oa_n, axis=1), wb_ref[0], preferred_element_type=F32)
    pb = jnp.dot(jnp.concatenate(ob_n, axis=1), wb_ref[1], preferred_element_type=F32)
    ga = jnp.dot(h, w2_ref[:, P2_GA:P2_GB], preferred_element_type=F32)
    gb = jnp.dot(h, w2_ref[:, P2_GB:P2_END], preferred_element_type=F32)
    mixed = (_sigmoid(ga) * pa + _sigmoid(gb) * pb).astype(BF16)
    out_ref[...] = x + jnp.dot(mixed, wo_ref[...], preferred_element_type=F32)


def _merge(x, oa, ob, nw, w2, onw, sbw, wb, wo, tm, ob_scale):
    r = x.shape[0]
    row = lambda w: pl.BlockSpec((tm, w), lambda i: (i, 0))
    full = lambda a: pl.BlockSpec(a.shape, lambda i: (0,) * a.ndim)
    return pl.pallas_call(
        functools.partial(_merge_kernel, ob_scale=ob_scale),
        grid=(r // tm,),
        in_specs=[row(D_MODEL), row(H_A * DV_A), row(H_B * DV_B), full(nw), full(w2), full(onw), full(sbw),
                  full(wb), full(wo)],
        out_specs=row(D_MODEL),
        out_shape=jax.ShapeDtypeStruct((r, D_MODEL), F32),
        compiler_params=_cparams(1, 52),
        name="merge",
    )(x, oa, ob, nw, w2, onw, sbw, wb, wo)


FF_CHUNK = 256


def _ffn_kernel(x_ref, st_ref, nw_ref, wup_ref, cw_ref, cb_ref, wdn_ref, out_ref, stout_ref,
                carry_ref, ext_ref, *, tm, shift, halo):
    i = pl.program_id(1)

    @pl.when(i == 0)
    def _():
        carry_ref[...] = st_ref[0]

    x = x_ref[...]
    h = _rms(x, nw_ref[...]).astype(BF16)
    cw = cw_ref[...]
    cb = cb_ref[...]
    acc = x
    for c in range(D_FF // FF_CHUNK):
        parts = []
        for part in range(2):
            lo = part * D_FF + c * FF_CHUNK
            sl = slice(lo, lo + FF_CHUNK)
            es = slice(part * FF_CHUNK, (part + 1) * FF_CHUNK)
            up = jnp.dot(h, wup_ref[:, sl], preferred_element_type=F32)
            ext_ref[0:halo, es] = carry_ref[:, sl]
            ext_ref[halo:halo + tm, es] = up
            u = cw[2:3, sl] * up + cb[:, sl]
            u = u + cw[1:2, sl] * ext_ref[halo - shift:halo - shift + tm, es]
            u = u + cw[0:1, sl] * ext_ref[halo - 2 * shift:halo - 2 * shift + tm, es]
            carry_ref[:, sl] = ext_ref[tm:tm + halo, es]
            parts.append(u)
        act = (_silu(parts[0]) * parts[1]).astype(BF16)
        acc = acc + jnp.dot(act, wdn_ref[c * FF_CHUNK:(c + 1) * FF_CHUNK, :], preferred_element_type=F32)
    out_ref[...] = acc
    stout_ref[0] = carry_ref[...]


def _ffn(x, st, nw, wup, cw, cb, wdn, groups, tm, shift, halo):
    r = x.shape[0]
    tiles = r // groups // tm
    full = lambda a: pl.BlockSpec(a.shape, lambda g, i: (0,) * a.ndim)
    return pl.pallas_call(
        functools.partial(_ffn_kernel, tm=tm, shift=shift, halo=halo),
        grid=(groups, tiles),
        in_specs=[
            pl.BlockSpec((tm, D_MODEL), lambda g, i: (g * tiles + i, 0)),
            pl.BlockSpec((1, halo, 2 * D_FF), lambda g, i: (g, 0, 0)),
            full(nw), full(wup), full(cw), full(cb), full(wdn),
        ],
        out_specs=[
            pl.BlockSpec((tm, D_MODEL), lambda g, i: (g * tiles + i, 0)),
            pl.BlockSpec((1, halo, 2 * D_FF), lambda g, i: (g, 0, 0)),
        ],
        out_shape=[jax.ShapeDtypeStruct((r, D_MODEL), F32),
                   jax.ShapeDtypeStruct((groups, halo, 2 * D_FF), F32)],
        scratch_shapes=[pltpu.VMEM((halo, 2 * D_FF), F32), pltpu.VMEM((halo + tm, 2 * FF_CHUNK), F32)],
        compiler_params=_cparams(2, 56),
        name="ffn",
    )(x, st, nw, wup, cw, cb, wdn)


def _sgdn_pre_kernel(qkv_ref, st_ref, ba_ref, cw_ref, prm_ref,
                     wq_ref, u_ref, kd_ref, a_ref, gl_ref, cst_ref, *, t_new, nb):
    cw = cw_ref[...]
    xp = [st_ref[i] for i in range(CONV_A - 1)] + [qkv_ref[t] for t in range(t_new)]
    for i in range(CONV_A - 1):
        cst_ref[i] = xp[t_new + i]
    prm = prm_ref[...]
    lane = lax.broadcasted_iota(jnp.int32, (nb, LANES), 1)

    ys, betas, gs = [], [], []
    for t in range(t_new):
        y = cw[0:1] * xp[t]
        for i in range(1, CONV_A):
            y = y + cw[i:i + 1] * xp[t + i]
        ys.append(_silu(y))
        bg = ba_ref[t]
        betas.append(_sigmoid(bg))
        gs.append(-jnp.exp(prm[0:1]) * _softplus(bg + prm[1:2]))
    gcs = [gs[0]]
    for t in range(1, t_new):
        gcs.append(gcs[-1] + gs[t])

    a_out = [jnp.zeros((nb, LANES), F32) for _ in range(t_new)]
    for h in range(H_A):
        q, k, v, beta, gc = [], [], [], [], []
        for t in range(t_new):
            qt = ys[t][:, h * DK_A:(h + 1) * DK_A]
            kt = ys[t][:, H_A * DK_A + h * DK_A:H_A * DK_A + (h + 1) * DK_A]
            q.append(qt * lax.rsqrt(jnp.sum(qt * qt, axis=-1, keepdims=True) + EPS) * (DK_A ** -0.5))
            k.append(kt * lax.rsqrt(jnp.sum(kt * kt, axis=-1, keepdims=True) + EPS))
            v.append(ys[t][:, 2 * H_A * DK_A + h * DV_A:2 * H_A * DK_A + (h + 1) * DV_A])
            beta.append(betas[t][:, h:h + 1])
            gc.append(gcs[t][:, H_A + h:H_A + h + 1])
        m = [[None] * t_new for _ in range(t_new)]
        for i in range(t_new):
            for jj in range(i + 1):
                dec = jnp.exp(gc[i] - gc[jj])
                if jj < i:
                    m[i][jj] = beta[i] * jnp.sum(k[i] * k[jj], axis=-1, keepdims=True) * dec
                aij = jnp.sum(q[i] * k[jj], axis=-1, keepdims=True) * dec
                a_out[i] = jnp.where(lane == h * SUBLANES + jj, aij, a_out[i])
        tm_ = [[None] * t_new for _ in range(t_new)]
        for i in range(t_new):
            for jj in range(i):
                acc = m[i][jj]
                for l in range(jj + 1, i):
                    acc = acc + m[i][l] * tm_[l][jj]
                tm_[i][jj] = -acc
        vb = [v[t] * beta[t] for t in range(t_new)]
        kbg = [k[t] * (beta[t] * jnp.exp(gc[t])) for t in range(t_new)]
        hs = slice(h * DK_A, (h + 1) * DK_A)
        for i in range(t_new):
            u = vb[i]
            w = kbg[i]
            for jj in range(i):
                u = u + tm_[i][jj] * vb[jj]
                w = w + tm_[i][jj] * kbg[jj]
            u_ref[i, :, hs] = u
            wq_ref[i, :, hs] = w
            wq_ref[t_new + i, :, hs] = q[i] * jnp.exp(gc[i])
            kd_ref[i, :, hs] = k[i] * jnp.exp(gc[t_new - 1] - gc[i])
        gl_ref[:, hs] = jnp.broadcast_to(jnp.exp(gc[t_new - 1]), (nb, DK_A))
    for i in range(t_new):
        a_ref[i] = a_out[i]


def _sgdn_pre(qkv_t, st_t, ba_t, cw, prm, t_new, nb):
    wide = H_A * DK_A
    return pl.pallas_call(
        functools.partial(_sgdn_pre_kernel, t_new=t_new, nb=nb),
        out_shape=[
            jax.ShapeDtypeStruct((2 * t_new, nb, wide), F32),
            jax.ShapeDtypeStruct((t_new, nb, wide), F32),
            jax.ShapeDtypeStruct((t_new, nb, wide), F32),
            jax.ShapeDtypeStruct((t_new, nb, LANES), F32),
            jax.ShapeDtypeStruct((nb, wide), F32),
            jax.ShapeDtypeStruct((CONV_A - 1, nb, QKV_A), F32),
        ],
        compiler_params=pltpu.CompilerParams(vmem_limit_bytes=48 * 1024 * 1024),
        name="sample_gdn_pre",
    )(qkv_t, st_t, ba_t, cw, prm)


SGDN_BB = 8


def _sgdn_state_kernel(wq_ref, u_ref, kd_ref, a_ref, gl_ref, s0_ref, o_ref, s1_ref, *, t_new):
    rows = 2 * t_new
    rid = lax.broadcasted_iota(jnp.int32, (rows, DK_A), 0)
    zpad = jnp.zeros((rows - t_new, DK_A), F32)
    for bi in range(SGDN_BB):
        amat = a_ref[:, bi, :]
        glrow = gl_ref[bi:bi + 1, :]
        for h in range(H_A):
            hs = slice(h * DK_A, (h + 1) * DK_A)
            s0 = s0_ref[bi, h]
            r = jnp.dot(wq_ref[:, bi, hs].astype(BF16), s0.astype(BF16), preferred_element_type=F32)
            u8 = jnp.concatenate([u_ref[:, bi, hs], zpad], axis=0)
            v_new = jnp.where(rid < t_new, u8 - r, 0.0)
            o = r[t_new:rows]
            for jj in range(t_new):
                col = amat[:, h * SUBLANES + jj:h * SUBLANES + jj + 1]
                o = o + col * v_new[jj:jj + 1, :]
            kd8 = jnp.concatenate([kd_ref[:, bi, hs], zpad], axis=0)
            s1 = s0 * glrow[:, hs] + lax.dot_general(kd8.astype(BF16), v_new.astype(BF16),
                                                     (((0,), (0,)), ((), ())), preferred_element_type=F32)
            s1_ref[bi, h] = s1
            o_ref[:, bi, hs] = o


def _sgdn_state(wq, u, kd, a, gl, s0_all, li, t_new, nb):
    wide = H_A * DK_A
    bb = SGDN_BB
    return pl.pallas_call(
        functools.partial(_sgdn_state_kernel, t_new=t_new),
        grid=(nb // bb,),
        in_specs=[
            pl.BlockSpec((2 * t_new, bb, wide), lambda i: (0, i, 0)),
            pl.BlockSpec((t_new, bb, wide), lambda i: (0, i, 0)),
            pl.BlockSpec((t_new, bb, wide), lambda i: (0, i, 0)),
            pl.BlockSpec((t_new, bb, LANES), lambda i: (0, i, 0)),
            pl.BlockSpec((bb, wide), lambda i: (i, 0)),
            pl.BlockSpec((None, bb, H_A, DK_A, DV_A), lambda i: (li, i, 0, 0, 0)),
        ],
        out_specs=[
            pl.BlockSpec((t_new, bb, wide), lambda i: (0, i, 0)),
            pl.BlockSpec((bb, H_A, DK_A, DV_A), lambda i: (i, 0, 0, 0)),
        ],
        out_shape=[jax.ShapeDtypeStruct((t_new, nb, wide), F32),
                   jax.ShapeDtypeStruct((nb, H_A, DK_A, DV_A), F32)],
        compiler_params=_cparams(1, 32),
        name="sample_gdn_state",
    )(wq, u, kd, a, gl, s0_all)


NEW_ROWS = 16


def _sattn_kernel(pt_ref, lam_ref, q_ref, kn_ref, vn_ref, bias_ref, *rest, n_pages, t_new, lam_init):
    del pt_ref
    k_pages = rest[:n_pages]
    v_pages = rest[n_pages:2 * n_pages]
    o_ref = rest[2 * n_pages]
    rows = 2 * t_new
    q = q_ref[0]
    bias = bias_ref[...]
    lam = _lam_value(lam_ref[...], lam_init)
    r = lax.broadcasted_iota(jnp.int32, (rows, 2 * DH_B), 0)
    c = lax.broadcasted_iota(jnp.int32, (rows, 2 * DH_B), 1)
    map_mask = (c // DH_B) == (r // t_new)
    pad = jnp.zeros((PAGE - NEW_ROWS, LANES), BF16)
    nt = (((1,), (1,)), ((), ()))

    heads = range(H_B)
    lanes = [slice(h * LANES, (h + 1) * LANES) for h in heads]
    head_rows = [pl.ds(h, PAGE, stride=H_B) for h in heads]
    qx = [jnp.where(map_mask, jnp.concatenate([q[:, lanes[h]]] * 2, axis=0), 0.0).astype(BF16) for h in heads]
    bh = [bias[h * rows:(h + 1) * rows] for h in heads]

    s_parts = [[] for _ in heads]
    for p in range(n_pages):
        for h in heads:
            kh = k_pages[p][0, 0, head_rows[h], :].astype(BF16)
            s = lax.dot_general(qx[h], kh, nt, preferred_element_type=F32)
            if p == n_pages - 1:
                s = s + bh[h][:, 0:PAGE]
            s_parts[h].append(s)
    for h in heads:
        kn = jnp.concatenate([kn_ref[0, :, lanes[h]], pad], axis=0)
        s_parts[h].append(lax.dot_general(qx[h], kn, nt, preferred_element_type=F32) + bh[h][:, PAGE:2 * PAGE])

    m = []
    for h in heads:
        mh = s_parts[h][0].max(axis=-1, keepdims=True)
        for s in s_parts[h][1:]:
            mh = jnp.maximum(mh, s.max(axis=-1, keepdims=True))
        m.append(mh)
    l = [jnp.zeros((rows, 1), F32) for _ in heads]
    acc = [jnp.zeros((rows, DV_B), F32) for _ in heads]
    for p in range(n_pages + 1):
        for h in heads:
            pr = jnp.exp2(s_parts[h][p] - m[h])
            l[h] = l[h] + jnp.sum(pr, axis=-1, keepdims=True)
            if p < n_pages:
                vv = v_pages[p][0, 0, head_rows[h], :].astype(BF16)
            else:
                vv = jnp.concatenate([vn_ref[0, :, lanes[h]], pad], axis=0)
            acc[h] = acc[h] + jnp.dot(pr.astype(BF16), vv, preferred_element_type=F32)
    for h in heads:
        a = acc[h] / l[h]
        o_ref[0, :, lanes[h]] = a[0:t_new] - lam * a[t_new:rows]


def _sample_attn(page_table, lamv, q_b, kn_b, vn_b, bias, ck, cv, li, t_new, lam_init):
    nb, n_pages = page_table.shape
    page_spec = lambda p: pl.BlockSpec((1, 1, PAGE * H_B, LANES), lambda b, pt, p=p: (li, pt[b, p], 0, 0))
    tok_spec = pl.BlockSpec((1, t_new, HB_W), lambda b, pt: (b, 0, 0))
    new_spec = pl.BlockSpec((1, NEW_ROWS, HB_W), lambda b, pt: (b, 0, 0))
    grid_spec = pltpu.PrefetchScalarGridSpec(
        num_scalar_prefetch=1,
        grid=(nb,),
        in_specs=[pl.BlockSpec(lamv.shape, lambda b, pt: (0, 0)), tok_spec, new_spec, new_spec,
                  pl.BlockSpec(bias.shape, lambda b, pt: (0, 0))]
                 + [page_spec(p) for p in range(n_pages)] * 2,
        out_specs=pl.BlockSpec((1, t_new, HB_W), lambda b, pt: (b, 0, 0)),
    )
    return pl.pallas_call(
        functools.partial(_sattn_kernel, n_pages=n_pages, t_new=t_new, lam_init=lam_init),
        grid_spec=grid_spec,
        out_shape=jax.ShapeDtypeStruct((nb, t_new, HB_W), F32),
        compiler_params=_cparams(1, 40),
        name="sample_attn",
    )(page_table, lamv, q_b, kn_b, vn_b, bias, *([ck] * n_pages), *([cv] * n_pages))


def _pick(n, pref):
    return pref if n % pref == 0 else n


def _to_bmajor(a_t, t_new, nb):
    return jnp.swapaxes(a_t.reshape(t_new, nb, -1), 0, 1)


def _layer_weights(li, rel_bias, norm_mix, w_in, conv_a, a_log, dt_bias, onorm_a, qnorm_b, knorm_b,
                   lam_q1, lam_k1, lam_q2, lam_k2, subln_b, w_branch, w_o, norm_ffn, w_up, conv_f,
                   conv_f_b, w_down):
    w = w_in[li]
    o_z = QKV_A
    o_b = o_z + H_A * DV_A
    o_qb = o_b + 2 * H_A
    o_kb = o_qb + HB_W
    o_vb = o_kb + HB_W
    o_ga = o_vb + H_B * DV_B
    o_gb = o_ga + D_MODEL
    ba_cols = jnp.pad(w[:, o_b:o_qb], ((0, 0), (0, LANES - 2 * H_A)))
    w1 = jnp.concatenate([w[:, 0:o_z], w[:, o_qb:o_ga], ba_cols], axis=1).astype(BF16)
    w2 = jnp.concatenate([w[:, o_z:o_b], w[:, o_ga:]], axis=1).astype(BF16)
    prm = jnp.zeros((2, LANES), F32)
    prm = prm.at[0, H_A:2 * H_A].set(a_log[li]).at[1, H_A:2 * H_A].set(dt_bias[li])
    k_bound = 1.01 * math.sqrt(DH_B) * jnp.max(jnp.abs(knorm_b[li]))
    q_bound = 1.01 * Q_SCALE * math.sqrt(DH_B) * jnp.max(jnp.abs(qnorm_b[li]))
    rel = (rel_bias - rel_bias[NUM_BUCKETS - 1:NUM_BUCKETS]) * LOG2E
    spread = 2.0 * q_bound * k_bound + jnp.max(jnp.max(rel, axis=0) - jnp.min(rel, axis=0))
    scal = jnp.concatenate([jnp.where(spread <= SAFE_SPREAD_BITS, 1.0, 0.0).reshape(1), k_bound.reshape(1),
                            jnp.max(rel, axis=0), jnp.zeros((2,), F32)]).astype(F32)
    return dict(
        scal=scal,
        nw=norm_mix[li].reshape(1, D_MODEL), w1=w1, w2=w2,
        qw=jnp.tile(qnorm_b[li], 2 * H_B).reshape(1, HB_W),
        kw=jnp.tile(knorm_b[li], 2 * H_B).reshape(1, HB_W),
        cw_a=conv_a[li], prm=prm,
        onw=onorm_a[li].reshape(1, DV_A), sbw=subln_b[li].reshape(1, DV_B),
        lamv=jnp.stack([lam_q1[li], lam_k1[li], lam_q2[li], lam_k2[li]]),
        wb=w_branch[li].astype(BF16), wo=w_o[li].astype(BF16),
        nwf=norm_ffn[li].reshape(1, D_MODEL), wup=w_up[li].astype(BF16),
        cw_f=conv_f[li], cb_f=conv_f_b[li].reshape(1, 2 * D_FF), wdn=w_down[li].astype(BF16),
    )


def _group_ones():
    i = jnp.arange(MXU_DIM)
    return ((i[:, None] // DH_B) == (i[None, :] // DH_B)).astype(BF16)


def _prompt_layer(x, wt, bias_p, g, n_seq, seq, lam_init):
    tm = _pick(seq, 512)
    blk = bias_p.shape[-1]
    qkv, ba, qn, kn, kf, vf, vb = _inproj(x, wt["nw"], wt["w1"], wt["qw"], wt["kw"], g, tm)
    o_a, s_fin, cst_a = _prompt_gdn(qkv.reshape(n_seq, seq, QKV_A), ba.reshape(n_seq, seq, LANES),
                                    wt["cw_a"], wt["prm"], n_seq, seq, _pick(seq, 256))
    o_a = o_a.reshape(n_seq * seq, H_A * DV_A)
    o_b = _prompt_attn(wt["scal"], wt["lamv"], qn, kn, vb, bias_p, n_seq, seq, blk, lam_init)
    x = _merge(x, o_a, o_b, wt["nw"], wt["w2"], wt["onw"], wt["sbw"], wt["wb"], wt["wo"], tm, 1.0 - lam_init)
    zero_st = jnp.zeros((n_seq, SUBLANES, 2 * D_FF), F32)
    x, cst_f = _ffn(x, zero_st, wt["nwf"], wt["wup"], wt["cw_f"], wt["cb_f"], wt["wdn"],
                    n_seq, tm, 1, SUBLANES)
    return (x, kf.reshape(n_seq, seq, H_B, 2 * DH_B), vf.reshape(n_seq, seq, H_B, DV_B), s_fin,
            cst_a[:, SUBLANES - (CONV_A - 1):], cst_f[:, SUBLANES - (CONV_F - 1):])


def _sample_layer(x_t, wt, bias_s, g, page_table, ck, cv, li, s0_all, cst_a, cst_f, nb, t_new, lam_init):
    r = nb * t_new
    qkv, ba, qn, kn, kf, vf, vb = _inproj(x_t, wt["nw"], wt["w1"], wt["qw"], wt["kw"], g, r)
    st_a_t = jnp.swapaxes(cst_a, 0, 1)
    wq, u, kd, a, gl, cst_a_t = _sgdn_pre(qkv.reshape(t_new, nb, QKV_A), st_a_t,
                                          ba.reshape(t_new, nb, LANES), wt["cw_a"], wt["prm"], t_new, nb)
    o_a_t, s1 = _sgdn_state(wq, u, kd, a, gl, s0_all, li, t_new, nb)
    pad_new = lambda a: jnp.pad(_to_bmajor(a, t_new, nb), ((0, 0), (0, NEW_ROWS - t_new), (0, 0)))
    o_b = _sample_attn(page_table, wt["lamv"], _to_bmajor(qn, t_new, nb).astype(F32), pad_new(kn),
                       pad_new(vb), bias_s, ck, cv, li, t_new, lam_init)
    o_b_t = jnp.swapaxes(o_b, 0, 1).reshape(r, HB_W)
    x_t = _merge(x_t, o_a_t.reshape(r, H_A * DV_A), o_b_t, wt["nw"], wt["w2"], wt["onw"], wt["sbw"],
                 wt["wb"], wt["wo"], r, 1.0 - lam_init)
    halo = (CONV_F - 1) * nb
    st_f_t = jnp.swapaxes(cst_f, 0, 1).reshape(1, halo, 2 * D_FF)
    x_t, cst_f_t = _ffn(x_t, st_f_t, wt["nwf"], wt["wup"], wt["cw_f"], wt["cb_f"], wt["wdn"],
                        1, r, nb, halo)
    return (x_t, _to_bmajor(kf, t_new, nb).reshape(nb, t_new, H_B, 2 * DH_B),
            _to_bmajor(vf, t_new, nb).reshape(nb, t_new, H_B, DV_B), s1,
            jnp.swapaxes(cst_a_t, 0, 1), jnp.swapaxes(cst_f_t.reshape(CONV_F - 1, nb, 2 * D_FF), 0, 1))


def kernel(x_prompt, x_sample, cache_k, cache_v, state_delta, state_conv_a, state_conv_ffn, page_table,
           rel_bias, norm_mix, w_in, conv_a, a_log, dt_bias, onorm_a, qnorm_b, knorm_b, lam_q1, lam_k1,
           lam_q2, lam_k2, subln_b, w_branch, w_o, norm_ffn, w_up, conv_f, conv_f_b, w_down):
    n_seq, seq, _ = x_prompt.shape
    nb, t_new, _ = x_sample.shape
    depth = w_in.shape[0]
    n_pool = cache_k.shape[1]

    rb_flat = rel_bias.T.reshape(-1)
    bias_p = _prompt_bias(rb_flat, _pick(seq, 512))
    bias_s = _sample_bias(rb_flat, t_new)
    g = _group_ones()
    ck = cache_k.reshape(depth, n_pool, PAGE * H_B, 2 * DH_B)
    cv = cache_v.reshape(depth, n_pool, PAGE * H_B, DV_B)

    xp = x_prompt.reshape(n_seq * seq, D_MODEL)
    xs = jnp.swapaxes(x_sample, 0, 1).reshape(t_new * nb, D_MODEL)
    outs_p, outs_s = [], []
    for li in range(depth):
        wt = _layer_weights(li, rel_bias, norm_mix, w_in, conv_a, a_log, dt_bias, onorm_a, qnorm_b, knorm_b,
                            lam_q1, lam_k1, lam_q2, lam_k2, subln_b, w_branch, w_o, norm_ffn, w_up,
                            conv_f, conv_f_b, w_down)
        lam_init = 0.8 - 0.6 * math.exp(-0.3 * li)
        xp, *rest_p = _prompt_layer(xp, wt, bias_p, g, n_seq, seq, lam_init)
        xs, *rest_s = _sample_layer(xs, wt, bias_s, g, page_table, ck, cv, li, state_delta,
                                    state_conv_a[li], state_conv_ffn[li], nb, t_new, lam_init)
        outs_p.append(rest_p)
        outs_s.append(rest_s)

    stack = lambda outs, i: jnp.stack([o[i] for o in outs])
    y_prompt = xp.reshape(n_seq, seq, D_MODEL)
    y_sample = jnp.swapaxes(xs.reshape(t_new, nb, D_MODEL), 0, 1)
    return (y_prompt, y_sample,
            stack(outs_p, 0), stack(outs_p, 1), stack(outs_p, 2), stack(outs_p, 3), stack(outs_p, 4),
            stack(outs_s, 0), stack(outs_s, 1), stack(outs_s, 2), stack(outs_s, 3), stack(outs_s, 4))
```

```python
import functools
import math

import jax
import jax.numpy as jnp
from jax import lax
from jax.experimental import pallas as pl
from jax.experimental.pallas import tpu as pltpu

F32 = jnp.float32
BF16 = jnp.bfloat16

D_MODEL = 1024
H_A, DK_A, DV_A, CONV_A, CHUNK = 4, 128, 128, 4, 64
QKV_A = 2 * H_A * DK_A + H_A * DV_A
H_B, DH_B, DV_B = 4, 64, 128
HB_W = H_B * 2 * DH_B
PAGE = 128
D_FF, CONV_F = 2816, 3
NUM_BUCKETS, MAX_DISTANCE = 32, 128
NEG_INF = -1e30
EPS = 1e-6
ATT_SCALE = DH_B ** -0.5
LOG2E = math.log2(math.e)
Q_SCALE = ATT_SCALE * LOG2E
SAFE_SPREAD_BITS = 100.0

V7X_VMEM_BYTES = 64 * 1024 * 1024
LANES = 128
SUBLANES = 8
MXU_DIM = 256

P1_QKV, P1_Q, P1_K, P1_V, P1_BA, P1_END = 0, 1536, 2048, 2560, 3072, 3200
P2_Z, P2_GA, P2_GB, P2_END = 0, 512, 1536, 2560


def _cparams(n_axes, vmem_mib):
    return pltpu.CompilerParams(
        dimension_semantics=("arbitrary",) * n_axes,
        vmem_limit_bytes=min(vmem_mib * 1024 * 1024, V7X_VMEM_BYTES - 8 * 1024 * 1024),
    )


def _bdot(a, b):
    return jnp.dot(a.astype(BF16), b.astype(BF16), preferred_element_type=F32)


def _rms(x, w):
    return x * lax.rsqrt(jnp.mean(x * x, axis=-1, keepdims=True) + EPS) * w


def _sigmoid(x):
    return 1.0 / (1.0 + jnp.exp(-x))


def _silu(x):
    return x * _sigmoid(x)


def _softplus(x):
    return jnp.maximum(x, 0.0) + jnp.log1p(jnp.exp(-jnp.abs(x)))


def _inproj_kernel(x_ref, nw_ref, w_ref, qw_ref, kw_ref, g_ref,
                   qkv_ref, ba_ref, qn_ref, kn_ref, kf_ref, vf_ref, vb_ref):
    h = _rms(x_ref[...], nw_ref[...]).astype(BF16)

    def proj(lo, hi):
        return jnp.dot(h, w_ref[:, lo:hi], preferred_element_type=F32)

    tm = x_ref.shape[0]
    head_rows = [pl.ds(hd, tm, stride=H_B) for hd in range(H_B)]

    qkv_ref[...] = proj(P1_QKV, P1_Q)
    ba_ref[...] = proj(P1_BA, P1_END)
    v = proj(P1_V, P1_BA)
    for hd in range(H_B):
        vf_ref[head_rows[hd], :] = v[:, hd * DV_B:(hd + 1) * DV_B]
    vb_ref[...] = v.astype(BF16)

    g = g_ref[...]

    def qk_norm(y, w):
        sq = y * y
        hi = sq.astype(BF16)
        lo = (sq - hi.astype(F32)).astype(BF16)
        outs = []
        for c in range(HB_W // MXU_DIM):
            sl = slice(c * MXU_DIM, (c + 1) * MXU_DIM)
            ss = (jnp.dot(hi[:, sl], g, preferred_element_type=F32)
                  + jnp.dot(lo[:, sl], g, preferred_element_type=F32))
            outs.append(y[:, sl] * lax.rsqrt(ss * (1.0 / DH_B) + EPS) * w[:, sl])
        return outs

    qn = qk_norm(proj(P1_Q, P1_K), qw_ref[...])
    kn = qk_norm(proj(P1_K, P1_V), kw_ref[...])
    for c in range(HB_W // MXU_DIM):
        sl = slice(c * MXU_DIM, (c + 1) * MXU_DIM)
        qn_ref[:, sl] = (qn[c] * Q_SCALE).astype(BF16)
        kn_ref[:, sl] = kn[c].astype(BF16)
        for i in range(MXU_DIM // LANES):
            kf_ref[head_rows[c * (MXU_DIM // LANES) + i], :] = kn[c][:, i * LANES:(i + 1) * LANES]


def _inproj(x, nw, w1, qw, kw, g, tm):
    r = x.shape[0]
    row = lambda w: pl.BlockSpec((tm, w), lambda i: (i, 0))
    full = lambda a: pl.BlockSpec(a.shape, lambda i: (0,) * a.ndim)
    outs = [(1, QKV_A, F32), (1, LANES, F32), (1, HB_W, BF16), (1, HB_W, BF16), (H_B, LANES, F32),
            (H_B, LANES, F32), (1, HB_W, BF16)]
    return pl.pallas_call(
        _inproj_kernel,
        grid=(r // tm,),
        in_specs=[row(D_MODEL), full(nw), full(w1), full(qw), full(kw), full(g)],
        out_specs=[pl.BlockSpec((tm * m, w), lambda i: (i, 0)) for m, w, _ in outs],
        out_shape=[jax.ShapeDtypeStruct((r * m, w), dt) for m, w, dt in outs],
        compiler_params=_cparams(1, 52),
        name="inproj",
    )(x, nw, w1, qw, kw, g)


def _bucket_bias(d, table):
    n = jnp.maximum(d, 0)
    max_exact = NUM_BUCKETS // 2
    nf = jnp.maximum(n, 1).astype(F32)
    large = max_exact + (jnp.log(nf / max_exact) / math.log(MAX_DISTANCE / max_exact)
                         * (NUM_BUCKETS - max_exact)).astype(jnp.int32)
    large = jnp.minimum(large, NUM_BUCKETS - 1)
    bucket = jnp.where(n < max_exact, n, large)
    val = jnp.zeros(d.shape, F32)
    for b in range(NUM_BUCKETS):
        val = jnp.where(bucket == b, table(b), val)
    return val


def _prompt_bias_kernel(rb_ref, o_ref, *, blk):
    h = pl.program_id(0)
    table = lambda b: rb_ref[h * NUM_BUCKETS + b]
    far = table(NUM_BUCKETS - 1)
    i = lax.broadcasted_iota(jnp.int32, (blk, blk), 0)
    j = lax.broadcasted_iota(jnp.int32, (blk, blk), 1)
    d0 = i - j
    o_ref[0, :, 0:blk] = (_bucket_bias(d0 + blk, table) - far) * LOG2E
    o_ref[0, :, blk:2 * blk] = jnp.where(d0 >= 0, (_bucket_bias(d0, table) - far) * LOG2E, NEG_INF)


def _prompt_bias(rb_flat, blk):
    return pl.pallas_call(
        functools.partial(_prompt_bias_kernel, blk=blk),
        grid=(H_B,),
        in_specs=[pl.BlockSpec(memory_space=pltpu.SMEM)],
        out_specs=pl.BlockSpec((1, blk, 2 * blk), lambda h: (h, 0, 0)),
        out_shape=jax.ShapeDtypeStruct((H_B, blk, 2 * blk), F32),
        compiler_params=_cparams(1, 32),
        name="prompt_bias",
    )(rb_flat)


def _sample_bias_kernel(rb_ref, o_ref, *, t_new):
    rows, cols = o_ref.shape
    r = lax.broadcasted_iota(jnp.int32, (rows, cols), 0)
    c = lax.broadcasted_iota(jnp.int32, (rows, cols), 1)
    t = r % t_new
    hd = r // (2 * t_new)
    is_new = c >= PAGE
    d = jnp.where(is_new, t - (c - PAGE), t + PAGE - c)
    val = jnp.zeros((rows, cols), F32)
    for h in range(H_B):
        table = lambda b, h=h: rb_ref[h * NUM_BUCKETS + b]
        vh = _bucket_bias(d, table) - table(NUM_BUCKETS - 1)
        val = jnp.where(hd == h, vh, val)
    visible = jnp.logical_and(d >= 0, jnp.logical_or(~is_new, (c - PAGE) < t_new))
    o_ref[...] = jnp.where(visible, val * LOG2E, NEG_INF)


def _sample_bias(rb_flat, t_new):
    rows = H_B * 2 * t_new
    return pl.pallas_call(
        functools.partial(_sample_bias_kernel, t_new=t_new),
        in_specs=[pl.BlockSpec(memory_space=pltpu.SMEM)],
        out_shape=jax.ShapeDtypeStruct((rows, 2 * PAGE), F32),
        name="sample_bias",
    )(rb_flat)


def _lam_value(lv, lam_init):
    s1 = jnp.sum(lv[0:1] * lv[1:2], axis=-1, keepdims=True)
    s2 = jnp.sum(lv[2:3] * lv[3:4], axis=-1, keepdims=True)
    return jnp.exp(s1) - jnp.exp(s2) + lam_init


FAR_BLOCKS = 4


def _causal_sweep(qi, blk, bias_ref, update):
    n_far = jnp.maximum(qi - 1, 0)
    span = FAR_BLOCKS * blk

    def far(kk, carry):
        update(pl.multiple_of(kk * span, span), FAR_BLOCKS, None)
        return carry

    n_trips = n_far // FAR_BLOCKS
    lax.fori_loop(0, n_trips, far, 0)
    done = n_trips * FAR_BLOCKS
    rem = n_far - done
    size = FAR_BLOCKS // 2
    while size >= 1:
        @pl.when((rem // size) % 2 == 1)
        def _(size=size):
            start = done + (rem // (2 * size)) * (2 * size)
            update(pl.multiple_of(start * blk, blk), size, None)

        size //= 2

    @pl.when(qi >= 1)
    def _():
        update(pl.multiple_of((qi - 1) * blk, blk), 2, bias_ref[0])

    @pl.when(qi == 0)
    def _():
        update(0, 1, bias_ref[0, :, blk:2 * blk])


def _attn_kernel(scal_ref, lam_ref, q_ref, k_ref, v_ref, bias_ref, o_ref, m_ref, l_ref, acc_ref, accx_ref,
                 *, blk, lam_init):
    hd = pl.program_id(1)
    qi = pl.program_id(2)
    q = q_ref[...]
    lane = lax.broadcasted_iota(jnp.int32, (1, 2 * DH_B), 1)
    zero = jnp.zeros_like(q)
    qs = (jnp.where(lane < DH_B, q, zero), jnp.where(lane >= DH_B, q, zero))
    lam = _lam_value(lam_ref[...], lam_init)
    nt = (((1,), (1,)), ((), ()))

    @pl.when(scal_ref[0] > 0.5)
    def _():
        one_col = jnp.where(lane == 0, 1.0, 0.0).astype(BF16)
        sizes = {1, 2}
        sizes.update(2 ** e for e in range(FAR_BLOCKS.bit_length()))
        ext_k = {n: jnp.broadcast_to(one_col, (n * blk, 2 * DH_B)) for n in sizes}
        qe = []
        for mp in range(2):
            qf = qs[mp].astype(F32)
            shift = jnp.sqrt(jnp.sum(qf * qf, axis=-1, keepdims=True)) * scal_ref[1] + scal_ref[2 + hd]
            qe.append(jnp.concatenate([qs[mp], jnp.where(lane == 0, -shift, 0.0).astype(BF16)], axis=1))
        accx_ref[...] = jnp.zeros(accx_ref.shape, F32)

        def update(off, nblk, bias):
            rows = nblk * blk
            kb = jnp.concatenate([k_ref[pl.ds(off, rows), :], ext_k[nblk]], axis=1)
            vb = jnp.concatenate([v_ref[pl.ds(off, rows), :], ext_k[nblk]], axis=1)
            for mp in range(2):
                s = lax.dot_general(qe[mp], kb, nt, preferred_element_type=F32)
                if bias is not None:
                    s = s + bias
                accx_ref[mp] += jnp.dot(jnp.exp2(s).astype(BF16), vb, preferred_element_type=F32)

        _causal_sweep(qi, blk, bias_ref, update)
        a1 = accx_ref[0]
        a2 = accx_ref[1]
        o_ref[...] = (a1[:, 0:DV_B] / a1[:, DV_B:DV_B + 1]
                      - lam * (a2[:, 0:DV_B] / a2[:, DV_B:DV_B + 1]))

    @pl.when(scal_ref[0] <= 0.5)
    def _():
        m_ref[...] = jnp.full(m_ref.shape, NEG_INF, F32)
        l_ref[...] = jnp.zeros(l_ref.shape, F32)
        acc_ref[...] = jnp.zeros(acc_ref.shape, F32)

        def update(off, nblk, bias):
            for i in range(nblk):
                update_one(pl.multiple_of(off + i * blk, blk),
                           None if bias is None else bias[:, i * blk:(i + 1) * blk])

        def update_one(off, bias):
            kb = k_ref[pl.ds(off, blk), :]
            vb = v_ref[pl.ds(off, blk), :]
            for mp in range(2):
                s = lax.dot_general(qs[mp], kb, nt, preferred_element_type=F32)
                if bias is not None:
                    s = s + bias
                m_old = m_ref[mp]
                m_new = jnp.maximum(m_old, jnp.max(s, axis=-1, keepdims=True))
                alpha = jnp.exp2(m_old - m_new)
                p = jnp.exp2(s - jnp.concatenate([m_new] * (blk // LANES), axis=1))
                l_ref[mp] = alpha * l_ref[mp] + jnp.sum(p, axis=-1, keepdims=True)
                acc_ref[mp] = alpha * acc_ref[mp] + jnp.dot(p.astype(BF16), vb, preferred_element_type=F32)
                m_ref[mp] = m_new

        _causal_sweep(qi, blk, bias_ref, update)
        o_ref[...] = acc_ref[0] / l_ref[0] - lam * (acc_ref[1] / l_ref[1])


def _prompt_attn(scal, lamv, qn, kn, vb, bias, n_seq, seq, blk, lam_init):
    nq = seq // blk
    r = n_seq * seq
    return pl.pallas_call(
        functools.partial(_attn_kernel, blk=blk, lam_init=lam_init),
        grid=(n_seq, H_B, nq),
        in_specs=[
            pl.BlockSpec(memory_space=pltpu.SMEM),
            pl.BlockSpec(lamv.shape, lambda b, h, i: (0, 0)),
            pl.BlockSpec((blk, LANES), lambda b, h, i: (b * nq + i, h)),
            pl.BlockSpec((seq, LANES), lambda b, h, i: (b, h)),
            pl.BlockSpec((seq, LANES), lambda b, h, i: (b, h)),
            pl.BlockSpec((1, blk, 2 * blk), lambda b, h, i: (h, 0, 0)),
        ],
        out_specs=pl.BlockSpec((blk, LANES), lambda b, h, i: (b * nq + i, h)),
        out_shape=jax.ShapeDtypeStruct((r, HB_W), F32),
        scratch_shapes=[pltpu.VMEM((2, blk, LANES), F32), pltpu.VMEM((2, blk, LANES), F32),
                        pltpu.VMEM((2, blk, DV_B), F32), pltpu.VMEM((2, blk, 2 * DV_B), F32)],
        compiler_params=_cparams(3, 48),
        name="prompt_attn",
    )(scal, lamv, qn, kn, vb, bias)


def _tri_masks():
    i = lax.broadcasted_iota(jnp.int32, (CHUNK, CHUNK), 0)
    j = lax.broadcasted_iota(jnp.int32, (CHUNK, CHUNK), 1)
    incl = i >= j
    strict = i > j
    eye = (i == j).astype(F32)
    base = jnp.logical_and(strict, (i // SUBLANES) == (j // SUBLANES))
    levels = []
    s = SUBLANES
    while s < CHUNK:
        levels.append(jnp.logical_and((i // (2 * s)) == (j // (2 * s)), (i // s) > (j // s)))
        s *= 2
    return incl, strict, eye, base, levels


def _unit_lower_inverse(ms, eye, base, levels):
    d = [jnp.where(base, m, 0.0) for m in ms]
    d2 = [_bdot(a, a) for a in d]
    d4 = [_bdot(a, a) for a in d2]
    x = [_bdot(eye - a, eye + b) for a, b in zip(d, d2)]
    x = [_bdot(a, eye + b) for a, b in zip(x, d4)]
    for lvl in levels:
        c = [jnp.where(lvl, m, 0.0) for m in ms]
        xc = [_bdot(a, b) for a, b in zip(x, c)]
        xcx = [_bdot(a, b) for a, b in zip(xc, x)]
        x = [a - b for a, b in zip(x, xcx)]
    return x


def _gdn_kernel(qkv_ref, ba_ref, cw_ref, prm_ref, o_ref, sout_ref, cst_ref, s_ref, ext_ref, *, tb, n_seq):
    j = pl.program_id(0)
    nchunk = tb // CHUNK
    halo = SUBLANES

    @pl.when(j == 0)
    def _():
        s_ref[...] = jnp.zeros(s_ref.shape, F32)
        ext_ref[:, 0:halo, :] = jnp.zeros((n_seq, halo, QKV_A), F32)

    cw = cw_ref[...]
    prm = prm_ref[...]
    incl, strict, eye, base, levels = _tri_masks()
    tril = incl.astype(F32)
    nt = (((1,), (1,)), ((), ()))

    pairs = [(b, c) for b in range(n_seq) for c in range(nchunk)]
    ys, betas, gcs = {}, {}, {}
    for b in range(n_seq):
        x = qkv_ref[b]
        ext_ref[b, halo:halo + tb, :] = x
        y = cw[3:4] * x
        for i in range(CONV_A - 1):
            sh = CONV_A - 1 - i
            y = y + cw[i:i + 1] * ext_ref[b, halo - sh:halo - sh + tb, :]
        tail = x[tb - halo:tb, :]
        ext_ref[b, 0:halo, :] = tail
        cst_ref[b] = tail
        ys[b] = _silu(y)
        bg = ba_ref[b]
        betas[b] = _sigmoid(bg)
        g_all = -jnp.exp(prm[0:1]) * _softplus(bg + prm[1:2])
        for c in range(nchunk):
            gcs[b, c] = jnp.dot(tril, g_all[c * CHUNK:(c + 1) * CHUNK],
                                precision=lax.Precision.HIGHEST, preferred_element_type=F32)

    grp = [(b, c, h) for (b, c) in pairs for h in range(H_A)]
    q_, k_, kb_, vb_, dec_, eg_, ekd_, gl_ = [], [], [], [], [], [], [], []
    gct = {bc: gcs[bc].T for bc in pairs}
    for (b, c, h) in grp:
        rs = slice(c * CHUNK, (c + 1) * CHUNK)
        y = ys[b]
        q = y[rs, h * DK_A:(h + 1) * DK_A]
        k = y[rs, H_A * DK_A + h * DK_A:H_A * DK_A + (h + 1) * DK_A]
        v = y[rs, 2 * H_A * DK_A + h * DV_A:2 * H_A * DK_A + (h + 1) * DV_A]
        q = q * lax.rsqrt(jnp.sum(q * q, axis=-1, keepdims=True) + EPS) * (DK_A ** -0.5)
        k = k * lax.rsqrt(jnp.sum(k * k, axis=-1, keepdims=True) + EPS)
        beta = betas[b][rs, h:h + 1]
        gc = gcs[b, c]
        gcol = gc[:, H_A + h:H_A + h + 1]
        grow = gct[b, c][H_A + h:H_A + h + 1, :]
        glast = gc[CHUNK - 1:CHUNK, H_A + h:H_A + h + 1]
        dec_.append(jnp.where(incl, jnp.exp(jnp.where(incl, gcol - grow, 0.0)), 0.0))
        eg_.append(jnp.exp(gcol))
        ekd_.append(jnp.exp(glast - gcol))
        gl_.append(jnp.exp(glast))
        q_.append(q)
        k_.append(k)
        kb_.append(k * beta)
        vb_.append(v * beta)

    n = len(grp)
    kk_qk = [lax.dot_general(jnp.concatenate([kb_[g], q_[g]], axis=0).astype(BF16), k_[g].astype(BF16),
                             nt, preferred_element_type=F32) for g in range(n)]
    ms = [jnp.where(strict, kk_qk[g][0:CHUNK] * dec_[g], 0.0) for g in range(n)]
    a_ = [(kk_qk[g][CHUNK:2 * CHUNK] * dec_[g]).astype(BF16) for g in range(n)]
    t_ = _unit_lower_inverse(ms, eye, base, levels)
    uw = [_bdot(t_[g], jnp.concatenate([vb_[g], kb_[g] * eg_[g]], axis=1)) for g in range(n)]
    wq_ = [jnp.concatenate([uw[g][:, DV_A:DV_A + DK_A], q_[g] * eg_[g]], axis=0).astype(BF16)
           for g in range(n)]
    kd_ = [(k_[g] * ekd_[g]).astype(BF16) for g in range(n)]

    lanes = [(b, h) for b in range(n_seq) for h in range(H_A)]
    s = {bh: s_ref[bh[0], bh[1]] for bh in lanes}
    for c in range(nchunk):
        gi = {(b, h): grp.index((b, c, h)) for (b, h) in lanes}
        ws = {bh: jnp.dot(wq_[gi[bh]], s[bh].astype(BF16), preferred_element_type=F32) for bh in lanes}
        vn = {bh: (uw[gi[bh]][:, 0:DV_A] - ws[bh][0:CHUNK]).astype(BF16) for bh in lanes}
        av = {bh: jnp.dot(a_[gi[bh]], vn[bh], preferred_element_type=F32) for bh in lanes}
        kv = {bh: lax.dot_general(kd_[gi[bh]], vn[bh], (((0,), (0,)), ((), ())), preferred_element_type=F32)
              for bh in lanes}
        for (b, h) in lanes:
            o_ref[b, c * CHUNK:(c + 1) * CHUNK, h * DV_A:(h + 1) * DV_A] = ws[b, h][CHUNK:2 * CHUNK] + av[b, h]
            s[b, h] = s[b, h] * gl_[gi[b, h]] + kv[b, h]
    for (b, h) in lanes:
        s_ref[b, h] = s[b, h]
        sout_ref[b, h] = s[b, h]


def _prompt_gdn(qkv, ba, cw, prm, n_seq, seq, tb):
    nb = seq // tb
    return pl.pallas_call(
        functools.partial(_gdn_kernel, tb=tb, n_seq=n_seq),
        grid=(nb,),
        in_specs=[
            pl.BlockSpec((n_seq, tb, QKV_A), lambda j: (0, j, 0)),
            pl.BlockSpec((n_seq, tb, LANES), lambda j: (0, j, 0)),
            pl.BlockSpec(cw.shape, lambda j: (0, 0)),
            pl.BlockSpec(prm.shape, lambda j: (0, 0)),
        ],
        out_specs=[
            pl.BlockSpec((n_seq, tb, H_A * DV_A), lambda j: (0, j, 0)),
            pl.BlockSpec((n_seq, H_A, DK_A, DV_A), lambda j: (0, 0, 0, 0)),
            pl.BlockSpec((n_seq, SUBLANES, QKV_A), lambda j: (0, 0, 0)),
        ],
        out_shape=[
            jax.ShapeDtypeStruct((n_seq, seq, H_A * DV_A), F32),
            jax.ShapeDtypeStruct((n_seq, H_A, DK_A, DV_A), F32),
            jax.ShapeDtypeStruct((n_seq, SUBLANES, QKV_A), F32),
        ],
        scratch_shapes=[pltpu.VMEM((n_seq, H_A, DK_A, DV_A), F32),
                        pltpu.VMEM((n_seq, tb + SUBLANES, QKV_A), F32)],
        compiler_params=_cparams(1, 48),
        name="prompt_gdn",
    )(qkv, ba, cw, prm)


def _merge_kernel(x_ref, oa_ref, ob_ref, nw_ref, w2_ref, onw_ref, sbw_ref, wb_ref, wo_ref, out_ref, *, ob_scale):
    x = x_ref[...]
    h = _rms(x, nw_ref[...]).astype(BF16)
    z = jnp.dot(h, w2_ref[:, P2_Z:P2_GA], preferred_element_type=F32)
    oa = oa_ref[...]
    ob = ob_ref[...]
    onw = onw_ref[...]
    sbw = sbw_ref[...]
    oa_n, ob_n = [], []
    for hd in range(H_A):
        sl = slice(hd * DV_A, (hd + 1) * DV_A)
        oa_n.append((_rms(oa[:, sl], onw) * _silu(z[:, sl])).astype(BF16))
    for hd in range(H_B):
        sl = slice(hd * DV_B, (hd + 1) * DV_B)
        ob_n.append((_rms(ob[:, sl], sbw) * ob_scale).astype(BF16))
    pa = jnp.dot(jnp.concatenate(oa_n, axis=1), wb_ref[0], preferred_element_type=F32)
    pb = jnp.dot(jnp.concatenate(ob_n, axis=1), wb_ref[1], preferred_element_type=F32)
    ga = jnp.dot(h, w2_ref[:, P2_GA:P2_GB], preferred_element_type=F32)
    gb = jnp.dot(h, w2_ref[:, P2_GB:P2_END], preferred_element_type=F32)
    mixed = (_sigmoid(ga) * pa + _sigmoid(gb) * pb).astype(BF16)
    out_ref[...] = x + jnp.dot(mixed, wo_ref[...], preferred_element_type=F32)


def _merge(x, oa, ob, nw, w2, onw, sbw, wb, wo, tm, ob_scale):
    r = x.shape[0]
    row = lambda w: pl.BlockSpec((tm, w), lambda i: (i, 0))
    full = lambda a: pl.BlockSpec(a.shape, lambda i: (0,) * a.ndim)
    return pl.pallas_call(
        functools.partial(_merge_kernel, ob_scale=ob_scale),
        grid=(r // tm,),
        in_specs=[row(D_MODEL), row(H_A * DV_A), row(H_B * DV_B), full(nw), full(w2), full(onw), full(sbw),
                  full(wb), full(wo)],
        out_specs=row(D_MODEL),
        out_shape=jax.ShapeDtypeStruct((r, D_MODEL), F32),
        compiler_params=_cparams(1, 52),
        name="merge",
    )(x, oa, ob, nw, w2, onw, sbw, wb, wo)


FF_CHUNK = 256


def _ffn_kernel(x_ref, st_ref, nw_ref, wup_ref, cw_ref, cb_ref, wdn_ref, out_ref, stout_ref,
                carry_ref, ext_ref, *, tm, shift, halo):
    i = pl.program_id(1)

    @pl.when(i == 0)
    def _():
        carry_ref[...] = st_ref[0]

    x = x_ref[...]
    h = _rms(x, nw_ref[...]).astype(BF16)
    cw = cw_ref[...]
    cb = cb_ref[...]
    n_chunks = D_FF // FF_CHUNK

    def col(c, part):
        lo = part * D_FF + c * FF_CHUNK
        return slice(lo, lo + FF_CHUNK)

    def up_proj(c):
        return [jnp.dot(h, wup_ref[:, col(c, part)], preferred_element_type=F32) for part in range(2)]

    def gated(c, ups):
        ext = ext_ref.at[c % 2]
        parts = []
        for part in range(2):
            sl = col(c, part)
            es = slice(part * FF_CHUNK, (part + 1) * FF_CHUNK)
            ext[0:halo, es] = carry_ref[:, sl]
            ext[halo:halo + tm, es] = ups[part]
            u = cw[2:3, sl] * ups[part] + cb[:, sl]
            u = u + cw[1:2, sl] * ext[halo - shift:halo - shift + tm, es]
            u = u + cw[0:1, sl] * ext[halo - 2 * shift:halo - 2 * shift + tm, es]
            carry_ref[:, sl] = ext[tm:tm + halo, es]
            parts.append(u)
        return (_silu(parts[0]) * parts[1]).astype(BF16)

    acc = x
    ups = up_proj(0)
    for c in range(n_chunks):
        nxt = up_proj(c + 1) if c + 1 < n_chunks else None
        act = gated(c, ups)
        acc = acc + jnp.dot(act, wdn_ref[c * FF_CHUNK:(c + 1) * FF_CHUNK, :], preferred_element_type=F32)
        ups = nxt
    out_ref[...] = acc
    stout_ref[0] = carry_ref[...]


def _ffn(x, st, nw, wup, cw, cb, wdn, groups, tm, shift, halo):
    r = x.shape[0]
    tiles = r // groups // tm
    full = lambda a: pl.BlockSpec(a.shape, lambda g, i: (0,) * a.ndim)
    return pl.pallas_call(
        functools.partial(_ffn_kernel, tm=tm, shift=shift, halo=halo),
        grid=(groups, tiles),
        in_specs=[
            pl.BlockSpec((tm, D_MODEL), lambda g, i: (g * tiles + i, 0)),
            pl.BlockSpec((1, halo, 2 * D_FF), lambda g, i: (g, 0, 0)),
            full(nw), full(wup), full(cw), full(cb), full(wdn),
        ],
        out_specs=[
            pl.BlockSpec((tm, D_MODEL), lambda g, i: (g * tiles + i, 0)),
            pl.BlockSpec((1, halo, 2 * D_FF), lambda g, i: (g, 0, 0)),
        ],
        out_shape=[jax.ShapeDtypeStruct((r, D_MODEL), F32),
                   jax.ShapeDtypeStruct((groups, halo, 2 * D_FF), F32)],
        scratch_shapes=[pltpu.VMEM((halo, 2 * D_FF), F32), pltpu.VMEM((2, halo + tm, 2 * FF_CHUNK), F32)],
        compiler_params=_cparams(2, 56),
        name="ffn",
    )(x, st, nw, wup, cw, cb, wdn)


def _sgdn_pre_kernel(qkv_ref, st_ref, ba_ref, cw_ref, prm_ref,
                     wq_ref, u_ref, kd_ref, a_ref, gl_ref, cst_ref, *, t_new, nb):
    cw = cw_ref[...]
    xp = [st_ref[i] for i in range(CONV_A - 1)] + [qkv_ref[t] for t in range(t_new)]
    for i in range(CONV_A - 1):
        cst_ref[i] = xp[t_new + i]
    prm = prm_ref[...]
    lane = lax.broadcasted_iota(jnp.int32, (nb, LANES), 1)

    ys, betas, gs = [], [], []
    for t in range(t_new):
        y = cw[0:1] * xp[t]
        for i in range(1, CONV_A):
            y = y + cw[i:i + 1] * xp[t + i]
        ys.append(_silu(y))
        bg = ba_ref[t]
        betas.append(_sigmoid(bg))
        gs.append(-jnp.exp(prm[0:1]) * _softplus(bg + prm[1:2]))
    gcs = [gs[0]]
    for t in range(1, t_new):
        gcs.append(gcs[-1] + gs[t])

    a_out = [jnp.zeros((nb, LANES), F32) for _ in range(t_new)]
    for h in range(H_A):
        q, k, v, beta, gc = [], [], [], [], []
        for t in range(t_new):
            qt = ys[t][:, h * DK_A:(h + 1) * DK_A]
            kt = ys[t][:, H_A * DK_A + h * DK_A:H_A * DK_A + (h + 1) * DK_A]
            q.append(qt * lax.rsqrt(jnp.sum(qt * qt, axis=-1, keepdims=True) + EPS) * (DK_A ** -0.5))
            k.append(kt * lax.rsqrt(jnp.sum(kt * kt, axis=-1, keepdims=True) + EPS))
            v.append(ys[t][:, 2 * H_A * DK_A + h * DV_A:2 * H_A * DK_A + (h + 1) * DV_A])
            beta.append(betas[t][:, h:h + 1])
            gc.append(gcs[t][:, H_A + h:H_A + h + 1])
        m = [[None] * t_new for _ in range(t_new)]
        for i in range(t_new):
            for jj in range(i + 1):
                dec = jnp.exp(gc[i] - gc[jj])
                if jj < i:
                    m[i][jj] = beta[i] * jnp.sum(k[i] * k[jj], axis=-1, keepdims=True) * dec
                aij = jnp.sum(q[i] * k[jj], axis=-1, keepdims=True) * dec
                a_out[i] = jnp.where(lane == h * SUBLANES + jj, aij, a_out[i])
        tm_ = [[None] * t_new for _ in range(t_new)]
        for i in range(t_new):
            for jj in range(i):
                acc = m[i][jj]
                for l in range(jj + 1, i):
                    acc = acc + m[i][l] * tm_[l][jj]
                tm_[i][jj] = -acc
        vb = [v[t] * beta[t] for t in range(t_new)]
        kbg = [k[t] * (beta[t] * jnp.exp(gc[t])) for t in range(t_new)]
        hs = slice(h * DK_A, (h + 1) * DK_A)
        for i in range(t_new):
            u = vb[i]
            w = kbg[i]
            for jj in range(i):
                u = u + tm_[i][jj] * vb[jj]
                w = w + tm_[i][jj] * kbg[jj]
            u_ref[i, :, hs] = u
            wq_ref[i, :, hs] = w
            wq_ref[t_new + i, :, hs] = q[i] * jnp.exp(gc[i])
            kd_ref[i, :, hs] = k[i] * jnp.exp(gc[t_new - 1] - gc[i])
        gl_ref[:, hs] = jnp.broadcast_to(jnp.exp(gc[t_new - 1]), (nb, DK_A))
    for i in range(t_new):
        a_ref[i] = a_out[i]


def _sgdn_pre(qkv_t, st_t, ba_t, cw, prm, t_new, nb):
    wide = H_A * DK_A
    return pl.pallas_call(
        functools.partial(_sgdn_pre_kernel, t_new=t_new, nb=nb),
        out_shape=[
            jax.ShapeDtypeStruct((2 * t_new, nb, wide), F32),
            jax.ShapeDtypeStruct((t_new, nb, wide), F32),
            jax.ShapeDtypeStruct((t_new, nb, wide), F32),
            jax.ShapeDtypeStruct((t_new, nb, LANES), F32),
            jax.ShapeDtypeStruct((nb, wide), F32),
            jax.ShapeDtypeStruct((CONV_A - 1, nb, QKV_A), F32),
        ],
        compiler_params=pltpu.CompilerParams(vmem_limit_bytes=48 * 1024 * 1024),
        name="sample_gdn_pre",
    )(qkv_t, st_t, ba_t, cw, prm)


SGDN_BB = 8


def _sgdn_state_kernel(wq_ref, u_ref, kd_ref, a_ref, gl_ref, s0_ref, o_ref, s1_ref, *, t_new):
    rows = 2 * t_new
    rid = lax.broadcasted_iota(jnp.int32, (rows, DK_A), 0)
    zpad = jnp.zeros((rows - t_new, DK_A), F32)
    for bi in range(SGDN_BB):
        amat = a_ref[:, bi, :]
        glrow = gl_ref[bi:bi + 1, :]
        for h in range(H_A):
            hs = slice(h * DK_A, (h + 1) * DK_A)
            s0 = s0_ref[bi, h]
            r = jnp.dot(wq_ref[:, bi, hs].astype(BF16), s0.astype(BF16), preferred_element_type=F32)
            u8 = jnp.concatenate([u_ref[:, bi, hs], zpad], axis=0)
            v_new = jnp.where(rid < t_new, u8 - r, 0.0)
            o = r[t_new:rows]
            for jj in range(t_new):
                col = amat[:, h * SUBLANES + jj:h * SUBLANES + jj + 1]
                o = o + col * v_new[jj:jj + 1, :]
            kd8 = jnp.concatenate([kd_ref[:, bi, hs], zpad], axis=0)
            s1 = s0 * glrow[:, hs] + lax.dot_general(kd8.astype(BF16), v_new.astype(BF16),
                                                     (((0,), (0,)), ((), ())), preferred_element_type=F32)
            s1_ref[bi, h] = s1
            o_ref[:, bi, hs] = o


def _sgdn_state(wq, u, kd, a, gl, s0_all, li, t_new, nb):
    wide = H_A * DK_A
    bb = SGDN_BB
    return pl.pallas_call(
        functools.partial(_sgdn_state_kernel, t_new=t_new),
        grid=(nb // bb,),
        in_specs=[
            pl.BlockSpec((2 * t_new, bb, wide), lambda i: (0, i, 0)),
            pl.BlockSpec((t_new, bb, wide), lambda i: (0, i, 0)),
            pl.BlockSpec((t_new, bb, wide), lambda i: (0, i, 0)),
            pl.BlockSpec((t_new, bb, LANES), lambda i: (0, i, 0)),
            pl.BlockSpec((bb, wide), lambda i: (i, 0)),
            pl.BlockSpec((None, bb, H_A, DK_A, DV_A), lambda i: (li, i, 0, 0, 0)),
        ],
        out_specs=[
            pl.BlockSpec((t_new, bb, wide), lambda i: (0, i, 0)),
            pl.BlockSpec((bb, H_A, DK_A, DV_A), lambda i: (i, 0, 0, 0)),
        ],
        out_shape=[jax.ShapeDtypeStruct((t_new, nb, wide), F32),
                   jax.ShapeDtypeStruct((nb, H_A, DK_A, DV_A), F32)],
        compiler_params=_cparams(1, 32),
        name="sample_gdn_state",
    )(wq, u, kd, a, gl, s0_all)


NEW_ROWS = 16


def _sattn_kernel(pt_ref, lam_ref, q_ref, kn_ref, vn_ref, bias_ref, *rest, n_pages, t_new, lam_init):
    del pt_ref
    k_pages = rest[:n_pages]
    v_pages = rest[n_pages:2 * n_pages]
    o_ref = rest[2 * n_pages]
    rows = 2 * t_new
    q = q_ref[0]
    bias = bias_ref[...]
    lam = _lam_value(lam_ref[...], lam_init)
    r = lax.broadcasted_iota(jnp.int32, (rows, 2 * DH_B), 0)
    c = lax.broadcasted_iota(jnp.int32, (rows, 2 * DH_B), 1)
    map_mask = (c // DH_B) == (r // t_new)
    pad = jnp.zeros((PAGE - NEW_ROWS, LANES), BF16)
    nt = (((1,), (1,)), ((), ()))

    heads = range(H_B)
    lanes = [slice(h * LANES, (h + 1) * LANES) for h in heads]
    head_rows = [pl.ds(h, PAGE, stride=H_B) for h in heads]
    qx = [jnp.where(map_mask, jnp.concatenate([q[:, lanes[h]]] * 2, axis=0), 0.0).astype(BF16) for h in heads]
    bh = [bias[h * rows:(h + 1) * rows] for h in heads]

    groups = [list(range(p, min(p + MXU_DIM // PAGE, n_pages))) for p in range(0, n_pages, MXU_DIM // PAGE)]

    def head_tile(page_refs, grp, h):
        return jnp.concatenate([page_refs[p][0, 0, head_rows[h], :] for p in grp], axis=0).astype(BF16)

    s_parts = [[] for _ in heads]
    for grp in groups:
        for h in heads:
            s = lax.dot_general(qx[h], head_tile(k_pages, grp, h), nt, preferred_element_type=F32)
            if grp[-1] == n_pages - 1:
                zeros = [jnp.zeros((rows, PAGE), F32)] * (len(grp) - 1)
                s = s + jnp.concatenate(zeros + [bh[h][:, 0:PAGE]], axis=1)
            s_parts[h].append(s)
    for h in heads:
        kn = jnp.concatenate([kn_ref[0, :, lanes[h]], pad], axis=0)
        s_parts[h].append(lax.dot_general(qx[h], kn, nt, preferred_element_type=F32) + bh[h][:, PAGE:2 * PAGE])

    m = []
    for h in heads:
        mh = s_parts[h][0].max(axis=-1, keepdims=True)
        for s in s_parts[h][1:]:
            mh = jnp.maximum(mh, s.max(axis=-1, keepdims=True))
        m.append(mh)
    l = [jnp.zeros((rows, 1), F32) for _ in heads]
    acc = [jnp.zeros((rows, DV_B), F32) for _ in heads]
    for gi in range(len(groups) + 1):
        for h in heads:
            pr = jnp.exp2(s_parts[h][gi] - m[h])
            l[h] = l[h] + jnp.sum(pr, axis=-1, keepdims=True)
            if gi < len(groups):
                vv = head_tile(v_pages, groups[gi], h)
            else:
                vv = jnp.concatenate([vn_ref[0, :, lanes[h]], pad], axis=0)
            acc[h] = acc[h] + jnp.dot(pr.astype(BF16), vv, preferred_element_type=F32)
    for h in heads:
        a = acc[h] / l[h]
        o_ref[0, :, lanes[h]] = a[0:t_new] - lam * a[t_new:rows]


def _sample_attn(page_table, lamv, q_b, kn_b, vn_b, bias, ck, cv, li, t_new, lam_init):
    nb, n_pages = page_table.shape
    page_spec = lambda p: pl.BlockSpec((1, 1, PAGE * H_B, LANES), lambda b, pt, p=p: (li, pt[b, p], 0, 0))
    tok_spec = pl.BlockSpec((1, t_new, HB_W), lambda b, pt: (b, 0, 0))
    new_spec = pl.BlockSpec((1, NEW_ROWS, HB_W), lambda b, pt: (b, 0, 0))
    grid_spec = pltpu.PrefetchScalarGridSpec(
        num_scalar_prefetch=1,
        grid=(nb,),
        in_specs=[pl.BlockSpec(lamv.shape, lambda b, pt: (0, 0)), tok_spec, new_spec, new_spec,
                  pl.BlockSpec(bias.shape, lambda b, pt: (0, 0))]
                 + [page_spec(p) for p in range(n_pages)] * 2,
        out_specs=pl.BlockSpec((1, t_new, HB_W), lambda b, pt: (b, 0, 0)),
    )
    return pl.pallas_call(
        functools.partial(_sattn_kernel, n_pages=n_pages, t_new=t_new, lam_init=lam_init),
        grid_spec=grid_spec,
        out_shape=jax.ShapeDtypeStruct((nb, t_new, HB_W), F32),
        compiler_params=_cparams(1, 40),
        name="sample_attn",
    )(page_table, lamv, q_b, kn_b, vn_b, bias, *([ck] * n_pages), *([cv] * n_pages))


def _pick(n, pref):
    return pref if n % pref == 0 else n


def _to_bmajor(a_t, t_new, nb):
    return jnp.swapaxes(a_t.reshape(t_new, nb, -1), 0, 1)


def _layer_weights(li, rel_bias, norm_mix, w_in, conv_a, a_log, dt_bias, onorm_a, qnorm_b, knorm_b,
                   lam_q1, lam_k1, lam_q2, lam_k2, subln_b, w_branch, w_o, norm_ffn, w_up, conv_f,
                   conv_f_b, w_down):
    w = w_in[li]
    o_z = QKV_A
    o_b = o_z + H_A * DV_A
    o_qb = o_b + 2 * H_A
    o_kb = o_qb + HB_W
    o_vb = o_kb + HB_W
    o_ga = o_vb + H_B * DV_B
    o_gb = o_ga + D_MODEL
    ba_cols = jnp.pad(w[:, o_b:o_qb], ((0, 0), (0, LANES - 2 * H_A)))
    w1 = jnp.concatenate([w[:, 0:o_z], w[:, o_qb:o_ga], ba_cols], axis=1).astype(BF16)
    w2 = jnp.concatenate([w[:, o_z:o_b], w[:, o_ga:]], axis=1).astype(BF16)
    prm = jnp.zeros((2, LANES), F32)
    prm = prm.at[0, H_A:2 * H_A].set(a_log[li]).at[1, H_A:2 * H_A].set(dt_bias[li])
    k_bound = 1.01 * math.sqrt(DH_B) * jnp.max(jnp.abs(knorm_b[li]))
    q_bound = 1.01 * Q_SCALE * math.sqrt(DH_B) * jnp.max(jnp.abs(qnorm_b[li]))
    rel = (rel_bias - rel_bias[NUM_BUCKETS - 1:NUM_BUCKETS]) * LOG2E
    spread = 2.0 * q_bound * k_bound + jnp.max(jnp.max(rel, axis=0) - jnp.min(rel, axis=0))
    scal = jnp.concatenate([jnp.where(spread <= SAFE_SPREAD_BITS, 1.0, 0.0).reshape(1), k_bound.reshape(1),
                            jnp.max(rel, axis=0), jnp.zeros((2,), F32)]).astype(F32)
    return dict(
        scal=scal,
        nw=norm_mix[li].reshape(1, D_MODEL), w1=w1, w2=w2,
        qw=jnp.tile(qnorm_b[li], 2 * H_B).reshape(1, HB_W),
        kw=jnp.tile(knorm_b[li], 2 * H_B).reshape(1, HB_W),
        cw_a=conv_a[li], prm=prm,
        onw=onorm_a[li].reshape(1, DV_A), sbw=subln_b[li].reshape(1, DV_B),
        lamv=jnp.stack([lam_q1[li], lam_k1[li], lam_q2[li], lam_k2[li]]),
        wb=w_branch[li].astype(BF16), wo=w_o[li].astype(BF16),
        nwf=norm_ffn[li].reshape(1, D_MODEL), wup=w_up[li].astype(BF16),
        cw_f=conv_f[li], cb_f=conv_f_b[li].reshape(1, 2 * D_FF), wdn=w_down[li].astype(BF16),
    )


def _group_ones():
    i = jnp.arange(MXU_DIM)
    return ((i[:, None] // DH_B) == (i[None, :] // DH_B)).astype(BF16)


def _prompt_layer(x, wt, bias_p, g, n_seq, seq, lam_init):
    tm = _pick(seq, 512)
    blk = bias_p.shape[1]
    qkv, ba, qn, kn, kf, vf, vb = _inproj(x, wt["nw"], wt["w1"], wt["qw"], wt["kw"], g, tm)
    o_a, s_fin, cst_a = _prompt_gdn(qkv.reshape(n_seq, seq, QKV_A), ba.reshape(n_seq, seq, LANES),
                                    wt["cw_a"], wt["prm"], n_seq, seq, _pick(seq, 256))
    o_a = o_a.reshape(n_seq * seq, H_A * DV_A)
    o_b = _prompt_attn(wt["scal"], wt["lamv"], qn, kn, vb, bias_p, n_seq, seq, blk, lam_init)
    x = _merge(x, o_a, o_b, wt["nw"], wt["w2"], wt["onw"], wt["sbw"], wt["wb"], wt["wo"], tm, 1.0 - lam_init)
    zero_st = jnp.zeros((n_seq, SUBLANES, 2 * D_FF), F32)
    x, cst_f = _ffn(x, zero_st, wt["nwf"], wt["wup"], wt["cw_f"], wt["cb_f"], wt["wdn"],
                    n_seq, _pick(seq, 256), 1, SUBLANES)
    return (x, kf.reshape(n_seq, seq, H_B, 2 * DH_B), vf.reshape(n_seq, seq, H_B, DV_B), s_fin,
            cst_a[:, SUBLANES - (CONV_A - 1):], cst_f[:, SUBLANES - (CONV_F - 1):])


def _sample_layer(x_t, wt, bias_s, g, page_table, ck, cv, li, s0_all, cst_a, cst_f, nb, t_new, lam_init):
    r = nb * t_new
    qkv, ba, qn, kn, kf, vf, vb = _inproj(x_t, wt["nw"], wt["w1"], wt["qw"], wt["kw"], g, r)
    st_a_t = jnp.swapaxes(cst_a, 0, 1)
    wq, u, kd, a, gl, cst_a_t = _sgdn_pre(qkv.reshape(t_new, nb, QKV_A), st_a_t,
                                          ba.reshape(t_new, nb, LANES), wt["cw_a"], wt["prm"], t_new, nb)
    o_a_t, s1 = _sgdn_state(wq, u, kd, a, gl, s0_all, li, t_new, nb)
    pad_new = lambda a: jnp.pad(_to_bmajor(a, t_new, nb), ((0, 0), (0, NEW_ROWS - t_new), (0, 0)))
    o_b = _sample_attn(page_table, wt["lamv"], _to_bmajor(qn, t_new, nb).astype(F32), pad_new(kn),
                       pad_new(vb), bias_s, ck, cv, li, t_new, lam_init)
    o_b_t = jnp.swapaxes(o_b, 0, 1).reshape(r, HB_W)
    x_t = _merge(x_t, o_a_t.reshape(r, H_A * DV_A), o_b_t, wt["nw"], wt["w2"], wt["onw"], wt["sbw"],
                 wt["wb"], wt["wo"], r, 1.0 - lam_init)
    halo = (CONV_F - 1) * nb
    st_f_t = jnp.swapaxes(cst_f, 0, 1).reshape(1, halo, 2 * D_FF)
    x_t, cst_f_t = _ffn(x_t, st_f_t, wt["nwf"], wt["wup"], wt["cw_f"], wt["cb_f"], wt["wdn"],
                        1, r, nb, halo)
    return (x_t, _to_bmajor(kf, t_new, nb).reshape(nb, t_new, H_B, 2 * DH_B),
            _to_bmajor(vf, t_new, nb).reshape(nb, t_new, H_B, DV_B), s1,
            jnp.swapaxes(cst_a_t, 0, 1), jnp.swapaxes(cst_f_t.reshape(CONV_F - 1, nb, 2 * D_FF), 0, 1))


def kernel(x_prompt, x_sample, cache_k, cache_v, state_delta, state_conv_a, state_conv_ffn, page_table,
           rel_bias, norm_mix, w_in, conv_a, a_log, dt_bias, onorm_a, qnorm_b, knorm_b, lam_q1, lam_k1,
           lam_q2, lam_k2, subln_b, w_branch, w_o, norm_ffn, w_up, conv_f, conv_f_b, w_down):
    n_seq, seq, _ = x_prompt.shape
    nb, t_new, _ = x_sample.shape
    depth = w_in.shape[0]
    n_pool = cache_k.shape[1]

    rb_flat = rel_bias.T.reshape(-1)
    bias_p = _prompt_bias(rb_flat, _pick(seq, 512))
    bias_s = _sample_bias(rb_flat, t_new)
    g = _group_ones()
    ck = cache_k.reshape(depth, n_pool, PAGE * H_B, 2 * DH_B)
    cv = cache_v.reshape(depth, n_pool, PAGE * H_B, DV_B)

    xp = x_prompt.reshape(n_seq * seq, D_MODEL)
    xs = jnp.swapaxes(x_sample, 0, 1).reshape(t_new * nb, D_MODEL)
    outs_p, outs_s = [], []
    for li in range(depth):
        wt = _layer_weights(li, rel_bias, norm_mix, w_in, conv_a, a_log, dt_bias, onorm_a, qnorm_b, knorm_b,
                            lam_q1, lam_k1, lam_q2, lam_k2, subln_b, w_branch, w_o, norm_ffn, w_up,
                            conv_f, conv_f_b, w_down)
        lam_init = 0.8 - 0.6 * math.exp(-0.3 * li)
        xp, *rest_p = _prompt_layer(xp, wt, bias_p, g, n_seq, seq, lam_init)
        xs, *rest_s = _sample_layer(xs, wt, bias_s, g, page_table, ck, cv, li, state_delta,
                                    state_conv_a[li], state_conv_ffn[li], nb, t_new, lam_init)
        outs_p.append(rest_p)
        outs_s.append(rest_s)

    stack = lambda outs, i: jnp.stack([o[i] for o in outs])
    y_prompt = xp.reshape(n_seq, seq, D_MODEL)
    y_sample = jnp.swapaxes(xs.reshape(t_new, nb, D_MODEL), 0, 1)
    return (y_prompt, y_sample,
            stack(outs_p, 0), stack(outs_p, 1), stack(outs_p, 2), stack(outs_p, 3), stack(outs_p, 4),
            stack(outs_s, 0), stack(outs_s, 1), stack(outs_s, 2), stack(outs_s, 3), stack(outs_s, 4))
```

```python
import functools
import math

import jax
import jax.numpy as jnp
from jax import lax
from jax.experimental import pallas as pl
from jax.experimental.pallas import tpu as pltpu

F32 = jnp.float32
BF16 = jnp.bfloat16

D_MODEL = 1024
H_A, DK_A, DV_A, CONV_A, CHUNK = 4, 128, 128, 4, 64
QKV_A = 2 * H_A * DK_A + H_A * DV_A
H_B, DH_B, DV_B = 4, 64, 128
HB_W = H_B * 2 * DH_B
PAGE = 128
D_FF, CONV_F = 2816, 3
NUM_BUCKETS, MAX_DISTANCE = 32, 128
NEG_INF = -1e30
EPS = 1e-6
ATT_SCALE = DH_B ** -0.5
LOG2E = math.log2(math.e)
Q_SCALE = ATT_SCALE * LOG2E
SAFE_SPREAD_BITS = 100.0

V7X_VMEM_BYTES = 64 * 1024 * 1024
LANES = 128
SUBLANES = 8
MXU_DIM = 256

P1_QKV, P1_Q, P1_K, P1_V, P1_BA, P1_END = 0, 1536, 2048, 2560, 3072, 3200
P2_Z, P2_GA, P2_GB, P2_END = 0, 512, 1536, 2560


def _cparams(n_axes, vmem_mib):
    return pltpu.CompilerParams(
        dimension_semantics=("arbitrary",) * n_axes,
        vmem_limit_bytes=min(vmem_mib * 1024 * 1024, V7X_VMEM_BYTES - 8 * 1024 * 1024),
    )


def _bdot(a, b):
    return jnp.dot(a.astype(BF16), b.astype(BF16), preferred_element_type=F32)


def _rms(x, w):
    return x * lax.rsqrt(jnp.mean(x * x, axis=-1, keepdims=True) + EPS) * w


def _sigmoid(x):
    return 1.0 / (1.0 + jnp.exp(-x))


def _silu(x):
    return x * _sigmoid(x)


def _softplus(x):
    return jnp.maximum(x, 0.0) + jnp.log1p(jnp.exp(-jnp.abs(x)))


def _inproj_kernel(x_ref, nw_ref, w_ref, qw_ref, kw_ref, g_ref, *refs):
    qkv_ref, ba_ref, qn_ref, kn_ref, kf_ref, vf_ref, vb_ref = refs[-7:]
    h = _rms(x_ref[...], nw_ref[...]).astype(BF16)

    def proj(lo, hi):
        return jnp.dot(h, w_ref[:, lo:hi], preferred_element_type=F32)

    tm = x_ref.shape[0]
    head_rows = [pl.ds(hd, tm, stride=H_B) for hd in range(H_B)]

    qkv_ref[...] = proj(P1_QKV, P1_Q)
    ba_ref[...] = proj(P1_BA, P1_END)
    v = proj(P1_V, P1_BA)
    for hd in range(H_B):
        vf_ref[head_rows[hd], :] = v[:, hd * DV_B:(hd + 1) * DV_B]
    vb_ref[...] = v.astype(BF16)

    g = g_ref[...]

    def qk_norm(y, w):
        sq = y * y
        hi = sq.astype(BF16)
        lo = (sq - hi.astype(F32)).astype(BF16)
        outs = []
        for c in range(HB_W // MXU_DIM):
            sl = slice(c * MXU_DIM, (c + 1) * MXU_DIM)
            ss = (jnp.dot(hi[:, sl], g, preferred_element_type=F32)
                  + jnp.dot(lo[:, sl], g, preferred_element_type=F32))
            outs.append(y[:, sl] * lax.rsqrt(ss * (1.0 / DH_B) + EPS) * w[:, sl])
        return outs

    qn = qk_norm(proj(P1_Q, P1_K), qw_ref[...])
    kn = qk_norm(proj(P1_K, P1_V), kw_ref[...])
    for c in range(HB_W // MXU_DIM):
        sl = slice(c * MXU_DIM, (c + 1) * MXU_DIM)
        qn_ref[:, sl] = (qn[c] * Q_SCALE).astype(BF16)
        kn_ref[:, sl] = kn[c].astype(BF16)
        for i in range(MXU_DIM // LANES):
            kf_ref[head_rows[c * (MXU_DIM // LANES) + i], :] = kn[c][:, i * LANES:(i + 1) * LANES]


def _inproj(x, nw, w1, qw, kw, g, tm, layer=None, depth=None, stacked=None):
    r = x.shape[0]
    row = lambda w: pl.BlockSpec((tm, w), lambda i: (i, 0))
    full = lambda a: pl.BlockSpec(a.shape, lambda i: (0,) * a.ndim)
    outs = [(1, QKV_A, F32), (1, LANES, F32), (1, HB_W, BF16), (1, HB_W, BF16), (H_B, LANES, F32),
            (H_B, LANES, F32), (1, HB_W, BF16)]
    out_specs = [pl.BlockSpec((tm * m, w), lambda i: (i, 0)) for m, w, _ in outs]
    out_shape = [jax.ShapeDtypeStruct((r * m, w), dt) for m, w, dt in outs]
    in_specs = [row(D_MODEL), full(nw), full(w1), full(qw), full(kw), full(g)]
    args = [x, nw, w1, qw, kw, g]
    aliases = {}
    if layer is not None:
        for o in (4, 5):
            m, w, dt = outs[o]
            out_specs[o] = pl.BlockSpec((None, tm * m, w), lambda i: (layer, i, 0))
            out_shape[o] = jax.ShapeDtypeStruct((depth, r * m, w), dt)
        if stacked is not None:
            in_specs += [pl.BlockSpec(memory_space=pl.ANY)] * 2
            args += list(stacked)
            aliases = {6: 4, 7: 5}
    return pl.pallas_call(
        _inproj_kernel,
        grid=(r // tm,),
        in_specs=in_specs,
        out_specs=out_specs,
        out_shape=out_shape,
        input_output_aliases=aliases,
        compiler_params=_cparams(1, 52),
        name="inproj",
    )(*args)


def _bucket_bias(d, table):
    n = jnp.maximum(d, 0)
    max_exact = NUM_BUCKETS // 2
    nf = jnp.maximum(n, 1).astype(F32)
    large = max_exact + (jnp.log(nf / max_exact) / math.log(MAX_DISTANCE / max_exact)
                         * (NUM_BUCKETS - max_exact)).astype(jnp.int32)
    large = jnp.minimum(large, NUM_BUCKETS - 1)
    bucket = jnp.where(n < max_exact, n, large)
    val = jnp.zeros(d.shape, F32)
    for b in range(NUM_BUCKETS):
        val = jnp.where(bucket == b, table(b), val)
    return val


def _prompt_bias_kernel(rb_ref, o_ref, *, blk):
    h = pl.program_id(0)
    table = lambda b: rb_ref[h * NUM_BUCKETS + b]
    far = table(NUM_BUCKETS - 1)
    i = lax.broadcasted_iota(jnp.int32, (blk, blk), 0)
    j = lax.broadcasted_iota(jnp.int32, (blk, blk), 1)
    d0 = i - j
    o_ref[0, :, 0:blk] = (_bucket_bias(d0 + blk, table) - far) * LOG2E
    o_ref[0, :, blk:2 * blk] = jnp.where(d0 >= 0, (_bucket_bias(d0, table) - far) * LOG2E, NEG_INF)


def _prompt_bias(rb_flat, blk):
    return pl.pallas_call(
        functools.partial(_prompt_bias_kernel, blk=blk),
        grid=(H_B,),
        in_specs=[pl.BlockSpec(memory_space=pltpu.SMEM)],
        out_specs=pl.BlockSpec((1, blk, 2 * blk), lambda h: (h, 0, 0)),
        out_shape=jax.ShapeDtypeStruct((H_B, blk, 2 * blk), F32),
        compiler_params=_cparams(1, 32),
        name="prompt_bias",
    )(rb_flat)


def _sample_bias_kernel(rb_ref, o_ref, *, t_new):
    rows, cols = o_ref.shape
    r = lax.broadcasted_iota(jnp.int32, (rows, cols), 0)
    c = lax.broadcasted_iota(jnp.int32, (rows, cols), 1)
    t = r % t_new
    hd = r // (2 * t_new)
    is_new = c >= PAGE
    d = jnp.where(is_new, t - (c - PAGE), t + PAGE - c)
    val = jnp.zeros((rows, cols), F32)
    for h in range(H_B):
        table = lambda b, h=h: rb_ref[h * NUM_BUCKETS + b]
        vh = _bucket_bias(d, table) - table(NUM_BUCKETS - 1)
        val = jnp.where(hd == h, vh, val)
    visible = jnp.logical_and(d >= 0, jnp.logical_or(~is_new, (c - PAGE) < t_new))
    o_ref[...] = jnp.where(visible, val * LOG2E, NEG_INF)


def _sample_bias(rb_flat, t_new):
    rows = H_B * 2 * t_new
    return pl.pallas_call(
        functools.partial(_sample_bias_kernel, t_new=t_new),
        in_specs=[pl.BlockSpec(memory_space=pltpu.SMEM)],
        out_shape=jax.ShapeDtypeStruct((rows, 2 * PAGE), F32),
        name="sample_bias",
    )(rb_flat)


def _lam_value(lv, lam_init):
    s1 = jnp.sum(lv[0:1] * lv[1:2], axis=-1, keepdims=True)
    s2 = jnp.sum(lv[2:3] * lv[3:4], axis=-1, keepdims=True)
    return jnp.exp(s1) - jnp.exp(s2) + lam_init


FAR_BLOCKS = 4


def _causal_sweep(qi, blk, bias_ref, update):
    n_far = jnp.maximum(qi - 1, 0)
    span = FAR_BLOCKS * blk

    def far(kk, carry):
        update(pl.multiple_of(kk * span, span), FAR_BLOCKS, None)
        return carry

    n_trips = n_far // FAR_BLOCKS
    lax.fori_loop(0, n_trips, far, 0)
    done = n_trips * FAR_BLOCKS
    rem = n_far - done
    size = FAR_BLOCKS // 2
    while size >= 1:
        @pl.when((rem // size) % 2 == 1)
        def _(size=size):
            start = done + (rem // (2 * size)) * (2 * size)
            update(pl.multiple_of(start * blk, blk), size, None)

        size //= 2

    @pl.when(qi >= 1)
    def _():
        update(pl.multiple_of((qi - 1) * blk, blk), 2, bias_ref[0])

    @pl.when(qi == 0)
    def _():
        update(0, 1, bias_ref[0, :, blk:2 * blk])


def _attn_kernel(scal_ref, lam_ref, q_ref, k_ref, v_ref, bias_ref, o_ref, m_ref, l_ref, acc_ref, accx_ref,
                 *, blk, lam_init):
    hd = pl.program_id(1)
    qi = pl.program_id(2)
    q = q_ref[...]
    lane = lax.broadcasted_iota(jnp.int32, (1, 2 * DH_B), 1)
    zero = jnp.zeros_like(q)
    qs = (jnp.where(lane < DH_B, q, zero), jnp.where(lane >= DH_B, q, zero))
    lam = _lam_value(lam_ref[...], lam_init)
    nt = (((1,), (1,)), ((), ()))

    @pl.when(scal_ref[0] > 0.5)
    def _():
        one_col = jnp.where(lane == 0, 1.0, 0.0).astype(BF16)
        sizes = {1, 2}
        sizes.update(2 ** e for e in range(FAR_BLOCKS.bit_length()))
        ext_k = {n: jnp.broadcast_to(one_col, (n * blk, 2 * DH_B)) for n in sizes}
        qe = []
        for mp in range(2):
            qf = qs[mp].astype(F32)
            shift = jnp.sqrt(jnp.sum(qf * qf, axis=-1, keepdims=True)) * scal_ref[1] + scal_ref[2 + hd]
            qe.append(jnp.concatenate([qs[mp], jnp.where(lane == 0, -shift, 0.0).astype(BF16)], axis=1))
        accx_ref[...] = jnp.zeros(accx_ref.shape, F32)

        def update(off, nblk, bias):
            rows = nblk * blk
            kb = jnp.concatenate([k_ref[pl.ds(off, rows), :], ext_k[nblk]], axis=1)
            vb = jnp.concatenate([v_ref[pl.ds(off, rows), :], ext_k[nblk]], axis=1)
            for mp in range(2):
                s = lax.dot_general(qe[mp], kb, nt, preferred_element_type=F32)
                if bias is not None:
                    s = s + bias
                accx_ref[mp] += jnp.dot(jnp.exp2(s).astype(BF16), vb, preferred_element_type=F32)

        _causal_sweep(qi, blk, bias_ref, update)
        a1 = accx_ref[0]
        a2 = accx_ref[1]
        o_ref[...] = (a1[:, 0:DV_B] / a1[:, DV_B:DV_B + 1]
                      - lam * (a2[:, 0:DV_B] / a2[:, DV_B:DV_B + 1]))

    @pl.when(scal_ref[0] <= 0.5)
    def _():
        m_ref[...] = jnp.full(m_ref.shape, NEG_INF, F32)
        l_ref[...] = jnp.zeros(l_ref.shape, F32)
        acc_ref[...] = jnp.zeros(acc_ref.shape, F32)

        def update(off, nblk, bias):
            for i in range(nblk):
                update_one(pl.multiple_of(off + i * blk, blk),
                           None if bias is None else bias[:, i * blk:(i + 1) * blk])

        def update_one(off, bias):
            kb = k_ref[pl.ds(off, blk), :]
            vb = v_ref[pl.ds(off, blk), :]
            for mp in range(2):
                s = lax.dot_general(qs[mp], kb, nt, preferred_element_type=F32)
                if bias is not None:
                    s = s + bias
                m_old = m_ref[mp]
                m_new = jnp.maximum(m_old, jnp.max(s, axis=-1, keepdims=True))
                alpha = jnp.exp2(m_old - m_new)
                p = jnp.exp2(s - jnp.concatenate([m_new] * (blk // LANES), axis=1))
                l_ref[mp] = alpha * l_ref[mp] + jnp.sum(p, axis=-1, keepdims=True)
                acc_ref[mp] = alpha * acc_ref[mp] + jnp.dot(p.astype(BF16), vb, preferred_element_type=F32)
                m_ref[mp] = m_new

        _causal_sweep(qi, blk, bias_ref, update)
        o_ref[...] = acc_ref[0] / l_ref[0] - lam * (acc_ref[1] / l_ref[1])


def _prompt_attn(scal, lamv, qn, kn, vb, bias, n_seq, seq, blk, lam_init):
    nq = seq // blk
    r = n_seq * seq
    return pl.pallas_call(
        functools.partial(_attn_kernel, blk=blk, lam_init=lam_init),
        grid=(n_seq, H_B, nq),
        in_specs=[
            pl.BlockSpec(memory_space=pltpu.SMEM),
            pl.BlockSpec(lamv.shape, lambda b, h, i: (0, 0)),
            pl.BlockSpec((blk, LANES), lambda b, h, i: (b * nq + i, h)),
            pl.BlockSpec((seq, LANES), lambda b, h, i: (b, h)),
            pl.BlockSpec((seq, LANES), lambda b, h, i: (b, h)),
            pl.BlockSpec((1, blk, 2 * blk), lambda b, h, i: (h, 0, 0)),
        ],
        out_specs=pl.BlockSpec((blk, LANES), lambda b, h, i: (b * nq + i, h)),
        out_shape=jax.ShapeDtypeStruct((r, HB_W), F32),
        scratch_shapes=[pltpu.VMEM((2, blk, LANES), F32), pltpu.VMEM((2, blk, LANES), F32),
                        pltpu.VMEM((2, blk, DV_B), F32), pltpu.VMEM((2, blk, 2 * DV_B), F32)],
        compiler_params=_cparams(3, 48),
        name="prompt_attn",
    )(scal, lamv, qn, kn, vb, bias)


def _tri_masks():
    i = lax.broadcasted_iota(jnp.int32, (CHUNK, CHUNK), 0)
    j = lax.broadcasted_iota(jnp.int32, (CHUNK, CHUNK), 1)
    incl = i >= j
    strict = i > j
    eye = (i == j).astype(F32)
    base = jnp.logical_and(strict, (i // SUBLANES) == (j // SUBLANES))
    levels = []
    s = SUBLANES
    while s < CHUNK:
        levels.append(jnp.logical_and((i // (2 * s)) == (j // (2 * s)), (i // s) > (j // s)))
        s *= 2
    return incl, strict, eye, base, levels


def _unit_lower_inverse(ms, eye, base, levels):
    d = [jnp.where(base, m, 0.0) for m in ms]
    d2 = [_bdot(a, a) for a in d]
    d4 = [_bdot(a, a) for a in d2]
    x = [_bdot(eye - a, eye + b) for a, b in zip(d, d2)]
    x = [_bdot(a, eye + b) for a, b in zip(x, d4)]
    for lvl in levels:
        c = [jnp.where(lvl, m, 0.0) for m in ms]
        xc = [_bdot(a, b) for a, b in zip(x, c)]
        xcx = [_bdot(a, b) for a, b in zip(xc, x)]
        x = [a - b for a, b in zip(x, xcx)]
    return x


def _gdn_kernel(qkv_ref, ba_ref, cw_ref, prm_ref, o_ref, sout_ref, cst_ref, s_ref, ext_ref, *, tb, n_seq):
    j = pl.program_id(0)
    nchunk = tb // CHUNK
    halo = SUBLANES

    @pl.when(j == 0)
    def _():
        s_ref[...] = jnp.zeros(s_ref.shape, F32)
        ext_ref[:, 0:halo, :] = jnp.zeros((n_seq, halo, QKV_A), F32)

    cw = cw_ref[...]
    prm = prm_ref[...]
    incl, strict, eye, base, levels = _tri_masks()
    tril = incl.astype(F32)
    nt = (((1,), (1,)), ((), ()))

    pairs = [(b, c) for b in range(n_seq) for c in range(nchunk)]
    ys, betas, gcs = {}, {}, {}
    for b in range(n_seq):
        x = qkv_ref[b]
        ext_ref[b, halo:halo + tb, :] = x
        y = cw[3:4] * x
        for i in range(CONV_A - 1):
            sh = CONV_A - 1 - i
            y = y + cw[i:i + 1] * ext_ref[b, halo - sh:halo - sh + tb, :]
        tail = x[tb - halo:tb, :]
        ext_ref[b, 0:halo, :] = tail
        cst_ref[b] = tail
        ys[b] = _silu(y)
        bg = ba_ref[b]
        betas[b] = _sigmoid(bg)
        g_all = -jnp.exp(prm[0:1]) * _softplus(bg + prm[1:2])
        for c in range(nchunk):
            gcs[b, c] = jnp.dot(tril, g_all[c * CHUNK:(c + 1) * CHUNK],
                                precision=lax.Precision.HIGHEST, preferred_element_type=F32)

    grp = [(b, c, h) for (b, c) in pairs for h in range(H_A)]
    q_, k_, kb_, vb_, dec_, eg_, ekd_, gl_ = [], [], [], [], [], [], [], []
    gct = {bc: gcs[bc].T for bc in pairs}
    for (b, c, h) in grp:
        rs = slice(c * CHUNK, (c + 1) * CHUNK)
        y = ys[b]
        q = y[rs, h * DK_A:(h + 1) * DK_A]
        k = y[rs, H_A * DK_A + h * DK_A:H_A * DK_A + (h + 1) * DK_A]
        v = y[rs, 2 * H_A * DK_A + h * DV_A:2 * H_A * DK_A + (h + 1) * DV_A]
        q = q * lax.rsqrt(jnp.sum(q * q, axis=-1, keepdims=True) + EPS) * (DK_A ** -0.5)
        k = k * lax.rsqrt(jnp.sum(k * k, axis=-1, keepdims=True) + EPS)
        beta = betas[b][rs, h:h + 1]
        gc = gcs[b, c]
        gcol = gc[:, H_A + h:H_A + h + 1]
        grow = gct[b, c][H_A + h:H_A + h + 1, :]
        glast = gc[CHUNK - 1:CHUNK, H_A + h:H_A + h + 1]
        dec_.append(jnp.where(incl, jnp.exp(jnp.where(incl, gcol - grow, 0.0)), 0.0))
        eg_.append(jnp.exp(gcol))
        ekd_.append(jnp.exp(glast - gcol))
        gl_.append(jnp.exp(glast))
        q_.append(q)
        k_.append(k)
        kb_.append(k * beta)
        vb_.append(v * beta)

    n = len(grp)
    kk_qk = [lax.dot_general(jnp.concatenate([kb_[g], q_[g]], axis=0).astype(BF16), k_[g].astype(BF16),
                             nt, preferred_element_type=F32) for g in range(n)]
    ms = [jnp.where(strict, kk_qk[g][0:CHUNK] * dec_[g], 0.0) for g in range(n)]
    a_ = [(kk_qk[g][CHUNK:2 * CHUNK] * dec_[g]).astype(BF16) for g in range(n)]
    t_ = _unit_lower_inverse(ms, eye, base, levels)
    uw = [_bdot(t_[g], jnp.concatenate([vb_[g], kb_[g] * eg_[g]], axis=1)) for g in range(n)]
    wq_ = [jnp.concatenate([uw[g][:, DV_A:DV_A + DK_A], q_[g] * eg_[g]], axis=0).astype(BF16)
           for g in range(n)]
    kd_ = [(k_[g] * ekd_[g]).astype(BF16) for g in range(n)]

    lanes = [(b, h) for b in range(n_seq) for h in range(H_A)]
    s = {bh: s_ref[bh[0], bh[1]] for bh in lanes}
    for c in range(nchunk):
        gi = {(b, h): grp.index((b, c, h)) for (b, h) in lanes}
        ws = {bh: jnp.dot(wq_[gi[bh]], s[bh].astype(BF16), preferred_element_type=F32) for bh in lanes}
        vn = {bh: (uw[gi[bh]][:, 0:DV_A] - ws[bh][0:CHUNK]).astype(BF16) for bh in lanes}
        av = {bh: jnp.dot(a_[gi[bh]], vn[bh], preferred_element_type=F32) for bh in lanes}
        kv = {bh: lax.dot_general(kd_[gi[bh]], vn[bh], (((0,), (0,)), ((), ())), preferred_element_type=F32)
              for bh in lanes}
        for (b, h) in lanes:
            o_ref[b, c * CHUNK:(c + 1) * CHUNK, h * DV_A:(h + 1) * DV_A] = ws[b, h][CHUNK:2 * CHUNK] + av[b, h]
            s[b, h] = s[b, h] * gl_[gi[b, h]] + kv[b, h]
    for (b, h) in lanes:
        s_ref[b, h] = s[b, h]
        sout_ref[b, h] = s[b, h]


def _prompt_gdn(qkv, ba, cw, prm, n_seq, seq, tb):
    nb = seq // tb
    return pl.pallas_call(
        functools.partial(_gdn_kernel, tb=tb, n_seq=n_seq),
        grid=(nb,),
        in_specs=[
            pl.BlockSpec((n_seq, tb, QKV_A), lambda j: (0, j, 0)),
            pl.BlockSpec((n_seq, tb, LANES), lambda j: (0, j, 0)),
            pl.BlockSpec(cw.shape, lambda j: (0, 0)),
            pl.BlockSpec(prm.shape, lambda j: (0, 0)),
        ],
        out_specs=[
            pl.BlockSpec((n_seq, tb, H_A * DV_A), lambda j: (0, j, 0)),
            pl.BlockSpec((n_seq, H_A, DK_A, DV_A), lambda j: (0, 0, 0, 0)),
            pl.BlockSpec((n_seq, SUBLANES, QKV_A), lambda j: (0, 0, 0)),
        ],
        out_shape=[
            jax.ShapeDtypeStruct((n_seq, seq, H_A * DV_A), F32),
            jax.ShapeDtypeStruct((n_seq, H_A, DK_A, DV_A), F32),
            jax.ShapeDtypeStruct((n_seq, SUBLANES, QKV_A), F32),
        ],
        scratch_shapes=[pltpu.VMEM((n_seq, H_A, DK_A, DV_A), F32),
                        pltpu.VMEM((n_seq, tb + SUBLANES, QKV_A), F32)],
        compiler_params=_cparams(1, 48),
        name="prompt_gdn",
    )(qkv, ba, cw, prm)


def _merge_kernel(x_ref, oa_ref, ob_ref, nw_ref, w2_ref, onw_ref, sbw_ref, wb_ref, wo_ref, out_ref, *, ob_scale):
    x = x_ref[...]
    h = _rms(x, nw_ref[...]).astype(BF16)
    z = jnp.dot(h, w2_ref[:, P2_Z:P2_GA], preferred_element_type=F32)
    oa = oa_ref[...]
    ob = ob_ref[...]
    onw = onw_ref[...]
    sbw = sbw_ref[...]
    oa_n, ob_n = [], []
    for hd in range(H_A):
        sl = slice(hd * DV_A, (hd + 1) * DV_A)
        oa_n.append((_rms(oa[:, sl], onw) * _silu(z[:, sl])).astype(BF16))
    for hd in range(H_B):
        sl = slice(hd * DV_B, (hd + 1) * DV_B)
        ob_n.append((_rms(ob[:, sl], sbw) * ob_scale).astype(BF16))
    pa = jnp.dot(jnp.concatenate(oa_n, axis=1), wb_ref[0], preferred_element_type=F32)
    pb = jnp.dot(jnp.concatenate(ob_n, axis=1), wb_ref[1], preferred_element_type=F32)
    ga = jnp.dot(h, w2_ref[:, P2_GA:P2_GB], preferred_element_type=F32)
    gb = jnp.dot(h, w2_ref[:, P2_GB:P2_END], preferred_element_type=F32)
    mixed = (_sigmoid(ga) * pa + _sigmoid(gb) * pb).astype(BF16)
    out_ref[...] = x + jnp.dot(mixed, wo_ref[...], preferred_element_type=F32)


def _merge(x, oa, ob, nw, w2, onw, sbw, wb, wo, tm, ob_scale):
    r = x.shape[0]
    row = lambda w: pl.BlockSpec((tm, w), lambda i: (i, 0))
    full = lambda a: pl.BlockSpec(a.shape, lambda i: (0,) * a.ndim)
    return pl.pallas_call(
        functools.partial(_merge_kernel, ob_scale=ob_scale),
        grid=(r // tm,),
        in_specs=[row(D_MODEL), row(H_A * DV_A), row(H_B * DV_B), full(nw), full(w2), full(onw), full(sbw),
                  full(wb), full(wo)],
        out_specs=row(D_MODEL),
        out_shape=jax.ShapeDtypeStruct((r, D_MODEL), F32),
        compiler_params=_cparams(1, 52),
        name="merge",
    )(x, oa, ob, nw, w2, onw, sbw, wb, wo)


FF_CHUNK = 256


def _ffn_kernel(x_ref, st_ref, nw_ref, wup_ref, cw_ref, cb_ref, wdn_ref, out_ref, stout_ref,
                carry_ref, ext_ref, *, tm, shift, halo):
    i = pl.program_id(1)

    @pl.when(i == 0)
    def _():
        carry_ref[...] = st_ref[0]

    x = x_ref[...]
    h = _rms(x, nw_ref[...]).astype(BF16)
    cw = cw_ref[...]
    cb = cb_ref[...]
    n_chunks = D_FF // FF_CHUNK

    def col(c, part):
        lo = part * D_FF + c * FF_CHUNK
        return slice(lo, lo + FF_CHUNK)

    def up_proj(c):
        return [jnp.dot(h, wup_ref[:, col(c, part)], preferred_element_type=F32) for part in range(2)]

    def gated(c, ups):
        ext = ext_ref.at[c % 2]
        parts = []
        for part in range(2):
            sl = col(c, part)
            es = slice(part * FF_CHUNK, (part + 1) * FF_CHUNK)
            ext[0:halo, es] = carry_ref[:, sl]
            ext[halo:halo + tm, es] = ups[part]
            u = cw[2:3, sl] * ups[part] + cb[:, sl]
            u = u + cw[1:2, sl] * ext[halo - shift:halo - shift + tm, es]
            u = u + cw[0:1, sl] * ext[halo - 2 * shift:halo - 2 * shift + tm, es]
            carry_ref[:, sl] = ext[tm:tm + halo, es]
            parts.append(u)
        return (_silu(parts[0]) * parts[1]).astype(BF16)

    acc = x
    ups = up_proj(0)
    for c in range(n_chunks):
        nxt = up_proj(c + 1) if c + 1 < n_chunks else None
        act = gated(c, ups)
        acc = acc + jnp.dot(act, wdn_ref[c * FF_CHUNK:(c + 1) * FF_CHUNK, :], preferred_element_type=F32)
        ups = nxt
    out_ref[...] = acc
    stout_ref[0] = carry_ref[...]


def _ffn(x, st, nw, wup, cw, cb, wdn, groups, tm, shift, halo):
    r = x.shape[0]
    tiles = r // groups // tm
    full = lambda a: pl.BlockSpec(a.shape, lambda g, i: (0,) * a.ndim)
    return pl.pallas_call(
        functools.partial(_ffn_kernel, tm=tm, shift=shift, halo=halo),
        grid=(groups, tiles),
        in_specs=[
            pl.BlockSpec((tm, D_MODEL), lambda g, i: (g * tiles + i, 0)),
            pl.BlockSpec((1, halo, 2 * D_FF), lambda g, i: (g, 0, 0)),
            full(nw), full(wup), full(cw), full(cb), full(wdn),
        ],
        out_specs=[
            pl.BlockSpec((tm, D_MODEL), lambda g, i: (g * tiles + i, 0)),
            pl.BlockSpec((1, halo, 2 * D_FF), lambda g, i: (g, 0, 0)),
        ],
        out_shape=[jax.ShapeDtypeStruct((r, D_MODEL), F32),
                   jax.ShapeDtypeStruct((groups, halo, 2 * D_FF), F32)],
        scratch_shapes=[pltpu.VMEM((halo, 2 * D_FF), F32), pltpu.VMEM((2, halo + tm, 2 * FF_CHUNK), F32)],
        compiler_params=_cparams(2, 56),
        name="ffn",
    )(x, st, nw, wup, cw, cb, wdn)


def _sgdn_pre_kernel(qkv_ref, st_ref, ba_ref, cw_ref, prm_ref,
                     wq_ref, u_ref, kd_ref, a_ref, gl_ref, cst_ref, *, t_new, nb):
    cw = cw_ref[...]
    xp = [st_ref[i] for i in range(CONV_A - 1)] + [qkv_ref[t] for t in range(t_new)]
    for i in range(CONV_A - 1):
        cst_ref[i] = xp[t_new + i]
    prm = prm_ref[...]
    lane = lax.broadcasted_iota(jnp.int32, (nb, LANES), 1)

    ys, betas, gs = [], [], []
    for t in range(t_new):
        y = cw[0:1] * xp[t]
        for i in range(1, CONV_A):
            y = y + cw[i:i + 1] * xp[t + i]
        ys.append(_silu(y))
        bg = ba_ref[t]
        betas.append(_sigmoid(bg))
        gs.append(-jnp.exp(prm[0:1]) * _softplus(bg + prm[1:2]))
    gcs = [gs[0]]
    for t in range(1, t_new):
        gcs.append(gcs[-1] + gs[t])

    a_out = [jnp.zeros((nb, LANES), F32) for _ in range(t_new)]
    for h in range(H_A):
        q, k, v, beta, gc = [], [], [], [], []
        for t in range(t_new):
            qt = ys[t][:, h * DK_A:(h + 1) * DK_A]
            kt = ys[t][:, H_A * DK_A + h * DK_A:H_A * DK_A + (h + 1) * DK_A]
            q.append(qt * lax.rsqrt(jnp.sum(qt * qt, axis=-1, keepdims=True) + EPS) * (DK_A ** -0.5))
            k.append(kt * lax.rsqrt(jnp.sum(kt * kt, axis=-1, keepdims=True) + EPS))
            v.append(ys[t][:, 2 * H_A * DK_A + h * DV_A:2 * H_A * DK_A + (h + 1) * DV_A])
            beta.append(betas[t][:, h:h + 1])
            gc.append(gcs[t][:, H_A + h:H_A + h + 1])
        m = [[None] * t_new for _ in range(t_new)]
        for i in range(t_new):
            for jj in range(i + 1):
                dec = jnp.exp(gc[i] - gc[jj])
                if jj < i:
                    m[i][jj] = beta[i] * jnp.sum(k[i] * k[jj], axis=-1, keepdims=True) * dec
                aij = jnp.sum(q[i] * k[jj], axis=-1, keepdims=True) * dec
                a_out[i] = jnp.where(lane == h * SUBLANES + jj, aij, a_out[i])
        tm_ = [[None] * t_new for _ in range(t_new)]
        for i in range(t_new):
            for jj in range(i):
                acc = m[i][jj]
                for l in range(jj + 1, i):
                    acc = acc + m[i][l] * tm_[l][jj]
                tm_[i][jj] = -acc
        vb = [v[t] * beta[t] for t in range(t_new)]
        kbg = [k[t] * (beta[t] * jnp.exp(gc[t])) for t in range(t_new)]
        hs = slice(h * DK_A, (h + 1) * DK_A)
        for i in range(t_new):
            u = vb[i]
            w = kbg[i]
            for jj in range(i):
                u = u + tm_[i][jj] * vb[jj]
                w = w + tm_[i][jj] * kbg[jj]
            u_ref[i, :, hs] = u
            wq_ref[i, :, hs] = w
            wq_ref[t_new + i, :, hs] = q[i] * jnp.exp(gc[i])
            kd_ref[i, :, hs] = k[i] * jnp.exp(gc[t_new - 1] - gc[i])
        gl_ref[:, hs] = jnp.broadcast_to(jnp.exp(gc[t_new - 1]), (nb, DK_A))
    for i in range(t_new):
        a_ref[i] = a_out[i]


def _sgdn_pre(qkv_t, st_t, ba_t, cw, prm, t_new, nb):
    wide = H_A * DK_A
    return pl.pallas_call(
        functools.partial(_sgdn_pre_kernel, t_new=t_new, nb=nb),
        out_shape=[
            jax.ShapeDtypeStruct((2 * t_new, nb, wide), F32),
            jax.ShapeDtypeStruct((t_new, nb, wide), F32),
            jax.ShapeDtypeStruct((t_new, nb, wide), F32),
            jax.ShapeDtypeStruct((t_new, nb, LANES), F32),
            jax.ShapeDtypeStruct((nb, wide), F32),
            jax.ShapeDtypeStruct((CONV_A - 1, nb, QKV_A), F32),
        ],
        compiler_params=pltpu.CompilerParams(vmem_limit_bytes=48 * 1024 * 1024),
        name="sample_gdn_pre",
    )(qkv_t, st_t, ba_t, cw, prm)


SGDN_BB = 8


def _sgdn_state_kernel(wq_ref, u_ref, kd_ref, a_ref, gl_ref, s0_ref, *refs, t_new):
    o_ref, s1_ref = refs[-2:]
    rows = 2 * t_new
    rid = lax.broadcasted_iota(jnp.int32, (rows, DK_A), 0)
    zpad = jnp.zeros((rows - t_new, DK_A), F32)
    pairs = [(bi, h) for bi in range(SGDN_BB) for h in range(H_A)]
    hs = [slice(h * DK_A, (h + 1) * DK_A) for h in range(H_A)]
    r = {(bi, h): jnp.dot(wq_ref[:, bi, hs[h]].astype(BF16), s0_ref[bi, h].astype(BF16),
                          preferred_element_type=F32) for (bi, h) in pairs}
    v_new = {}
    for (bi, h) in pairs:
        u8 = jnp.concatenate([u_ref[:, bi, hs[h]], zpad], axis=0)
        v_new[bi, h] = jnp.where(rid < t_new, u8 - r[bi, h], 0.0)
    kv = {}
    for (bi, h) in pairs:
        kd8 = jnp.concatenate([kd_ref[:, bi, hs[h]], zpad], axis=0)
        kv[bi, h] = lax.dot_general(kd8.astype(BF16), v_new[bi, h].astype(BF16), (((0,), (0,)), ((), ())),
                                    preferred_element_type=F32)
    for (bi, h) in pairs:
        amat = a_ref[:, bi, :]
        o = r[bi, h][t_new:rows]
        for jj in range(t_new):
            col = amat[:, h * SUBLANES + jj:h * SUBLANES + jj + 1]
            o = o + col * v_new[bi, h][jj:jj + 1, :]
        o_ref[:, bi, hs[h]] = o
        s1_ref[bi, h] = s0_ref[bi, h] * gl_ref[bi:bi + 1, hs[h]] + kv[bi, h]


def _sgdn_state(wq, u, kd, a, gl, s0_all, li, t_new, nb, stacked=None):
    wide = H_A * DK_A
    bb = SGDN_BB
    state_spec = pl.BlockSpec((None, bb, H_A, DK_A, DV_A), lambda i: (li, i, 0, 0, 0))
    in_specs = [
        pl.BlockSpec((2 * t_new, bb, wide), lambda i: (0, i, 0)),
        pl.BlockSpec((t_new, bb, wide), lambda i: (0, i, 0)),
        pl.BlockSpec((t_new, bb, wide), lambda i: (0, i, 0)),
        pl.BlockSpec((t_new, bb, LANES), lambda i: (0, i, 0)),
        pl.BlockSpec((bb, wide), lambda i: (i, 0)),
        state_spec,
    ]
    args = [wq, u, kd, a, gl, s0_all]
    aliases = {}
    if stacked is not None:
        in_specs.append(pl.BlockSpec(memory_space=pl.ANY))
        args.append(stacked)
        aliases = {6: 1}
    return pl.pallas_call(
        functools.partial(_sgdn_state_kernel, t_new=t_new),
        grid=(nb // bb,),
        in_specs=in_specs,
        out_specs=[pl.BlockSpec((t_new, bb, wide), lambda i: (0, i, 0)), state_spec],
        out_shape=[jax.ShapeDtypeStruct((t_new, nb, wide), F32),
                   jax.ShapeDtypeStruct(s0_all.shape, F32)],
        input_output_aliases=aliases,
        compiler_params=_cparams(1, 32),
        name="sample_gdn_state",
    )(*args)


NEW_ROWS = 16


def _sattn_kernel(pt_ref, lam_ref, q_ref, kn_ref, vn_ref, bias_ref, *rest, n_pages, t_new, lam_init):
    del pt_ref
    k_pages = rest[:n_pages]
    v_pages = rest[n_pages:2 * n_pages]
    o_ref = rest[2 * n_pages]
    rows = 2 * t_new
    q = q_ref[0]
    bias = bias_ref[...]
    lam = _lam_value(lam_ref[...], lam_init)
    r = lax.broadcasted_iota(jnp.int32, (rows, 2 * DH_B), 0)
    c = lax.broadcasted_iota(jnp.int32, (rows, 2 * DH_B), 1)
    map_mask = (c // DH_B) == (r // t_new)
    pad = jnp.zeros((PAGE - NEW_ROWS, LANES), BF16)
    nt = (((1,), (1,)), ((), ()))

    heads = range(H_B)
    lanes = [slice(h * LANES, (h + 1) * LANES) for h in heads]
    head_rows = [pl.ds(h, PAGE, stride=H_B) for h in heads]
    qx = [jnp.where(map_mask, jnp.concatenate([q[:, lanes[h]]] * 2, axis=0), 0.0).astype(BF16) for h in heads]
    bh = [bias[h * rows:(h + 1) * rows] for h in heads]

    groups = [list(range(p, min(p + MXU_DIM // PAGE, n_pages))) for p in range(0, n_pages, MXU_DIM // PAGE)]

    def head_tile(page_refs, grp, h):
        return jnp.concatenate([page_refs[p][0, 0, head_rows[h], :] for p in grp], axis=0).astype(BF16)

    s_parts = [[] for _ in heads]
    for grp in groups:
        for h in heads:
            s = lax.dot_general(qx[h], head_tile(k_pages, grp, h), nt, preferred_element_type=F32)
            if grp[-1] == n_pages - 1:
                zeros = [jnp.zeros((rows, PAGE), F32)] * (len(grp) - 1)
                s = s + jnp.concatenate(zeros + [bh[h][:, 0:PAGE]], axis=1)
            s_parts[h].append(s)
    for h in heads:
        kn = jnp.concatenate([kn_ref[0, :, lanes[h]], pad], axis=0)
        s_parts[h].append(lax.dot_general(qx[h], kn, nt, preferred_element_type=F32) + bh[h][:, PAGE:2 * PAGE])

    m = []
    for h in heads:
        mh = s_parts[h][0].max(axis=-1, keepdims=True)
        for s in s_parts[h][1:]:
            mh = jnp.maximum(mh, s.max(axis=-1, keepdims=True))
        m.append(mh)
    l = [jnp.zeros((rows, 1), F32) for _ in heads]
    acc = [jnp.zeros((rows, DV_B), F32) for _ in heads]
    for gi in range(len(groups) + 1):
        for h in heads:
            pr = jnp.exp2(s_parts[h][gi] - m[h])
            l[h] = l[h] + jnp.sum(pr, axis=-1, keepdims=True)
            if gi < len(groups):
                vv = head_tile(v_pages, groups[gi], h)
            else:
                vv = jnp.concatenate([vn_ref[0, :, lanes[h]], pad], axis=0)
            acc[h] = acc[h] + jnp.dot(pr.astype(BF16), vv, preferred_element_type=F32)
    for h in heads:
        a = acc[h] / l[h]
        o_ref[0, :, lanes[h]] = a[0:t_new] - lam * a[t_new:rows]


def _sample_attn(page_table, lamv, q_b, kn_b, vn_b, bias, ck, cv, li, t_new, lam_init):
    nb, n_pages = page_table.shape
    page_spec = lambda p: pl.BlockSpec((1, 1, PAGE * H_B, LANES), lambda b, pt, p=p: (li, pt[b, p], 0, 0))
    tok_spec = pl.BlockSpec((1, t_new, HB_W), lambda b, pt: (b, 0, 0))
    new_spec = pl.BlockSpec((1, NEW_ROWS, HB_W), lambda b, pt: (b, 0, 0))
    grid_spec = pltpu.PrefetchScalarGridSpec(
        num_scalar_prefetch=1,
        grid=(nb,),
        in_specs=[pl.BlockSpec(lamv.shape, lambda b, pt: (0, 0)), tok_spec, new_spec, new_spec,
                  pl.BlockSpec(bias.shape, lambda b, pt: (0, 0))]
                 + [page_spec(p) for p in range(n_pages)] * 2,
        out_specs=pl.BlockSpec((1, t_new, HB_W), lambda b, pt: (b, 0, 0)),
    )
    return pl.pallas_call(
        functools.partial(_sattn_kernel, n_pages=n_pages, t_new=t_new, lam_init=lam_init),
        grid_spec=grid_spec,
        out_shape=jax.ShapeDtypeStruct((nb, t_new, HB_W), F32),
        compiler_params=_cparams(1, 40),
        name="sample_attn",
    )(page_table, lamv, q_b, kn_b, vn_b, bias, *([ck] * n_pages), *([cv] * n_pages))


def _pick(n, pref):
    return pref if n % pref == 0 else n


def _to_bmajor(a_t, t_new, nb):
    return jnp.swapaxes(a_t.reshape(t_new, nb, -1), 0, 1)


def _layer_weights(li, rel_bias, norm_mix, w_in, conv_a, a_log, dt_bias, onorm_a, qnorm_b, knorm_b,
                   lam_q1, lam_k1, lam_q2, lam_k2, subln_b, w_branch, w_o, norm_ffn, w_up, conv_f,
                   conv_f_b, w_down):
    w = w_in[li]
    o_z = QKV_A
    o_b = o_z + H_A * DV_A
    o_qb = o_b + 2 * H_A
    o_kb = o_qb + HB_W
    o_vb = o_kb + HB_W
    o_ga = o_vb + H_B * DV_B
    o_gb = o_ga + D_MODEL
    ba_cols = jnp.pad(w[:, o_b:o_qb], ((0, 0), (0, LANES - 2 * H_A)))
    w1 = jnp.concatenate([w[:, 0:o_z], w[:, o_qb:o_ga], ba_cols], axis=1).astype(BF16)
    w2 = jnp.concatenate([w[:, o_z:o_b], w[:, o_ga:]], axis=1).astype(BF16)
    prm = jnp.zeros((2, LANES), F32)
    prm = prm.at[0, H_A:2 * H_A].set(a_log[li]).at[1, H_A:2 * H_A].set(dt_bias[li])
    k_bound = 1.01 * math.sqrt(DH_B) * jnp.max(jnp.abs(knorm_b[li]))
    q_bound = 1.01 * Q_SCALE * math.sqrt(DH_B) * jnp.max(jnp.abs(qnorm_b[li]))
    rel = (rel_bias - rel_bias[NUM_BUCKETS - 1:NUM_BUCKETS]) * LOG2E
    spread = 2.0 * q_bound * k_bound + jnp.max(jnp.max(rel, axis=0) - jnp.min(rel, axis=0))
    scal = jnp.concatenate([jnp.where(spread <= SAFE_SPREAD_BITS, 1.0, 0.0).reshape(1), k_bound.reshape(1),
                            jnp.max(rel, axis=0), jnp.zeros((2,), F32)]).astype(F32)
    return dict(
        scal=scal,
        nw=norm_mix[li].reshape(1, D_MODEL), w1=w1, w2=w2,
        qw=jnp.tile(qnorm_b[li], 2 * H_B).reshape(1, HB_W),
        kw=jnp.tile(knorm_b[li], 2 * H_B).reshape(1, HB_W),
        cw_a=conv_a[li], prm=prm,
        onw=onorm_a[li].reshape(1, DV_A), sbw=subln_b[li].reshape(1, DV_B),
        lamv=jnp.stack([lam_q1[li], lam_k1[li], lam_q2[li], lam_k2[li]]),
        wb=w_branch[li].astype(BF16), wo=w_o[li].astype(BF16),
        nwf=norm_ffn[li].reshape(1, D_MODEL), wup=w_up[li].astype(BF16),
        cw_f=conv_f[li], cb_f=conv_f_b[li].reshape(1, 2 * D_FF), wdn=w_down[li].astype(BF16),
    )


def _group_ones():
    i = jnp.arange(MXU_DIM)
    return ((i[:, None] // DH_B) == (i[None, :] // DH_B)).astype(BF16)


def _prompt_layer(x, wt, bias_p, g, n_seq, seq, lam_init, li, depth, kv_stacked):
    tm = _pick(seq, 512)
    blk = bias_p.shape[1]
    qkv, ba, qn, kn, kf, vf, vb = _inproj(x, wt["nw"], wt["w1"], wt["qw"], wt["kw"], g, tm,
                                          layer=li, depth=depth, stacked=kv_stacked)
    o_a, s_fin, cst_a = _prompt_gdn(qkv.reshape(n_seq, seq, QKV_A), ba.reshape(n_seq, seq, LANES),
                                    wt["cw_a"], wt["prm"], n_seq, seq, _pick(seq, 256))
    o_a = o_a.reshape(n_seq * seq, H_A * DV_A)
    o_b = _prompt_attn(wt["scal"], wt["lamv"], qn, kn, vb, bias_p, n_seq, seq, blk, lam_init)
    x = _merge(x, o_a, o_b, wt["nw"], wt["w2"], wt["onw"], wt["sbw"], wt["wb"], wt["wo"], tm, 1.0 - lam_init)
    zero_st = jnp.zeros((n_seq, SUBLANES, 2 * D_FF), F32)
    x, cst_f = _ffn(x, zero_st, wt["nwf"], wt["wup"], wt["cw_f"], wt["cb_f"], wt["wdn"],
                    n_seq, _pick(seq, 256), 1, SUBLANES)
    return (x, kf, vf, s_fin, cst_a[:, SUBLANES - (CONV_A - 1):], cst_f[:, SUBLANES - (CONV_F - 1):])


def _sample_layer(x_t, wt, bias_s, g, page_table, ck, cv, li, s0_all, s1_stacked, cst_a, cst_f, nb, t_new,
                  lam_init):
    r = nb * t_new
    qkv, ba, qn, kn, kf, vf, vb = _inproj(x_t, wt["nw"], wt["w1"], wt["qw"], wt["kw"], g, r)
    st_a_t = jnp.swapaxes(cst_a, 0, 1)
    wq, u, kd, a, gl, cst_a_t = _sgdn_pre(qkv.reshape(t_new, nb, QKV_A), st_a_t,
                                          ba.reshape(t_new, nb, LANES), wt["cw_a"], wt["prm"], t_new, nb)
    o_a_t, s1 = _sgdn_state(wq, u, kd, a, gl, s0_all, li, t_new, nb, stacked=s1_stacked)
    pad_new = lambda a: jnp.pad(_to_bmajor(a, t_new, nb), ((0, 0), (0, NEW_ROWS - t_new), (0, 0)))
    o_b = _sample_attn(page_table, wt["lamv"], _to_bmajor(qn, t_new, nb).astype(F32), pad_new(kn),
                       pad_new(vb), bias_s, ck, cv, li, t_new, lam_init)
    o_b_t = jnp.swapaxes(o_b, 0, 1).reshape(r, HB_W)
    x_t = _merge(x_t, o_a_t.reshape(r, H_A * DV_A), o_b_t, wt["nw"], wt["w2"], wt["onw"], wt["sbw"],
                 wt["wb"], wt["wo"], r, 1.0 - lam_init)
    halo = (CONV_F - 1) * nb
    st_f_t = jnp.swapaxes(cst_f, 0, 1).reshape(1, halo, 2 * D_FF)
    x_t, cst_f_t = _ffn(x_t, st_f_t, wt["nwf"], wt["wup"], wt["cw_f"], wt["cb_f"], wt["wdn"],
                        1, r, nb, halo)
    return (x_t, _to_bmajor(kf, t_new, nb).reshape(nb, t_new, H_B, 2 * DH_B),
            _to_bmajor(vf, t_new, nb).reshape(nb, t_new, H_B, DV_B), s1,
            jnp.swapaxes(cst_a_t, 0, 1), jnp.swapaxes(cst_f_t.reshape(CONV_F - 1, nb, 2 * D_FF), 0, 1))


def kernel(x_prompt, x_sample, cache_k, cache_v, state_delta, state_conv_a, state_conv_ffn, page_table,
           rel_bias, norm_mix, w_in, conv_a, a_log, dt_bias, onorm_a, qnorm_b, knorm_b, lam_q1, lam_k1,
           lam_q2, lam_k2, subln_b, w_branch, w_o, norm_ffn, w_up, conv_f, conv_f_b, w_down):
    n_seq, seq, _ = x_prompt.shape
    nb, t_new, _ = x_sample.shape
    depth = w_in.shape[0]
    n_pool = cache_k.shape[1]

    rb_flat = rel_bias.T.reshape(-1)
    bias_p = _prompt_bias(rb_flat, _pick(seq, 512))
    bias_s = _sample_bias(rb_flat, t_new)
    g = _group_ones()
    ck = cache_k.reshape(depth, n_pool, PAGE * H_B, 2 * DH_B)
    cv = cache_v.reshape(depth, n_pool, PAGE * H_B, DV_B)

    xp = x_prompt.reshape(n_seq * seq, D_MODEL)
    xs = jnp.swapaxes(x_sample, 0, 1).reshape(t_new * nb, D_MODEL)
    outs_p, outs_s = [], []
    kv_stacked, s1_stacked = None, None
    for li in range(depth):
        wt = _layer_weights(li, rel_bias, norm_mix, w_in, conv_a, a_log, dt_bias, onorm_a, qnorm_b, knorm_b,
                            lam_q1, lam_k1, lam_q2, lam_k2, subln_b, w_branch, w_o, norm_ffn, w_up,
                            conv_f, conv_f_b, w_down)
        lam_init = 0.8 - 0.6 * math.exp(-0.3 * li)
        xp, kf_all, vf_all, *rest_p = _prompt_layer(xp, wt, bias_p, g, n_seq, seq, lam_init, li, depth,
                                                    kv_stacked)
        kv_stacked = (kf_all, vf_all)
        xs, ks, vs, s1_stacked, *rest_s = _sample_layer(xs, wt, bias_s, g, page_table, ck, cv, li, state_delta,
                                                        s1_stacked, state_conv_a[li], state_conv_ffn[li], nb,
                                                        t_new, lam_init)
        outs_p.append(rest_p)
        outs_s.append([ks, vs] + rest_s)

    stack = lambda outs, i: jnp.stack([o[i] for o in outs])
    y_prompt = xp.reshape(n_seq, seq, D_MODEL)
    y_sample = jnp.swapaxes(xs.reshape(t_new, nb, D_MODEL), 0, 1)
    k_prompt = kv_stacked[0].reshape(depth, n_seq, seq, H_B, 2 * DH_B)
    v_prompt = kv_stacked[1].reshape(depth, n_seq, seq, H_B, DV_B)
    return (y_prompt, y_sample,
            k_prompt, v_prompt, stack(outs_p, 0), stack(outs_p, 1), stack(outs_p, 2),
            stack(outs_s, 0), stack(outs_s, 1), s1_stacked, stack(outs_s, 2), stack(outs_s, 3))
```

```python
import functools
import math

import jax
import jax.numpy as jnp
from jax import lax
from jax.experimental import pallas as pl
from jax.experimental.pallas import tpu as pltpu

F32 = jnp.float32
BF16 = jnp.bfloat16

D_MODEL = 1024
H_A, DK_A, DV_A, CONV_A, CHUNK = 4, 128, 128, 4, 64
QKV_A = 2 * H_A * DK_A + H_A * DV_A
H_B, DH_B, DV_B = 4, 64, 128
HB_W = H_B * 2 * DH_B
PAGE = 128
D_FF, CONV_F = 2816, 3
NUM_BUCKETS, MAX_DISTANCE = 32, 128
NEG_INF = -1e30
EPS = 1e-6
ATT_SCALE = DH_B ** -0.5
LOG2E = math.log2(math.e)
Q_SCALE = ATT_SCALE * LOG2E
SAFE_SPREAD_BITS = 100.0

V7X_VMEM_BYTES = 64 * 1024 * 1024
LANES = 128
SUBLANES = 8
MXU_DIM = 256

P1_QKV, P1_Q, P1_K, P1_V, P1_BA, P1_END = 0, 1536, 2048, 2560, 3072, 3200
P2_Z, P2_GA, P2_GB, P2_END = 0, 512, 1536, 2560


def _cparams(n_axes, vmem_mib):
    return pltpu.CompilerParams(
        dimension_semantics=("arbitrary",) * n_axes,
        vmem_limit_bytes=min(vmem_mib * 1024 * 1024, V7X_VMEM_BYTES - 8 * 1024 * 1024),
    )


def _bdot(a, b):
    return jnp.dot(a.astype(BF16), b.astype(BF16), preferred_element_type=F32)


def _rms(x, w):
    return x * lax.rsqrt(jnp.mean(x * x, axis=-1, keepdims=True) + EPS) * w


def _sigmoid(x):
    return 1.0 / (1.0 + jnp.exp(-x))


def _silu(x):
    return x * _sigmoid(x)


def _softplus(x):
    return jnp.maximum(x, 0.0) + jnp.log1p(jnp.exp(-jnp.abs(x)))


def _inproj_kernel(x_ref, nw_ref, w_ref, qw_ref, kw_ref, g_ref, st_ref, cw_ref, *refs, shift, halo):
    n_out = 8
    qkv_ref, ba_ref, qn_ref, kn_ref, kf_ref, vf_ref, vb_ref, stout_ref = refs[-n_out - 2:-2]
    carry_ref, ext_ref = refs[-2:]
    h = _rms(x_ref[...], nw_ref[...]).astype(BF16)

    def proj(lo, hi):
        return jnp.dot(h, w_ref[:, lo:hi], preferred_element_type=F32)

    tm = x_ref.shape[0]
    head_rows = [pl.ds(hd, tm, stride=H_B) for hd in range(H_B)]

    @pl.when(pl.program_id(1) == 0)
    def _():
        carry_ref[...] = st_ref[0]

    cw = cw_ref[...]
    g = g_ref[...]
    seg = H_A * DK_A

    def delta_front(part, raw):
        cols = slice(part * seg, (part + 1) * seg)
        ext_ref[0:halo, cols] = carry_ref[:, cols]
        ext_ref[halo:halo + tm, cols] = raw
        y = cw[CONV_A - 1:CONV_A, cols] * raw
        for i in range(CONV_A - 1):
            back = (CONV_A - 1 - i) * shift
            y = y + cw[i:i + 1, cols] * ext_ref[halo - back:halo - back + tm, cols]
        new_carry = ext_ref[tm:tm + halo, cols]
        carry_ref[:, cols] = new_carry
        stout_ref[0, :, cols] = new_carry
        y = _silu(y)
        if part == 2:
            qkv_ref[:, cols] = y
            return
        for hd in range(H_A):
            sl = slice(hd * DK_A, (hd + 1) * DK_A)
            t = y[:, sl]
            t = t * lax.rsqrt(jnp.sum(t * t, axis=-1, keepdims=True) + EPS)
            qkv_ref[:, part * seg + hd * DK_A:part * seg + (hd + 1) * DK_A] = t * (DK_A ** -0.5) if part == 0 else t

    def qk_norm(y, w):
        sq = y * y
        hi = sq.astype(BF16)
        lo = (sq - hi.astype(F32)).astype(BF16)
        outs = []
        for c in range(HB_W // MXU_DIM):
            sl = slice(c * MXU_DIM, (c + 1) * MXU_DIM)
            ss = (jnp.dot(hi[:, sl], g, preferred_element_type=F32)
                  + jnp.dot(lo[:, sl], g, preferred_element_type=F32))
            outs.append(y[:, sl] * lax.rsqrt(ss * (1.0 / DH_B) + EPS) * w[:, sl])
        return outs

    def attn_q(y):
        qn = qk_norm(y, qw_ref[...])
        for c in range(HB_W // MXU_DIM):
            qn_ref[:, c * MXU_DIM:(c + 1) * MXU_DIM] = (qn[c] * Q_SCALE).astype(BF16)

    def attn_k(y):
        kn = qk_norm(y, kw_ref[...])
        for c in range(HB_W // MXU_DIM):
            kn_ref[:, c * MXU_DIM:(c + 1) * MXU_DIM] = kn[c].astype(BF16)
            for i in range(MXU_DIM // LANES):
                kf_ref[head_rows[c * (MXU_DIM // LANES) + i], :] = kn[c][:, i * LANES:(i + 1) * LANES]

    def attn_v(v):
        for hd in range(H_B):
            vf_ref[head_rows[hd], :] = v[:, hd * DV_B:(hd + 1) * DV_B]
        vb_ref[...] = v.astype(BF16)

    def store_ba(y):
        ba_ref[...] = y

    stages = [
        ((P1_QKV, P1_QKV + seg), functools.partial(delta_front, 0)),
        ((P1_QKV + seg, P1_QKV + 2 * seg), functools.partial(delta_front, 1)),
        ((P1_QKV + 2 * seg, P1_Q), functools.partial(delta_front, 2)),
        ((P1_Q, P1_K), attn_q),
        ((P1_K, P1_V), attn_k),
        ((P1_V, P1_BA), attn_v),
        ((P1_BA, P1_END), store_ba),
    ]
    pending = None
    for cols, post in stages:
        y = proj(*cols)
        if pending is not None:
            pending[1](pending[0])
        pending = (y, post)
    pending[1](pending[0])


def _inproj(x, nw, w1, qw, kw, g, st, cw, groups, tm, shift, halo, layer=None, depth=None, stacked=None):
    r = x.shape[0]
    tiles = r // groups // tm
    row = lambda w: pl.BlockSpec((tm, w), lambda gi, i: (gi * tiles + i, 0))
    full = lambda a: pl.BlockSpec(a.shape, lambda gi, i: (0,) * a.ndim)
    st_spec = pl.BlockSpec((1, halo, QKV_A), lambda gi, i: (gi, 0, 0))
    outs = [(1, QKV_A, F32), (1, LANES, F32), (1, HB_W, BF16), (1, HB_W, BF16), (H_B, LANES, F32),
            (H_B, LANES, F32), (1, HB_W, BF16)]
    out_specs = [pl.BlockSpec((tm * m, w), lambda gi, i: (gi * tiles + i, 0)) for m, w, _ in outs]
    out_shape = [jax.ShapeDtypeStruct((r * m, w), dt) for m, w, dt in outs]
    out_specs.append(st_spec)
    out_shape.append(jax.ShapeDtypeStruct((groups, halo, QKV_A), F32))
    in_specs = [row(D_MODEL), full(nw), full(w1), full(qw), full(kw), full(g), st_spec, full(cw)]
    args = [x, nw, w1, qw, kw, g, st, cw]
    aliases = {}
    if layer is not None:
        for o in (4, 5):
            m, w, dt = outs[o]
            out_specs[o] = pl.BlockSpec((None, tm * m, w), lambda gi, i: (layer, gi * tiles + i, 0))
            out_shape[o] = jax.ShapeDtypeStruct((depth, r * m, w), dt)
        if stacked is not None:
            in_specs += [pl.BlockSpec(memory_space=pl.ANY)] * 2
            args += list(stacked)
            aliases = {len(args) - 2: 4, len(args) - 1: 5}
    return pl.pallas_call(
        functools.partial(_inproj_kernel, shift=shift, halo=halo),
        grid=(groups, tiles),
        in_specs=in_specs,
        out_specs=out_specs,
        out_shape=out_shape,
        input_output_aliases=aliases,
        scratch_shapes=[pltpu.VMEM((halo, QKV_A), F32), pltpu.VMEM((halo + tm, QKV_A), F32)],
        compiler_params=_cparams(2, 56),
        name="inproj",
    )(*args)


def _bucket_bias(d, table):
    n = jnp.maximum(d, 0)
    max_exact = NUM_BUCKETS // 2
    nf = jnp.maximum(n, 1).astype(F32)
    large = max_exact + (jnp.log(nf / max_exact) / math.log(MAX_DISTANCE / max_exact)
                         * (NUM_BUCKETS - max_exact)).astype(jnp.int32)
    large = jnp.minimum(large, NUM_BUCKETS - 1)
    bucket = jnp.where(n < max_exact, n, large)
    val = jnp.zeros(d.shape, F32)
    for b in range(NUM_BUCKETS):
        val = jnp.where(bucket == b, table(b), val)
    return val


def _prompt_bias_kernel(rb_ref, o_ref, *, blk):
    h = pl.program_id(0)
    table = lambda b: rb_ref[h * NUM_BUCKETS + b]
    far = table(NUM_BUCKETS - 1)
    i = lax.broadcasted_iota(jnp.int32, (blk, blk), 0)
    j = lax.broadcasted_iota(jnp.int32, (blk, blk), 1)
    d0 = i - j
    o_ref[0, :, 0:blk] = (_bucket_bias(d0 + blk, table) - far) * LOG2E
    o_ref[0, :, blk:2 * blk] = jnp.where(d0 >= 0, (_bucket_bias(d0, table) - far) * LOG2E, NEG_INF)


def _prompt_bias(rb_flat, blk):
    return pl.pallas_call(
        functools.partial(_prompt_bias_kernel, blk=blk),
        grid=(H_B,),
        in_specs=[pl.BlockSpec(memory_space=pltpu.SMEM)],
        out_specs=pl.BlockSpec((1, blk, 2 * blk), lambda h: (h, 0, 0)),
        out_shape=jax.ShapeDtypeStruct((H_B, blk, 2 * blk), F32),
        compiler_params=_cparams(1, 32),
        name="prompt_bias",
    )(rb_flat)


def _sample_bias_kernel(rb_ref, o_ref, *, t_new):
    rows, cols = o_ref.shape
    r = lax.broadcasted_iota(jnp.int32, (rows, cols), 0)
    c = lax.broadcasted_iota(jnp.int32, (rows, cols), 1)
    t = r % t_new
    hd = r // (2 * t_new)
    is_new = c >= PAGE
    d = jnp.where(is_new, t - (c - PAGE), t + PAGE - c)
    val = jnp.zeros((rows, cols), F32)
    for h in range(H_B):
        table = lambda b, h=h: rb_ref[h * NUM_BUCKETS + b]
        vh = _bucket_bias(d, table) - table(NUM_BUCKETS - 1)
        val = jnp.where(hd == h, vh, val)
    visible = jnp.logical_and(d >= 0, jnp.logical_or(~is_new, (c - PAGE) < t_new))
    o_ref[...] = jnp.where(visible, val * LOG2E, NEG_INF)


def _sample_bias(rb_flat, t_new):
    rows = H_B * 2 * t_new
    return pl.pallas_call(
        functools.partial(_sample_bias_kernel, t_new=t_new),
        in_specs=[pl.BlockSpec(memory_space=pltpu.SMEM)],
        out_shape=jax.ShapeDtypeStruct((rows, 2 * PAGE), F32),
        name="sample_bias",
    )(rb_flat)


def _lam_value(lv, lam_init):
    s1 = jnp.sum(lv[0:1] * lv[1:2], axis=-1, keepdims=True)
    s2 = jnp.sum(lv[2:3] * lv[3:4], axis=-1, keepdims=True)
    return jnp.exp(s1) - jnp.exp(s2) + lam_init


FAR_BLOCKS = 4


def _causal_sweep(qi, blk, bias_ref, update):
    n_far = jnp.maximum(qi - 1, 0)
    span = FAR_BLOCKS * blk

    def far(kk, carry):
        update(pl.multiple_of(kk * span, span), FAR_BLOCKS, None)
        return carry

    n_trips = n_far // FAR_BLOCKS
    lax.fori_loop(0, n_trips, far, 0)
    done = n_trips * FAR_BLOCKS
    rem = n_far - done
    size = FAR_BLOCKS // 2
    while size >= 1:
        @pl.when((rem // size) % 2 == 1)
        def _(size=size):
            start = done + (rem // (2 * size)) * (2 * size)
            update(pl.multiple_of(start * blk, blk), size, None)

        size //= 2

    @pl.when(qi >= 1)
    def _():
        update(pl.multiple_of((qi - 1) * blk, blk), 2, bias_ref[0])

    @pl.when(qi == 0)
    def _():
        update(0, 1, bias_ref[0, :, blk:2 * blk])


def _attn_kernel(scal_ref, lam_ref, q_ref, k_ref, v_ref, bias_ref, o_ref, m_ref, l_ref, acc_ref, accx_ref,
                 *, blk, lam_init):
    hd = pl.program_id(1)
    qi = pl.program_id(2)
    q = q_ref[...]
    lane = lax.broadcasted_iota(jnp.int32, (1, 2 * DH_B), 1)
    zero = jnp.zeros_like(q)
    qs = (jnp.where(lane < DH_B, q, zero), jnp.where(lane >= DH_B, q, zero))
    lam = _lam_value(lam_ref[...], lam_init)
    nt = (((1,), (1,)), ((), ()))

    @pl.when(scal_ref[0] > 0.5)
    def _():
        one_col = jnp.where(lane == 0, 1.0, 0.0).astype(BF16)
        sizes = {1, 2}
        sizes.update(2 ** e for e in range(FAR_BLOCKS.bit_length()))
        ext_k = {n: jnp.broadcast_to(one_col, (n * blk, 2 * DH_B)) for n in sizes}
        qe = []
        for mp in range(2):
            qf = qs[mp].astype(F32)
            shift = jnp.sqrt(jnp.sum(qf * qf, axis=-1, keepdims=True)) * scal_ref[1] + scal_ref[2 + hd]
            qe.append(jnp.concatenate([qs[mp], jnp.where(lane == 0, -shift, 0.0).astype(BF16)], axis=1))
        accx_ref[...] = jnp.zeros(accx_ref.shape, F32)

        def update(off, nblk, bias):
            rows = nblk * blk
            kb = jnp.concatenate([k_ref[pl.ds(off, rows), :], ext_k[nblk]], axis=1)
            vb = jnp.concatenate([v_ref[pl.ds(off, rows), :], ext_k[nblk]], axis=1)
            for mp in range(2):
                s = lax.dot_general(qe[mp], kb, nt, preferred_element_type=F32)
                if bias is not None:
                    s = s + bias
                accx_ref[mp] += jnp.dot(jnp.exp2(s).astype(BF16), vb, preferred_element_type=F32)

        _causal_sweep(qi, blk, bias_ref, update)
        a1 = accx_ref[0]
        a2 = accx_ref[1]
        o_ref[...] = (a1[:, 0:DV_B] / a1[:, DV_B:DV_B + 1]
                      - lam * (a2[:, 0:DV_B] / a2[:, DV_B:DV_B + 1]))

    @pl.when(scal_ref[0] <= 0.5)
    def _():
        m_ref[...] = jnp.full(m_ref.shape, NEG_INF, F32)
        l_ref[...] = jnp.zeros(l_ref.shape, F32)
        acc_ref[...] = jnp.zeros(acc_ref.shape, F32)

        def update(off, nblk, bias):
            for i in range(nblk):
                update_one(pl.multiple_of(off + i * blk, blk),
                           None if bias is None else bias[:, i * blk:(i + 1) * blk])

        def update_one(off, bias):
            kb = k_ref[pl.ds(off, blk), :]
            vb = v_ref[pl.ds(off, blk), :]
            for mp in range(2):
                s = lax.dot_general(qs[mp], kb, nt, preferred_element_type=F32)
                if bias is not None:
                    s = s + bias
                m_old = m_ref[mp]
                m_new = jnp.maximum(m_old, jnp.max(s, axis=-1, keepdims=True))
                alpha = jnp.exp2(m_old - m_new)
                p = jnp.exp2(s - jnp.concatenate([m_new] * (blk // LANES), axis=1))
                l_ref[mp] = alpha * l_ref[mp] + jnp.sum(p, axis=-1, keepdims=True)
                acc_ref[mp] = alpha * acc_ref[mp] + jnp.dot(p.astype(BF16), vb, preferred_element_type=F32)
                m_ref[mp] = m_new

        _causal_sweep(qi, blk, bias_ref, update)
        o_ref[...] = acc_ref[0] / l_ref[0] - lam * (acc_ref[1] / l_ref[1])


def _prompt_attn(scal, lamv, qn, kn, vb, bias, n_seq, seq, blk, lam_init):
    nq = seq // blk
    r = n_seq * seq
    return pl.pallas_call(
        functools.partial(_attn_kernel, blk=blk, lam_init=lam_init),
        grid=(n_seq, H_B, nq),
        in_specs=[
            pl.BlockSpec(memory_space=pltpu.SMEM),
            pl.BlockSpec(lamv.shape, lambda b, h, i: (0, 0)),
            pl.BlockSpec((blk, LANES), lambda b, h, i: (b * nq + i, h)),
            pl.BlockSpec((seq, LANES), lambda b, h, i: (b, h)),
            pl.BlockSpec((seq, LANES), lambda b, h, i: (b, h)),
            pl.BlockSpec((1, blk, 2 * blk), lambda b, h, i: (h, 0, 0)),
        ],
        out_specs=pl.BlockSpec((blk, LANES), lambda b, h, i: (b * nq + i, h)),
        out_shape=jax.ShapeDtypeStruct((r, HB_W), F32),
        scratch_shapes=[pltpu.VMEM((2, blk, LANES), F32), pltpu.VMEM((2, blk, LANES), F32),
                        pltpu.VMEM((2, blk, DV_B), F32), pltpu.VMEM((2, blk, 2 * DV_B), F32)],
        compiler_params=_cparams(3, 48),
        name="prompt_attn",
    )(scal, lamv, qn, kn, vb, bias)


def _tri_masks():
    i = lax.broadcasted_iota(jnp.int32, (CHUNK, CHUNK), 0)
    j = lax.broadcasted_iota(jnp.int32, (CHUNK, CHUNK), 1)
    incl = i >= j
    strict = i > j
    eye = (i == j).astype(F32)
    base = jnp.logical_and(strict, (i // SUBLANES) == (j // SUBLANES))
    levels = []
    s = SUBLANES
    while s < CHUNK:
        levels.append(jnp.logical_and((i // (2 * s)) == (j // (2 * s)), (i // s) > (j // s)))
        s *= 2
    return incl, strict, eye, base, levels


def _unit_lower_inverse(ms, eye, base, levels):
    d = [jnp.where(base, m, 0.0) for m in ms]
    d2 = [_bdot(a, a) for a in d]
    d4 = [_bdot(a, a) for a in d2]
    x = [_bdot(eye - a, eye + b) for a, b in zip(d, d2)]
    x = [_bdot(a, eye + b) for a, b in zip(x, d4)]
    for lvl in levels:
        c = [jnp.where(lvl, m, 0.0) for m in ms]
        xc = [_bdot(a, b) for a, b in zip(x, c)]
        xcx = [_bdot(a, b) for a, b in zip(xc, x)]
        x = [a - b for a, b in zip(x, xcx)]
    return x


def _gdn_kernel(qkv_ref, ba_ref, prm_ref, o_ref, sout_ref, s_ref, *, tb, n_seq):
    j = pl.program_id(0)
    nchunk = tb // CHUNK

    @pl.when(j == 0)
    def _():
        s_ref[...] = jnp.zeros(s_ref.shape, F32)

    prm = prm_ref[...]
    incl, strict, eye, base, levels = _tri_masks()
    tril = incl.astype(F32)
    nt = (((1,), (1,)), ((), ()))

    pairs = [(b, c) for b in range(n_seq) for c in range(nchunk)]
    ys, betas, gcs = {}, {}, {}
    for b in range(n_seq):
        ys[b] = qkv_ref[b]
        bg = ba_ref[b]
        betas[b] = _sigmoid(bg)
        g_all = -jnp.exp(prm[0:1]) * _softplus(bg + prm[1:2])
        for c in range(nchunk):
            gcs[b, c] = jnp.dot(tril, g_all[c * CHUNK:(c + 1) * CHUNK],
                                precision=lax.Precision.HIGHEST, preferred_element_type=F32)

    grp = [(b, c, h) for (b, c) in pairs for h in range(H_A)]
    q_, k_, kb_, vb_, dec_, eg_, ekd_, gl_ = [], [], [], [], [], [], [], []
    gct = {bc: gcs[bc].T for bc in pairs}
    for (b, c, h) in grp:
        rs = slice(c * CHUNK, (c + 1) * CHUNK)
        y = ys[b]
        q = y[rs, h * DK_A:(h + 1) * DK_A]
        k = y[rs, H_A * DK_A + h * DK_A:H_A * DK_A + (h + 1) * DK_A]
        v = y[rs, 2 * H_A * DK_A + h * DV_A:2 * H_A * DK_A + (h + 1) * DV_A]
        beta = betas[b][rs, h:h + 1]
        gc = gcs[b, c]
        gcol = gc[:, H_A + h:H_A + h + 1]
        grow = gct[b, c][H_A + h:H_A + h + 1, :]
        glast = gc[CHUNK - 1:CHUNK, H_A + h:H_A + h + 1]
        dec_.append(jnp.where(incl, jnp.exp(jnp.where(incl, gcol - grow, 0.0)), 0.0))
        eg_.append(jnp.exp(gcol))
        ekd_.append(jnp.exp(glast - gcol))
        gl_.append(jnp.exp(glast))
        q_.append(q)
        k_.append(k)
        kb_.append(k * beta)
        vb_.append(v * beta)

    n = len(grp)
    kk_qk = [lax.dot_general(jnp.concatenate([kb_[g], q_[g]], axis=0).astype(BF16), k_[g].astype(BF16),
                             nt, preferred_element_type=F32) for g in range(n)]
    ms = [jnp.where(strict, kk_qk[g][0:CHUNK] * dec_[g], 0.0) for g in range(n)]
    a_ = [(kk_qk[g][CHUNK:2 * CHUNK] * dec_[g]).astype(BF16) for g in range(n)]
    t_ = _unit_lower_inverse(ms, eye, base, levels)
    uw = [_bdot(t_[g], jnp.concatenate([vb_[g], kb_[g] * eg_[g]], axis=1)) for g in range(n)]
    wq_ = [jnp.concatenate([uw[g][:, DV_A:DV_A + DK_A], q_[g] * eg_[g]], axis=0).astype(BF16)
           for g in range(n)]
    kd_ = [(k_[g] * ekd_[g]).astype(BF16) for g in range(n)]

    lanes = [(b, h) for b in range(n_seq) for h in range(H_A)]
    s = {bh: s_ref[bh[0], bh[1]] for bh in lanes}
    for c in range(nchunk):
        gi = {(b, h): grp.index((b, c, h)) for (b, h) in lanes}
        ws = {bh: jnp.dot(wq_[gi[bh]], s[bh].astype(BF16), preferred_element_type=F32) for bh in lanes}
        vn = {bh: (uw[gi[bh]][:, 0:DV_A] - ws[bh][0:CHUNK]).astype(BF16) for bh in lanes}
        av = {bh: jnp.dot(a_[gi[bh]], vn[bh], preferred_element_type=F32) for bh in lanes}
        kv = {bh: lax.dot_general(kd_[gi[bh]], vn[bh], (((0,), (0,)), ((), ())), preferred_element_type=F32)
              for bh in lanes}
        for (b, h) in lanes:
            o_ref[b, c * CHUNK:(c + 1) * CHUNK, h * DV_A:(h + 1) * DV_A] = ws[b, h][CHUNK:2 * CHUNK] + av[b, h]
            s[b, h] = s[b, h] * gl_[gi[b, h]] + kv[b, h]
    for (b, h) in lanes:
        s_ref[b, h] = s[b, h]
        sout_ref[b, h] = s[b, h]


def _prompt_gdn(qkv, ba, prm, n_seq, seq, tb):
    nb = seq // tb
    return pl.pallas_call(
        functools.partial(_gdn_kernel, tb=tb, n_seq=n_seq),
        grid=(nb,),
        in_specs=[
            pl.BlockSpec((n_seq, tb, QKV_A), lambda j: (0, j, 0)),
            pl.BlockSpec((n_seq, tb, LANES), lambda j: (0, j, 0)),
            pl.BlockSpec(prm.shape, lambda j: (0, 0)),
        ],
        out_specs=[
            pl.BlockSpec((n_seq, tb, H_A * DV_A), lambda j: (0, j, 0)),
            pl.BlockSpec((n_seq, H_A, DK_A, DV_A), lambda j: (0, 0, 0, 0)),
        ],
        out_shape=[
            jax.ShapeDtypeStruct((n_seq, seq, H_A * DV_A), F32),
            jax.ShapeDtypeStruct((n_seq, H_A, DK_A, DV_A), F32),
        ],
        scratch_shapes=[pltpu.VMEM((n_seq, H_A, DK_A, DV_A), F32)],
        compiler_params=_cparams(1, 48),
        name="prompt_gdn",
    )(qkv, ba, prm)


def _merge_kernel(x_ref, oa_ref, ob_ref, nw_ref, w2_ref, onw_ref, sbw_ref, wb_ref, wo_ref, out_ref, *, ob_scale):
    x = x_ref[...]
    h = _rms(x, nw_ref[...]).astype(BF16)
    z = jnp.dot(h, w2_ref[:, P2_Z:P2_GA], preferred_element_type=F32)
    oa = oa_ref[...]
    ob = ob_ref[...]
    onw = onw_ref[...]
    sbw = sbw_ref[...]
    oa_n, ob_n = [], []
    for hd in range(H_A):
        sl = slice(hd * DV_A, (hd + 1) * DV_A)
        oa_n.append((_rms(oa[:, sl], onw) * _silu(z[:, sl])).astype(BF16))
    for hd in range(H_B):
        sl = slice(hd * DV_B, (hd + 1) * DV_B)
        ob_n.append((_rms(ob[:, sl], sbw) * ob_scale).astype(BF16))
    pa = jnp.dot(jnp.concatenate(oa_n, axis=1), wb_ref[0], preferred_element_type=F32)
    pb = jnp.dot(jnp.concatenate(ob_n, axis=1), wb_ref[1], preferred_element_type=F32)
    ga = jnp.dot(h, w2_ref[:, P2_GA:P2_GB], preferred_element_type=F32)
    gb = jnp.dot(h, w2_ref[:, P2_GB:P2_END], preferred_element_type=F32)
    mixed = (_sigmoid(ga) * pa + _sigmoid(gb) * pb).astype(BF16)
    out_ref[...] = x + jnp.dot(mixed, wo_ref[...], preferred_element_type=F32)


def _merge(x, oa, ob, nw, w2, onw, sbw, wb, wo, tm, ob_scale):
    r = x.shape[0]
    row = lambda w: pl.BlockSpec((tm, w), lambda i: (i, 0))
    full = lambda a: pl.BlockSpec(a.shape, lambda i: (0,) * a.ndim)
    return pl.pallas_call(
        functools.partial(_merge_kernel, ob_scale=ob_scale),
        grid=(r // tm,),
        in_specs=[row(D_MODEL), row(H_A * DV_A), row(H_B * DV_B), full(nw), full(w2), full(onw), full(sbw),
                  full(wb), full(wo)],
        out_specs=row(D_MODEL),
        out_shape=jax.ShapeDtypeStruct((r, D_MODEL), F32),
        compiler_params=_cparams(1, 52),
        name="merge",
    )(x, oa, ob, nw, w2, onw, sbw, wb, wo)


FF_CHUNK = 256
UP_AHEAD = 2
DOWN_GROUP = 4


def _ffn_kernel(x_ref, st_ref, nw_ref, wup_ref, cw_ref, cb_ref, wdn_ref, out_ref, stout_ref,
                carry_ref, ext_ref, *, tm, shift, halo):
    i = pl.program_id(1)

    @pl.when(i == 0)
    def _():
        carry_ref[...] = st_ref[0]

    x = x_ref[...]
    h = _rms(x, nw_ref[...]).astype(BF16)
    cw = cw_ref[...]
    cb = cb_ref[...]
    n_chunks = D_FF // FF_CHUNK

    def col(c, part):
        lo = part * D_FF + c * FF_CHUNK
        return slice(lo, lo + FF_CHUNK)

    def up_proj(c):
        return [jnp.dot(h, wup_ref[:, col(c, part)], preferred_element_type=F32) for part in range(2)]

    def gated(c, ups):
        ext = ext_ref.at[c % 2]
        parts = []
        for part in range(2):
            sl = col(c, part)
            es = slice(part * FF_CHUNK, (part + 1) * FF_CHUNK)
            ext[0:halo, es] = carry_ref[:, sl]
            ext[halo:halo + tm, es] = ups[part]
            u = cw[2:3, sl] * ups[part] + cb[:, sl]
            u = u + cw[1:2, sl] * ext[halo - shift:halo - shift + tm, es]
            u = u + cw[0:1, sl] * ext[halo - 2 * shift:halo - 2 * shift + tm, es]
            carry_ref[:, sl] = ext[tm:tm + halo, es]
            parts.append(u)
        return (_silu(parts[0]) * parts[1]).astype(BF16)

    acc = x
    ups, acts = {}, {}
    for s in range(n_chunks + UP_AHEAD + 1):
        if s < n_chunks:
            ups[s] = up_proj(s)
        c = s - UP_AHEAD
        if 0 <= c < n_chunks:
            acts[c] = gated(c, ups.pop(c))
        c = s - UP_AHEAD - 1
        if 0 <= c < n_chunks and (c % DOWN_GROUP == DOWN_GROUP - 1 or c == n_chunks - 1):
            c0 = c - c % DOWN_GROUP
            a = jnp.concatenate([acts.pop(i) for i in range(c0, c + 1)], axis=1)
            acc = acc + jnp.dot(a, wdn_ref[c0 * FF_CHUNK:(c + 1) * FF_CHUNK, :], preferred_element_type=F32)
    out_ref[...] = acc
    stout_ref[0] = carry_ref[...]


def _ffn(x, st, nw, wup, cw, cb, wdn, groups, tm, shift, halo):
    r = x.shape[0]
    tiles = r // groups // tm
    full = lambda a: pl.BlockSpec(a.shape, lambda g, i: (0,) * a.ndim)
    return pl.pallas_call(
        functools.partial(_ffn_kernel, tm=tm, shift=shift, halo=halo),
        grid=(groups, tiles),
        in_specs=[
            pl.BlockSpec((tm, D_MODEL), lambda g, i: (g * tiles + i, 0)),
            pl.BlockSpec((1, halo, 2 * D_FF), lambda g, i: (g, 0, 0)),
            full(nw), full(wup), full(cw), full(cb), full(wdn),
        ],
        out_specs=[
            pl.BlockSpec((tm, D_MODEL), lambda g, i: (g * tiles + i, 0)),
            pl.BlockSpec((1, halo, 2 * D_FF), lambda g, i: (g, 0, 0)),
        ],
        out_shape=[jax.ShapeDtypeStruct((r, D_MODEL), F32),
                   jax.ShapeDtypeStruct((groups, halo, 2 * D_FF), F32)],
        scratch_shapes=[pltpu.VMEM((halo, 2 * D_FF), F32), pltpu.VMEM((2, halo + tm, 2 * FF_CHUNK), F32)],
        compiler_params=_cparams(2, 56),
        name="ffn",
    )(x, st, nw, wup, cw, cb, wdn)


def _sgdn_pre_kernel(qkv_ref, ba_ref, prm_ref, wq_ref, u_ref, kd_ref, a_ref, gl_ref, *, t_new, nb):
    prm = prm_ref[...]
    lane = lax.broadcasted_iota(jnp.int32, (nb, LANES), 1)

    ys, betas, gs = [], [], []
    for t in range(t_new):
        ys.append(qkv_ref[t])
        bg = ba_ref[t]
        betas.append(_sigmoid(bg))
        gs.append(-jnp.exp(prm[0:1]) * _softplus(bg + prm[1:2]))
    gcs = [gs[0]]
    for t in range(1, t_new):
        gcs.append(gcs[-1] + gs[t])

    a_out = [jnp.zeros((nb, LANES), F32) for _ in range(t_new)]
    for h in range(H_A):
        q, k, v, beta, gc = [], [], [], [], []
        for t in range(t_new):
            q.append(ys[t][:, h * DK_A:(h + 1) * DK_A])
            k.append(ys[t][:, H_A * DK_A + h * DK_A:H_A * DK_A + (h + 1) * DK_A])
            v.append(ys[t][:, 2 * H_A * DK_A + h * DV_A:2 * H_A * DK_A + (h + 1) * DV_A])
            beta.append(betas[t][:, h:h + 1])
            gc.append(gcs[t][:, H_A + h:H_A + h + 1])
        m = [[None] * t_new for _ in range(t_new)]
        for i in range(t_new):
            for jj in range(i + 1):
                dec = jnp.exp(gc[i] - gc[jj])
                if jj < i:
                    m[i][jj] = beta[i] * jnp.sum(k[i] * k[jj], axis=-1, keepdims=True) * dec
                aij = jnp.sum(q[i] * k[jj], axis=-1, keepdims=True) * dec
                a_out[i] = jnp.where(lane == h * SUBLANES + jj, aij, a_out[i])
        tm_ = [[None] * t_new for _ in range(t_new)]
        for i in range(t_new):
            for jj in range(i):
                acc = m[i][jj]
                for l in range(jj + 1, i):
                    acc = acc + m[i][l] * tm_[l][jj]
                tm_[i][jj] = -acc
        vb = [v[t] * beta[t] for t in range(t_new)]
        kbg = [k[t] * (beta[t] * jnp.exp(gc[t])) for t in range(t_new)]
        hs = slice(h * DK_A, (h + 1) * DK_A)
        for i in range(t_new):
            u = vb[i]
            w = kbg[i]
            for jj in range(i):
                u = u + tm_[i][jj] * vb[jj]
                w = w + tm_[i][jj] * kbg[jj]
            u_ref[i, :, hs] = u
            wq_ref[i, :, hs] = w
            wq_ref[t_new + i, :, hs] = q[i] * jnp.exp(gc[i])
            kd_ref[i, :, hs] = k[i] * jnp.exp(gc[t_new - 1] - gc[i])
        gl_ref[:, hs] = jnp.broadcast_to(jnp.exp(gc[t_new - 1]), (nb, DK_A))
    for i in range(t_new):
        a_ref[i] = a_out[i]


def _sgdn_pre(qkv_t, ba_t, prm, t_new, nb):
    wide = H_A * DK_A
    return pl.pallas_call(
        functools.partial(_sgdn_pre_kernel, t_new=t_new, nb=nb),
        out_shape=[
            jax.ShapeDtypeStruct((2 * t_new, nb, wide), F32),
            jax.ShapeDtypeStruct((t_new, nb, wide), F32),
            jax.ShapeDtypeStruct((t_new, nb, wide), F32),
            jax.ShapeDtypeStruct((t_new, nb, LANES), F32),
            jax.ShapeDtypeStruct((nb, wide), F32),
        ],
        compiler_params=pltpu.CompilerParams(vmem_limit_bytes=48 * 1024 * 1024),
        name="sample_gdn_pre",
    )(qkv_t, ba_t, prm)


SGDN_BB = 8


def _sgdn_state_kernel(wq_ref, u_ref, kd_ref, a_ref, gl_ref, s0_ref, *refs, t_new):
    o_ref, s1_ref = refs[-2:]
    rows = 2 * t_new
    rid = lax.broadcasted_iota(jnp.int32, (rows, DK_A), 0)
    zpad = jnp.zeros((rows - t_new, DK_A), F32)
    pairs = [(bi, h) for bi in range(SGDN_BB) for h in range(H_A)]
    hs = [slice(h * DK_A, (h + 1) * DK_A) for h in range(H_A)]
    r = {(bi, h): jnp.dot(wq_ref[:, bi, hs[h]].astype(BF16), s0_ref[bi, h].astype(BF16),
                          preferred_element_type=F32) for (bi, h) in pairs}
    v_new = {}
    for (bi, h) in pairs:
        u8 = jnp.concatenate([u_ref[:, bi, hs[h]], zpad], axis=0)
        v_new[bi, h] = jnp.where(rid < t_new, u8 - r[bi, h], 0.0)
    kv = {}
    for (bi, h) in pairs:
        kd8 = jnp.concatenate([kd_ref[:, bi, hs[h]], zpad], axis=0)
        kv[bi, h] = lax.dot_general(kd8.astype(BF16), v_new[bi, h].astype(BF16), (((0,), (0,)), ((), ())),
                                    preferred_element_type=F32)
    for (bi, h) in pairs:
        amat = a_ref[:, bi, :]
        o = r[bi, h][t_new:rows]
        for jj in range(t_new):
            col = amat[:, h * SUBLANES + jj:h * SUBLANES + jj + 1]
            o = o + col * v_new[bi, h][jj:jj + 1, :]
        o_ref[:, bi, hs[h]] = o
        s1_ref[bi, h] = s0_ref[bi, h] * gl_ref[bi:bi + 1, hs[h]] + kv[bi, h]


def _sgdn_state(wq, u, kd, a, gl, s0_all, li, t_new, nb, stacked=None):
    wide = H_A * DK_A
    bb = SGDN_BB
    state_spec = pl.BlockSpec((None, bb, H_A, DK_A, DV_A), lambda i: (li, i, 0, 0, 0))
    in_specs = [
        pl.BlockSpec((2 * t_new, bb, wide), lambda i: (0, i, 0)),
        pl.BlockSpec((t_new, bb, wide), lambda i: (0, i, 0)),
        pl.BlockSpec((t_new, bb, wide), lambda i: (0, i, 0)),
        pl.BlockSpec((t_new, bb, LANES), lambda i: (0, i, 0)),
        pl.BlockSpec((bb, wide), lambda i: (i, 0)),
        state_spec,
    ]
    args = [wq, u, kd, a, gl, s0_all]
    aliases = {}
    if stacked is not None:
        in_specs.append(pl.BlockSpec(memory_space=pl.ANY))
        args.append(stacked)
        aliases = {6: 1}
    return pl.pallas_call(
        functools.partial(_sgdn_state_kernel, t_new=t_new),
        grid=(nb // bb,),
        in_specs=in_specs,
        out_specs=[pl.BlockSpec((t_new, bb, wide), lambda i: (0, i, 0)), state_spec],
        out_shape=[jax.ShapeDtypeStruct((t_new, nb, wide), F32),
                   jax.ShapeDtypeStruct(s0_all.shape, F32)],
        input_output_aliases=aliases,
        compiler_params=_cparams(1, 32),
        name="sample_gdn_state",
    )(*args)


NEW_ROWS = 16


def _sattn_kernel(pt_ref, lam_ref, q_ref, kn_ref, vn_ref, bias_ref, *rest, n_pages, t_new, lam_init):
    del pt_ref
    k_pages = rest[:n_pages]
    v_pages = rest[n_pages:2 * n_pages]
    o_ref = rest[2 * n_pages]
    rows = 2 * t_new
    q = q_ref[0]
    bias = bias_ref[...]
    lam = _lam_value(lam_ref[...], lam_init)
    r = lax.broadcasted_iota(jnp.int32, (rows, 2 * DH_B), 0)
    c = lax.broadcasted_iota(jnp.int32, (rows, 2 * DH_B), 1)
    map_mask = (c // DH_B) == (r // t_new)
    pad = jnp.zeros((PAGE - NEW_ROWS, LANES), BF16)
    nt = (((1,), (1,)), ((), ()))

    heads = range(H_B)
    lanes = [slice(h * LANES, (h + 1) * LANES) for h in heads]
    head_rows = [pl.ds(h, PAGE, stride=H_B) for h in heads]
    qx = [jnp.where(map_mask, jnp.concatenate([q[:, lanes[h]]] * 2, axis=0), 0.0).astype(BF16) for h in heads]
    bh = [bias[h * rows:(h + 1) * rows] for h in heads]

    groups = [list(range(p, min(p + MXU_DIM // PAGE, n_pages))) for p in range(0, n_pages, MXU_DIM // PAGE)]

    def head_tile(page_refs, grp, h):
        return jnp.concatenate([page_refs[p][0, 0, head_rows[h], :] for p in grp], axis=0).astype(BF16)

    s_parts = [[] for _ in heads]
    for grp in groups:
        for h in heads:
            s = lax.dot_general(qx[h], head_tile(k_pages, grp, h), nt, preferred_element_type=F32)
            if grp[-1] == n_pages - 1:
                zeros = [jnp.zeros((rows, PAGE), F32)] * (len(grp) - 1)
                s = s + jnp.concatenate(zeros + [bh[h][:, 0:PAGE]], axis=1)
            s_parts[h].append(s)
    for h in heads:
        kn = jnp.concatenate([kn_ref[0, :, lanes[h]], pad], axis=0)
        s_parts[h].append(lax.dot_general(qx[h], kn, nt, preferred_element_type=F32) + bh[h][:, PAGE:2 * PAGE])

    m = []
    for h in heads:
        mh = s_parts[h][0].max(axis=-1, keepdims=True)
        for s in s_parts[h][1:]:
            mh = jnp.maximum(mh, s.max(axis=-1, keepdims=True))
        m.append(mh)
    l = [jnp.zeros((rows, 1), F32) for _ in heads]
    acc = [jnp.zeros((rows, DV_B), F32) for _ in heads]
    for gi in range(len(groups) + 1):
        for h in heads:
            pr = jnp.exp2(s_parts[h][gi] - m[h])
            l[h] = l[h] + jnp.sum(pr, axis=-1, keepdims=True)
            if gi < len(groups):
                vv = head_tile(v_pages, groups[gi], h)
            else:
                vv = jnp.concatenate([vn_ref[0, :, lanes[h]], pad], axis=0)
            acc[h] = acc[h] + jnp.dot(pr.astype(BF16), vv, preferred_element_type=F32)
    for h in heads:
        a = acc[h] / l[h]
        o_ref[0, :, lanes[h]] = a[0:t_new] - lam * a[t_new:rows]


def _sample_attn(page_table, lamv, q_b, kn_b, vn_b, bias, ck, cv, li, t_new, lam_init):
    nb, n_pages = page_table.shape
    page_spec = lambda p: pl.BlockSpec((1, 1, PAGE * H_B, LANES), lambda b, pt, p=p: (li, pt[b, p], 0, 0))
    tok_spec = pl.BlockSpec((1, t_new, HB_W), lambda b, pt: (b, 0, 0))
    new_spec = pl.BlockSpec((1, NEW_ROWS, HB_W), lambda b, pt: (b, 0, 0))
    grid_spec = pltpu.PrefetchScalarGridSpec(
        num_scalar_prefetch=1,
        grid=(nb,),
        in_specs=[pl.BlockSpec(lamv.shape, lambda b, pt: (0, 0)), tok_spec, new_spec, new_spec,
                  pl.BlockSpec(bias.shape, lambda b, pt: (0, 0))]
                 + [page_spec(p) for p in range(n_pages)] * 2,
        out_specs=pl.BlockSpec((1, t_new, HB_W), lambda b, pt: (b, 0, 0)),
    )
    return pl.pallas_call(
        functools.partial(_sattn_kernel, n_pages=n_pages, t_new=t_new, lam_init=lam_init),
        grid_spec=grid_spec,
        out_shape=jax.ShapeDtypeStruct((nb, t_new, HB_W), F32),
        compiler_params=_cparams(1, 40),
        name="sample_attn",
    )(page_table, lamv, q_b, kn_b, vn_b, bias, *([ck] * n_pages), *([cv] * n_pages))


def _pick(n, pref):
    return pref if n % pref == 0 else n


def _to_bmajor(a_t, t_new, nb):
    return jnp.swapaxes(a_t.reshape(t_new, nb, -1), 0, 1)


def _layer_weights(li, rel_bias, norm_mix, w_in, conv_a, a_log, dt_bias, onorm_a, qnorm_b, knorm_b,
                   lam_q1, lam_k1, lam_q2, lam_k2, subln_b, w_branch, w_o, norm_ffn, w_up, conv_f,
                   conv_f_b, w_down):
    w = w_in[li]
    o_z = QKV_A
    o_b = o_z + H_A * DV_A
    o_qb = o_b + 2 * H_A
    o_kb = o_qb + HB_W
    o_vb = o_kb + HB_W
    o_ga = o_vb + H_B * DV_B
    o_gb = o_ga + D_MODEL
    ba_cols = jnp.pad(w[:, o_b:o_qb], ((0, 0), (0, LANES - 2 * H_A)))
    w1 = jnp.concatenate([w[:, 0:o_z], w[:, o_qb:o_ga], ba_cols], axis=1).astype(BF16)
    w2 = jnp.concatenate([w[:, o_z:o_b], w[:, o_ga:]], axis=1).astype(BF16)
    prm = jnp.zeros((2, LANES), F32)
    prm = prm.at[0, H_A:2 * H_A].set(a_log[li]).at[1, H_A:2 * H_A].set(dt_bias[li])
    k_bound = 1.01 * math.sqrt(DH_B) * jnp.max(jnp.abs(knorm_b[li]))
    q_bound = 1.01 * Q_SCALE * math.sqrt(DH_B) * jnp.max(jnp.abs(qnorm_b[li]))
    rel = (rel_bias - rel_bias[NUM_BUCKETS - 1:NUM_BUCKETS]) * LOG2E
    spread = 2.0 * q_bound * k_bound + jnp.max(jnp.max(rel, axis=0) - jnp.min(rel, axis=0))
    scal = jnp.concatenate([jnp.where(spread <= SAFE_SPREAD_BITS, 1.0, 0.0).reshape(1), k_bound.reshape(1),
                            jnp.max(rel, axis=0), jnp.zeros((2,), F32)]).astype(F32)
    return dict(
        scal=scal,
        nw=norm_mix[li].reshape(1, D_MODEL), w1=w1, w2=w2,
        qw=jnp.tile(qnorm_b[li], 2 * H_B).reshape(1, HB_W),
        kw=jnp.tile(knorm_b[li], 2 * H_B).reshape(1, HB_W),
        cw_a=conv_a[li], prm=prm,
        onw=onorm_a[li].reshape(1, DV_A), sbw=subln_b[li].reshape(1, DV_B),
        lamv=jnp.stack([lam_q1[li], lam_k1[li], lam_q2[li], lam_k2[li]]),
        wb=w_branch[li].astype(BF16), wo=w_o[li].astype(BF16),
        nwf=norm_ffn[li].reshape(1, D_MODEL), wup=w_up[li].astype(BF16),
        cw_f=conv_f[li], cb_f=conv_f_b[li].reshape(1, 2 * D_FF), wdn=w_down[li].astype(BF16),
    )


def _group_ones():
    i = jnp.arange(MXU_DIM)
    return ((i[:, None] // DH_B) == (i[None, :] // DH_B)).astype(BF16)


def _prompt_layer(x, wt, bias_p, g, n_seq, seq, lam_init, li, depth, kv_stacked):
    tm = _pick(seq, 512)
    blk = bias_p.shape[1]
    zero_st_a = jnp.zeros((n_seq, SUBLANES, QKV_A), F32)
    qkv, ba, qn, kn, kf, vf, vb, cst_a = _inproj(x, wt["nw"], wt["w1"], wt["qw"], wt["kw"], g, zero_st_a,
                                                 wt["cw_a"], n_seq, tm, 1, SUBLANES,
                                                 layer=li, depth=depth, stacked=kv_stacked)
    o_a, s_fin = _prompt_gdn(qkv.reshape(n_seq, seq, QKV_A), ba.reshape(n_seq, seq, LANES),
                             wt["prm"], n_seq, seq, _pick(seq, 256))
    o_a = o_a.reshape(n_seq * seq, H_A * DV_A)
    o_b = _prompt_attn(wt["scal"], wt["lamv"], qn, kn, vb, bias_p, n_seq, seq, blk, lam_init)
    x = _merge(x, o_a, o_b, wt["nw"], wt["w2"], wt["onw"], wt["sbw"], wt["wb"], wt["wo"], tm, 1.0 - lam_init)
    zero_st = jnp.zeros((n_seq, SUBLANES, 2 * D_FF), F32)
    x, cst_f = _ffn(x, zero_st, wt["nwf"], wt["wup"], wt["cw_f"], wt["cb_f"], wt["wdn"],
                    n_seq, _pick(seq, 512), 1, SUBLANES)
    return (x, kf, vf, s_fin, cst_a[:, SUBLANES - (CONV_A - 1):], cst_f[:, SUBLANES - (CONV_F - 1):])


def _sample_layer(x_t, wt, bias_s, g, page_table, ck, cv, li, s0_all, s1_stacked, cst_a, cst_f, nb, t_new,
                  lam_init):
    r = nb * t_new
    halo_a = (CONV_A - 1) * nb
    st_a_t = jnp.swapaxes(cst_a, 0, 1).reshape(1, halo_a, QKV_A)
    qkv, ba, qn, kn, kf, vf, vb, cst_a_t = _inproj(x_t, wt["nw"], wt["w1"], wt["qw"], wt["kw"], g, st_a_t,
                                                   wt["cw_a"], 1, r, nb, halo_a)
    cst_a_t = cst_a_t.reshape(CONV_A - 1, nb, QKV_A)
    wq, u, kd, a, gl = _sgdn_pre(qkv.reshape(t_new, nb, QKV_A), ba.reshape(t_new, nb, LANES), wt["prm"],
                                 t_new, nb)
    o_a_t, s1 = _sgdn_state(wq, u, kd, a, gl, s0_all, li, t_new, nb, stacked=s1_stacked)
    pad_new = lambda a: jnp.pad(_to_bmajor(a, t_new, nb), ((0, 0), (0, NEW_ROWS - t_new), (0, 0)))
    o_b = _sample_attn(page_table, wt["lamv"], _to_bmajor(qn, t_new, nb).astype(F32), pad_new(kn),
                       pad_new(vb), bias_s, ck, cv, li, t_new, lam_init)
    o_b_t = jnp.swapaxes(o_b, 0, 1).reshape(r, HB_W)
    x_t = _merge(x_t, o_a_t.reshape(r, H_A * DV_A), o_b_t, wt["nw"], wt["w2"], wt["onw"], wt["sbw"],
                 wt["wb"], wt["wo"], r, 1.0 - lam_init)
    halo = (CONV_F - 1) * nb
    st_f_t = jnp.swapaxes(cst_f, 0, 1).reshape(1, halo, 2 * D_FF)
    x_t, cst_f_t = _ffn(x_t, st_f_t, wt["nwf"], wt["wup"], wt["cw_f"], wt["cb_f"], wt["wdn"],
                        1, r, nb, halo)
    return (x_t, _to_bmajor(kf, t_new, nb).reshape(nb, t_new, H_B, 2 * DH_B),
            _to_bmajor(vf, t_new, nb).reshape(nb, t_new, H_B, DV_B), s1,
            jnp.swapaxes(cst_a_t, 0, 1), jnp.swapaxes(cst_f_t.reshape(CONV_F - 1, nb, 2 * D_FF), 0, 1))


def kernel(x_prompt, x_sample, cache_k, cache_v, state_delta, state_conv_a, state_conv_ffn, page_table,
           rel_bias, norm_mix, w_in, conv_a, a_log, dt_bias, onorm_a, qnorm_b, knorm_b, lam_q1, lam_k1,
           lam_q2, lam_k2, subln_b, w_branch, w_o, norm_ffn, w_up, conv_f, conv_f_b, w_down):
    n_seq, seq, _ = x_prompt.shape
    nb, t_new, _ = x_sample.shape
    depth = w_in.shape[0]
    n_pool = cache_k.shape[1]

    rb_flat = rel_bias.T.reshape(-1)
    bias_p = _prompt_bias(rb_flat, _pick(seq, 512))
    bias_s = _sample_bias(rb_flat, t_new)
    g = _group_ones()
    ck = cache_k.reshape(depth, n_pool, PAGE * H_B, 2 * DH_B)
    cv = cache_v.reshape(depth, n_pool, PAGE * H_B, DV_B)

    xp = x_prompt.reshape(n_seq * seq, D_MODEL)
    xs = jnp.swapaxes(x_sample, 0, 1).reshape(t_new * nb, D_MODEL)
    outs_p, outs_s = [], []
    kv_stacked, s1_stacked = None, None
    for li in range(depth):
        wt = _layer_weights(li, rel_bias, norm_mix, w_in, conv_a, a_log, dt_bias, onorm_a, qnorm_b, knorm_b,
                            lam_q1, lam_k1, lam_q2, lam_k2, subln_b, w_branch, w_o, norm_ffn, w_up,
                            conv_f, conv_f_b, w_down)
        lam_init = 0.8 - 0.6 * math.exp(-0.3 * li)
        xp, kf_all, vf_all, *rest_p = _prompt_layer(xp, wt, bias_p, g, n_seq, seq, lam_init, li, depth,
                                                    kv_stacked)
        kv_stacked = (kf_all, vf_all)
        xs, ks, vs, s1_stacked, *rest_s = _sample_layer(xs, wt, bias_s, g, page_table, ck, cv, li, state_delta,
                                                        s1_stacked, state_conv_a[li], state_conv_ffn[li], nb,
                                                        t_new, lam_init)
        outs_p.append(rest_p)
        outs_s.append([ks, vs] + rest_s)

    stack = lambda outs, i: jnp.stack([o[i] for o in outs])
    y_prompt = xp.reshape(n_seq, seq, D_MODEL)
    y_sample = jnp.swapaxes(xs.reshape(t_new, nb, D_MODEL), 0, 1)
    k_prompt = kv_stacked[0].reshape(depth, n_seq, seq, H_B, 2 * DH_B)
    v_prompt = kv_stacked[1].reshape(depth, n_seq, seq, H_B, DV_B)
    return (y_prompt, y_sample,
            k_prompt, v_prompt, stack(outs_p, 0), stack(outs_p, 1), stack(outs_p, 2),
            stack(outs_s, 0), stack(outs_s, 1), s1_stacked, stack(outs_s, 2), stack(outs_s, 3))
```

```python
import functools
import math

import jax
import jax.numpy as jnp
from jax import lax
from jax.experimental import pallas as pl
from jax.experimental.pallas import tpu as pltpu

F32 = jnp.float32
BF16 = jnp.bfloat16

D_MODEL = 1024
H_A, DK_A, DV_A, CONV_A, CHUNK = 4, 128, 128, 4, 64
QKV_A = 2 * H_A * DK_A + H_A * DV_A
H_B, DH_B, DV_B = 4, 64, 128
HB_W = H_B * 2 * DH_B
PAGE = 128
D_FF, CONV_F = 2816, 3
NUM_BUCKETS, MAX_DISTANCE = 32, 128
NEG_INF = -1e30
EPS = 1e-6
ATT_SCALE = DH_B ** -0.5
LOG2E = math.log2(math.e)
Q_SCALE = ATT_SCALE * LOG2E
SAFE_SPREAD_BITS = 100.0

V7X_VMEM_BYTES = 64 * 1024 * 1024
LANES = 128
SUBLANES = 8
MXU_DIM = 256

P1_QKV, P1_Q, P1_K, P1_V, P1_BA, P1_END = 0, 1536, 2048, 2560, 3072, 3200
P2_Z, P2_GA, P2_GB, P2_END = 0, 512, 1536, 2560


def _cparams(n_axes, vmem_mib):
    return pltpu.CompilerParams(
        dimension_semantics=("arbitrary",) * n_axes,
        vmem_limit_bytes=min(vmem_mib * 1024 * 1024, V7X_VMEM_BYTES - 8 * 1024 * 1024),
    )


def _bdot(a, b):
    return jnp.dot(a.astype(BF16), b.astype(BF16), preferred_element_type=F32)


def _rms(x, w):
    return x * lax.rsqrt(jnp.mean(x * x, axis=-1, keepdims=True) + EPS) * w


def _sigmoid(x):
    return 1.0 / (1.0 + jnp.exp(-x))


def _silu(x):
    return x * _sigmoid(x)


def _softplus(x):
    return jnp.maximum(x, 0.0) + jnp.log1p(jnp.exp(-jnp.abs(x)))


def _inproj_kernel(x_ref, nw_ref, w_ref, qw_ref, kw_ref, g_ref, st_ref, cw_ref, *refs, shift, halo):
    n_out = 8
    qkv_ref, ba_ref, qn_ref, kn_ref, kf_ref, vf_ref, vb_ref, stout_ref = refs[-n_out - 2:-2]
    carry_ref, ext_ref = refs[-2:]
    h = _rms(x_ref[...], nw_ref[...]).astype(BF16)

    def proj(lo, hi):
        return jnp.dot(h, w_ref[:, lo:hi], preferred_element_type=F32)

    tm = x_ref.shape[0]
    head_rows = [pl.ds(hd, tm, stride=H_B) for hd in range(H_B)]

    @pl.when(pl.program_id(1) == 0)
    def _():
        carry_ref[...] = st_ref[0]

    cw = cw_ref[...]
    g = g_ref[...]
    seg = H_A * DK_A

    def delta_front(part, raw):
        cols = slice(part * seg, (part + 1) * seg)
        ext_ref[0:halo, cols] = carry_ref[:, cols]
        ext_ref[halo:halo + tm, cols] = raw
        y = cw[CONV_A - 1:CONV_A, cols] * raw
        for i in range(CONV_A - 1):
            back = (CONV_A - 1 - i) * shift
            y = y + cw[i:i + 1, cols] * ext_ref[halo - back:halo - back + tm, cols]
        new_carry = ext_ref[tm:tm + halo, cols]
        carry_ref[:, cols] = new_carry
        stout_ref[0, :, cols] = new_carry
        y = _silu(y)
        if part == 2:
            qkv_ref[:, cols] = y
            return
        for hd in range(H_A):
            sl = slice(hd * DK_A, (hd + 1) * DK_A)
            t = y[:, sl]
            t = t * lax.rsqrt(jnp.sum(t * t, axis=-1, keepdims=True) + EPS)
            qkv_ref[:, part * seg + hd * DK_A:part * seg + (hd + 1) * DK_A] = t * (DK_A ** -0.5) if part == 0 else t

    def qk_norm(y, w):
        sq = y * y
        hi = sq.astype(BF16)
        lo = (sq - hi.astype(F32)).astype(BF16)
        outs = []
        for c in range(HB_W // MXU_DIM):
            sl = slice(c * MXU_DIM, (c + 1) * MXU_DIM)
            ss = (jnp.dot(hi[:, sl], g, preferred_element_type=F32)
                  + jnp.dot(lo[:, sl], g, preferred_element_type=F32))
            outs.append(y[:, sl] * lax.rsqrt(ss * (1.0 / DH_B) + EPS) * w[:, sl])
        return outs

    def attn_q(y):
        qn = qk_norm(y, qw_ref[...])
        for c in range(HB_W // MXU_DIM):
            qn_ref[:, c * MXU_DIM:(c + 1) * MXU_DIM] = (qn[c] * Q_SCALE).astype(BF16)

    def attn_k(y):
        kn = qk_norm(y, kw_ref[...])
        for c in range(HB_W // MXU_DIM):
            kn_ref[:, c * MXU_DIM:(c + 1) * MXU_DIM] = kn[c].astype(BF16)
            for i in range(MXU_DIM // LANES):
                kf_ref[head_rows[c * (MXU_DIM // LANES) + i], :] = kn[c][:, i * LANES:(i + 1) * LANES]

    def attn_v(v):
        for hd in range(H_B):
            vf_ref[head_rows[hd], :] = v[:, hd * DV_B:(hd + 1) * DV_B]
        vb_ref[...] = v.astype(BF16)

    def store_ba(y):
        ba_ref[...] = y

    stages = [
        ((P1_QKV, P1_QKV + seg), functools.partial(delta_front, 0)),
        ((P1_QKV + seg, P1_QKV + 2 * seg), functools.partial(delta_front, 1)),
        ((P1_QKV + 2 * seg, P1_Q), functools.partial(delta_front, 2)),
        ((P1_Q, P1_K), attn_q),
        ((P1_K, P1_V), attn_k),
        ((P1_V, P1_BA), attn_v),
        ((P1_BA, P1_END), store_ba),
    ]
    pending = None
    for cols, post in stages:
        y = proj(*cols)
        if pending is not None:
            pending[1](pending[0])
        pending = (y, post)
    pending[1](pending[0])


def _inproj(x, nw, w1, qw, kw, g, st, cw, groups, tm, shift, halo, layer=None, depth=None, stacked=None):
    r = x.shape[0]
    tiles = r // groups // tm
    row = lambda w: pl.BlockSpec((tm, w), lambda gi, i: (gi * tiles + i, 0))
    full = lambda a: pl.BlockSpec(a.shape, lambda gi, i: (0,) * a.ndim)
    st_spec = pl.BlockSpec((1, halo, QKV_A), lambda gi, i: (gi, 0, 0))
    outs = [(1, QKV_A, F32), (1, LANES, F32), (1, HB_W, BF16), (1, HB_W, BF16), (H_B, LANES, F32),
            (H_B, LANES, F32), (1, HB_W, BF16)]
    out_specs = [pl.BlockSpec((tm * m, w), lambda gi, i: (gi * tiles + i, 0)) for m, w, _ in outs]
    out_shape = [jax.ShapeDtypeStruct((r * m, w), dt) for m, w, dt in outs]
    out_specs.append(st_spec)
    out_shape.append(jax.ShapeDtypeStruct((groups, halo, QKV_A), F32))
    in_specs = [row(D_MODEL), full(nw), full(w1), full(qw), full(kw), full(g), st_spec, full(cw)]
    args = [x, nw, w1, qw, kw, g, st, cw]
    aliases = {}
    if layer is not None:
        for o in (4, 5):
            m, w, dt = outs[o]
            out_specs[o] = pl.BlockSpec((None, tm * m, w), lambda gi, i: (layer, gi * tiles + i, 0))
            out_shape[o] = jax.ShapeDtypeStruct((depth, r * m, w), dt)
        if stacked is not None:
            in_specs += [pl.BlockSpec(memory_space=pl.ANY)] * 2
            args += list(stacked)
            aliases = {len(args) - 2: 4, len(args) - 1: 5}
    return pl.pallas_call(
        functools.partial(_inproj_kernel, shift=shift, halo=halo),
        grid=(groups, tiles),
        in_specs=in_specs,
        out_specs=out_specs,
        out_shape=out_shape,
        input_output_aliases=aliases,
        scratch_shapes=[pltpu.VMEM((halo, QKV_A), F32), pltpu.VMEM((halo + tm, QKV_A), F32)],
        compiler_params=_cparams(2, 56),
        name="inproj",
    )(*args)


def _bucket_bias(d, table):
    n = jnp.maximum(d, 0)
    max_exact = NUM_BUCKETS // 2
    nf = jnp.maximum(n, 1).astype(F32)
    large = max_exact + (jnp.log(nf / max_exact) / math.log(MAX_DISTANCE / max_exact)
                         * (NUM_BUCKETS - max_exact)).astype(jnp.int32)
    large = jnp.minimum(large, NUM_BUCKETS - 1)
    bucket = jnp.where(n < max_exact, n, large)
    val = jnp.zeros(d.shape, F32)
    for b in range(NUM_BUCKETS):
        val = jnp.where(bucket == b, table(b), val)
    return val


def _prompt_bias_kernel(rb_ref, o_ref, *, blk):
    h = pl.program_id(0)
    table = lambda b: rb_ref[h * NUM_BUCKETS + b]
    far = table(NUM_BUCKETS - 1)
    i = lax.broadcasted_iota(jnp.int32, (blk, blk), 0)
    j = lax.broadcasted_iota(jnp.int32, (blk, blk), 1)
    d0 = i - j
    o_ref[0, :, 0:blk] = (_bucket_bias(d0 + blk, table) - far) * LOG2E
    o_ref[0, :, blk:2 * blk] = jnp.where(d0 >= 0, (_bucket_bias(d0, table) - far) * LOG2E, NEG_INF)


def _prompt_bias(rb_flat, blk):
    return pl.pallas_call(
        functools.partial(_prompt_bias_kernel, blk=blk),
        grid=(H_B,),
        in_specs=[pl.BlockSpec(memory_space=pltpu.SMEM)],
        out_specs=pl.BlockSpec((1, blk, 2 * blk), lambda h: (h, 0, 0)),
        out_shape=jax.ShapeDtypeStruct((H_B, blk, 2 * blk), F32),
        compiler_params=_cparams(1, 32),
        name="prompt_bias",
    )(rb_flat)


def _sample_bias_kernel(rb_ref, o_ref, *, t_new):
    rows, cols = o_ref.shape
    r = lax.broadcasted_iota(jnp.int32, (rows, cols), 0)
    c = lax.broadcasted_iota(jnp.int32, (rows, cols), 1)
    t = r % t_new
    hd = r // (2 * t_new)
    is_new = c >= PAGE
    d = jnp.where(is_new, t - (c - PAGE), t + PAGE - c)
    val = jnp.zeros((rows, cols), F32)
    for h in range(H_B):
        table = lambda b, h=h: rb_ref[h * NUM_BUCKETS + b]
        vh = _bucket_bias(d, table) - table(NUM_BUCKETS - 1)
        val = jnp.where(hd == h, vh, val)
    visible = jnp.logical_and(d >= 0, jnp.logical_or(~is_new, (c - PAGE) < t_new))
    o_ref[...] = jnp.where(visible, val * LOG2E, NEG_INF)


def _sample_bias(rb_flat, t_new):
    rows = H_B * 2 * t_new
    return pl.pallas_call(
        functools.partial(_sample_bias_kernel, t_new=t_new),
        in_specs=[pl.BlockSpec(memory_space=pltpu.SMEM)],
        out_shape=jax.ShapeDtypeStruct((rows, 2 * PAGE), F32),
        name="sample_bias",
    )(rb_flat)


def _lam_value(lv, lam_init):
    s1 = jnp.sum(lv[0:1] * lv[1:2], axis=-1, keepdims=True)
    s2 = jnp.sum(lv[2:3] * lv[3:4], axis=-1, keepdims=True)
    return jnp.exp(s1) - jnp.exp(s2) + lam_init


FAR_BLOCKS = 4


def _causal_sweep(qi, blk, bias_ref, update):
    n_far = jnp.maximum(qi - 1, 0)
    span = FAR_BLOCKS * blk

    def far(kk, carry):
        update(pl.multiple_of(kk * span, span), FAR_BLOCKS, None)
        return carry

    n_trips = n_far // FAR_BLOCKS
    lax.fori_loop(0, n_trips, far, 0)
    done = n_trips * FAR_BLOCKS
    rem = n_far - done
    size = FAR_BLOCKS // 2
    while size >= 1:
        @pl.when((rem // size) % 2 == 1)
        def _(size=size):
            start = done + (rem // (2 * size)) * (2 * size)
            update(pl.multiple_of(start * blk, blk), size, None)

        size //= 2

    @pl.when(qi >= 1)
    def _():
        update(pl.multiple_of((qi - 1) * blk, blk), 2, bias_ref[0])

    @pl.when(qi == 0)
    def _():
        update(0, 1, bias_ref[0, :, blk:2 * blk])


def _attn_kernel(scal_ref, lam_ref, q_ref, k_ref, v_ref, bias_ref, o_ref, m_ref, l_ref, acc_ref, accx_ref,
                 *, blk, nq, lam_init):
    hd = pl.program_id(1)
    lane = lax.broadcasted_iota(jnp.int32, (1, 2 * DH_B), 1)
    lam = _lam_value(lam_ref[...], lam_init)
    nt = (((1,), (1,)), ((), ()))

    def q_maps(qi):
        q = q_ref[pl.ds(pl.multiple_of(qi * blk, blk), blk), :]
        zero = jnp.zeros_like(q)
        return jnp.where(lane < DH_B, q, zero), jnp.where(lane >= DH_B, q, zero)

    def store_o(qi, val):
        o_ref[pl.ds(pl.multiple_of(qi * blk, blk), blk), :] = val

    @pl.when(scal_ref[0] > 0.5)
    def _():
        one_col = jnp.where(lane == 0, 1.0, 0.0).astype(BF16)
        sizes = {1, 2}
        sizes.update(2 ** e for e in range(FAR_BLOCKS.bit_length()))
        ext_k = {n: jnp.broadcast_to(one_col, (n * blk, 2 * DH_B)) for n in sizes}

        def q_block(qi, carry):
            qs = q_maps(qi)
            qe = []
            for mp in range(2):
                qf = qs[mp].astype(F32)
                shift = jnp.sqrt(jnp.sum(qf * qf, axis=-1, keepdims=True)) * scal_ref[1] + scal_ref[2 + hd]
                qe.append(jnp.concatenate([qs[mp], jnp.where(lane == 0, -shift, 0.0).astype(BF16)], axis=1))
            accx_ref[...] = jnp.zeros(accx_ref.shape, F32)

            def update(off, nblk, bias):
                rows = nblk * blk
                kb = jnp.concatenate([k_ref[pl.ds(off, rows), :], ext_k[nblk]], axis=1)
                vb = jnp.concatenate([v_ref[pl.ds(off, rows), :], ext_k[nblk]], axis=1)
                for mp in range(2):
                    s = lax.dot_general(qe[mp], kb, nt, preferred_element_type=F32)
                    if bias is not None:
                        s = s + bias
                    accx_ref[mp] += jnp.dot(jnp.exp2(s).astype(BF16), vb, preferred_element_type=F32)

            _causal_sweep(qi, blk, bias_ref, update)
            a1 = accx_ref[0]
            a2 = accx_ref[1]
            store_o(qi, a1[:, 0:DV_B] / a1[:, DV_B:DV_B + 1] - lam * (a2[:, 0:DV_B] / a2[:, DV_B:DV_B + 1]))
            return carry

        lax.fori_loop(0, nq, q_block, 0)

    @pl.when(scal_ref[0] <= 0.5)
    def _():
        def q_block(qi, carry):
            qs = q_maps(qi)
            m_ref[...] = jnp.full(m_ref.shape, NEG_INF, F32)
            l_ref[...] = jnp.zeros(l_ref.shape, F32)
            acc_ref[...] = jnp.zeros(acc_ref.shape, F32)

            def update(off, nblk, bias):
                for i in range(nblk):
                    update_one(pl.multiple_of(off + i * blk, blk),
                               None if bias is None else bias[:, i * blk:(i + 1) * blk])

            def update_one(off, bias):
                kb = k_ref[pl.ds(off, blk), :]
                vb = v_ref[pl.ds(off, blk), :]
                for mp in range(2):
                    s = lax.dot_general(qs[mp], kb, nt, preferred_element_type=F32)
                    if bias is not None:
                        s = s + bias
                    m_old = m_ref[mp]
                    m_new = jnp.maximum(m_old, jnp.max(s, axis=-1, keepdims=True))
                    alpha = jnp.exp2(m_old - m_new)
                    p = jnp.exp2(s - jnp.concatenate([m_new] * (blk // LANES), axis=1))
                    l_ref[mp] = alpha * l_ref[mp] + jnp.sum(p, axis=-1, keepdims=True)
                    acc_ref[mp] = alpha * acc_ref[mp] + jnp.dot(p.astype(BF16), vb, preferred_element_type=F32)
                    m_ref[mp] = m_new

            _causal_sweep(qi, blk, bias_ref, update)
            store_o(qi, acc_ref[0] / l_ref[0] - lam * (acc_ref[1] / l_ref[1]))
            return carry

        lax.fori_loop(0, nq, q_block, 0)


def _prompt_attn(scal, lamv, qn, kn, vb, bias, n_seq, seq, blk, lam_init):
    nq = seq // blk
    r = n_seq * seq
    return pl.pallas_call(
        functools.partial(_attn_kernel, blk=blk, nq=nq, lam_init=lam_init),
        grid=(n_seq, H_B),
        in_specs=[
            pl.BlockSpec(memory_space=pltpu.SMEM),
            pl.BlockSpec(lamv.shape, lambda b, h: (0, 0)),
            pl.BlockSpec((seq, LANES), lambda b, h: (b, h)),
            pl.BlockSpec((seq, LANES), lambda b, h: (b, h)),
            pl.BlockSpec((seq, LANES), lambda b, h: (b, h)),
            pl.BlockSpec((1, blk, 2 * blk), lambda b, h: (h, 0, 0)),
        ],
        out_specs=pl.BlockSpec((seq, LANES), lambda b, h: (b, h)),
        out_shape=jax.ShapeDtypeStruct((r, HB_W), F32),
        scratch_shapes=[pltpu.VMEM((2, blk, LANES), F32), pltpu.VMEM((2, blk, LANES), F32),
                        pltpu.VMEM((2, blk, DV_B), F32), pltpu.VMEM((2, blk, 2 * DV_B), F32)],
        compiler_params=_cparams(2, 48),
        name="prompt_attn",
    )(scal, lamv, qn, kn, vb, bias)


def _tri_masks():
    i = lax.broadcasted_iota(jnp.int32, (CHUNK, CHUNK), 0)
    j = lax.broadcasted_iota(jnp.int32, (CHUNK, CHUNK), 1)
    incl = i >= j
    strict = i > j
    eye = (i == j).astype(F32)
    base = jnp.logical_and(strict, (i // SUBLANES) == (j // SUBLANES))
    levels = []
    s = SUBLANES
    while s < CHUNK:
        levels.append(jnp.logical_and((i // (2 * s)) == (j // (2 * s)), (i // s) > (j // s)))
        s *= 2
    return incl, strict, eye, base, levels


def _unit_lower_inverse(ms, eye, base, levels):
    d = [jnp.where(base, m, 0.0) for m in ms]
    d2 = [_bdot(a, a) for a in d]
    d4 = [_bdot(a, a) for a in d2]
    x = [_bdot(eye - a, eye + b) for a, b in zip(d, d2)]
    x = [_bdot(a, eye + b) for a, b in zip(x, d4)]
    for lvl in levels:
        c = [jnp.where(lvl, m, 0.0) for m in ms]
        xc = [_bdot(a, b) for a, b in zip(x, c)]
        xcx = [_bdot(a, b) for a, b in zip(xc, x)]
        x = [a - b for a, b in zip(x, xcx)]
    return x


def _gdn_kernel(qkv_ref, ba_ref, prm_ref, o_ref, sout_ref, s_ref, *, tb, n_seq):
    j = pl.program_id(0)
    nchunk = tb // CHUNK

    @pl.when(j == 0)
    def _():
        s_ref[...] = jnp.zeros(s_ref.shape, F32)

    prm = prm_ref[...]
    incl, strict, eye, base, levels = _tri_masks()
    tril = incl.astype(F32)
    nt = (((1,), (1,)), ((), ()))

    pairs = [(b, c) for b in range(n_seq) for c in range(nchunk)]
    ys, betas, gcs = {}, {}, {}
    for b in range(n_seq):
        ys[b] = qkv_ref[b]
        bg = ba_ref[b]
        betas[b] = _sigmoid(bg)
        g_all = -jnp.exp(prm[0:1]) * _softplus(bg + prm[1:2])
        for c in range(nchunk):
            gcs[b, c] = jnp.dot(tril, g_all[c * CHUNK:(c + 1) * CHUNK],
                                precision=lax.Precision.HIGHEST, preferred_element_type=F32)

    grp = [(b, c, h) for (b, c) in pairs for h in range(H_A)]
    q_, k_, kb_, vb_, dec_, eg_, ekd_, gl_ = [], [], [], [], [], [], [], []
    gct = {bc: gcs[bc].T for bc in pairs}
    for (b, c, h) in grp:
        rs = slice(c * CHUNK, (c + 1) * CHUNK)
        y = ys[b]
        q = y[rs, h * DK_A:(h + 1) * DK_A]
        k = y[rs, H_A * DK_A + h * DK_A:H_A * DK_A + (h + 1) * DK_A]
        v = y[rs, 2 * H_A * DK_A + h * DV_A:2 * H_A * DK_A + (h + 1) * DV_A]
        beta = betas[b][rs, h:h + 1]
        gc = gcs[b, c]
        gcol = gc[:, H_A + h:H_A + h + 1]
        grow = gct[b, c][H_A + h:H_A + h + 1, :]
        glast = gc[CHUNK - 1:CHUNK, H_A + h:H_A + h + 1]
        dec_.append(jnp.where(incl, jnp.exp(jnp.where(incl, gcol - grow, 0.0)), 0.0))
        eg_.append(jnp.exp(gcol))
        ekd_.append(jnp.exp(glast - gcol))
        gl_.append(jnp.exp(glast))
        q_.append(q)
        k_.append(k)
        kb_.append(k * beta)
        vb_.append(v * beta)

    n = len(grp)
    kk_qk = [lax.dot_general(jnp.concatenate([kb_[g], q_[g]], axis=0).astype(BF16), k_[g].astype(BF16),
                             nt, preferred_element_type=F32) for g in range(n)]
    ms = [jnp.where(strict, kk_qk[g][0:CHUNK] * dec_[g], 0.0) for g in range(n)]
    a_ = [(kk_qk[g][CHUNK:2 * CHUNK] * dec_[g]).astype(BF16) for g in range(n)]
    t_ = _unit_lower_inverse(ms, eye, base, levels)
    uw = [_bdot(t_[g], jnp.concatenate([vb_[g], kb_[g] * eg_[g]], axis=1)) for g in range(n)]
    wq_ = [jnp.concatenate([uw[g][:, DV_A:DV_A + DK_A], q_[g] * eg_[g]], axis=0).astype(BF16)
           for g in range(n)]
    kd_ = [(k_[g] * ekd_[g]).astype(BF16) for g in range(n)]

    lanes = [(b, h) for b in range(n_seq) for h in range(H_A)]
    s = {bh: s_ref[bh[0], bh[1]] for bh in lanes}
    for c in range(nchunk):
        gi = {(b, h): grp.index((b, c, h)) for (b, h) in lanes}
        ws = {bh: jnp.dot(wq_[gi[bh]], s[bh].astype(BF16), preferred_element_type=F32) for bh in lanes}
        vn = {bh: (uw[gi[bh]][:, 0:DV_A] - ws[bh][0:CHUNK]).astype(BF16) for bh in lanes}
        av = {bh: jnp.dot(a_[gi[bh]], vn[bh], preferred_element_type=F32) for bh in lanes}
        kv = {bh: lax.dot_general(kd_[gi[bh]], vn[bh], (((0,), (0,)), ((), ())), preferred_element_type=F32)
              for bh in lanes}
        for (b, h) in lanes:
            o_ref[b, c * CHUNK:(c + 1) * CHUNK, h * DV_A:(h + 1) * DV_A] = ws[b, h][CHUNK:2 * CHUNK] + av[b, h]
            s[b, h] = s[b, h] * gl_[gi[b, h]] + kv[b, h]
    for (b, h) in lanes:
        s_ref[b, h] = s[b, h]
        sout_ref[b, h] = s[b, h]


def _prompt_gdn(qkv, ba, prm, n_seq, seq, tb):
    nb = seq // tb
    return pl.pallas_call(
        functools.partial(_gdn_kernel, tb=tb, n_seq=n_seq),
        grid=(nb,),
        in_specs=[
            pl.BlockSpec((n_seq, tb, QKV_A), lambda j: (0, j, 0)),
            pl.BlockSpec((n_seq, tb, LANES), lambda j: (0, j, 0)),
            pl.BlockSpec(prm.shape, lambda j: (0, 0)),
        ],
        out_specs=[
            pl.BlockSpec((n_seq, tb, H_A * DV_A), lambda j: (0, j, 0)),
            pl.BlockSpec((n_seq, H_A, DK_A, DV_A), lambda j: (0, 0, 0, 0)),
        ],
        out_shape=[
            jax.ShapeDtypeStruct((n_seq, seq, H_A * DV_A), F32),
            jax.ShapeDtypeStruct((n_seq, H_A, DK_A, DV_A), F32),
        ],
        scratch_shapes=[pltpu.VMEM((n_seq, H_A, DK_A, DV_A), F32)],
        compiler_params=_cparams(1, 48),
        name="prompt_gdn",
    )(qkv, ba, prm)


def _merge_kernel(x_ref, oa_ref, ob_ref, nw_ref, w2_ref, onw_ref, sbw_ref, wb_ref, wo_ref, out_ref, *, ob_scale):
    x = x_ref[...]
    h = _rms(x, nw_ref[...]).astype(BF16)
    z = jnp.dot(h, w2_ref[:, P2_Z:P2_GA], preferred_element_type=F32)
    oa = oa_ref[...]
    ob = ob_ref[...]
    onw = onw_ref[...]
    sbw = sbw_ref[...]
    oa_n, ob_n = [], []
    for hd in range(H_A):
        sl = slice(hd * DV_A, (hd + 1) * DV_A)
        oa_n.append((_rms(oa[:, sl], onw) * _silu(z[:, sl])).astype(BF16))
    for hd in range(H_B):
        sl = slice(hd * DV_B, (hd + 1) * DV_B)
        ob_n.append((_rms(ob[:, sl], sbw) * ob_scale).astype(BF16))
    pa = jnp.dot(jnp.concatenate(oa_n, axis=1), wb_ref[0], preferred_element_type=F32)
    pb = jnp.dot(jnp.concatenate(ob_n, axis=1), wb_ref[1], preferred_element_type=F32)
    ga = jnp.dot(h, w2_ref[:, P2_GA:P2_GB], preferred_element_type=F32)
    gb = jnp.dot(h, w2_ref[:, P2_GB:P2_END], preferred_element_type=F32)
    mixed = (_sigmoid(ga) * pa + _sigmoid(gb) * pb).astype(BF16)
    out_ref[...] = x + jnp.dot(mixed, wo_ref[...], preferred_element_type=F32)


def _merge(x, oa, ob, nw, w2, onw, sbw, wb, wo, tm, ob_scale):
    r = x.shape[0]
    row = lambda w: pl.BlockSpec((tm, w), lambda i: (i, 0))
    full = lambda a: pl.BlockSpec(a.shape, lambda i: (0,) * a.ndim)
    return pl.pallas_call(
        functools.partial(_merge_kernel, ob_scale=ob_scale),
        grid=(r // tm,),
        in_specs=[row(D_MODEL), row(H_A * DV_A), row(H_B * DV_B), full(nw), full(w2), full(onw), full(sbw),
                  full(wb), full(wo)],
        out_specs=row(D_MODEL),
        out_shape=jax.ShapeDtypeStruct((r, D_MODEL), F32),
        compiler_params=_cparams(1, 52),
        name="merge",
    )(x, oa, ob, nw, w2, onw, sbw, wb, wo)


FF_CHUNK = 256
UP_AHEAD = 2
DOWN_GROUP = 4


def _ffn_kernel(x_ref, st_ref, nw_ref, wup_ref, cw_ref, cb_ref, wdn_ref, out_ref, stout_ref,
                carry_ref, ext_ref, *, tm, shift, halo):
    i = pl.program_id(1)

    @pl.when(i == 0)
    def _():
        carry_ref[...] = st_ref[0]

    x = x_ref[...]
    h = _rms(x, nw_ref[...]).astype(BF16)
    cw = cw_ref[...]
    cb = cb_ref[...]
    n_chunks = D_FF // FF_CHUNK

    def col(c, part):
        lo = part * D_FF + c * FF_CHUNK
        return slice(lo, lo + FF_CHUNK)

    def up_proj(c):
        return [jnp.dot(h, wup_ref[:, col(c, part)], preferred_element_type=F32) for part in range(2)]

    def gated(c, ups):
        ext = ext_ref.at[c % 2]
        parts = []
        for part in range(2):
            sl = col(c, part)
            es = slice(part * FF_CHUNK, (part + 1) * FF_CHUNK)
            ext[0:halo, es] = carry_ref[:, sl]
            ext[halo:halo + tm, es] = ups[part]
            u = cw[2:3, sl] * ups[part] + cb[:, sl]
            u = u + cw[1:2, sl] * ext[halo - shift:halo - shift + tm, es]
            u = u + cw[0:1, sl] * ext[halo - 2 * shift:halo - 2 * shift + tm, es]
            carry_ref[:, sl] = ext[tm:tm + halo, es]
            parts.append(u)
        return (_silu(parts[0]) * parts[1]).astype(BF16)

    acc = x
    ups, acts = {}, {}
    for s in range(n_chunks + UP_AHEAD + 1):
        if s < n_chunks:
            ups[s] = up_proj(s)
        c = s - UP_AHEAD
        if 0 <= c < n_chunks:
            acts[c] = gated(c, ups.pop(c))
        c = s - UP_AHEAD - 1
        if 0 <= c < n_chunks and (c % DOWN_GROUP == DOWN_GROUP - 1 or c == n_chunks - 1):
            c0 = c - c % DOWN_GROUP
            a = jnp.concatenate([acts.pop(i) for i in range(c0, c + 1)], axis=1)
            acc = acc + jnp.dot(a, wdn_ref[c0 * FF_CHUNK:(c + 1) * FF_CHUNK, :], preferred_element_type=F32)
    out_ref[...] = acc
    stout_ref[0] = carry_ref[...]


def _ffn(x, st, nw, wup, cw, cb, wdn, groups, tm, shift, halo):
    r = x.shape[0]
    tiles = r // groups // tm
    full = lambda a: pl.BlockSpec(a.shape, lambda g, i: (0,) * a.ndim)
    return pl.pallas_call(
        functools.partial(_ffn_kernel, tm=tm, shift=shift, halo=halo),
        grid=(groups, tiles),
        in_specs=[
            pl.BlockSpec((tm, D_MODEL), lambda g, i: (g * tiles + i, 0)),
            pl.BlockSpec((1, halo, 2 * D_FF), lambda g, i: (g, 0, 0)),
            full(nw), full(wup), full(cw), full(cb), full(wdn),
        ],
        out_specs=[
            pl.BlockSpec((tm, D_MODEL), lambda g, i: (g * tiles + i, 0)),
            pl.BlockSpec((1, halo, 2 * D_FF), lambda g, i: (g, 0, 0)),
        ],
        out_shape=[jax.ShapeDtypeStruct((r, D_MODEL), F32),
                   jax.ShapeDtypeStruct((groups, halo, 2 * D_FF), F32)],
        scratch_shapes=[pltpu.VMEM((halo, 2 * D_FF), F32), pltpu.VMEM((2, halo + tm, 2 * FF_CHUNK), F32)],
        compiler_params=_cparams(2, 56),
        name="ffn",
    )(x, st, nw, wup, cw, cb, wdn)


def _sgdn_pre_kernel(qkv_ref, ba_ref, prm_ref, wq_ref, u_ref, kd_ref, a_ref, gl_ref, *, t_new, nb):
    prm = prm_ref[...]
    lane = lax.broadcasted_iota(jnp.int32, (nb, LANES), 1)

    ys, betas, gs = [], [], []
    for t in range(t_new):
        ys.append(qkv_ref[t])
        bg = ba_ref[t]
        betas.append(_sigmoid(bg))
        gs.append(-jnp.exp(prm[0:1]) * _softplus(bg + prm[1:2]))
    gcs = [gs[0]]
    for t in range(1, t_new):
        gcs.append(gcs[-1] + gs[t])

    a_out = [jnp.zeros((nb, LANES), F32) for _ in range(t_new)]
    for h in range(H_A):
        q, k, v, beta, gc = [], [], [], [], []
        for t in range(t_new):
            q.append(ys[t][:, h * DK_A:(h + 1) * DK_A])
            k.append(ys[t][:, H_A * DK_A + h * DK_A:H_A * DK_A + (h + 1) * DK_A])
            v.append(ys[t][:, 2 * H_A * DK_A + h * DV_A:2 * H_A * DK_A + (h + 1) * DV_A])
            beta.append(betas[t][:, h:h + 1])
            gc.append(gcs[t][:, H_A + h:H_A + h + 1])
        m = [[None] * t_new for _ in range(t_new)]
        for i in range(t_new):
            for jj in range(i + 1):
                dec = jnp.exp(gc[i] - gc[jj])
                if jj < i:
                    m[i][jj] = beta[i] * jnp.sum(k[i] * k[jj], axis=-1, keepdims=True) * dec
                aij = jnp.sum(q[i] * k[jj], axis=-1, keepdims=True) * dec
                a_out[i] = jnp.where(lane == h * SUBLANES + jj, aij, a_out[i])
        tm_ = [[None] * t_new for _ in range(t_new)]
        for i in range(t_new):
            for jj in range(i):
                acc = m[i][jj]
                for l in range(jj + 1, i):
                    acc = acc + m[i][l] * tm_[l][jj]
                tm_[i][jj] = -acc
        vb = [v[t] * beta[t] for t in range(t_new)]
        kbg = [k[t] * (beta[t] * jnp.exp(gc[t])) for t in range(t_new)]
        hs = slice(h * DK_A, (h + 1) * DK_A)
        for i in range(t_new):
            u = vb[i]
            w = kbg[i]
            for jj in range(i):
                u = u + tm_[i][jj] * vb[jj]
                w = w + tm_[i][jj] * kbg[jj]
            u_ref[i, :, hs] = u
            wq_ref[i, :, hs] = w
            wq_ref[t_new + i, :, hs] = q[i] * jnp.exp(gc[i])
            kd_ref[i, :, hs] = k[i] * jnp.exp(gc[t_new - 1] - gc[i])
        gl_ref[:, hs] = jnp.broadcast_to(jnp.exp(gc[t_new - 1]), (nb, DK_A))
    for i in range(t_new):
        a_ref[i] = a_out[i]


def _sgdn_pre(qkv_t, ba_t, prm, t_new, nb):
    wide = H_A * DK_A
    return pl.pallas_call(
        functools.partial(_sgdn_pre_kernel, t_new=t_new, nb=nb),
        out_shape=[
            jax.ShapeDtypeStruct((2 * t_new, nb, wide), F32),
            jax.ShapeDtypeStruct((t_new, nb, wide), F32),
            jax.ShapeDtypeStruct((t_new, nb, wide), F32),
            jax.ShapeDtypeStruct((t_new, nb, LANES), F32),
            jax.ShapeDtypeStruct((nb, wide), F32),
        ],
        compiler_params=pltpu.CompilerParams(vmem_limit_bytes=48 * 1024 * 1024),
        name="sample_gdn_pre",
    )(qkv_t, ba_t, prm)


SGDN_BB = 8


def _sgdn_state_kernel(wq_ref, u_ref, kd_ref, a_ref, gl_ref, s0_ref, *refs, t_new):
    o_ref, s1_ref = refs[-2:]
    rows = 2 * t_new
    rid = lax.broadcasted_iota(jnp.int32, (rows, DK_A), 0)
    zpad = jnp.zeros((rows - t_new, DK_A), F32)
    pairs = [(bi, h) for bi in range(SGDN_BB) for h in range(H_A)]
    hs = [slice(h * DK_A, (h + 1) * DK_A) for h in range(H_A)]
    r = {(bi, h): jnp.dot(wq_ref[:, bi, hs[h]].astype(BF16), s0_ref[bi, h].astype(BF16),
                          preferred_element_type=F32) for (bi, h) in pairs}
    v_new = {}
    for (bi, h) in pairs:
        u8 = jnp.concatenate([u_ref[:, bi, hs[h]], zpad], axis=0)
        v_new[bi, h] = jnp.where(rid < t_new, u8 - r[bi, h], 0.0)
    kv = {}
    for (bi, h) in pairs:
        kd8 = jnp.concatenate([kd_ref[:, bi, hs[h]], zpad], axis=0)
        kv[bi, h] = lax.dot_general(kd8.astype(BF16), v_new[bi, h].astype(BF16), (((0,), (0,)), ((), ())),
                                    preferred_element_type=F32)
    for (bi, h) in pairs:
        amat = a_ref[:, bi, :]
        o = r[bi, h][t_new:rows]
        for jj in range(t_new):
            col = amat[:, h * SUBLANES + jj:h * SUBLANES + jj + 1]
            o = o + col * v_new[bi, h][jj:jj + 1, :]
        o_ref[:, bi, hs[h]] = o
        s1_ref[bi, h] = s0_ref[bi, h] * gl_ref[bi:bi + 1, hs[h]] + kv[bi, h]


def _sgdn_state(wq, u, kd, a, gl, s0_all, li, t_new, nb, stacked=None):
    wide = H_A * DK_A
    bb = SGDN_BB
    state_spec = pl.BlockSpec((None, bb, H_A, DK_A, DV_A), lambda i: (li, i, 0, 0, 0))
    in_specs = [
        pl.BlockSpec((2 * t_new, bb, wide), lambda i: (0, i, 0)),
        pl.BlockSpec((t_new, bb, wide), lambda i: (0, i, 0)),
        pl.BlockSpec((t_new, bb, wide), lambda i: (0, i, 0)),
        pl.BlockSpec((t_new, bb, LANES), lambda i: (0, i, 0)),
        pl.BlockSpec((bb, wide), lambda i: (i, 0)),
        state_spec,
    ]
    args = [wq, u, kd, a, gl, s0_all]
    aliases = {}
    if stacked is not None:
        in_specs.append(pl.BlockSpec(memory_space=pl.ANY))
        args.append(stacked)
        aliases = {6: 1}
    return pl.pallas_call(
        functools.partial(_sgdn_state_kernel, t_new=t_new),
        grid=(nb // bb,),
        in_specs=in_specs,
        out_specs=[pl.BlockSpec((t_new, bb, wide), lambda i: (0, i, 0)), state_spec],
        out_shape=[jax.ShapeDtypeStruct((t_new, nb, wide), F32),
                   jax.ShapeDtypeStruct(s0_all.shape, F32)],
        input_output_aliases=aliases,
        compiler_params=_cparams(1, 32),
        name="sample_gdn_state",
    )(*args)


NEW_ROWS = 16


def _sattn_kernel(pt_ref, lam_ref, q_ref, kn_ref, vn_ref, bias_ref, *rest, n_pages, t_new, lam_init):
    del pt_ref
    k_pages = rest[:n_pages]
    v_pages = rest[n_pages:2 * n_pages]
    o_ref = rest[2 * n_pages]
    rows = 2 * t_new
    q = q_ref[0]
    bias = bias_ref[...]
    lam = _lam_value(lam_ref[...], lam_init)
    r = lax.broadcasted_iota(jnp.int32, (rows, 2 * DH_B), 0)
    c = lax.broadcasted_iota(jnp.int32, (rows, 2 * DH_B), 1)
    map_mask = (c // DH_B) == (r // t_new)
    pad = jnp.zeros((PAGE - NEW_ROWS, LANES), BF16)
    nt = (((1,), (1,)), ((), ()))

    heads = range(H_B)
    lanes = [slice(h * LANES, (h + 1) * LANES) for h in heads]
    head_rows = [pl.ds(h, PAGE, stride=H_B) for h in heads]
    qx = [jnp.where(map_mask, jnp.concatenate([q[:, lanes[h]]] * 2, axis=0), 0.0).astype(BF16) for h in heads]
    bh = [bias[h * rows:(h + 1) * rows] for h in heads]

    groups = [list(range(p, min(p + MXU_DIM // PAGE, n_pages))) for p in range(0, n_pages, MXU_DIM // PAGE)]

    def head_tile(page_refs, grp, h):
        return jnp.concatenate([page_refs[p][0, 0, head_rows[h], :] for p in grp], axis=0).astype(BF16)

    s_parts = [[] for _ in heads]
    for grp in groups:
        for h in heads:
            s = lax.dot_general(qx[h], head_tile(k_pages, grp, h), nt, preferred_element_type=F32)
            if grp[-1] == n_pages - 1:
                zeros = [jnp.zeros((rows, PAGE), F32)] * (len(grp) - 1)
                s = s + jnp.concatenate(zeros + [bh[h][:, 0:PAGE]], axis=1)
            s_parts[h].append(s)
    for h in heads:
        kn = jnp.concatenate([kn_ref[0, :, lanes[h]], pad], axis=0)
        s_parts[h].append(lax.dot_general(qx[h], kn, nt, preferred_element_type=F32) + bh[h][:, PAGE:2 * PAGE])

    m = []
    for h in heads:
        mh = s_parts[h][0].max(axis=-1, keepdims=True)
        for s in s_parts[h][1:]:
            mh = jnp.maximum(mh, s.max(axis=-1, keepdims=True))
        m.append(mh)
    l = [jnp.zeros((rows, 1), F32) for _ in heads]
    acc = [jnp.zeros((rows, DV_B), F32) for _ in heads]
    for gi in range(len(groups) + 1):
        for h in heads:
            pr = jnp.exp2(s_parts[h][gi] - m[h])
            l[h] = l[h] + jnp.sum(pr, axis=-1, keepdims=True)
            if gi < len(groups):
                vv = head_tile(v_pages, groups[gi], h)
            else:
                vv = jnp.concatenate([vn_ref[0, :, lanes[h]], pad], axis=0)
            acc[h] = acc[h] + jnp.dot(pr.astype(BF16), vv, preferred_element_type=F32)
    for h in heads:
        a = acc[h] / l[h]
        o_ref[0, :, lanes[h]] = a[0:t_new] - lam * a[t_new:rows]


def _sample_attn(page_table, lamv, q_b, kn_b, vn_b, bias, ck, cv, li, t_new, lam_init):
    nb, n_pages = page_table.shape
    page_spec = lambda p: pl.BlockSpec((1, 1, PAGE * H_B, LANES), lambda b, pt, p=p: (li, pt[b, p], 0, 0))
    tok_spec = pl.BlockSpec((1, t_new, HB_W), lambda b, pt: (b, 0, 0))
    new_spec = pl.BlockSpec((1, NEW_ROWS, HB_W), lambda b, pt: (b, 0, 0))
    grid_spec = pltpu.PrefetchScalarGridSpec(
        num_scalar_prefetch=1,
        grid=(nb,),
        in_specs=[pl.BlockSpec(lamv.shape, lambda b, pt: (0, 0)), tok_spec, new_spec, new_spec,
                  pl.BlockSpec(bias.shape, lambda b, pt: (0, 0))]
                 + [page_spec(p) for p in range(n_pages)] * 2,
        out_specs=pl.BlockSpec((1, t_new, HB_W), lambda b, pt: (b, 0, 0)),
    )
    return pl.pallas_call(
        functools.partial(_sattn_kernel, n_pages=n_pages, t_new=t_new, lam_init=lam_init),
        grid_spec=grid_spec,
        out_shape=jax.ShapeDtypeStruct((nb, t_new, HB_W), F32),
        compiler_params=_cparams(1, 40),
        name="sample_attn",
    )(page_table, lamv, q_b, kn_b, vn_b, bias, *([ck] * n_pages), *([cv] * n_pages))


def _pick(n, pref):
    return pref if n % pref == 0 else n


def _to_bmajor(a_t, t_new, nb):
    return jnp.swapaxes(a_t.reshape(t_new, nb, -1), 0, 1)


def _layer_weights(li, rel_bias, norm_mix, w_in, conv_a, a_log, dt_bias, onorm_a, qnorm_b, knorm_b,
                   lam_q1, lam_k1, lam_q2, lam_k2, subln_b, w_branch, w_o, norm_ffn, w_up, conv_f,
                   conv_f_b, w_down):
    w = w_in[li]
    o_z = QKV_A
    o_b = o_z + H_A * DV_A
    o_qb = o_b + 2 * H_A
    o_kb = o_qb + HB_W
    o_vb = o_kb + HB_W
    o_ga = o_vb + H_B * DV_B
    o_gb = o_ga + D_MODEL
    ba_cols = jnp.pad(w[:, o_b:o_qb], ((0, 0), (0, LANES - 2 * H_A)))
    w1 = jnp.concatenate([w[:, 0:o_z], w[:, o_qb:o_ga], ba_cols], axis=1).astype(BF16)
    w2 = jnp.concatenate([w[:, o_z:o_b], w[:, o_ga:]], axis=1).astype(BF16)
    prm = jnp.zeros((2, LANES), F32)
    prm = prm.at[0, H_A:2 * H_A].set(a_log[li]).at[1, H_A:2 * H_A].set(dt_bias[li])
    k_bound = 1.01 * math.sqrt(DH_B) * jnp.max(jnp.abs(knorm_b[li]))
    q_bound = 1.01 * Q_SCALE * math.sqrt(DH_B) * jnp.max(jnp.abs(qnorm_b[li]))
    rel = (rel_bias - rel_bias[NUM_BUCKETS - 1:NUM_BUCKETS]) * LOG2E
    spread = 2.0 * q_bound * k_bound + jnp.max(jnp.max(rel, axis=0) - jnp.min(rel, axis=0))
    scal = jnp.concatenate([jnp.where(spread <= SAFE_SPREAD_BITS, 1.0, 0.0).reshape(1), k_bound.reshape(1),
                            jnp.max(rel, axis=0), jnp.zeros((2,), F32)]).astype(F32)
    return dict(
        scal=scal,
        nw=norm_mix[li].reshape(1, D_MODEL), w1=w1, w2=w2,
        qw=jnp.tile(qnorm_b[li], 2 * H_B).reshape(1, HB_W),
        kw=jnp.tile(knorm_b[li], 2 * H_B).reshape(1, HB_W),
        cw_a=conv_a[li], prm=prm,
        onw=onorm_a[li].reshape(1, DV_A), sbw=subln_b[li].reshape(1, DV_B),
        lamv=jnp.stack([lam_q1[li], lam_k1[li], lam_q2[li], lam_k2[li]]),
        wb=w_branch[li].astype(BF16), wo=w_o[li].astype(BF16),
        nwf=norm_ffn[li].reshape(1, D_MODEL), wup=w_up[li].astype(BF16),
        cw_f=conv_f[li], cb_f=conv_f_b[li].reshape(1, 2 * D_FF), wdn=w_down[li].astype(BF16),
    )


def _group_ones():
    i = jnp.arange(MXU_DIM)
    return ((i[:, None] // DH_B) == (i[None, :] // DH_B)).astype(BF16)


def _prompt_layer(x, wt, bias_p, g, n_seq, seq, lam_init, li, depth, kv_stacked):
    tm = _pick(seq, 512)
    blk = bias_p.shape[1]
    zero_st_a = jnp.zeros((n_seq, SUBLANES, QKV_A), F32)
    qkv, ba, qn, kn, kf, vf, vb, cst_a = _inproj(x, wt["nw"], wt["w1"], wt["qw"], wt["kw"], g, zero_st_a,
                                                 wt["cw_a"], n_seq, tm, 1, SUBLANES,
                                                 layer=li, depth=depth, stacked=kv_stacked)
    o_a, s_fin = _prompt_gdn(qkv.reshape(n_seq, seq, QKV_A), ba.reshape(n_seq, seq, LANES),
                             wt["prm"], n_seq, seq, _pick(seq, 256))
    o_a = o_a.reshape(n_seq * seq, H_A * DV_A)
    o_b = _prompt_attn(wt["scal"], wt["lamv"], qn, kn, vb, bias_p, n_seq, seq, blk, lam_init)
    x = _merge(x, o_a, o_b, wt["nw"], wt["w2"], wt["onw"], wt["sbw"], wt["wb"], wt["wo"], tm, 1.0 - lam_init)
    zero_st = jnp.zeros((n_seq, SUBLANES, 2 * D_FF), F32)
    x, cst_f = _ffn(x, zero_st, wt["nwf"], wt["wup"], wt["cw_f"], wt["cb_f"], wt["wdn"],
                    n_seq, _pick(seq, 512), 1, SUBLANES)
    return (x, kf, vf, s_fin, cst_a[:, SUBLANES - (CONV_A - 1):], cst_f[:, SUBLANES - (CONV_F - 1):])


def _sample_layer(x_t, wt, bias_s, g, page_table, ck, cv, li, s0_all, s1_stacked, cst_a, cst_f, nb, t_new,
                  lam_init):
    r = nb * t_new
    halo_a = (CONV_A - 1) * nb
    st_a_t = jnp.swapaxes(cst_a, 0, 1).reshape(1, halo_a, QKV_A)
    qkv, ba, qn, kn, kf, vf, vb, cst_a_t = _inproj(x_t, wt["nw"], wt["w1"], wt["qw"], wt["kw"], g, st_a_t,
                                                   wt["cw_a"], 1, r, nb, halo_a)
    cst_a_t = cst_a_t.reshape(CONV_A - 1, nb, QKV_A)
    wq, u, kd, a, gl = _sgdn_pre(qkv.reshape(t_new, nb, QKV_A), ba.reshape(t_new, nb, LANES), wt["prm"],
                                 t_new, nb)
    o_a_t, s1 = _sgdn_state(wq, u, kd, a, gl, s0_all, li, t_new, nb, stacked=s1_stacked)
    pad_new = lambda a: jnp.pad(_to_bmajor(a, t_new, nb), ((0, 0), (0, NEW_ROWS - t_new), (0, 0)))
    o_b = _sample_attn(page_table, wt["lamv"], _to_bmajor(qn, t_new, nb).astype(F32), pad_new(kn),
                       pad_new(vb), bias_s, ck, cv, li, t_new, lam_init)
    o_b_t = jnp.swapaxes(o_b, 0, 1).reshape(r, HB_W)
    x_t = _merge(x_t, o_a_t.reshape(r, H_A * DV_A), o_b_t, wt["nw"], wt["w2"], wt["onw"], wt["sbw"],
                 wt["wb"], wt["wo"], r, 1.0 - lam_init)
    halo = (CONV_F - 1) * nb
    st_f_t = jnp.swapaxes(cst_f, 0, 1).reshape(1, halo, 2 * D_FF)
    x_t, cst_f_t = _ffn(x_t, st_f_t, wt["nwf"], wt["wup"], wt["cw_f"], wt["cb_f"], wt["wdn"],
                        1, r, nb, halo)
    return (x_t, _to_bmajor(kf, t_new, nb).reshape(nb, t_new, H_B, 2 * DH_B),
            _to_bmajor(vf, t_new, nb).reshape(nb, t_new, H_B, DV_B), s1,
            jnp.swapaxes(cst_a_t, 0, 1), jnp.swapaxes(cst_f_t.reshape(CONV_F - 1, nb, 2 * D_FF), 0, 1))


def kernel(x_prompt, x_sample, cache_k, cache_v, state_delta, state_conv_a, state_conv_ffn, page_table,
           rel_bias, norm_mix, w_in, conv_a, a_log, dt_bias, onorm_a, qnorm_b, knorm_b, lam_q1, lam_k1,
           lam_q2, lam_k2, subln_b, w_branch, w_o, norm_ffn, w_up, conv_f, conv_f_b, w_down):
    n_seq, seq, _ = x_prompt.shape
    nb, t_new, _ = x_sample.shape
    depth = w_in.shape[0]
    n_pool = cache_k.shape[1]

    rb_flat = rel_bias.T.reshape(-1)
    bias_p = _prompt_bias(rb_flat, _pick(seq, 512))
    bias_s = _sample_bias(rb_flat, t_new)
    g = _group_ones()
    ck = cache_k.reshape(depth, n_pool, PAGE * H_B, 2 * DH_B)
    cv = cache_v.reshape(depth, n_pool, PAGE * H_B, DV_B)

    xp = x_prompt.reshape(n_seq * seq, D_MODEL)
    xs = jnp.swapaxes(x_sample, 0, 1).reshape(t_new * nb, D_MODEL)
    outs_p, outs_s = [], []
    kv_stacked, s1_stacked = None, None
    for li in range(depth):
        wt = _layer_weights(li, rel_bias, norm_mix, w_in, conv_a, a_log, dt_bias, onorm_a, qnorm_b, knorm_b,
                            lam_q1, lam_k1, lam_q2, lam_k2, subln_b, w_branch, w_o, norm_ffn, w_up,
                            conv_f, conv_f_b, w_down)
        lam_init = 0.8 - 0.6 * math.exp(-0.3 * li)
        xp, kf_all, vf_all, *rest_p = _prompt_layer(xp, wt, bias_p, g, n_seq, seq, lam_init, li, depth,
                                                    kv_stacked)
        kv_stacked = (kf_all, vf_all)
        xs, ks, vs, s1_stacked, *rest_s = _sample_layer(xs, wt, bias_s, g, page_table, ck, cv, li, state_delta,
                                                        s1_stacked, state_conv_a[li], state_conv_ffn[li], nb,
                                                        t_new, lam_init)
        outs_p.append(rest_p)
        outs_s.append([ks, vs] + rest_s)

    stack = lambda outs, i: jnp.stack([o[i] for o in outs])
    y_prompt = xp.reshape(n_seq, seq, D_MODEL)
    y_sample = jnp.swapaxes(xs.reshape(t_new, nb, D_MODEL), 0, 1)
    k_prompt = kv_stacked[0].reshape(depth, n_seq, seq, H_B, 2 * DH_B)
    v_prompt = kv_stacked[1].reshape(depth, n_seq, seq, H_B, DV_B)
    return (y_prompt, y_sample,
            k_prompt, v_prompt, stack(outs_p, 0), stack(outs_p, 1), stack(outs_p, 2),
            stack(outs_s, 0), stack(outs_s, 1), s1_stacked, stack(outs_s, 2), stack(outs_s, 3))
```

```python
import functools
import math

import jax
import jax.numpy as jnp
from jax import lax
from jax.experimental import pallas as pl
from jax.experimental.pallas import tpu as pltpu

F32 = jnp.float32
BF16 = jnp.bfloat16

D_MODEL = 1024
H_A, DK_A, DV_A, CONV_A, CHUNK = 4, 128, 128, 4, 64
QKV_A = 2 * H_A * DK_A + H_A * DV_A
H_B, DH_B, DV_B = 4, 64, 128
HB_W = H_B * 2 * DH_B
PAGE = 128
D_FF, CONV_F = 2816, 3
NUM_BUCKETS, MAX_DISTANCE = 32, 128
NEG_INF = -1e30
EPS = 1e-6
ATT_SCALE = DH_B ** -0.5
LOG2E = math.log2(math.e)
Q_SCALE = ATT_SCALE * LOG2E
SAFE_SPREAD_BITS = 100.0

V7X_VMEM_BYTES = 64 * 1024 * 1024
LANES = 128
SUBLANES = 8
MXU_DIM = 256

P1_QKV, P1_Q, P1_K, P1_V, P1_BA, P1_END = 0, 1536, 2048, 2560, 3072, 3200
P2_Z, P2_GA, P2_GB, P2_END = 0, 512, 1536, 2560


def _cparams(n_axes, vmem_mib):
    return pltpu.CompilerParams(
        dimension_semantics=("arbitrary",) * n_axes,
        vmem_limit_bytes=min(vmem_mib * 1024 * 1024, V7X_VMEM_BYTES - 8 * 1024 * 1024),
    )


def _bdot(a, b):
    return jnp.dot(a.astype(BF16), b.astype(BF16), preferred_element_type=F32)


def _rms(x, w):
    return x * lax.rsqrt(jnp.mean(x * x, axis=-1, keepdims=True) + EPS) * w


def _sigmoid(x):
    return 1.0 / (1.0 + jnp.exp(-x))


def _silu(x):
    return x * _sigmoid(x)


def _softplus(x):
    return jnp.maximum(x, 0.0) + jnp.log1p(jnp.exp(-jnp.abs(x)))


def _inproj_kernel(x_ref, nw_ref, w_ref, qw_ref, kw_ref, g_ref, st_ref, cw_ref, *refs, shift, halo):
    n_out = 8
    qkv_ref, ba_ref, qn_ref, kn_ref, kf_ref, vf_ref, vb_ref, stout_ref = refs[-n_out - 2:-2]
    carry_ref, ext_ref = refs[-2:]
    h = _rms(x_ref[...], nw_ref[...]).astype(BF16)

    def proj(lo, hi):
        return jnp.dot(h, w_ref[:, lo:hi], preferred_element_type=F32)

    tm = x_ref.shape[0]
    head_rows = [pl.ds(hd, tm, stride=H_B) for hd in range(H_B)]

    @pl.when(pl.program_id(1) == 0)
    def _():
        carry_ref[...] = st_ref[0]

    cw = cw_ref[...]
    g = g_ref[...]
    seg = H_A * DK_A

    def delta_front(part, raw):
        cols = slice(part * seg, (part + 1) * seg)
        ext_ref[0:halo, cols] = carry_ref[:, cols]
        ext_ref[halo:halo + tm, cols] = raw
        y = cw[CONV_A - 1:CONV_A, cols] * raw
        for i in range(CONV_A - 1):
            back = (CONV_A - 1 - i) * shift
            y = y + cw[i:i + 1, cols] * ext_ref[halo - back:halo - back + tm, cols]
        new_carry = ext_ref[tm:tm + halo, cols]
        carry_ref[:, cols] = new_carry
        stout_ref[0, :, cols] = new_carry
        y = _silu(y)
        if part == 2:
            qkv_ref[:, cols] = y
            return
        for hd in range(H_A):
            sl = slice(hd * DK_A, (hd + 1) * DK_A)
            t = y[:, sl]
            t = t * lax.rsqrt(jnp.sum(t * t, axis=-1, keepdims=True) + EPS)
            qkv_ref[:, part * seg + hd * DK_A:part * seg + (hd + 1) * DK_A] = t * (DK_A ** -0.5) if part == 0 else t

    def qk_norm(y, w):
        sq = y * y
        hi = sq.astype(BF16)
        lo = (sq - hi.astype(F32)).astype(BF16)
        outs = []
        for c in range(HB_W // MXU_DIM):
            sl = slice(c * MXU_DIM, (c + 1) * MXU_DIM)
            ss = (jnp.dot(hi[:, sl], g, preferred_element_type=F32)
                  + jnp.dot(lo[:, sl], g, preferred_element_type=F32))
            outs.append(y[:, sl] * lax.rsqrt(ss * (1.0 / DH_B) + EPS) * w[:, sl])
        return outs

    def attn_q(y):
        qn = qk_norm(y, qw_ref[...])
        for c in range(HB_W // MXU_DIM):
            qn_ref[:, c * MXU_DIM:(c + 1) * MXU_DIM] = (qn[c] * Q_SCALE).astype(BF16)

    def attn_k(y):
        kn = qk_norm(y, kw_ref[...])
        for c in range(HB_W // MXU_DIM):
            kn_ref[:, c * MXU_DIM:(c + 1) * MXU_DIM] = kn[c].astype(BF16)
            for i in range(MXU_DIM // LANES):
                kf_ref[head_rows[c * (MXU_DIM // LANES) + i], :] = kn[c][:, i * LANES:(i + 1) * LANES]

    def attn_v(v):
        for hd in range(H_B):
            vf_ref[head_rows[hd], :] = v[:, hd * DV_B:(hd + 1) * DV_B]
        vb_ref[...] = v.astype(BF16)

    def store_ba(y):
        ba_ref[...] = y

    stages = [
        ((P1_QKV, P1_QKV + seg), functools.partial(delta_front, 0)),
        ((P1_QKV + seg, P1_QKV + 2 * seg), functools.partial(delta_front, 1)),
        ((P1_QKV + 2 * seg, P1_Q), functools.partial(delta_front, 2)),
        ((P1_Q, P1_K), attn_q),
        ((P1_K, P1_V), attn_k),
        ((P1_V, P1_BA), attn_v),
        ((P1_BA, P1_END), store_ba),
    ]
    pending = None
    for cols, post in stages:
        y = proj(*cols)
        if pending is not None:
            pending[1](pending[0])
        pending = (y, post)
    pending[1](pending[0])


def _inproj(x, nw, w1, qw, kw, g, st, cw, groups, tm, shift, halo, layer=None, depth=None, stacked=None):
    r = x.shape[0]
    tiles = r // groups // tm
    row = lambda w: pl.BlockSpec((tm, w), lambda gi, i: (gi * tiles + i, 0))
    full = lambda a: pl.BlockSpec(a.shape, lambda gi, i: (0,) * a.ndim)
    st_spec = pl.BlockSpec((1, halo, QKV_A), lambda gi, i: (gi, 0, 0))
    outs = [(1, QKV_A, F32), (1, LANES, F32), (1, HB_W, BF16), (1, HB_W, BF16), (H_B, LANES, F32),
            (H_B, LANES, F32), (1, HB_W, BF16)]
    out_specs = [pl.BlockSpec((tm * m, w), lambda gi, i: (gi * tiles + i, 0)) for m, w, _ in outs]
    out_shape = [jax.ShapeDtypeStruct((r * m, w), dt) for m, w, dt in outs]
    out_specs.append(st_spec)
    out_shape.append(jax.ShapeDtypeStruct((groups, halo, QKV_A), F32))
    in_specs = [row(D_MODEL), full(nw), full(w1), full(qw), full(kw), full(g), st_spec, full(cw)]
    args = [x, nw, w1, qw, kw, g, st, cw]
    aliases = {}
    if layer is not None:
        for o in (4, 5):
            m, w, dt = outs[o]
            out_specs[o] = pl.BlockSpec((None, tm * m, w), lambda gi, i: (layer, gi * tiles + i, 0))
            out_shape[o] = jax.ShapeDtypeStruct((depth, r * m, w), dt)
        if stacked is not None:
            in_specs += [pl.BlockSpec(memory_space=pl.ANY)] * 2
            args += list(stacked)
            aliases = {len(args) - 2: 4, len(args) - 1: 5}
    return pl.pallas_call(
        functools.partial(_inproj_kernel, shift=shift, halo=halo),
        grid=(groups, tiles),
        in_specs=in_specs,
        out_specs=out_specs,
        out_shape=out_shape,
        input_output_aliases=aliases,
        scratch_shapes=[pltpu.VMEM((halo, QKV_A), F32), pltpu.VMEM((halo + tm, QKV_A), F32)],
        compiler_params=_cparams(2, 56),
        name="inproj",
    )(*args)


def _bucket_bias(d, table):
    n = jnp.maximum(d, 0)
    max_exact = NUM_BUCKETS // 2
    nf = jnp.maximum(n, 1).astype(F32)
    large = max_exact + (jnp.log(nf / max_exact) / math.log(MAX_DISTANCE / max_exact)
                         * (NUM_BUCKETS - max_exact)).astype(jnp.int32)
    large = jnp.minimum(large, NUM_BUCKETS - 1)
    bucket = jnp.where(n < max_exact, n, large)
    val = jnp.zeros(d.shape, F32)
    for b in range(NUM_BUCKETS):
        val = jnp.where(bucket == b, table(b), val)
    return val


def _prompt_bias_kernel(rb_ref, o_ref, *, blk):
    h = pl.program_id(0)
    table = lambda b: rb_ref[h * NUM_BUCKETS + b]
    far = table(NUM_BUCKETS - 1)
    i = lax.broadcasted_iota(jnp.int32, (blk, blk), 0)
    j = lax.broadcasted_iota(jnp.int32, (blk, blk), 1)
    d0 = i - j
    o_ref[0, :, 0:blk] = (_bucket_bias(d0 + blk, table) - far) * LOG2E
    o_ref[0, :, blk:2 * blk] = jnp.where(d0 >= 0, (_bucket_bias(d0, table) - far) * LOG2E, NEG_INF)


def _prompt_bias(rb_flat, blk):
    return pl.pallas_call(
        functools.partial(_prompt_bias_kernel, blk=blk),
        grid=(H_B,),
        in_specs=[pl.BlockSpec(memory_space=pltpu.SMEM)],
        out_specs=pl.BlockSpec((1, blk, 2 * blk), lambda h: (h, 0, 0)),
        out_shape=jax.ShapeDtypeStruct((H_B, blk, 2 * blk), F32),
        compiler_params=_cparams(1, 32),
        name="prompt_bias",
    )(rb_flat)


def _sample_bias_kernel(rb_ref, o_ref, *, t_new):
    rows, cols = o_ref.shape
    r = lax.broadcasted_iota(jnp.int32, (rows, cols), 0)
    c = lax.broadcasted_iota(jnp.int32, (rows, cols), 1)
    t = r % t_new
    hd = r // (2 * t_new)
    is_new = c >= PAGE
    d = jnp.where(is_new, t - (c - PAGE), t + PAGE - c)
    val = jnp.zeros((rows, cols), F32)
    for h in range(H_B):
        table = lambda b, h=h: rb_ref[h * NUM_BUCKETS + b]
        vh = _bucket_bias(d, table) - table(NUM_BUCKETS - 1)
        val = jnp.where(hd == h, vh, val)
    visible = jnp.logical_and(d >= 0, jnp.logical_or(~is_new, (c - PAGE) < t_new))
    o_ref[...] = jnp.where(visible, val * LOG2E, NEG_INF)


def _sample_bias(rb_flat, t_new):
    rows = H_B * 2 * t_new
    return pl.pallas_call(
        functools.partial(_sample_bias_kernel, t_new=t_new),
        in_specs=[pl.BlockSpec(memory_space=pltpu.SMEM)],
        out_shape=jax.ShapeDtypeStruct((rows, 2 * PAGE), F32),
        name="sample_bias",
    )(rb_flat)


def _lam_value(lv, lam_init):
    s1 = jnp.sum(lv[0:1] * lv[1:2], axis=-1, keepdims=True)
    s2 = jnp.sum(lv[2:3] * lv[3:4], axis=-1, keepdims=True)
    return jnp.exp(s1) - jnp.exp(s2) + lam_init


FAR_BLOCKS = 4


def _causal_sweep(qi, blk, bias_ref, update):
    n_far = jnp.maximum(qi - 1, 0)
    span = FAR_BLOCKS * blk

    def far(kk, carry):
        update(pl.multiple_of(kk * span, span), FAR_BLOCKS, None)
        return carry

    n_trips = n_far // FAR_BLOCKS
    lax.fori_loop(0, n_trips, far, 0)
    done = n_trips * FAR_BLOCKS
    rem = n_far - done
    size = FAR_BLOCKS // 2
    while size >= 1:
        @pl.when((rem // size) % 2 == 1)
        def _(size=size):
            start = done + (rem // (2 * size)) * (2 * size)
            update(pl.multiple_of(start * blk, blk), size, None)

        size //= 2

    @pl.when(qi >= 1)
    def _():
        update(pl.multiple_of((qi - 1) * blk, blk), 2, bias_ref[0])

    @pl.when(qi == 0)
    def _():
        update(0, 1, bias_ref[0, :, blk:2 * blk])


def _attn_kernel(scal_ref, lam_ref, q_ref, k_ref, v_ref, bias_ref, o_ref, m_ref, l_ref, acc_ref, accx_ref,
                 *, blk, nq, lam_init):
    hd = pl.program_id(1)
    lane = lax.broadcasted_iota(jnp.int32, (1, 2 * DH_B), 1)
    lam = _lam_value(lam_ref[...], lam_init)
    nt = (((1,), (1,)), ((), ()))

    def q_maps(qi):
        q = q_ref[pl.ds(pl.multiple_of(qi * blk, blk), blk), :]
        zero = jnp.zeros_like(q)
        return jnp.where(lane < DH_B, q, zero), jnp.where(lane >= DH_B, q, zero)

    def store_o(qi, val):
        o_ref[pl.ds(pl.multiple_of(qi * blk, blk), blk), :] = val

    @pl.when(scal_ref[0] > 0.5)
    def _():
        one_col = jnp.where(lane == 0, 1.0, 0.0).astype(BF16)
        sizes = {1, 2}
        sizes.update(2 ** e for e in range(FAR_BLOCKS.bit_length()))
        ext_k = {n: jnp.broadcast_to(one_col, (n * blk, 2 * DH_B)) for n in sizes}

        def q_block(qi, carry):
            qs = q_maps(qi)
            qe = []
            for mp in range(2):
                qf = qs[mp].astype(F32)
                shift = jnp.sqrt(jnp.sum(qf * qf, axis=-1, keepdims=True)) * scal_ref[1] + scal_ref[2 + hd]
                qe.append(jnp.concatenate([qs[mp], jnp.where(lane == 0, -shift, 0.0).astype(BF16)], axis=1))
            qe = jnp.concatenate(qe, axis=0)
            accx_ref[...] = jnp.zeros(accx_ref.shape, F32)

            def update(off, nblk, bias):
                rows = nblk * blk
                kb = jnp.concatenate([k_ref[pl.ds(off, rows), :], ext_k[nblk]], axis=1)
                vb = jnp.concatenate([v_ref[pl.ds(off, rows), :], ext_k[nblk]], axis=1)
                s = lax.dot_general(qe, kb, nt, preferred_element_type=F32)
                if bias is not None:
                    s = s + jnp.concatenate([bias, bias], axis=0)
                accx_ref[...] += jnp.dot(jnp.exp2(s).astype(BF16), vb, preferred_element_type=F32)

            _causal_sweep(qi, blk, bias_ref, update)
            a1 = accx_ref[0:blk]
            a2 = accx_ref[blk:2 * blk]
            store_o(qi, a1[:, 0:DV_B] / a1[:, DV_B:DV_B + 1] - lam * (a2[:, 0:DV_B] / a2[:, DV_B:DV_B + 1]))
            return carry

        lax.fori_loop(0, nq, q_block, 0)

    @pl.when(scal_ref[0] <= 0.5)
    def _():
        def q_block(qi, carry):
            qs = q_maps(qi)
            m_ref[...] = jnp.full(m_ref.shape, NEG_INF, F32)
            l_ref[...] = jnp.zeros(l_ref.shape, F32)
            acc_ref[...] = jnp.zeros(acc_ref.shape, F32)

            def update(off, nblk, bias):
                for i in range(nblk):
                    update_one(pl.multiple_of(off + i * blk, blk),
                               None if bias is None else bias[:, i * blk:(i + 1) * blk])

            def update_one(off, bias):
                kb = k_ref[pl.ds(off, blk), :]
                vb = v_ref[pl.ds(off, blk), :]
                for mp in range(2):
                    s = lax.dot_general(qs[mp], kb, nt, preferred_element_type=F32)
                    if bias is not None:
                        s = s + bias
                    m_old = m_ref[mp]
                    m_new = jnp.maximum(m_old, jnp.max(s, axis=-1, keepdims=True))
                    alpha = jnp.exp2(m_old - m_new)
                    p = jnp.exp2(s - jnp.concatenate([m_new] * (blk // LANES), axis=1))
                    l_ref[mp] = alpha * l_ref[mp] + jnp.sum(p, axis=-1, keepdims=True)
                    acc_ref[mp] = alpha * acc_ref[mp] + jnp.dot(p.astype(BF16), vb, preferred_element_type=F32)
                    m_ref[mp] = m_new

            _causal_sweep(qi, blk, bias_ref, update)
            store_o(qi, acc_ref[0] / l_ref[0] - lam * (acc_ref[1] / l_ref[1]))
            return carry

        lax.fori_loop(0, nq, q_block, 0)


def _prompt_attn(scal, lamv, qn, kn, vb, bias, n_seq, seq, blk, lam_init):
    nq = seq // blk
    r = n_seq * seq
    return pl.pallas_call(
        functools.partial(_attn_kernel, blk=blk, nq=nq, lam_init=lam_init),
        grid=(n_seq, H_B),
        in_specs=[
            pl.BlockSpec(memory_space=pltpu.SMEM),
            pl.BlockSpec(lamv.shape, lambda b, h: (0, 0)),
            pl.BlockSpec((seq, LANES), lambda b, h: (b, h)),
            pl.BlockSpec((seq, LANES), lambda b, h: (b, h)),
            pl.BlockSpec((seq, LANES), lambda b, h: (b, h)),
            pl.BlockSpec((1, blk, 2 * blk), lambda b, h: (h, 0, 0)),
        ],
        out_specs=pl.BlockSpec((seq, LANES), lambda b, h: (b, h)),
        out_shape=jax.ShapeDtypeStruct((r, HB_W), F32),
        scratch_shapes=[pltpu.VMEM((2, blk, LANES), F32), pltpu.VMEM((2, blk, LANES), F32),
                        pltpu.VMEM((2, blk, DV_B), F32), pltpu.VMEM((2 * blk, 2 * DV_B), F32)],
        compiler_params=_cparams(2, 48),
        name="prompt_attn",
    )(scal, lamv, qn, kn, vb, bias)


def _tri_masks():
    i = lax.broadcasted_iota(jnp.int32, (CHUNK, CHUNK), 0)
    j = lax.broadcasted_iota(jnp.int32, (CHUNK, CHUNK), 1)
    incl = i >= j
    strict = i > j
    eye = (i == j).astype(F32)
    base = jnp.logical_and(strict, (i // SUBLANES) == (j // SUBLANES))
    levels = []
    s = SUBLANES
    while s < CHUNK:
        levels.append(jnp.logical_and((i // (2 * s)) == (j // (2 * s)), (i // s) > (j // s)))
        s *= 2
    return incl, strict, eye, base, levels


def _gdn_sequence(b, qkv_ref, ba_ref, prm, o_ref, sout_ref, s_ref, masks, nchunk):
    incl, strict, eye, base, levels = masks
    tril = incl.astype(F32)
    nt = (((1,), (1,)), ((), ()))
    y = qkv_ref[b]
    bg = ba_ref[b]
    beta_all = _sigmoid(bg)
    g_all = -jnp.exp(prm[0:1]) * _softplus(bg + prm[1:2])
    gcs = [jnp.dot(tril, g_all[c * CHUNK:(c + 1) * CHUNK], precision=lax.Precision.HIGHEST,
                   preferred_element_type=F32) for c in range(nchunk)]
    yield
    gct = [gc.T for gc in gcs]
    grp = [(c, h) for c in range(nchunk) for h in range(H_A)]
    q_, k_, kb_, vb_, dec_, eg_, ekd_, gl_ = [], [], [], [], [], [], [], []
    for (c, h) in grp:
        rs = slice(c * CHUNK, (c + 1) * CHUNK)
        q = y[rs, h * DK_A:(h + 1) * DK_A]
        k = y[rs, H_A * DK_A + h * DK_A:H_A * DK_A + (h + 1) * DK_A]
        v = y[rs, 2 * H_A * DK_A + h * DV_A:2 * H_A * DK_A + (h + 1) * DV_A]
        beta = beta_all[rs, h:h + 1]
        gcol = gcs[c][:, H_A + h:H_A + h + 1]
        grow = gct[c][H_A + h:H_A + h + 1, :]
        glast = gcs[c][CHUNK - 1:CHUNK, H_A + h:H_A + h + 1]
        dec_.append(jnp.where(incl, jnp.exp(jnp.where(incl, gcol - grow, 0.0)), 0.0))
        eg_.append(jnp.exp(gcol))
        ekd_.append(jnp.exp(glast - gcol))
        gl_.append(jnp.exp(glast))
        q_.append(q)
        k_.append(k)
        kb_.append(k * beta)
        vb_.append(v * beta)
    n = len(grp)
    yield
    kk_qk = [lax.dot_general(jnp.concatenate([kb_[g], q_[g]], axis=0).astype(BF16), k_[g].astype(BF16),
                             nt, preferred_element_type=F32) for g in range(n)]
    yield
    ms = [jnp.where(strict, kk_qk[g][0:CHUNK] * dec_[g], 0.0) for g in range(n)]
    a_ = [(kk_qk[g][CHUNK:2 * CHUNK] * dec_[g]).astype(BF16) for g in range(n)]
    d = [jnp.where(base, m, 0.0) for m in ms]
    yield
    d2 = [_bdot(a, a) for a in d]
    yield
    d4 = [_bdot(a, a) for a in d2]
    x = [_bdot(eye - a, eye + bb) for a, bb in zip(d, d2)]
    yield
    x = [_bdot(a, eye + bb) for a, bb in zip(x, d4)]
    yield
    for lvl in levels:
        cm = [jnp.where(lvl, m, 0.0) for m in ms]
        xc = [_bdot(a, bb) for a, bb in zip(x, cm)]
        yield
        xcx = [_bdot(a, bb) for a, bb in zip(xc, x)]
        yield
        x = [a - bb for a, bb in zip(x, xcx)]
    rhs = [jnp.concatenate([vb_[g], kb_[g] * eg_[g]], axis=1) for g in range(n)]
    yield
    uw = [_bdot(x[g], rhs[g]) for g in range(n)]
    yield
    wq_ = [jnp.concatenate([uw[g][:, DV_A:DV_A + DK_A], q_[g] * eg_[g]], axis=0).astype(BF16)
           for g in range(n)]
    kd_ = [(k_[g] * ekd_[g]).astype(BF16) for g in range(n)]
    s = [s_ref[b, h] for h in range(H_A)]
    yield
    for c in range(nchunk):
        gi = [c * H_A + h for h in range(H_A)]
        ws = [jnp.dot(wq_[gi[h]], s[h].astype(BF16), preferred_element_type=F32) for h in range(H_A)]
        yield
        vn = [(uw[gi[h]][:, 0:DV_A] - ws[h][0:CHUNK]).astype(BF16) for h in range(H_A)]
        yield
        av = [jnp.dot(a_[gi[h]], vn[h], preferred_element_type=F32) for h in range(H_A)]
        kv = [lax.dot_general(kd_[gi[h]], vn[h], (((0,), (0,)), ((), ())), preferred_element_type=F32)
              for h in range(H_A)]
        yield
        for h in range(H_A):
            o_ref[b, c * CHUNK:(c + 1) * CHUNK, h * DV_A:(h + 1) * DV_A] = ws[h][CHUNK:2 * CHUNK] + av[h]
            s[h] = s[h] * gl_[gi[h]] + kv[h]
        yield
    for h in range(H_A):
        s_ref[b, h] = s[h]
        sout_ref[b, h] = s[h]


def _gdn_kernel(qkv_ref, ba_ref, prm_ref, o_ref, sout_ref, s_ref, *, tb, n_seq):
    j = pl.program_id(0)

    @pl.when(j == 0)
    def _():
        s_ref[...] = jnp.zeros(s_ref.shape, F32)

    prm = prm_ref[...]
    masks = _tri_masks()
    chains = [_gdn_sequence(b, qkv_ref, ba_ref, prm, o_ref, sout_ref, s_ref, masks, tb // CHUNK)
              for b in range(n_seq)]
    live = []
    while chains or live:
        if chains:
            live.append(chains.pop(0))
        for ch in list(live):
            if next(ch, StopIteration) is StopIteration:
                live.remove(ch)


def _prompt_gdn(qkv, ba, prm, n_seq, seq, tb):
    nb = seq // tb
    return pl.pallas_call(
        functools.partial(_gdn_kernel, tb=tb, n_seq=n_seq),
        grid=(nb,),
        in_specs=[
            pl.BlockSpec((n_seq, tb, QKV_A), lambda j: (0, j, 0)),
            pl.BlockSpec((n_seq, tb, LANES), lambda j: (0, j, 0)),
            pl.BlockSpec(prm.shape, lambda j: (0, 0)),
        ],
        out_specs=[
            pl.BlockSpec((n_seq, tb, H_A * DV_A), lambda j: (0, j, 0)),
            pl.BlockSpec((n_seq, H_A, DK_A, DV_A), lambda j: (0, 0, 0, 0)),
        ],
        out_shape=[
            jax.ShapeDtypeStruct((n_seq, seq, H_A * DV_A), F32),
            jax.ShapeDtypeStruct((n_seq, H_A, DK_A, DV_A), F32),
        ],
        scratch_shapes=[pltpu.VMEM((n_seq, H_A, DK_A, DV_A), F32)],
        compiler_params=_cparams(1, 48),
        name="prompt_gdn",
    )(qkv, ba, prm)


def _merge_kernel(x_ref, oa_ref, ob_ref, nw_ref, w2_ref, onw_ref, sbw_ref, wb_ref, wo_ref, out_ref, *, ob_scale):
    x = x_ref[...]
    h = _rms(x, nw_ref[...]).astype(BF16)
    z = jnp.dot(h, w2_ref[:, P2_Z:P2_GA], preferred_element_type=F32)
    ob = ob_ref[...]
    sbw = sbw_ref[...]
    ob_n = [(_rms(ob[:, hd * DV_B:(hd + 1) * DV_B], sbw) * ob_scale).astype(BF16) for hd in range(H_B)]
    ga = jnp.dot(h, w2_ref[:, P2_GA:P2_GB], preferred_element_type=F32)
    oa = oa_ref[...]
    onw = onw_ref[...]
    oa_n = [(_rms(oa[:, hd * DV_A:(hd + 1) * DV_A], onw) * _silu(z[:, hd * DV_A:(hd + 1) * DV_A])).astype(BF16)
            for hd in range(H_A)]
    pb = jnp.dot(jnp.concatenate(ob_n, axis=1), wb_ref[1], preferred_element_type=F32)
    sga = _sigmoid(ga)
    gb = jnp.dot(h, w2_ref[:, P2_GB:P2_END], preferred_element_type=F32)
    pa = jnp.dot(jnp.concatenate(oa_n, axis=1), wb_ref[0], preferred_element_type=F32)
    mixed = (sga * pa + _sigmoid(gb) * pb).astype(BF16)
    out_ref[...] = x + jnp.dot(mixed, wo_ref[...], preferred_element_type=F32)


def _merge(x, oa, ob, nw, w2, onw, sbw, wb, wo, tm, ob_scale):
    r = x.shape[0]
    row = lambda w: pl.BlockSpec((tm, w), lambda i: (i, 0))
    full = lambda a: pl.BlockSpec(a.shape, lambda i: (0,) * a.ndim)
    return pl.pallas_call(
        functools.partial(_merge_kernel, ob_scale=ob_scale),
        grid=(r // tm,),
        in_specs=[row(D_MODEL), row(H_A * DV_A), row(H_B * DV_B), full(nw), full(w2), full(onw), full(sbw),
                  full(wb), full(wo)],
        out_specs=row(D_MODEL),
        out_shape=jax.ShapeDtypeStruct((r, D_MODEL), F32),
        compiler_params=_cparams(1, 52),
        name="merge",
    )(x, oa, ob, nw, w2, onw, sbw, wb, wo)


FF_CHUNK = 256
UP_AHEAD = 2
DOWN_GROUP = 4


def _ffn_kernel(x_ref, st_ref, nw_ref, wup_ref, cw_ref, cb_ref, wdn_ref, out_ref, stout_ref,
                carry_ref, ext_ref, *, tm, shift, halo):
    i = pl.program_id(1)

    @pl.when(i == 0)
    def _():
        carry_ref[...] = st_ref[0]

    x = x_ref[...]
    h = _rms(x, nw_ref[...]).astype(BF16)
    cw = cw_ref[...]
    cb = cb_ref[...]
    n_chunks = D_FF // FF_CHUNK

    def col(c, part):
        lo = part * D_FF + c * FF_CHUNK
        return slice(lo, lo + FF_CHUNK)

    def up_proj(c):
        return [jnp.dot(h, wup_ref[:, col(c, part)], preferred_element_type=F32) for part in range(2)]

    def gated(c, ups):
        ext = ext_ref.at[c % 2]
        parts = []
        for part in range(2):
            sl = col(c, part)
            es = slice(part * FF_CHUNK, (part + 1) * FF_CHUNK)
            ext[0:halo, es] = carry_ref[:, sl]
            ext[halo:halo + tm, es] = ups[part]
            u = cw[2:3, sl] * ups[part] + cb[:, sl]
            u = u + cw[1:2, sl] * ext[halo - shift:halo - shift + tm, es]
            u = u + cw[0:1, sl] * ext[halo - 2 * shift:halo - 2 * shift + tm, es]
            carry_ref[:, sl] = ext[tm:tm + halo, es]
            parts.append(u)
        return (_silu(parts[0]) * parts[1]).astype(BF16)

    acc = x
    ups, acts = {}, {}
    for s in range(n_chunks + UP_AHEAD + 1):
        if s < n_chunks:
            ups[s] = up_proj(s)
        c = s - UP_AHEAD
        if 0 <= c < n_chunks:
            acts[c] = gated(c, ups.pop(c))
        c = s - UP_AHEAD - 1
        if 0 <= c < n_chunks and (c % DOWN_GROUP == DOWN_GROUP - 1 or c == n_chunks - 1):
            c0 = c - c % DOWN_GROUP
            a = jnp.concatenate([acts.pop(i) for i in range(c0, c + 1)], axis=1)
            acc = acc + jnp.dot(a, wdn_ref[c0 * FF_CHUNK:(c + 1) * FF_CHUNK, :], preferred_element_type=F32)
    out_ref[...] = acc
    stout_ref[0] = carry_ref[...]


def _ffn(x, st, nw, wup, cw, cb, wdn, groups, tm, shift, halo):
    r = x.shape[0]
    tiles = r // groups // tm
    full = lambda a: pl.BlockSpec(a.shape, lambda g, i: (0,) * a.ndim)
    return pl.pallas_call(
        functools.partial(_ffn_kernel, tm=tm, shift=shift, halo=halo),
        grid=(groups, tiles),
        in_specs=[
            pl.BlockSpec((tm, D_MODEL), lambda g, i: (g * tiles + i, 0)),
            pl.BlockSpec((1, halo, 2 * D_FF), lambda g, i: (g, 0, 0)),
            full(nw), full(wup), full(cw), full(cb), full(wdn),
        ],
        out_specs=[
            pl.BlockSpec((tm, D_MODEL), lambda g, i: (g * tiles + i, 0)),
            pl.BlockSpec((1, halo, 2 * D_FF), lambda g, i: (g, 0, 0)),
        ],
        out_shape=[jax.ShapeDtypeStruct((r, D_MODEL), F32),
                   jax.ShapeDtypeStruct((groups, halo, 2 * D_FF), F32)],
        scratch_shapes=[pltpu.VMEM((halo, 2 * D_FF), F32), pltpu.VMEM((2, halo + tm, 2 * FF_CHUNK), F32)],
        compiler_params=_cparams(2, 56),
        name="ffn",
    )(x, st, nw, wup, cw, cb, wdn)


def _sgdn_pre_kernel(qkv_ref, ba_ref, prm_ref, wq_ref, u_ref, kd_ref, a_ref, gl_ref, *, t_new, nb):
    prm = prm_ref[...]
    lane = lax.broadcasted_iota(jnp.int32, (nb, LANES), 1)

    ys, betas, gs = [], [], []
    for t in range(t_new):
        ys.append(qkv_ref[t])
        bg = ba_ref[t]
        betas.append(_sigmoid(bg))
        gs.append(-jnp.exp(prm[0:1]) * _softplus(bg + prm[1:2]))
    gcs = [gs[0]]
    for t in range(1, t_new):
        gcs.append(gcs[-1] + gs[t])

    a_out = [jnp.zeros((nb, LANES), F32) for _ in range(t_new)]
    for h in range(H_A):
        q, k, v, beta, gc = [], [], [], [], []
        for t in range(t_new):
            q.append(ys[t][:, h * DK_A:(h + 1) * DK_A])
            k.append(ys[t][:, H_A * DK_A + h * DK_A:H_A * DK_A + (h + 1) * DK_A])
            v.append(ys[t][:, 2 * H_A * DK_A + h * DV_A:2 * H_A * DK_A + (h + 1) * DV_A])
            beta.append(betas[t][:, h:h + 1])
            gc.append(gcs[t][:, H_A + h:H_A + h + 1])
        m = [[None] * t_new for _ in range(t_new)]
        for i in range(t_new):
            for jj in range(i + 1):
                dec = jnp.exp(gc[i] - gc[jj])
                if jj < i:
                    m[i][jj] = beta[i] * jnp.sum(k[i] * k[jj], axis=-1, keepdims=True) * dec
                aij = jnp.sum(q[i] * k[jj], axis=-1, keepdims=True) * dec
                a_out[i] = jnp.where(lane == h * SUBLANES + jj, aij, a_out[i])
        tm_ = [[None] * t_new for _ in range(t_new)]
        for i in range(t_new):
            for jj in range(i):
                acc = m[i][jj]
                for l in range(jj + 1, i):
                    acc = acc + m[i][l] * tm_[l][jj]
                tm_[i][jj] = -acc
        vb = [v[t] * beta[t] for t in range(t_new)]
        kbg = [k[t] * (beta[t] * jnp.exp(gc[t])) for t in range(t_new)]
        hs = slice(h * DK_A, (h + 1) * DK_A)
        for i in range(t_new):
            u = vb[i]
            w = kbg[i]
            for jj in range(i):
                u = u + tm_[i][jj] * vb[jj]
                w = w + tm_[i][jj] * kbg[jj]
            u_ref[i, :, hs] = u
            wq_ref[i, :, hs] = w
            wq_ref[t_new + i, :, hs] = q[i] * jnp.exp(gc[i])
            kd_ref[i, :, hs] = k[i] * jnp.exp(gc[t_new - 1] - gc[i])
        gl_ref[:, hs] = jnp.broadcast_to(jnp.exp(gc[t_new - 1]), (nb, DK_A))
    for i in range(t_new):
        a_ref[i] = a_out[i]


def _sgdn_pre(qkv_t, ba_t, prm, t_new, nb):
    wide = H_A * DK_A
    return pl.pallas_call(
        functools.partial(_sgdn_pre_kernel, t_new=t_new, nb=nb),
        out_shape=[
            jax.ShapeDtypeStruct((2 * t_new, nb, wide), F32),
            jax.ShapeDtypeStruct((t_new, nb, wide), F32),
            jax.ShapeDtypeStruct((t_new, nb, wide), F32),
            jax.ShapeDtypeStruct((t_new, nb, LANES), F32),
            jax.ShapeDtypeStruct((nb, wide), F32),
        ],
        compiler_params=pltpu.CompilerParams(vmem_limit_bytes=48 * 1024 * 1024),
        name="sample_gdn_pre",
    )(qkv_t, ba_t, prm)


SGDN_BB = 8


def _sgdn_state_kernel(wq_ref, u_ref, kd_ref, a_ref, gl_ref, s0_ref, *refs, t_new):
    o_ref, s1_ref = refs[-2:]
    rows = 2 * t_new
    rid = lax.broadcasted_iota(jnp.int32, (rows, DK_A), 0)
    zpad = jnp.zeros((rows - t_new, DK_A), F32)
    pairs = [(bi, h) for bi in range(SGDN_BB) for h in range(H_A)]
    hs = [slice(h * DK_A, (h + 1) * DK_A) for h in range(H_A)]
    r = {(bi, h): jnp.dot(wq_ref[:, bi, hs[h]].astype(BF16), s0_ref[bi, h].astype(BF16),
                          preferred_element_type=F32) for (bi, h) in pairs}
    v_new = {}
    for (bi, h) in pairs:
        u8 = jnp.concatenate([u_ref[:, bi, hs[h]], zpad], axis=0)
        v_new[bi, h] = jnp.where(rid < t_new, u8 - r[bi, h], 0.0)
    kv = {}
    for (bi, h) in pairs:
        kd8 = jnp.concatenate([kd_ref[:, bi, hs[h]], zpad], axis=0)
        kv[bi, h] = lax.dot_general(kd8.astype(BF16), v_new[bi, h].astype(BF16), (((0,), (0,)), ((), ())),
                                    preferred_element_type=F32)
    for (bi, h) in pairs:
        amat = a_ref[:, bi, :]
        o = r[bi, h][t_new:rows]
        for jj in range(t_new):
            col = amat[:, h * SUBLANES + jj:h * SUBLANES + jj + 1]
            o = o + col * v_new[bi, h][jj:jj + 1, :]
        o_ref[:, bi, hs[h]] = o
        s1_ref[bi, h] = s0_ref[bi, h] * gl_ref[bi:bi + 1, hs[h]] + kv[bi, h]


def _sgdn_state(wq, u, kd, a, gl, s0_all, li, t_new, nb, stacked=None):
    wide = H_A * DK_A
    bb = SGDN_BB
    state_spec = pl.BlockSpec((None, bb, H_A, DK_A, DV_A), lambda i: (li, i, 0, 0, 0))
    in_specs = [
        pl.BlockSpec((2 * t_new, bb, wide), lambda i: (0, i, 0)),
        pl.BlockSpec((t_new, bb, wide), lambda i: (0, i, 0)),
        pl.BlockSpec((t_new, bb, wide), lambda i: (0, i, 0)),
        pl.BlockSpec((t_new, bb, LANES), lambda i: (0, i, 0)),
        pl.BlockSpec((bb, wide), lambda i: (i, 0)),
        state_spec,
    ]
    args = [wq, u, kd, a, gl, s0_all]
    aliases = {}
    if stacked is not None:
        in_specs.append(pl.BlockSpec(memory_space=pl.ANY))
        args.append(stacked)
        aliases = {6: 1}
    return pl.pallas_call(
        functools.partial(_sgdn_state_kernel, t_new=t_new),
        grid=(nb // bb,),
        in_specs=in_specs,
        out_specs=[pl.BlockSpec((t_new, bb, wide), lambda i: (0, i, 0)), state_spec],
        out_shape=[jax.ShapeDtypeStruct((t_new, nb, wide), F32),
                   jax.ShapeDtypeStruct(s0_all.shape, F32)],
        input_output_aliases=aliases,
        compiler_params=_cparams(1, 32),
        name="sample_gdn_state",
    )(*args)


NEW_ROWS = 16


def _sattn_kernel(pt_ref, lam_ref, q_ref, kn_ref, vn_ref, bias_ref, *rest, n_pages, t_new, lam_init):
    del pt_ref
    k_pages = rest[:n_pages]
    v_pages = rest[n_pages:2 * n_pages]
    o_ref = rest[2 * n_pages]
    rows = 2 * t_new
    q = q_ref[0]
    bias = bias_ref[...]
    lam = _lam_value(lam_ref[...], lam_init)
    r = lax.broadcasted_iota(jnp.int32, (rows, 2 * DH_B), 0)
    c = lax.broadcasted_iota(jnp.int32, (rows, 2 * DH_B), 1)
    map_mask = (c // DH_B) == (r // t_new)
    pad = jnp.zeros((PAGE - NEW_ROWS, LANES), BF16)
    nt = (((1,), (1,)), ((), ()))

    heads = range(H_B)
    lanes = [slice(h * LANES, (h + 1) * LANES) for h in heads]
    head_rows = [pl.ds(h, PAGE, stride=H_B) for h in heads]
    qx = [jnp.where(map_mask, jnp.concatenate([q[:, lanes[h]]] * 2, axis=0), 0.0).astype(BF16) for h in heads]
    bh = [bias[h * rows:(h + 1) * rows] for h in heads]

    groups = [list(range(p, min(p + MXU_DIM // PAGE, n_pages))) for p in range(0, n_pages, MXU_DIM // PAGE)]

    def head_tile(page_refs, grp, h):
        return jnp.concatenate([page_refs[p][0, 0, head_rows[h], :] for p in grp], axis=0).astype(BF16)

    s_parts = [[] for _ in heads]
    for grp in groups:
        for h in heads:
            s = lax.dot_general(qx[h], head_tile(k_pages, grp, h), nt, preferred_element_type=F32)
            if grp[-1] == n_pages - 1:
                zeros = [jnp.zeros((rows, PAGE), F32)] * (len(grp) - 1)
                s = s + jnp.concatenate(zeros + [bh[h][:, 0:PAGE]], axis=1)
            s_parts[h].append(s)
    for h in heads:
        kn = jnp.concatenate([kn_ref[0, :, lanes[h]], pad], axis=0)
        s_parts[h].append(lax.dot_general(qx[h], kn, nt, preferred_element_type=F32) + bh[h][:, PAGE:2 * PAGE])

    m = []
    for h in heads:
        mh = s_parts[h][0].max(axis=-1, keepdims=True)
        for s in s_parts[h][1:]:
            mh = jnp.maximum(mh, s.max(axis=-1, keepdims=True))
        m.append(mh)
    l = [jnp.zeros((rows, 1), F32) for _ in heads]
    acc = [jnp.zeros((rows, DV_B), F32) for _ in heads]
    for gi in range(len(groups) + 1):
        for h in heads:
            pr = jnp.exp2(s_parts[h][gi] - m[h])
            l[h] = l[h] + jnp.sum(pr, axis=-1, keepdims=True)
            if gi < len(groups):
                vv = head_tile(v_pages, groups[gi], h)
            else:
                vv = jnp.concatenate([vn_ref[0, :, lanes[h]], pad], axis=0)
            acc[h] = acc[h] + jnp.dot(pr.astype(BF16), vv, preferred_element_type=F32)
    for h in heads:
        a = acc[h] / l[h]
        o_ref[0, :, lanes[h]] = a[0:t_new] - lam * a[t_new:rows]


def _sample_attn(page_table, lamv, q_b, kn_b, vn_b, bias, ck, cv, li, t_new, lam_init):
    nb, n_pages = page_table.shape
    page_spec = lambda p: pl.BlockSpec((1, 1, PAGE * H_B, LANES), lambda b, pt, p=p: (li, pt[b, p], 0, 0))
    tok_spec = pl.BlockSpec((1, t_new, HB_W), lambda b, pt: (b, 0, 0))
    new_spec = pl.BlockSpec((1, NEW_ROWS, HB_W), lambda b, pt: (b, 0, 0))
    grid_spec = pltpu.PrefetchScalarGridSpec(
        num_scalar_prefetch=1,
        grid=(nb,),
        in_specs=[pl.BlockSpec(lamv.shape, lambda b, pt: (0, 0)), tok_spec, new_spec, new_spec,
                  pl.BlockSpec(bias.shape, lambda b, pt: (0, 0))]
                 + [page_spec(p) for p in range(n_pages)] * 2,
        out_specs=pl.BlockSpec((1, t_new, HB_W), lambda b, pt: (b, 0, 0)),
    )
    return pl.pallas_call(
        functools.partial(_sattn_kernel, n_pages=n_pages, t_new=t_new, lam_init=lam_init),
        grid_spec=grid_spec,
        out_shape=jax.ShapeDtypeStruct((nb, t_new, HB_W), F32),
        compiler_params=_cparams(1, 40),
        name="sample_attn",
    )(page_table, lamv, q_b, kn_b, vn_b, bias, *([ck] * n_pages), *([cv] * n_pages))


def _pick(n, pref):
    return pref if n % pref == 0 else n


def _to_bmajor(a_t, t_new, nb):
    return jnp.swapaxes(a_t.reshape(t_new, nb, -1), 0, 1)


def _layer_weights(li, rel_bias, norm_mix, w_in, conv_a, a_log, dt_bias, onorm_a, qnorm_b, knorm_b,
                   lam_q1, lam_k1, lam_q2, lam_k2, subln_b, w_branch, w_o, norm_ffn, w_up, conv_f,
                   conv_f_b, w_down):
    w = w_in[li]
    o_z = QKV_A
    o_b = o_z + H_A * DV_A
    o_qb = o_b + 2 * H_A
    o_kb = o_qb + HB_W
    o_vb = o_kb + HB_W
    o_ga = o_vb + H_B * DV_B
    o_gb = o_ga + D_MODEL
    ba_cols = jnp.pad(w[:, o_b:o_qb], ((0, 0), (0, LANES - 2 * H_A)))
    w1 = jnp.concatenate([w[:, 0:o_z], w[:, o_qb:o_ga], ba_cols], axis=1).astype(BF16)
    w2 = jnp.concatenate([w[:, o_z:o_b], w[:, o_ga:]], axis=1).astype(BF16)
    prm = jnp.zeros((2, LANES), F32)
    prm = prm.at[0, H_A:2 * H_A].set(a_log[li]).at[1, H_A:2 * H_A].set(dt_bias[li])
    k_bound = 1.01 * math.sqrt(DH_B) * jnp.max(jnp.abs(knorm_b[li]))
    q_bound = 1.01 * Q_SCALE * math.sqrt(DH_B) * jnp.max(jnp.abs(qnorm_b[li]))
    rel = (rel_bias - rel_bias[NUM_BUCKETS - 1:NUM_BUCKETS]) * LOG2E
    spread = 2.0 * q_bound * k_bound + jnp.max(jnp.max(rel, axis=0) - jnp.min(rel, axis=0))
    scal = jnp.concatenate([jnp.where(spread <= SAFE_SPREAD_BITS, 1.0, 0.0).reshape(1), k_bound.reshape(1),
                            jnp.max(rel, axis=0), jnp.zeros((2,), F32)]).astype(F32)
    return dict(
        scal=scal,
        nw=norm_mix[li].reshape(1, D_MODEL), w1=w1, w2=w2,
        qw=jnp.tile(qnorm_b[li], 2 * H_B).reshape(1, HB_W),
        kw=jnp.tile(knorm_b[li], 2 * H_B).reshape(1, HB_W),
        cw_a=conv_a[li], prm=prm,
        onw=onorm_a[li].reshape(1, DV_A), sbw=subln_b[li].reshape(1, DV_B),
        lamv=jnp.stack([lam_q1[li], lam_k1[li], lam_q2[li], lam_k2[li]]),
        wb=w_branch[li].astype(BF16), wo=w_o[li].astype(BF16),
        nwf=norm_ffn[li].reshape(1, D_MODEL), wup=w_up[li].astype(BF16),
        cw_f=conv_f[li], cb_f=conv_f_b[li].reshape(1, 2 * D_FF), wdn=w_down[li].astype(BF16),
    )


def _group_ones():
    i = jnp.arange(MXU_DIM)
    return ((i[:, None] // DH_B) == (i[None, :] // DH_B)).astype(BF16)


def _prompt_layer(x, wt, bias_p, g, n_seq, seq, lam_init, li, depth, kv_stacked):
    tm = _pick(seq, 512)
    blk = bias_p.shape[1]
    zero_st_a = jnp.zeros((n_seq, SUBLANES, QKV_A), F32)
    qkv, ba, qn, kn, kf, vf, vb, cst_a = _inproj(x, wt["nw"], wt["w1"], wt["qw"], wt["kw"], g, zero_st_a,
                                                 wt["cw_a"], n_seq, tm, 1, SUBLANES,
                                                 layer=li, depth=depth, stacked=kv_stacked)
    o_a, s_fin = _prompt_gdn(qkv.reshape(n_seq, seq, QKV_A), ba.reshape(n_seq, seq, LANES),
                             wt["prm"], n_seq, seq, _pick(seq, 256))
    o_a = o_a.reshape(n_seq * seq, H_A * DV_A)
    o_b = _prompt_attn(wt["scal"], wt["lamv"], qn, kn, vb, bias_p, n_seq, seq, blk, lam_init)
    x = _merge(x, o_a, o_b, wt["nw"], wt["w2"], wt["onw"], wt["sbw"], wt["wb"], wt["wo"], tm, 1.0 - lam_init)
    zero_st = jnp.zeros((n_seq, SUBLANES, 2 * D_FF), F32)
    x, cst_f = _ffn(x, zero_st, wt["nwf"], wt["wup"], wt["cw_f"], wt["cb_f"], wt["wdn"],
                    n_seq, _pick(seq, 512), 1, SUBLANES)
    return (x, kf, vf, s_fin, cst_a[:, SUBLANES - (CONV_A - 1):], cst_f[:, SUBLANES - (CONV_F - 1):])


def _sample_layer(x_t, wt, bias_s, g, page_table, ck, cv, li, s0_all, s1_stacked, cst_a, cst_f, nb, t_new,
                  lam_init):
    r = nb * t_new
    halo_a = (CONV_A - 1) * nb
    st_a_t = jnp.swapaxes(cst_a, 0, 1).reshape(1, halo_a, QKV_A)
    qkv, ba, qn, kn, kf, vf, vb, cst_a_t = _inproj(x_t, wt["nw"], wt["w1"], wt["qw"], wt["kw"], g, st_a_t,
                                                   wt["cw_a"], 1, r, nb, halo_a)
    cst_a_t = cst_a_t.reshape(CONV_A - 1, nb, QKV_A)
    wq, u, kd, a, gl = _sgdn_pre(qkv.reshape(t_new, nb, QKV_A), ba.reshape(t_new, nb, LANES), wt["prm"],
                                 t_new, nb)
    o_a_t, s1 = _sgdn_state(wq, u, kd, a, gl, s0_all, li, t_new, nb, stacked=s1_stacked)
    pad_new = lambda a: jnp.pad(_to_bmajor(a, t_new, nb), ((0, 0), (0, NEW_ROWS - t_new), (0, 0)))
    o_b = _sample_attn(page_table, wt["lamv"], _to_bmajor(qn, t_new, nb).astype(F32), pad_new(kn),
                       pad_new(vb), bias_s, ck, cv, li, t_new, lam_init)
    o_b_t = jnp.swapaxes(o_b, 0, 1).reshape(r, HB_W)
    x_t = _merge(x_t, o_a_t.reshape(r, H_A * DV_A), o_b_t, wt["nw"], wt["w2"], wt["onw"], wt["sbw"],
                 wt["wb"], wt["wo"], r, 1.0 - lam_init)
    halo = (CONV_F - 1) * nb
    st_f_t = jnp.swapaxes(cst_f, 0, 1).reshape(1, halo, 2 * D_FF)
    x_t, cst_f_t = _ffn(x_t, st_f_t, wt["nwf"], wt["wup"], wt["cw_f"], wt["cb_f"], wt["wdn"],
                        1, r, nb, halo)
    return (x_t, _to_bmajor(kf, t_new, nb).reshape(nb, t_new, H_B, 2 * DH_B),
            _to_bmajor(vf, t_new, nb).reshape(nb, t_new, H_B, DV_B), s1,
            jnp.swapaxes(cst_a_t, 0, 1), jnp.swapaxes(cst_f_t.reshape(CONV_F - 1, nb, 2 * D_FF), 0, 1))


def kernel(x_prompt, x_sample, cache_k, cache_v, state_delta, state_conv_a, state_conv_ffn, page_table,
           rel_bias, norm_mix, w_in, conv_a, a_log, dt_bias, onorm_a, qnorm_b, knorm_b, lam_q1, lam_k1,
           lam_q2, lam_k2, subln_b, w_branch, w_o, norm_ffn, w_up, conv_f, conv_f_b, w_down):
    n_seq, seq, _ = x_prompt.shape
    nb, t_new, _ = x_sample.shape
    depth = w_in.shape[0]
    n_pool = cache_k.shape[1]

    rb_flat = rel_bias.T.reshape(-1)
    bias_p = _prompt_bias(rb_flat, _pick(seq, 512))
    bias_s = _sample_bias(rb_flat, t_new)
    g = _group_ones()
    ck = cache_k.reshape(depth, n_pool, PAGE * H_B, 2 * DH_B)
    cv = cache_v.reshape(depth, n_pool, PAGE * H_B, DV_B)

    xp = x_prompt.reshape(n_seq * seq, D_MODEL)
    xs = jnp.swapaxes(x_sample, 0, 1).reshape(t_new * nb, D_MODEL)
    outs_p, outs_s = [], []
    kv_stacked, s1_stacked = None, None
    for li in range(depth):
        wt = _layer_weights(li, rel_bias, norm_mix, w_in, conv_a, a_log, dt_bias, onorm_a, qnorm_b, knorm_b,
                            lam_q1, lam_k1, lam_q2, lam_k2, subln_b, w_branch, w_o, norm_ffn, w_up,
                            conv_f, conv_f_b, w_down)
        lam_init = 0.8 - 0.6 * math.exp(-0.3 * li)
        xp, kf_all, vf_all, *rest_p = _prompt_layer(xp, wt, bias_p, g, n_seq, seq, lam_init, li, depth,
                                                    kv_stacked)
        kv_stacked = (kf_all, vf_all)
        xs, ks, vs, s1_stacked, *rest_s = _sample_layer(xs, wt, bias_s, g, page_table, ck, cv, li, state_delta,
                                                        s1_stacked, state_conv_a[li], state_conv_ffn[li], nb,
                                                        t_new, lam_init)
        outs_p.append(rest_p)
        outs_s.append([ks, vs] + rest_s)

    stack = lambda outs, i: jnp.stack([o[i] for o in outs])
    y_prompt = xp.reshape(n_seq, seq, D_MODEL)
    y_sample = jnp.swapaxes(xs.reshape(t_new, nb, D_MODEL), 0, 1)
    k_prompt = kv_stacked[0].reshape(depth, n_seq, seq, H_B, 2 * DH_B)
    v_prompt = kv_stacked[1].reshape(depth, n_seq, seq, H_B, DV_B)
    return (y_prompt, y_sample,
            k_prompt, v_prompt, stack(outs_p, 0), stack(outs_p, 1), stack(outs_p, 2),
            stack(outs_s, 0), stack(outs_s, 1), s1_stacked, stack(outs_s, 2), stack(outs_s, 3))
```

```python
import functools
import math

import jax
import jax.numpy as jnp
from jax import lax
from jax.experimental import pallas as pl
from jax.experimental.pallas import tpu as pltpu

F32 = jnp.float32
BF16 = jnp.bfloat16

D_MODEL = 1024
H_A, DK_A, DV_A, CONV_A, CHUNK = 4, 128, 128, 4, 64
QKV_A = 2 * H_A * DK_A + H_A * DV_A
H_B, DH_B, DV_B = 4, 64, 128
HB_W = H_B * 2 * DH_B
PAGE = 128
D_FF, CONV_F = 2816, 3
NUM_BUCKETS, MAX_DISTANCE = 32, 128
NEG_INF = -1e30
EPS = 1e-6
ATT_SCALE = DH_B ** -0.5
LOG2E = math.log2(math.e)
Q_SCALE = ATT_SCALE * LOG2E
SAFE_SPREAD_BITS = 100.0

V7X_VMEM_BYTES = 64 * 1024 * 1024
LANES = 128
SUBLANES = 8
MXU_DIM = 256
MIB = 1024 * 1024

ROW_TILE = 512
ATTN_BLOCK = 512
GDN_TIME_BLOCK = 4 * CHUNK
VMEM_LARGE_MIB = 56
VMEM_MEDIUM_MIB = 48
VMEM_SMALL_MIB = 40

P1_QKV = 0
P1_Q = P1_QKV + QKV_A
P1_K = P1_Q + HB_W
P1_V = P1_K + HB_W
P1_BA = P1_V + H_B * DV_B
P1_END = P1_BA + LANES
P2_Z = 0
P2_GA = P2_Z + H_A * DV_A
P2_GB = P2_GA + D_MODEL
P2_END = P2_GB + D_MODEL


def _cparams(n_axes, vmem_mib):
    return pltpu.CompilerParams(
        dimension_semantics=("arbitrary",) * n_axes,
        vmem_limit_bytes=vmem_mib * MIB,
    )


def _bdot(a, b):
    return jnp.dot(a.astype(BF16), b.astype(BF16), preferred_element_type=F32)


def _rms(x, w):
    return x * lax.rsqrt(jnp.mean(x * x, axis=-1, keepdims=True) + EPS) * w


def _sigmoid(x):
    return 1.0 / (1.0 + jnp.exp(-x))


def _silu(x):
    return x * _sigmoid(x)


def _softplus(x):
    return jnp.maximum(x, 0.0) + jnp.log1p(jnp.exp(-jnp.abs(x)))


def _inproj_kernel(x_ref, nw_ref, w_ref, qw_ref, kw_ref, g_ref, st_ref, cw_ref, *refs, shift, halo):
    n_out = 8
    qkv_ref, ba_ref, qn_ref, kn_ref, kf_ref, vf_ref, vb_ref, stout_ref = refs[-n_out - 2:-2]
    carry_ref, ext_ref = refs[-2:]
    h = _rms(x_ref[...], nw_ref[...]).astype(BF16)

    def proj(lo, hi):
        return jnp.dot(h, w_ref[:, lo:hi], preferred_element_type=F32)

    tm = x_ref.shape[0]
    head_rows = [pl.ds(hd, tm, stride=H_B) for hd in range(H_B)]

    @pl.when(pl.program_id(1) == 0)
    def _():
        carry_ref[...] = st_ref[0]

    cw = cw_ref[...]
    g = g_ref[...]
    seg = H_A * DK_A

    def delta_front(part, raw):
        cols = slice(part * seg, (part + 1) * seg)
        ext_ref[0:halo, cols] = carry_ref[:, cols]
        ext_ref[halo:halo + tm, cols] = raw
        y = cw[CONV_A - 1:CONV_A, cols] * raw
        for i in range(CONV_A - 1):
            back = (CONV_A - 1 - i) * shift
            y = y + cw[i:i + 1, cols] * ext_ref[halo - back:halo - back + tm, cols]
        new_carry = ext_ref[tm:tm + halo, cols]
        carry_ref[:, cols] = new_carry
        stout_ref[0, :, cols] = new_carry
        y = _silu(y)
        if part == 2:
            qkv_ref[:, cols] = y
            return
        for hd in range(H_A):
            sl = slice(hd * DK_A, (hd + 1) * DK_A)
            t = y[:, sl]
            t = t * lax.rsqrt(jnp.sum(t * t, axis=-1, keepdims=True) + EPS)
            qkv_ref[:, part * seg + hd * DK_A:part * seg + (hd + 1) * DK_A] = t * (DK_A ** -0.5) if part == 0 else t

    def qk_norm(y, w):
        sq = y * y
        hi = sq.astype(BF16)
        lo = (sq - hi.astype(F32)).astype(BF16)
        outs = []
        for c in range(HB_W // MXU_DIM):
            sl = slice(c * MXU_DIM, (c + 1) * MXU_DIM)
            ss = (jnp.dot(hi[:, sl], g, preferred_element_type=F32)
                  + jnp.dot(lo[:, sl], g, preferred_element_type=F32))
            outs.append(y[:, sl] * lax.rsqrt(ss * (1.0 / DH_B) + EPS) * w[:, sl])
        return outs

    def attn_q(y):
        qn = qk_norm(y, qw_ref[...])
        for c in range(HB_W // MXU_DIM):
            qn_ref[:, c * MXU_DIM:(c + 1) * MXU_DIM] = (qn[c] * Q_SCALE).astype(BF16)

    def attn_k(y):
        kn = qk_norm(y, kw_ref[...])
        for c in range(HB_W // MXU_DIM):
            kn_ref[:, c * MXU_DIM:(c + 1) * MXU_DIM] = kn[c].astype(BF16)
            for i in range(MXU_DIM // LANES):
                kf_ref[head_rows[c * (MXU_DIM // LANES) + i], :] = kn[c][:, i * LANES:(i + 1) * LANES]

    def attn_v(v):
        for hd in range(H_B):
            vf_ref[head_rows[hd], :] = v[:, hd * DV_B:(hd + 1) * DV_B]
        vb_ref[...] = v.astype(BF16)

    def store_ba(y):
        ba_ref[...] = y

    stages = [
        ((P1_QKV, P1_QKV + seg), functools.partial(delta_front, 0)),
        ((P1_QKV + seg, P1_QKV + 2 * seg), functools.partial(delta_front, 1)),
        ((P1_QKV + 2 * seg, P1_Q), functools.partial(delta_front, 2)),
        ((P1_Q, P1_K), attn_q),
        ((P1_K, P1_V), attn_k),
        ((P1_V, P1_BA), attn_v),
        ((P1_BA, P1_END), store_ba),
    ]
    pending = None
    for cols, post in stages:
        y = proj(*cols)
        if pending is not None:
            pending[1](pending[0])
        pending = (y, post)
    pending[1](pending[0])


def _inproj(x, nw, w1, qw, kw, g, st, cw, groups, tm, shift, halo, layer=None, depth=None, stacked=None):
    r = x.shape[0]
    tiles = r // groups // tm
    row = lambda w: pl.BlockSpec((tm, w), lambda gi, i: (gi * tiles + i, 0))
    full = lambda a: pl.BlockSpec(a.shape, lambda gi, i: (0,) * a.ndim)
    st_spec = pl.BlockSpec((1, halo, QKV_A), lambda gi, i: (gi, 0, 0))
    outs = [(1, QKV_A, F32), (1, LANES, F32), (1, HB_W, BF16), (1, HB_W, BF16), (H_B, LANES, F32),
            (H_B, LANES, F32), (1, HB_W, BF16)]
    out_specs = [pl.BlockSpec((tm * m, w), lambda gi, i: (gi * tiles + i, 0)) for m, w, _ in outs]
    out_shape = [jax.ShapeDtypeStruct((r * m, w), dt) for m, w, dt in outs]
    out_specs.append(st_spec)
    out_shape.append(jax.ShapeDtypeStruct((groups, halo, QKV_A), F32))
    in_specs = [row(D_MODEL), full(nw), full(w1), full(qw), full(kw), full(g), st_spec, full(cw)]
    args = [x, nw, w1, qw, kw, g, st, cw]
    aliases = {}
    if layer is not None:
        for o in (4, 5):
            m, w, dt = outs[o]
            out_specs[o] = pl.BlockSpec((None, tm * m, w), lambda gi, i: (layer, gi * tiles + i, 0))
            out_shape[o] = jax.ShapeDtypeStruct((depth, r * m, w), dt)
        if stacked is not None:
            in_specs += [pl.BlockSpec(memory_space=pl.ANY)] * 2
            args += list(stacked)
            aliases = {len(args) - 2: 4, len(args) - 1: 5}
    return pl.pallas_call(
        functools.partial(_inproj_kernel, shift=shift, halo=halo),
        grid=(groups, tiles),
        in_specs=in_specs,
        out_specs=out_specs,
        out_shape=out_shape,
        input_output_aliases=aliases,
        scratch_shapes=[pltpu.VMEM((halo, QKV_A), F32), pltpu.VMEM((halo + tm, QKV_A), F32)],
        compiler_params=_cparams(2, VMEM_LARGE_MIB),
        name="inproj",
    )(*args)


def _bucket_bias(d, table):
    n = jnp.maximum(d, 0)
    max_exact = NUM_BUCKETS // 2
    nf = jnp.maximum(n, 1).astype(F32)
    large = max_exact + (jnp.log(nf / max_exact) / math.log(MAX_DISTANCE / max_exact)
                         * (NUM_BUCKETS - max_exact)).astype(jnp.int32)
    large = jnp.minimum(large, NUM_BUCKETS - 1)
    bucket = jnp.where(n < max_exact, n, large)
    val = jnp.zeros(d.shape, F32)
    for b in range(NUM_BUCKETS):
        val = jnp.where(bucket == b, table(b), val)
    return val


def _prompt_bias_kernel(rb_ref, o_ref, *, blk):
    h = pl.program_id(0)
    table = lambda b: rb_ref[h * NUM_BUCKETS + b]
    far = table(NUM_BUCKETS - 1)
    i = lax.broadcasted_iota(jnp.int32, (blk, blk), 0)
    j = lax.broadcasted_iota(jnp.int32, (blk, blk), 1)
    d0 = i - j
    o_ref[0, :, 0:blk] = (_bucket_bias(d0 + blk, table) - far) * LOG2E
    o_ref[0, :, blk:2 * blk] = jnp.where(d0 >= 0, (_bucket_bias(d0, table) - far) * LOG2E, NEG_INF)


def _prompt_bias(rb_flat, blk):
    return pl.pallas_call(
        functools.partial(_prompt_bias_kernel, blk=blk),
        grid=(H_B,),
        in_specs=[pl.BlockSpec(memory_space=pltpu.SMEM)],
        out_specs=pl.BlockSpec((1, blk, 2 * blk), lambda h: (h, 0, 0)),
        out_shape=jax.ShapeDtypeStruct((H_B, blk, 2 * blk), F32),
        compiler_params=_cparams(1, VMEM_SMALL_MIB),
        name="prompt_bias",
    )(rb_flat)


def _sample_bias_kernel(rb_ref, o_ref, *, t_new):
    rows, cols = o_ref.shape
    r = lax.broadcasted_iota(jnp.int32, (rows, cols), 0)
    c = lax.broadcasted_iota(jnp.int32, (rows, cols), 1)
    t = r % t_new
    hd = r // (2 * t_new)
    is_new = c >= PAGE
    d = jnp.where(is_new, t - (c - PAGE), t + PAGE - c)
    val = jnp.zeros((rows, cols), F32)
    for h in range(H_B):
        table = lambda b, h=h: rb_ref[h * NUM_BUCKETS + b]
        vh = _bucket_bias(d, table) - table(NUM_BUCKETS - 1)
        val = jnp.where(hd == h, vh, val)
    visible = jnp.logical_and(d >= 0, jnp.logical_or(~is_new, (c - PAGE) < t_new))
    o_ref[...] = jnp.where(visible, val * LOG2E, NEG_INF)


def _sample_bias(rb_flat, t_new):
    rows = H_B * 2 * t_new
    return pl.pallas_call(
        functools.partial(_sample_bias_kernel, t_new=t_new),
        in_specs=[pl.BlockSpec(memory_space=pltpu.SMEM)],
        out_shape=jax.ShapeDtypeStruct((rows, 2 * PAGE), F32),
        name="sample_bias",
    )(rb_flat)


def _lam_value(lv, lam_init):
    s1 = jnp.sum(lv[0:1] * lv[1:2], axis=-1, keepdims=True)
    s2 = jnp.sum(lv[2:3] * lv[3:4], axis=-1, keepdims=True)
    return jnp.exp(s1) - jnp.exp(s2) + lam_init


FAR_BLOCKS = 8


def _causal_sweep(qi, blk, bias_ref, update):
    n_far = jnp.maximum(qi - 1, 0)
    span = FAR_BLOCKS * blk

    def far(kk, carry):
        update(pl.multiple_of(kk * span, span), FAR_BLOCKS, None)
        return carry

    n_trips = n_far // FAR_BLOCKS
    lax.fori_loop(0, n_trips, far, 0)
    done = n_trips * FAR_BLOCKS
    rem = n_far - done
    size = FAR_BLOCKS // 2
    while size >= 1:
        @pl.when((rem // size) % 2 == 1)
        def _(size=size):
            start = done + (rem // (2 * size)) * (2 * size)
            update(pl.multiple_of(start * blk, blk), size, None)

        size //= 2

    @pl.when(qi >= 1)
    def _():
        update(pl.multiple_of((qi - 1) * blk, blk), 2, bias_ref[0])

    @pl.when(qi == 0)
    def _():
        update(0, 1, bias_ref[0, :, blk:2 * blk])


def _attn_kernel(scal_ref, lam_ref, q_ref, k_ref, v_ref, bias_ref, o_ref, m_ref, l_ref, acc_ref, accx_ref,
                 *, blk, nq, lam_init):
    hd = pl.program_id(1)
    lane = lax.broadcasted_iota(jnp.int32, (1, 2 * DH_B), 1)
    lam = _lam_value(lam_ref[...], lam_init)
    nt = (((1,), (1,)), ((), ()))

    def q_maps(qi):
        q = q_ref[pl.ds(pl.multiple_of(qi * blk, blk), blk), :]
        zero = jnp.zeros_like(q)
        return jnp.where(lane < DH_B, q, zero), jnp.where(lane >= DH_B, q, zero)

    def store_o(qi, val):
        o_ref[pl.ds(pl.multiple_of(qi * blk, blk), blk), :] = val

    @pl.when(scal_ref[0] > 0.5)
    def _():
        one_col = jnp.where(lane == 0, 1.0, 0.0).astype(BF16)
        sizes = {1, 2}
        sizes.update(2 ** e for e in range(FAR_BLOCKS.bit_length()))
        ext_k = {n: jnp.broadcast_to(one_col, (n * blk, 2 * DH_B)) for n in sizes}

        def q_block(qi, carry):
            qs = q_maps(qi)
            qe = []
            for mp in range(2):
                qf = qs[mp].astype(F32)
                shift = jnp.sqrt(jnp.sum(qf * qf, axis=-1, keepdims=True)) * scal_ref[1] + scal_ref[2 + hd]
                qe.append(jnp.concatenate([qs[mp], jnp.where(lane == 0, -shift, 0.0).astype(BF16)], axis=1))
            qe = jnp.concatenate(qe, axis=0)
            accx_ref[...] = jnp.zeros(accx_ref.shape, F32)

            def update(off, nblk, bias):
                rows = nblk * blk
                kb = jnp.concatenate([k_ref[pl.ds(off, rows), :], ext_k[nblk]], axis=1)
                vb = jnp.concatenate([v_ref[pl.ds(off, rows), :], ext_k[nblk]], axis=1)
                s = lax.dot_general(qe, kb, nt, preferred_element_type=F32)
                if bias is not None:
                    s = s + jnp.concatenate([bias, bias], axis=0)
                accx_ref[...] += jnp.dot(jnp.exp2(s).astype(BF16), vb, preferred_element_type=F32)

            _causal_sweep(qi, blk, bias_ref, update)
            a1 = accx_ref[0:blk]
            a2 = accx_ref[blk:2 * blk]
            store_o(qi, a1[:, 0:DV_B] / a1[:, DV_B:DV_B + 1] - lam * (a2[:, 0:DV_B] / a2[:, DV_B:DV_B + 1]))
            return carry

        lax.fori_loop(0, nq, q_block, 0)

    @pl.when(scal_ref[0] <= 0.5)
    def _():
        def q_block(qi, carry):
            qs = q_maps(qi)
            m_ref[...] = jnp.full(m_ref.shape, NEG_INF, F32)
            l_ref[...] = jnp.zeros(l_ref.shape, F32)
            acc_ref[...] = jnp.zeros(acc_ref.shape, F32)

            def update(off, nblk, bias):
                for i in range(nblk):
                    update_one(pl.multiple_of(off + i * blk, blk),
                               None if bias is None else bias[:, i * blk:(i + 1) * blk])

            def update_one(off, bias):
                kb = k_ref[pl.ds(off, blk), :]
                vb = v_ref[pl.ds(off, blk), :]
                for mp in range(2):
                    s = lax.dot_general(qs[mp], kb, nt, preferred_element_type=F32)
                    if bias is not None:
                        s = s + bias
                    m_old = m_ref[mp]
                    m_new = jnp.maximum(m_old, jnp.max(s, axis=-1, keepdims=True))
                    alpha = jnp.exp2(m_old - m_new)
                    p = jnp.exp2(s - jnp.concatenate([m_new] * (blk // LANES), axis=1))
                    l_ref[mp] = alpha * l_ref[mp] + jnp.sum(p, axis=-1, keepdims=True)
                    acc_ref[mp] = alpha * acc_ref[mp] + jnp.dot(p.astype(BF16), vb, preferred_element_type=F32)
                    m_ref[mp] = m_new

            _causal_sweep(qi, blk, bias_ref, update)
            store_o(qi, acc_ref[0] / l_ref[0] - lam * (acc_ref[1] / l_ref[1]))
            return carry

        lax.fori_loop(0, nq, q_block, 0)


def _prompt_attn(scal, lamv, qn, kn, vb, bias, n_seq, seq, blk, lam_init):
    nq = seq // blk
    r = n_seq * seq
    return pl.pallas_call(
        functools.partial(_attn_kernel, blk=blk, nq=nq, lam_init=lam_init),
        grid=(n_seq, H_B),
        in_specs=[
            pl.BlockSpec(memory_space=pltpu.SMEM),
            pl.BlockSpec(lamv.shape, lambda b, h: (0, 0)),
            pl.BlockSpec((seq, LANES), lambda b, h: (b, h)),
            pl.BlockSpec((seq, LANES), lambda b, h: (b, h)),
            pl.BlockSpec((seq, LANES), lambda b, h: (b, h)),
            pl.BlockSpec((1, blk, 2 * blk), lambda b, h: (h, 0, 0)),
        ],
        out_specs=pl.BlockSpec((seq, LANES), lambda b, h: (b, h)),
        out_shape=jax.ShapeDtypeStruct((r, HB_W), F32),
        scratch_shapes=[pltpu.VMEM((2, blk, LANES), F32), pltpu.VMEM((2, blk, LANES), F32),
                        pltpu.VMEM((2, blk, DV_B), F32), pltpu.VMEM((2 * blk, 2 * DV_B), F32)],
        compiler_params=_cparams(2, VMEM_LARGE_MIB),
        name="prompt_attn",
    )(scal, lamv, qn, kn, vb, bias)


def _tri_masks():
    i = lax.broadcasted_iota(jnp.int32, (CHUNK, CHUNK), 0)
    j = lax.broadcasted_iota(jnp.int32, (CHUNK, CHUNK), 1)
    incl = i >= j
    strict = i > j
    eye = (i == j).astype(F32)
    base = jnp.logical_and(strict, (i // SUBLANES) == (j // SUBLANES))
    levels = []
    s = SUBLANES
    while s < CHUNK:
        levels.append(jnp.logical_and((i // (2 * s)) == (j // (2 * s)), (i // s) > (j // s)))
        s *= 2
    return incl, strict, eye, base, levels


def _gdn_sequence(b, qkv_ref, ba_ref, prm, o_ref, sout_ref, s_ref, masks, nchunk):
    incl, strict, eye, base, levels = masks
    tril = incl.astype(F32)
    nt = (((1,), (1,)), ((), ()))
    y = qkv_ref[b]
    bg = ba_ref[b]
    beta_all = _sigmoid(bg)
    g_all = -jnp.exp(prm[0:1]) * _softplus(bg + prm[1:2])
    gcs = [jnp.dot(tril, g_all[c * CHUNK:(c + 1) * CHUNK], precision=lax.Precision.HIGHEST,
                   preferred_element_type=F32) for c in range(nchunk)]
    yield
    gct = [gc.T for gc in gcs]
    grp = [(c, h) for c in range(nchunk) for h in range(H_A)]
    q_, k_, kb_, vb_, dec_, eg_, ekd_, gl_ = [], [], [], [], [], [], [], []
    for (c, h) in grp:
        rs = slice(c * CHUNK, (c + 1) * CHUNK)
        q = y[rs, h * DK_A:(h + 1) * DK_A]
        k = y[rs, H_A * DK_A + h * DK_A:H_A * DK_A + (h + 1) * DK_A]
        v = y[rs, 2 * H_A * DK_A + h * DV_A:2 * H_A * DK_A + (h + 1) * DV_A]
        beta = beta_all[rs, h:h + 1]
        gcol = gcs[c][:, H_A + h:H_A + h + 1]
        grow = gct[c][H_A + h:H_A + h + 1, :]
        glast = gcs[c][CHUNK - 1:CHUNK, H_A + h:H_A + h + 1]
        dec_.append(jnp.where(incl, jnp.exp(jnp.where(incl, gcol - grow, 0.0)), 0.0))
        eg_.append(jnp.exp(gcol))
        ekd_.append(jnp.exp(glast - gcol))
        gl_.append(jnp.exp(glast))
        q_.append(q)
        k_.append(k)
        kb_.append(k * beta)
        vb_.append(v * beta)
    n = len(grp)
    yield
    kk_qk = [lax.dot_general(jnp.concatenate([kb_[g], q_[g]], axis=0).astype(BF16), k_[g].astype(BF16),
                             nt, preferred_element_type=F32) for g in range(n)]
    yield
    ms = [jnp.where(strict, kk_qk[g][0:CHUNK] * dec_[g], 0.0) for g in range(n)]
    a_ = [(kk_qk[g][CHUNK:2 * CHUNK] * dec_[g]).astype(BF16) for g in range(n)]
    d = [jnp.where(base, m, 0.0) for m in ms]
    yield
    d2 = [_bdot(a, a) for a in d]
    yield
    d4 = [_bdot(a, a) for a in d2]
    x = [_bdot(eye - a, eye + bb) for a, bb in zip(d, d2)]
    yield
    x = [_bdot(a, eye + bb) for a, bb in zip(x, d4)]
    yield
    for lvl in levels:
        cm = [jnp.where(lvl, m, 0.0) for m in ms]
        xc = [_bdot(a, bb) for a, bb in zip(x, cm)]
        yield
        xcx = [_bdot(a, bb) for a, bb in zip(xc, x)]
        yield
        x = [a - bb for a, bb in zip(x, xcx)]
    rhs = [jnp.concatenate([vb_[g], kb_[g] * eg_[g]], axis=1) for g in range(n)]
    yield
    uw = [_bdot(x[g], rhs[g]) for g in range(n)]
    yield
    wq_ = [jnp.concatenate([uw[g][:, DV_A:DV_A + DK_A], q_[g] * eg_[g]], axis=0).astype(BF16)
           for g in range(n)]
    kd_ = [(k_[g] * ekd_[g]).astype(BF16) for g in range(n)]
    s = [s_ref[b, h] for h in range(H_A)]
    yield
    for c in range(nchunk):
        gi = [c * H_A + h for h in range(H_A)]
        ws = [jnp.dot(wq_[gi[h]], s[h].astype(BF16), preferred_element_type=F32) for h in range(H_A)]
        yield
        vn = [(uw[gi[h]][:, 0:DV_A] - ws[h][0:CHUNK]).astype(BF16) for h in range(H_A)]
        yield
        av = [jnp.dot(a_[gi[h]], vn[h], preferred_element_type=F32) for h in range(H_A)]
        kv = [lax.dot_general(kd_[gi[h]], vn[h], (((0,), (0,)), ((), ())), preferred_element_type=F32)
              for h in range(H_A)]
        yield
        for h in range(H_A):
            o_ref[b, c * CHUNK:(c + 1) * CHUNK, h * DV_A:(h + 1) * DV_A] = ws[h][CHUNK:2 * CHUNK] + av[h]
            s[h] = s[h] * gl_[gi[h]] + kv[h]
        yield
    for h in range(H_A):
        s_ref[b, h] = s[h]
        sout_ref[b, h] = s[h]


def _gdn_kernel(qkv_ref, ba_ref, prm_ref, o_ref, sout_ref, s_ref, *, tb, n_seq):
    j = pl.program_id(0)

    @pl.when(j == 0)
    def _():
        s_ref[...] = jnp.zeros(s_ref.shape, F32)

    prm = prm_ref[...]
    masks = _tri_masks()
    chains = [_gdn_sequence(b, qkv_ref, ba_ref, prm, o_ref, sout_ref, s_ref, masks, tb // CHUNK)
              for b in range(n_seq)]
    live = []
    while chains or live:
        if chains:
            live.append(chains.pop(0))
        for ch in list(live):
            if next(ch, StopIteration) is StopIteration:
                live.remove(ch)


def _prompt_gdn(qkv, ba, prm, n_seq, seq, tb):
    nb = seq // tb
    return pl.pallas_call(
        functools.partial(_gdn_kernel, tb=tb, n_seq=n_seq),
        grid=(nb,),
        in_specs=[
            pl.BlockSpec((n_seq, tb, QKV_A), lambda j: (0, j, 0)),
            pl.BlockSpec((n_seq, tb, LANES), lambda j: (0, j, 0)),
            pl.BlockSpec(prm.shape, lambda j: (0, 0)),
        ],
        out_specs=[
            pl.BlockSpec((n_seq, tb, H_A * DV_A), lambda j: (0, j, 0)),
            pl.BlockSpec((n_seq, H_A, DK_A, DV_A), lambda j: (0, 0, 0, 0)),
        ],
        out_shape=[
            jax.ShapeDtypeStruct((n_seq, seq, H_A * DV_A), F32),
            jax.ShapeDtypeStruct((n_seq, H_A, DK_A, DV_A), F32),
        ],
        scratch_shapes=[pltpu.VMEM((n_seq, H_A, DK_A, DV_A), F32)],
        compiler_params=_cparams(1, VMEM_MEDIUM_MIB),
        name="prompt_gdn",
    )(qkv, ba, prm)


def _merge_kernel(x_ref, oa_ref, ob_ref, nw_ref, w2_ref, onw_ref, sbw_ref, wb_ref, wo_ref, out_ref, *, ob_scale):
    x = x_ref[...]
    h = _rms(x, nw_ref[...]).astype(BF16)
    z = jnp.dot(h, w2_ref[:, P2_Z:P2_GA], preferred_element_type=F32)
    ob = ob_ref[...]
    sbw = sbw_ref[...]
    ob_n = [(_rms(ob[:, hd * DV_B:(hd + 1) * DV_B], sbw) * ob_scale).astype(BF16) for hd in range(H_B)]
    ga = jnp.dot(h, w2_ref[:, P2_GA:P2_GB], preferred_element_type=F32)
    oa = oa_ref[...]
    onw = onw_ref[...]
    oa_n = [(_rms(oa[:, hd * DV_A:(hd + 1) * DV_A], onw) * _silu(z[:, hd * DV_A:(hd + 1) * DV_A])).astype(BF16)
            for hd in range(H_A)]
    pb = jnp.dot(jnp.concatenate(ob_n, axis=1), wb_ref[1], preferred_element_type=F32)
    sga = _sigmoid(ga)
    gb = jnp.dot(h, w2_ref[:, P2_GB:P2_END], preferred_element_type=F32)
    pa = jnp.dot(jnp.concatenate(oa_n, axis=1), wb_ref[0], preferred_element_type=F32)
    mixed = (sga * pa + _sigmoid(gb) * pb).astype(BF16)
    out_ref[...] = x + jnp.dot(mixed, wo_ref[...], preferred_element_type=F32)


def _merge(x, oa, ob, nw, w2, onw, sbw, wb, wo, tm, ob_scale):
    r = x.shape[0]
    row = lambda w: pl.BlockSpec((tm, w), lambda i: (i, 0))
    full = lambda a: pl.BlockSpec(a.shape, lambda i: (0,) * a.ndim)
    return pl.pallas_call(
        functools.partial(_merge_kernel, ob_scale=ob_scale),
        grid=(r // tm,),
        in_specs=[row(D_MODEL), row(H_A * DV_A), row(H_B * DV_B), full(nw), full(w2), full(onw), full(sbw),
                  full(wb), full(wo)],
        out_specs=row(D_MODEL),
        out_shape=jax.ShapeDtypeStruct((r, D_MODEL), F32),
        compiler_params=_cparams(1, VMEM_LARGE_MIB),
        name="merge",
    )(x, oa, ob, nw, w2, onw, sbw, wb, wo)


FF_CHUNK = MXU_DIM
UP_AHEAD = 2
DOWN_GROUP = 4


def _ffn_kernel(x_ref, st_ref, nw_ref, wup_ref, cw_ref, cb_ref, wdn_ref, out_ref, stout_ref,
                carry_ref, ext_ref, *, tm, shift, halo):
    i = pl.program_id(1)

    @pl.when(i == 0)
    def _():
        carry_ref[...] = st_ref[0]

    x = x_ref[...]
    h = _rms(x, nw_ref[...]).astype(BF16)
    cw = cw_ref[...]
    cb = cb_ref[...]
    n_chunks = D_FF // FF_CHUNK

    def col(c, part):
        lo = part * D_FF + c * FF_CHUNK
        return slice(lo, lo + FF_CHUNK)

    def up_proj(c):
        return [jnp.dot(h, wup_ref[:, col(c, part)], preferred_element_type=F32) for part in range(2)]

    def gated(c, ups):
        ext = ext_ref.at[c % 2]
        parts = []
        for part in range(2):
            sl = col(c, part)
            es = slice(part * FF_CHUNK, (part + 1) * FF_CHUNK)
            ext[0:halo, es] = carry_ref[:, sl]
            ext[halo:halo + tm, es] = ups[part]
            u = cw[2:3, sl] * ups[part] + cb[:, sl]
            u = u + cw[1:2, sl] * ext[halo - shift:halo - shift + tm, es]
            u = u + cw[0:1, sl] * ext[halo - 2 * shift:halo - 2 * shift + tm, es]
            carry_ref[:, sl] = ext[tm:tm + halo, es]
            parts.append(u)
        return (_silu(parts[0]) * parts[1]).astype(BF16)

    acc = x
    ups, acts = {}, {}
    for s in range(n_chunks + UP_AHEAD + 1):
        if s < n_chunks:
            ups[s] = up_proj(s)
        c = s - UP_AHEAD
        if 0 <= c < n_chunks:
            acts[c] = gated(c, ups.pop(c))
        c = s - UP_AHEAD - 1
        if 0 <= c < n_chunks and (c % DOWN_GROUP == DOWN_GROUP - 1 or c == n_chunks - 1):
            c0 = c - c % DOWN_GROUP
            a = jnp.concatenate([acts.pop(i) for i in range(c0, c + 1)], axis=1)
            acc = acc + jnp.dot(a, wdn_ref[c0 * FF_CHUNK:(c + 1) * FF_CHUNK, :], preferred_element_type=F32)
    out_ref[...] = acc
    stout_ref[0] = carry_ref[...]


def _ffn(x, st, nw, wup, cw, cb, wdn, groups, tm, shift, halo):
    r = x.shape[0]
    tiles = r // groups // tm
    full = lambda a: pl.BlockSpec(a.shape, lambda g, i: (0,) * a.ndim)
    return pl.pallas_call(
        functools.partial(_ffn_kernel, tm=tm, shift=shift, halo=halo),
        grid=(groups, tiles),
        in_specs=[
            pl.BlockSpec((tm, D_MODEL), lambda g, i: (g * tiles + i, 0)),
            pl.BlockSpec((1, halo, 2 * D_FF), lambda g, i: (g, 0, 0)),
            full(nw), full(wup), full(cw), full(cb), full(wdn),
        ],
        out_specs=[
            pl.BlockSpec((tm, D_MODEL), lambda g, i: (g * tiles + i, 0)),
            pl.BlockSpec((1, halo, 2 * D_FF), lambda g, i: (g, 0, 0)),
        ],
        out_shape=[jax.ShapeDtypeStruct((r, D_MODEL), F32),
                   jax.ShapeDtypeStruct((groups, halo, 2 * D_FF), F32)],
        scratch_shapes=[pltpu.VMEM((halo, 2 * D_FF), F32), pltpu.VMEM((2, halo + tm, 2 * FF_CHUNK), F32)],
        compiler_params=_cparams(2, VMEM_LARGE_MIB),
        name="ffn",
    )(x, st, nw, wup, cw, cb, wdn)


def _sgdn_pre_kernel(qkv_ref, ba_ref, prm_ref, wq_ref, u_ref, kd_ref, a_ref, gl_ref, *, t_new, nb):
    prm = prm_ref[...]
    lane = lax.broadcasted_iota(jnp.int32, (nb, LANES), 1)

    ys, betas, gs = [], [], []
    for t in range(t_new):
        ys.append(qkv_ref[t])
        bg = ba_ref[t]
        betas.append(_sigmoid(bg))
        gs.append(-jnp.exp(prm[0:1]) * _softplus(bg + prm[1:2]))
    gcs = [gs[0]]
    for t in range(1, t_new):
        gcs.append(gcs[-1] + gs[t])

    a_out = [jnp.zeros((nb, LANES), F32) for _ in range(t_new)]
    for h in range(H_A):
        q, k, v, beta, gc = [], [], [], [], []
        for t in range(t_new):
            q.append(ys[t][:, h * DK_A:(h + 1) * DK_A])
            k.append(ys[t][:, H_A * DK_A + h * DK_A:H_A * DK_A + (h + 1) * DK_A])
            v.append(ys[t][:, 2 * H_A * DK_A + h * DV_A:2 * H_A * DK_A + (h + 1) * DV_A])
            beta.append(betas[t][:, h:h + 1])
            gc.append(gcs[t][:, H_A + h:H_A + h + 1])
        m = [[None] * t_new for _ in range(t_new)]
        for i in range(t_new):
            for jj in range(i + 1):
                dec = jnp.exp(gc[i] - gc[jj])
                if jj < i:
                    m[i][jj] = beta[i] * jnp.sum(k[i] * k[jj], axis=-1, keepdims=True) * dec
                aij = jnp.sum(q[i] * k[jj], axis=-1, keepdims=True) * dec
                a_out[i] = jnp.where(lane == h * SUBLANES + jj, aij, a_out[i])
        tm_ = [[None] * t_new for _ in range(t_new)]
        for i in range(t_new):
            for jj in range(i):
                acc = m[i][jj]
                for l in range(jj + 1, i):
                    acc = acc + m[i][l] * tm_[l][jj]
                tm_[i][jj] = -acc
        vb = [v[t] * beta[t] for t in range(t_new)]
        kbg = [k[t] * (beta[t] * jnp.exp(gc[t])) for t in range(t_new)]
        hs = slice(h * DK_A, (h + 1) * DK_A)
        for i in range(t_new):
            u = vb[i]
            w = kbg[i]
            for jj in range(i):
                u = u + tm_[i][jj] * vb[jj]
                w = w + tm_[i][jj] * kbg[jj]
            u_ref[i, :, hs] = u
            wq_ref[i, :, hs] = w
            wq_ref[t_new + i, :, hs] = q[i] * jnp.exp(gc[i])
            kd_ref[i, :, hs] = k[i] * jnp.exp(gc[t_new - 1] - gc[i])
        gl_ref[:, hs] = jnp.broadcast_to(jnp.exp(gc[t_new - 1]), (nb, DK_A))
    for i in range(t_new):
        a_ref[i] = a_out[i]


def _sgdn_pre(qkv_t, ba_t, prm, t_new, nb):
    wide = H_A * DK_A
    return pl.pallas_call(
        functools.partial(_sgdn_pre_kernel, t_new=t_new, nb=nb),
        out_shape=[
            jax.ShapeDtypeStruct((2 * t_new, nb, wide), F32),
            jax.ShapeDtypeStruct((t_new, nb, wide), F32),
            jax.ShapeDtypeStruct((t_new, nb, wide), F32),
            jax.ShapeDtypeStruct((t_new, nb, LANES), F32),
            jax.ShapeDtypeStruct((nb, wide), F32),
        ],
        compiler_params=_cparams(0, VMEM_MEDIUM_MIB),
        name="sample_gdn_pre",
    )(qkv_t, ba_t, prm)


SGDN_BB = SUBLANES


def _sgdn_state_kernel(wq_ref, u_ref, kd_ref, a_ref, gl_ref, s0_ref, *refs, t_new):
    o_ref, s1_ref = refs[-2:]
    rows = 2 * t_new
    rid = lax.broadcasted_iota(jnp.int32, (rows, DK_A), 0)
    zpad = jnp.zeros((rows - t_new, DK_A), F32)
    pairs = [(bi, h) for bi in range(SGDN_BB) for h in range(H_A)]
    hs = [slice(h * DK_A, (h + 1) * DK_A) for h in range(H_A)]
    r = {(bi, h): jnp.dot(wq_ref[:, bi, hs[h]].astype(BF16), s0_ref[bi, h].astype(BF16),
                          preferred_element_type=F32) for (bi, h) in pairs}
    v_new = {}
    for (bi, h) in pairs:
        u8 = jnp.concatenate([u_ref[:, bi, hs[h]], zpad], axis=0)
        v_new[bi, h] = jnp.where(rid < t_new, u8 - r[bi, h], 0.0)
    kv = {}
    for (bi, h) in pairs:
        kd8 = jnp.concatenate([kd_ref[:, bi, hs[h]], zpad], axis=0)
        kv[bi, h] = lax.dot_general(kd8.astype(BF16), v_new[bi, h].astype(BF16), (((0,), (0,)), ((), ())),
                                    preferred_element_type=F32)
    for (bi, h) in pairs:
        amat = a_ref[:, bi, :]
        o = r[bi, h][t_new:rows]
        for jj in range(t_new):
            col = amat[:, h * SUBLANES + jj:h * SUBLANES + jj + 1]
            o = o + col * v_new[bi, h][jj:jj + 1, :]
        o_ref[:, bi, hs[h]] = o
        s1_ref[bi, h] = s0_ref[bi, h] * gl_ref[bi:bi + 1, hs[h]] + kv[bi, h]


def _sgdn_state(wq, u, kd, a, gl, s0_all, li, t_new, nb, stacked=None):
    wide = H_A * DK_A
    bb = SGDN_BB
    state_spec = pl.BlockSpec((None, bb, H_A, DK_A, DV_A), lambda i: (li, i, 0, 0, 0))
    in_specs = [
        pl.BlockSpec((2 * t_new, bb, wide), lambda i: (0, i, 0)),
        pl.BlockSpec((t_new, bb, wide), lambda i: (0, i, 0)),
        pl.BlockSpec((t_new, bb, wide), lambda i: (0, i, 0)),
        pl.BlockSpec((t_new, bb, LANES), lambda i: (0, i, 0)),
        pl.BlockSpec((bb, wide), lambda i: (i, 0)),
        state_spec,
    ]
    args = [wq, u, kd, a, gl, s0_all]
    aliases = {}
    if stacked is not None:
        in_specs.append(pl.BlockSpec(memory_space=pl.ANY))
        args.append(stacked)
        aliases = {6: 1}
    return pl.pallas_call(
        functools.partial(_sgdn_state_kernel, t_new=t_new),
        grid=(nb // bb,),
        in_specs=in_specs,
        out_specs=[pl.BlockSpec((t_new, bb, wide), lambda i: (0, i, 0)), state_spec],
        out_shape=[jax.ShapeDtypeStruct((t_new, nb, wide), F32),
                   jax.ShapeDtypeStruct(s0_all.shape, F32)],
        input_output_aliases=aliases,
        compiler_params=_cparams(1, VMEM_SMALL_MIB),
        name="sample_gdn_state",
    )(*args)


NEW_ROWS = 2 * SUBLANES


def _sattn_kernel(pt_ref, lam_ref, q_ref, kn_ref, vn_ref, bias_ref, *rest, n_pages, t_new, lam_init):
    del pt_ref
    k_pages = rest[:n_pages]
    v_pages = rest[n_pages:2 * n_pages]
    o_ref = rest[2 * n_pages]
    rows = 2 * t_new
    q = q_ref[0]
    bias = bias_ref[...]
    lam = _lam_value(lam_ref[...], lam_init)
    r = lax.broadcasted_iota(jnp.int32, (rows, 2 * DH_B), 0)
    c = lax.broadcasted_iota(jnp.int32, (rows, 2 * DH_B), 1)
    map_mask = (c // DH_B) == (r // t_new)
    pad = jnp.zeros((PAGE - NEW_ROWS, LANES), BF16)
    nt = (((1,), (1,)), ((), ()))

    heads = range(H_B)
    lanes = [slice(h * LANES, (h + 1) * LANES) for h in heads]
    head_rows = [pl.ds(h, PAGE, stride=H_B) for h in heads]
    qx = [jnp.where(map_mask, jnp.concatenate([q[:, lanes[h]]] * 2, axis=0), 0.0).astype(BF16) for h in heads]
    bh = [bias[h * rows:(h + 1) * rows] for h in heads]

    groups = [list(range(p, min(p + MXU_DIM // PAGE, n_pages))) for p in range(0, n_pages, MXU_DIM // PAGE)]

    def head_tile(page_refs, grp, h):
        return jnp.concatenate([page_refs[p][0, 0, head_rows[h], :] for p in grp], axis=0).astype(BF16)

    s_parts = [[] for _ in heads]
    for grp in groups:
        for h in heads:
            s = lax.dot_general(qx[h], head_tile(k_pages, grp, h), nt, preferred_element_type=F32)
            if grp[-1] == n_pages - 1:
                zeros = [jnp.zeros((rows, PAGE), F32)] * (len(grp) - 1)
                s = s + jnp.concatenate(zeros + [bh[h][:, 0:PAGE]], axis=1)
            s_parts[h].append(s)
    for h in heads:
        kn = jnp.concatenate([kn_ref[0, :, lanes[h]], pad], axis=0)
        s_parts[h].append(lax.dot_general(qx[h], kn, nt, preferred_element_type=F32) + bh[h][:, PAGE:2 * PAGE])

    m = []
    for h in heads:
        mh = s_parts[h][0].max(axis=-1, keepdims=True)
        for s in s_parts[h][1:]:
            mh = jnp.maximum(mh, s.max(axis=-1, keepdims=True))
        m.append(mh)
    l = [jnp.zeros((rows, 1), F32) for _ in heads]
    acc = [jnp.zeros((rows, DV_B), F32) for _ in heads]
    for gi in range(len(groups) + 1):
        for h in heads:
            pr = jnp.exp2(s_parts[h][gi] - m[h])
            l[h] = l[h] + jnp.sum(pr, axis=-1, keepdims=True)
            if gi < len(groups):
                vv = head_tile(v_pages, groups[gi], h)
            else:
                vv = jnp.concatenate([vn_ref[0, :, lanes[h]], pad], axis=0)
            acc[h] = acc[h] + jnp.dot(pr.astype(BF16), vv, preferred_element_type=F32)
    for h in heads:
        a = acc[h] / l[h]
        o_ref[0, :, lanes[h]] = a[0:t_new] - lam * a[t_new:rows]


def _sample_attn(page_table, lamv, q_b, kn_b, vn_b, bias, ck, cv, li, t_new, lam_init):
    nb, n_pages = page_table.shape
    page_spec = lambda p: pl.BlockSpec((1, 1, PAGE * H_B, LANES), lambda b, pt, p=p: (li, pt[b, p], 0, 0))
    tok_spec = pl.BlockSpec((1, t_new, HB_W), lambda b, pt: (b, 0, 0))
    new_spec = pl.BlockSpec((1, NEW_ROWS, HB_W), lambda b, pt: (b, 0, 0))
    grid_spec = pltpu.PrefetchScalarGridSpec(
        num_scalar_prefetch=1,
        grid=(nb,),
        in_specs=[pl.BlockSpec(lamv.shape, lambda b, pt: (0, 0)), tok_spec, new_spec, new_spec,
                  pl.BlockSpec(bias.shape, lambda b, pt: (0, 0))]
                 + [page_spec(p) for p in range(n_pages)] * 2,
        out_specs=pl.BlockSpec((1, t_new, HB_W), lambda b, pt: (b, 0, 0)),
    )
    return pl.pallas_call(
        functools.partial(_sattn_kernel, n_pages=n_pages, t_new=t_new, lam_init=lam_init),
        grid_spec=grid_spec,
        out_shape=jax.ShapeDtypeStruct((nb, t_new, HB_W), F32),
        compiler_params=_cparams(1, VMEM_SMALL_MIB),
        name="sample_attn",
    )(page_table, lamv, q_b, kn_b, vn_b, bias, *([ck] * n_pages), *([cv] * n_pages))


def _pick(n, pref):
    return pref if n % pref == 0 else n


def _to_bmajor(a_t, t_new, nb):
    return jnp.swapaxes(a_t.reshape(t_new, nb, -1), 0, 1)


def _layer_weights(li, rel_bias, norm_mix, w_in, conv_a, a_log, dt_bias, onorm_a, qnorm_b, knorm_b,
                   lam_q1, lam_k1, lam_q2, lam_k2, subln_b, w_branch, w_o, norm_ffn, w_up, conv_f,
                   conv_f_b, w_down):
    w = w_in[li]
    o_z = QKV_A
    o_b = o_z + H_A * DV_A
    o_qb = o_b + 2 * H_A
    o_kb = o_qb + HB_W
    o_vb = o_kb + HB_W
    o_ga = o_vb + H_B * DV_B
    o_gb = o_ga + D_MODEL
    ba_cols = jnp.pad(w[:, o_b:o_qb], ((0, 0), (0, LANES - 2 * H_A)))
    w1 = jnp.concatenate([w[:, 0:o_z], w[:, o_qb:o_ga], ba_cols], axis=1).astype(BF16)
    w2 = jnp.concatenate([w[:, o_z:o_b], w[:, o_ga:]], axis=1).astype(BF16)
    prm = jnp.zeros((2, LANES), F32)
    prm = prm.at[0, H_A:2 * H_A].set(a_log[li]).at[1, H_A:2 * H_A].set(dt_bias[li])
    k_bound = 1.01 * math.sqrt(DH_B) * jnp.max(jnp.abs(knorm_b[li]))
    q_bound = 1.01 * Q_SCALE * math.sqrt(DH_B) * jnp.max(jnp.abs(qnorm_b[li]))
    rel = (rel_bias - rel_bias[NUM_BUCKETS - 1:NUM_BUCKETS]) * LOG2E
    spread = 2.0 * q_bound * k_bound + jnp.max(jnp.max(rel, axis=0) - jnp.min(rel, axis=0))
    scal = jnp.concatenate([jnp.where(spread <= SAFE_SPREAD_BITS, 1.0, 0.0).reshape(1), k_bound.reshape(1),
                            jnp.max(rel, axis=0), jnp.zeros((2,), F32)]).astype(F32)
    return dict(
        scal=scal,
        nw=norm_mix[li].reshape(1, D_MODEL), w1=w1, w2=w2,
        qw=jnp.tile(qnorm_b[li], 2 * H_B).reshape(1, HB_W),
        kw=jnp.tile(knorm_b[li], 2 * H_B).reshape(1, HB_W),
        cw_a=conv_a[li], prm=prm,
        onw=onorm_a[li].reshape(1, DV_A), sbw=subln_b[li].reshape(1, DV_B),
        lamv=jnp.stack([lam_q1[li], lam_k1[li], lam_q2[li], lam_k2[li]]),
        wb=w_branch[li].astype(BF16), wo=w_o[li].astype(BF16),
        nwf=norm_ffn[li].reshape(1, D_MODEL), wup=w_up[li].astype(BF16),
        cw_f=conv_f[li], cb_f=conv_f_b[li].reshape(1, 2 * D_FF), wdn=w_down[li].astype(BF16),
    )


def _group_ones():
    i = jnp.arange(MXU_DIM)
    return ((i[:, None] // DH_B) == (i[None, :] // DH_B)).astype(BF16)


def _prompt_layer(x, wt, bias_p, g, n_seq, seq, lam_init, li, depth, kv_stacked):
    tm = _pick(seq, ROW_TILE)
    blk = bias_p.shape[1]
    zero_st_a = jnp.zeros((n_seq, SUBLANES, QKV_A), F32)
    qkv, ba, qn, kn, kf, vf, vb, cst_a = _inproj(x, wt["nw"], wt["w1"], wt["qw"], wt["kw"], g, zero_st_a,
                                                 wt["cw_a"], n_seq, tm, 1, SUBLANES,
                                                 layer=li, depth=depth, stacked=kv_stacked)
    o_a, s_fin = _prompt_gdn(qkv.reshape(n_seq, seq, QKV_A), ba.reshape(n_seq, seq, LANES),
                             wt["prm"], n_seq, seq, _pick(seq, GDN_TIME_BLOCK))
    o_a = o_a.reshape(n_seq * seq, H_A * DV_A)
    o_b = _prompt_attn(wt["scal"], wt["lamv"], qn, kn, vb, bias_p, n_seq, seq, blk, lam_init)
    x = _merge(x, o_a, o_b, wt["nw"], wt["w2"], wt["onw"], wt["sbw"], wt["wb"], wt["wo"], tm, 1.0 - lam_init)
    zero_st = jnp.zeros((n_seq, SUBLANES, 2 * D_FF), F32)
    x, cst_f = _ffn(x, zero_st, wt["nwf"], wt["wup"], wt["cw_f"], wt["cb_f"], wt["wdn"],
                    n_seq, tm, 1, SUBLANES)
    return (x, kf, vf, s_fin, cst_a[:, SUBLANES - (CONV_A - 1):], cst_f[:, SUBLANES - (CONV_F - 1):])


def _sample_layer(x_t, wt, bias_s, g, page_table, ck, cv, li, s0_all, s1_stacked, cst_a, cst_f, nb, t_new,
                  lam_init):
    r = nb * t_new
    halo_a = (CONV_A - 1) * nb
    st_a_t = jnp.swapaxes(cst_a, 0, 1).reshape(1, halo_a, QKV_A)
    qkv, ba, qn, kn, kf, vf, vb, cst_a_t = _inproj(x_t, wt["nw"], wt["w1"], wt["qw"], wt["kw"], g, st_a_t,
                                                   wt["cw_a"], 1, r, nb, halo_a)
    cst_a_t = cst_a_t.reshape(CONV_A - 1, nb, QKV_A)
    wq, u, kd, a, gl = _sgdn_pre(qkv.reshape(t_new, nb, QKV_A), ba.reshape(t_new, nb, LANES), wt["prm"],
                                 t_new, nb)
    o_a_t, s1 = _sgdn_state(wq, u, kd, a, gl, s0_all, li, t_new, nb, stacked=s1_stacked)
    pad_new = lambda a: jnp.pad(_to_bmajor(a, t_new, nb), ((0, 0), (0, NEW_ROWS - t_new), (0, 0)))
    o_b = _sample_attn(page_table, wt["lamv"], _to_bmajor(qn, t_new, nb).astype(F32), pad_new(kn),
                       pad_new(vb), bias_s, ck, cv, li, t_new, lam_init)
    o_b_t = jnp.swapaxes(o_b, 0, 1).reshape(r, HB_W)
    x_t = _merge(x_t, o_a_t.reshape(r, H_A * DV_A), o_b_t, wt["nw"], wt["w2"], wt["onw"], wt["sbw"],
                 wt["wb"], wt["wo"], r, 1.0 - lam_init)
    halo = (CONV_F - 1) * nb
    st_f_t = jnp.swapaxes(cst_f, 0, 1).reshape(1, halo, 2 * D_FF)
    x_t, cst_f_t = _ffn(x_t, st_f_t, wt["nwf"], wt["wup"], wt["cw_f"], wt["cb_f"], wt["wdn"],
                        1, r, nb, halo)
    return (x_t, _to_bmajor(kf, t_new, nb).reshape(nb, t_new, H_B, 2 * DH_B),
            _to_bmajor(vf, t_new, nb).reshape(nb, t_new, H_B, DV_B), s1,
            jnp.swapaxes(cst_a_t, 0, 1), jnp.swapaxes(cst_f_t.reshape(CONV_F - 1, nb, 2 * D_FF), 0, 1))


def kernel(x_prompt, x_sample, cache_k, cache_v, state_delta, state_conv_a, state_conv_ffn, page_table,
           rel_bias, norm_mix, w_in, conv_a, a_log, dt_bias, onorm_a, qnorm_b, knorm_b, lam_q1, lam_k1,
           lam_q2, lam_k2, subln_b, w_branch, w_o, norm_ffn, w_up, conv_f, conv_f_b, w_down):
    n_seq, seq, _ = x_prompt.shape
    nb, t_new, _ = x_sample.shape
    depth = w_in.shape[0]
    n_pool = cache_k.shape[1]

    rb_flat = rel_bias.T.reshape(-1)
    bias_p = _prompt_bias(rb_flat, _pick(seq, ATTN_BLOCK))
    bias_s = _sample_bias(rb_flat, t_new)
    g = _group_ones()
    ck = cache_k.reshape(depth, n_pool, PAGE * H_B, 2 * DH_B)
    cv = cache_v.reshape(depth, n_pool, PAGE * H_B, DV_B)

    xp = x_prompt.reshape(n_seq * seq, D_MODEL)
    xs = jnp.swapaxes(x_sample, 0, 1).reshape(t_new * nb, D_MODEL)
    outs_p, outs_s = [], []
    kv_stacked, s1_stacked = None, None
    for li in range(depth):
        wt = _layer_weights(li, rel_bias, norm_mix, w_in, conv_a, a_log, dt_bias, onorm_a, qnorm_b, knorm_b,
                            lam_q1, lam_k1, lam_q2, lam_k2, subln_b, w_branch, w_o, norm_ffn, w_up,
                            conv_f, conv_f_b, w_down)
        lam_init = 0.8 - 0.6 * math.exp(-0.3 * li)
        xp, kf_all, vf_all, *rest_p = _prompt_layer(xp, wt, bias_p, g, n_seq, seq, lam_init, li, depth,
                                                    kv_stacked)
        kv_stacked = (kf_all, vf_all)
        xs, ks, vs, s1_stacked, *rest_s = _sample_layer(xs, wt, bias_s, g, page_table, ck, cv, li, state_delta,
                                                        s1_stacked, state_conv_a[li], state_conv_ffn[li], nb,
                                                        t_new, lam_init)
        outs_p.append(rest_p)
        outs_s.append([ks, vs] + rest_s)

    stack = lambda outs, i: jnp.stack([o[i] for o in outs])
    y_prompt = xp.reshape(n_seq, seq, D_MODEL)
    y_sample = jnp.swapaxes(xs.reshape(t_new, nb, D_MODEL), 0, 1)
    k_prompt = kv_stacked[0].reshape(depth, n_seq, seq, H_B, 2 * DH_B)
    v_prompt = kv_stacked[1].reshape(depth, n_seq, seq, H_B, DV_B)
    return (y_prompt, y_sample,
            k_prompt, v_prompt, stack(outs_p, 0), stack(outs_p, 1), stack(outs_p, 2),
            stack(outs_s, 0), stack(outs_s, 1), s1_stacked, stack(outs_s, 2), stack(outs_s, 3))
```

```python
import functools
import math

import jax
import jax.numpy as jnp
from jax import lax
from jax.experimental import pallas as pl
from jax.experimental.pallas import tpu as pltpu

F32 = jnp.float32
BF16 = jnp.bfloat16

D_MODEL = 1024
H_A, DK_A, DV_A, CONV_A, CHUNK = 4, 128, 128, 4, 64
QKV_A = 2 * H_A * DK_A + H_A * DV_A
H_B, DH_B, DV_B = 4, 64, 128
HB_W = H_B * 2 * DH_B
PAGE = 128
D_FF, CONV_F = 2816, 3
NUM_BUCKETS, MAX_DISTANCE = 32, 128
NEG_INF = -1e30
EPS = 1e-6
ATT_SCALE = DH_B ** -0.5
LOG2E = math.log2(math.e)
Q_SCALE = ATT_SCALE * LOG2E
SAFE_SPREAD_BITS = 100.0

V7X_VMEM_BYTES = 64 * 1024 * 1024
LANES = 128
SUBLANES = 8
MXU_DIM = 256
MIB = 1024 * 1024

ROW_TILE = 512
ATTN_BLOCK = 512
GDN_TIME_BLOCK = 4 * CHUNK
VMEM_LARGE_MIB = 56
VMEM_MEDIUM_MIB = 48
VMEM_SMALL_MIB = 40

P1_QKV = 0
P1_Q = P1_QKV + QKV_A
P1_K = P1_Q + HB_W
P1_V = P1_K + HB_W
P1_BA = P1_V + H_B * DV_B
P1_END = P1_BA + LANES
P2_Z = 0
P2_GA = P2_Z + H_A * DV_A
P2_GB = P2_GA + D_MODEL
P2_END = P2_GB + D_MODEL


def _cparams(n_axes, vmem_mib):
    return pltpu.CompilerParams(
        dimension_semantics=("arbitrary",) * n_axes,
        vmem_limit_bytes=vmem_mib * MIB,
    )


def _bdot(a, b):
    return jnp.dot(a.astype(BF16), b.astype(BF16), preferred_element_type=F32)


def _rms(x, w):
    return x * lax.rsqrt(jnp.mean(x * x, axis=-1, keepdims=True) + EPS) * w


def _sigmoid(x):
    return 1.0 / (1.0 + jnp.exp(-x))


def _silu(x):
    return x * _sigmoid(x)


def _softplus(x):
    return jnp.maximum(x, 0.0) + jnp.log1p(jnp.exp(-jnp.abs(x)))


def _inproj_kernel(x_ref, nw_ref, w_ref, qw_ref, kw_ref, g_ref, st_ref, cw_ref, *refs, shift, halo):
    n_out = 8
    qkv_ref, ba_ref, qn_ref, kn_ref, kf_ref, vf_ref, vb_ref, stout_ref = refs[-n_out - 2:-2]
    carry_ref, ext_ref = refs[-2:]
    h = _rms(x_ref[...], nw_ref[...]).astype(BF16)

    def proj(lo, hi):
        return jnp.dot(h, w_ref[:, lo:hi], preferred_element_type=F32)

    tm = x_ref.shape[0]
    head_rows = [pl.ds(hd, tm, stride=H_B) for hd in range(H_B)]

    @pl.when(pl.program_id(1) == 0)
    def _():
        carry_ref[...] = st_ref[0]

    cw = cw_ref[...]
    g = g_ref[...]
    seg = H_A * DK_A

    def delta_front(part, raw):
        cols = slice(part * seg, (part + 1) * seg)
        ext_ref[0:halo, cols] = carry_ref[:, cols]
        ext_ref[halo:halo + tm, cols] = raw
        y = cw[CONV_A - 1:CONV_A, cols] * raw
        for i in range(CONV_A - 1):
            back = (CONV_A - 1 - i) * shift
            y = y + cw[i:i + 1, cols] * ext_ref[halo - back:halo - back + tm, cols]
        new_carry = ext_ref[tm:tm + halo, cols]
        carry_ref[:, cols] = new_carry
        stout_ref[0, :, cols] = new_carry
        y = _silu(y)
        if part == 2:
            qkv_ref[:, cols] = y
            return
        for hd in range(H_A):
            sl = slice(hd * DK_A, (hd + 1) * DK_A)
            t = y[:, sl]
            t = t * lax.rsqrt(jnp.sum(t * t, axis=-1, keepdims=True) + EPS)
            qkv_ref[:, part * seg + hd * DK_A:part * seg + (hd + 1) * DK_A] = t * (DK_A ** -0.5) if part == 0 else t

    def qk_norm(y, w):
        sq = y * y
        hi = sq.astype(BF16)
        lo = (sq - hi.astype(F32)).astype(BF16)
        outs = []
        for c in range(HB_W // MXU_DIM):
            sl = slice(c * MXU_DIM, (c + 1) * MXU_DIM)
            ss = (jnp.dot(hi[:, sl], g, preferred_element_type=F32)
                  + jnp.dot(lo[:, sl], g, preferred_element_type=F32))
            outs.append(y[:, sl] * lax.rsqrt(ss * (1.0 / DH_B) + EPS) * w[:, sl])
        return outs

    def attn_q(y):
        qn = qk_norm(y, qw_ref[...])
        for c in range(HB_W // MXU_DIM):
            qn_ref[:, c * MXU_DIM:(c + 1) * MXU_DIM] = (qn[c] * Q_SCALE).astype(BF16)

    def attn_k(y):
        kn = qk_norm(y, kw_ref[...])
        for c in range(HB_W // MXU_DIM):
            kn_ref[:, c * MXU_DIM:(c + 1) * MXU_DIM] = kn[c].astype(BF16)
            for i in range(MXU_DIM // LANES):
                kf_ref[head_rows[c * (MXU_DIM // LANES) + i], :] = kn[c][:, i * LANES:(i + 1) * LANES]

    def attn_v(v):
        for hd in range(H_B):
            vf_ref[head_rows[hd], :] = v[:, hd * DV_B:(hd + 1) * DV_B]
        vb_ref[...] = v.astype(BF16)

    def store_ba(y):
        ba_ref[...] = y

    stages = [
        ((P1_QKV, P1_QKV + seg), functools.partial(delta_front, 0)),
        ((P1_QKV + seg, P1_QKV + 2 * seg), functools.partial(delta_front, 1)),
        ((P1_QKV + 2 * seg, P1_Q), functools.partial(delta_front, 2)),
        ((P1_Q, P1_K), attn_q),
        ((P1_K, P1_V), attn_k),
        ((P1_V, P1_BA), attn_v),
        ((P1_BA, P1_END), store_ba),
    ]
    pending = None
    for cols, post in stages:
        y = proj(*cols)
        if pending is not None:
            pending[1](pending[0])
        pending = (y, post)
    pending[1](pending[0])


def _inproj(x, nw, w1, qw, kw, g, st, cw, groups, tm, shift, halo, layer=None, depth=None, stacked=None):
    r = x.shape[0]
    tiles = r // groups // tm
    row = lambda w: pl.BlockSpec((tm, w), lambda gi, i: (gi * tiles + i, 0))
    full = lambda a: pl.BlockSpec(a.shape, lambda gi, i: (0,) * a.ndim)
    st_spec = pl.BlockSpec((1, halo, QKV_A), lambda gi, i: (gi, 0, 0))
    outs = [(1, QKV_A, F32), (1, LANES, F32), (1, HB_W, BF16), (1, HB_W, BF16), (H_B, LANES, F32),
            (H_B, LANES, F32), (1, HB_W, BF16)]
    out_specs = [pl.BlockSpec((tm * m, w), lambda gi, i: (gi * tiles + i, 0)) for m, w, _ in outs]
    out_shape = [jax.ShapeDtypeStruct((r * m, w), dt) for m, w, dt in outs]
    out_specs.append(st_spec)
    out_shape.append(jax.ShapeDtypeStruct((groups, halo, QKV_A), F32))
    in_specs = [row(D_MODEL), full(nw), full(w1), full(qw), full(kw), full(g), st_spec, full(cw)]
    args = [x, nw, w1, qw, kw, g, st, cw]
    aliases = {}
    if layer is not None:
        for o in (4, 5):
            m, w, dt = outs[o]
            out_specs[o] = pl.BlockSpec((None, tm * m, w), lambda gi, i: (layer, gi * tiles + i, 0))
            out_shape[o] = jax.ShapeDtypeStruct((depth, r * m, w), dt)
        if stacked is not None:
            in_specs += [pl.BlockSpec(memory_space=pl.ANY)] * 2
            args += list(stacked)
            aliases = {len(args) - 2: 4, len(args) - 1: 5}
    return pl.pallas_call(
        functools.partial(_inproj_kernel, shift=shift, halo=halo),
        grid=(groups, tiles),
        in_specs=in_specs,
        out_specs=out_specs,
        out_shape=out_shape,
        input_output_aliases=aliases,
        scratch_shapes=[pltpu.VMEM((halo, QKV_A), F32), pltpu.VMEM((halo + tm, QKV_A), F32)],
        compiler_params=_cparams(2, VMEM_LARGE_MIB),
        name="inproj",
    )(*args)


def _bucket_bias(d, table):
    n = jnp.maximum(d, 0)
    max_exact = NUM_BUCKETS // 2
    nf = jnp.maximum(n, 1).astype(F32)
    large = max_exact + (jnp.log(nf / max_exact) / math.log(MAX_DISTANCE / max_exact)
                         * (NUM_BUCKETS - max_exact)).astype(jnp.int32)
    large = jnp.minimum(large, NUM_BUCKETS - 1)
    bucket = jnp.where(n < max_exact, n, large)
    val = jnp.zeros(d.shape, F32)
    for b in range(NUM_BUCKETS):
        val = jnp.where(bucket == b, table(b), val)
    return val


def _prompt_bias_kernel(rb_ref, o_ref, *, blk):
    h = pl.program_id(0)
    table = lambda b: rb_ref[h * NUM_BUCKETS + b]
    far = table(NUM_BUCKETS - 1)
    i = lax.broadcasted_iota(jnp.int32, (blk, blk), 0)
    j = lax.broadcasted_iota(jnp.int32, (blk, blk), 1)
    d0 = i - j
    o_ref[0, :, 0:blk] = (_bucket_bias(d0 + blk, table) - far) * LOG2E
    o_ref[0, :, blk:2 * blk] = jnp.where(d0 >= 0, (_bucket_bias(d0, table) - far) * LOG2E, NEG_INF)


def _prompt_bias(rb_flat, blk):
    return pl.pallas_call(
        functools.partial(_prompt_bias_kernel, blk=blk),
        grid=(H_B,),
        in_specs=[pl.BlockSpec(memory_space=pltpu.SMEM)],
        out_specs=pl.BlockSpec((1, blk, 2 * blk), lambda h: (h, 0, 0)),
        out_shape=jax.ShapeDtypeStruct((H_B, blk, 2 * blk), F32),
        compiler_params=_cparams(1, VMEM_SMALL_MIB),
        name="prompt_bias",
    )(rb_flat)


def _sample_bias_kernel(rb_ref, o_ref, *, t_new):
    rows, cols = o_ref.shape
    r = lax.broadcasted_iota(jnp.int32, (rows, cols), 0)
    c = lax.broadcasted_iota(jnp.int32, (rows, cols), 1)
    t = r % t_new
    hd = r // (2 * t_new)
    is_new = c >= PAGE
    d = jnp.where(is_new, t - (c - PAGE), t + PAGE - c)
    val = jnp.zeros((rows, cols), F32)
    for h in range(H_B):
        table = lambda b, h=h: rb_ref[h * NUM_BUCKETS + b]
        vh = _bucket_bias(d, table) - table(NUM_BUCKETS - 1)
        val = jnp.where(hd == h, vh, val)
    visible = jnp.logical_and(d >= 0, jnp.logical_or(~is_new, (c - PAGE) < t_new))
    o_ref[...] = jnp.where(visible, val * LOG2E, NEG_INF)


def _sample_bias(rb_flat, t_new):
    rows = H_B * 2 * t_new
    return pl.pallas_call(
        functools.partial(_sample_bias_kernel, t_new=t_new),
        in_specs=[pl.BlockSpec(memory_space=pltpu.SMEM)],
        out_shape=jax.ShapeDtypeStruct((rows, 2 * PAGE), F32),
        name="sample_bias",
    )(rb_flat)


def _lam_value(lv, lam_init):
    s1 = jnp.sum(lv[0:1] * lv[1:2], axis=-1, keepdims=True)
    s2 = jnp.sum(lv[2:3] * lv[3:4], axis=-1, keepdims=True)
    return jnp.exp(s1) - jnp.exp(s2) + lam_init


FAR_BLOCKS = 4


def _causal_sweep(qi, blk, bias_ref, update):
    n_far = jnp.maximum(qi - 1, 0)
    span = FAR_BLOCKS * blk

    def far(kk, carry):
        update(pl.multiple_of(kk * span, span), FAR_BLOCKS, None)
        return carry

    n_trips = n_far // FAR_BLOCKS
    lax.fori_loop(0, n_trips, far, 0)
    done = n_trips * FAR_BLOCKS
    rem = n_far - done
    for r in range(FAR_BLOCKS):
        @pl.when(jnp.logical_and(qi >= 1, rem == r))
        def _(r=r):
            update(pl.multiple_of(done * blk, blk), r + 2, bias_ref[0])

    @pl.when(qi == 0)
    def _():
        update(0, 1, bias_ref[0, :, blk:2 * blk])


def _attn_kernel(scal_ref, lam_ref, q_ref, k_ref, v_ref, bias_ref, o_ref, m_ref, l_ref, acc_ref, accx_ref,
                 *, blk, nq, lam_init):
    hd = pl.program_id(1)
    lane = lax.broadcasted_iota(jnp.int32, (1, 2 * DH_B), 1)
    lam = _lam_value(lam_ref[...], lam_init)
    nt = (((1,), (1,)), ((), ()))

    def q_maps(qi):
        q = q_ref[pl.ds(pl.multiple_of(qi * blk, blk), blk), :]
        zero = jnp.zeros_like(q)
        return jnp.where(lane < DH_B, q, zero), jnp.where(lane >= DH_B, q, zero)

    def store_o(qi, val):
        o_ref[pl.ds(pl.multiple_of(qi * blk, blk), blk), :] = val

    @pl.when(scal_ref[0] > 0.5)
    def _():
        one_col = jnp.where(lane == 0, 1.0, 0.0).astype(BF16)
        ext_k = {n: jnp.broadcast_to(one_col, (n * blk, 2 * DH_B)) for n in range(1, FAR_BLOCKS + 2)}

        def q_block(qi, carry):
            qs = q_maps(qi)
            qe = []
            for mp in range(2):
                qf = qs[mp].astype(F32)
                shift = jnp.sqrt(jnp.sum(qf * qf, axis=-1, keepdims=True)) * scal_ref[1] + scal_ref[2 + hd]
                qe.append(jnp.concatenate([qs[mp], jnp.where(lane == 0, -shift, 0.0).astype(BF16)], axis=1))
            qe = jnp.concatenate(qe, axis=0)
            accx_ref[...] = jnp.zeros(accx_ref.shape, F32)

            def update(off, nblk, bias):
                rows = nblk * blk
                kb = jnp.concatenate([k_ref[pl.ds(off, rows), :], ext_k[nblk]], axis=1)
                vb = jnp.concatenate([v_ref[pl.ds(off, rows), :], ext_k[nblk]], axis=1)
                s = lax.dot_general(qe, kb, nt, preferred_element_type=F32)
                if bias is not None:
                    plain = rows - bias.shape[1]
                    tail = s[:, plain:] + jnp.concatenate([bias, bias], axis=0)
                    s = tail if plain == 0 else jnp.concatenate([s[:, :plain], tail], axis=1)
                accx_ref[...] += jnp.dot(jnp.exp2(s).astype(BF16), vb, preferred_element_type=F32)

            _causal_sweep(qi, blk, bias_ref, update)
            a1 = accx_ref[0:blk]
            a2 = accx_ref[blk:2 * blk]
            store_o(qi, a1[:, 0:DV_B] / a1[:, DV_B:DV_B + 1] - lam * (a2[:, 0:DV_B] / a2[:, DV_B:DV_B + 1]))
            return carry

        lax.fori_loop(0, nq, q_block, 0)

    @pl.when(scal_ref[0] <= 0.5)
    def _():
        def q_block(qi, carry):
            qs = q_maps(qi)
            m_ref[...] = jnp.full(m_ref.shape, NEG_INF, F32)
            l_ref[...] = jnp.zeros(l_ref.shape, F32)
            acc_ref[...] = jnp.zeros(acc_ref.shape, F32)

            def update(off, nblk, bias):
                plain = nblk if bias is None else nblk - bias.shape[1] // blk
                for i in range(nblk):
                    update_one(pl.multiple_of(off + i * blk, blk),
                               None if i < plain else bias[:, (i - plain) * blk:(i - plain + 1) * blk])

            def update_one(off, bias):
                kb = k_ref[pl.ds(off, blk), :]
                vb = v_ref[pl.ds(off, blk), :]
                for mp in range(2):
                    s = lax.dot_general(qs[mp], kb, nt, preferred_element_type=F32)
                    if bias is not None:
                        s = s + bias
                    m_old = m_ref[mp]
                    m_new = jnp.maximum(m_old, jnp.max(s, axis=-1, keepdims=True))
                    alpha = jnp.exp2(m_old - m_new)
                    p = jnp.exp2(s - jnp.concatenate([m_new] * (blk // LANES), axis=1))
                    l_ref[mp] = alpha * l_ref[mp] + jnp.sum(p, axis=-1, keepdims=True)
                    acc_ref[mp] = alpha * acc_ref[mp] + jnp.dot(p.astype(BF16), vb, preferred_element_type=F32)
                    m_ref[mp] = m_new

            _causal_sweep(qi, blk, bias_ref, update)
            store_o(qi, acc_ref[0] / l_ref[0] - lam * (acc_ref[1] / l_ref[1]))
            return carry

        lax.fori_loop(0, nq, q_block, 0)


def _prompt_attn(scal, lamv, qn, kn, vb, bias, n_seq, seq, blk, lam_init):
    nq = seq // blk
    r = n_seq * seq
    return pl.pallas_call(
        functools.partial(_attn_kernel, blk=blk, nq=nq, lam_init=lam_init),
        grid=(n_seq, H_B),
        in_specs=[
            pl.BlockSpec(memory_space=pltpu.SMEM),
            pl.BlockSpec(lamv.shape, lambda b, h: (0, 0)),
            pl.BlockSpec((seq, LANES), lambda b, h: (b, h)),
            pl.BlockSpec((seq, LANES), lambda b, h: (b, h)),
            pl.BlockSpec((seq, LANES), lambda b, h: (b, h)),
            pl.BlockSpec((1, blk, 2 * blk), lambda b, h: (h, 0, 0)),
        ],
        out_specs=pl.BlockSpec((seq, LANES), lambda b, h: (b, h)),
        out_shape=jax.ShapeDtypeStruct((r, HB_W), F32),
        scratch_shapes=[pltpu.VMEM((2, blk, LANES), F32), pltpu.VMEM((2, blk, LANES), F32),
                        pltpu.VMEM((2, blk, DV_B), F32), pltpu.VMEM((2 * blk, 2 * DV_B), F32)],
        compiler_params=_cparams(2, VMEM_LARGE_MIB),
        name="prompt_attn",
    )(scal, lamv, qn, kn, vb, bias)


def _tri_masks():
    i = lax.broadcasted_iota(jnp.int32, (CHUNK, CHUNK), 0)
    j = lax.broadcasted_iota(jnp.int32, (CHUNK, CHUNK), 1)
    incl = i >= j
    strict = i > j
    eye = (i == j).astype(F32)
    base = jnp.logical_and(strict, (i // SUBLANES) == (j // SUBLANES))
    levels = []
    s = SUBLANES
    while s < CHUNK:
        levels.append(jnp.logical_and((i // (2 * s)) == (j // (2 * s)), (i // s) > (j // s)))
        s *= 2
    return incl, strict, eye, base, levels


def _gdn_sequence(b, qkv_ref, ba_ref, prm, o_ref, sout_ref, s_ref, masks, nchunk):
    incl, strict, eye, base, levels = masks
    tril = incl.astype(F32)
    nt = (((1,), (1,)), ((), ()))
    y = qkv_ref[b]
    bg = ba_ref[b]
    beta_all = _sigmoid(bg)
    g_all = -jnp.exp(prm[0:1]) * _softplus(bg + prm[1:2])
    gcs = [jnp.dot(tril, g_all[c * CHUNK:(c + 1) * CHUNK], precision=lax.Precision.HIGHEST,
                   preferred_element_type=F32) for c in range(nchunk)]
    yield
    gct = [gc.T for gc in gcs]
    grp = [(c, h) for c in range(nchunk) for h in range(H_A)]
    q_, k_, kb_, vb_, dec_, eg_, ekd_, gl_ = [], [], [], [], [], [], [], []
    for (c, h) in grp:
        rs = slice(c * CHUNK, (c + 1) * CHUNK)
        q = y[rs, h * DK_A:(h + 1) * DK_A]
        k = y[rs, H_A * DK_A + h * DK_A:H_A * DK_A + (h + 1) * DK_A]
        v = y[rs, 2 * H_A * DK_A + h * DV_A:2 * H_A * DK_A + (h + 1) * DV_A]
        beta = beta_all[rs, h:h + 1]
        gcol = gcs[c][:, H_A + h:H_A + h + 1]
        grow = gct[c][H_A + h:H_A + h + 1, :]
        glast = gcs[c][CHUNK - 1:CHUNK, H_A + h:H_A + h + 1]
        dec_.append(jnp.where(incl, jnp.exp(jnp.where(incl, gcol - grow, 0.0)), 0.0))
        eg_.append(jnp.exp(gcol))
        ekd_.append(jnp.exp(glast - gcol))
        gl_.append(jnp.exp(glast))
        q_.append(q)
        k_.append(k)
        kb_.append(k * beta)
        vb_.append(v * beta)
    n = len(grp)
    yield
    kk_qk = [lax.dot_general(jnp.concatenate([kb_[g], q_[g]], axis=0).astype(BF16), k_[g].astype(BF16),
                             nt, preferred_element_type=F32) for g in range(n)]
    yield
    ms = [jnp.where(strict, kk_qk[g][0:CHUNK] * dec_[g], 0.0) for g in range(n)]
    a_ = [(kk_qk[g][CHUNK:2 * CHUNK] * dec_[g]).astype(BF16) for g in range(n)]
    d = [jnp.where(base, m, 0.0) for m in ms]
    yield
    d2 = [_bdot(a, a) for a in d]
    yield
    d4 = [_bdot(a, a) for a in d2]
    x = [_bdot(eye - a, eye + bb) for a, bb in zip(d, d2)]
    yield
    x = [_bdot(a, eye + bb) for a, bb in zip(x, d4)]
    yield
    for lvl in levels:
        cm = [jnp.where(lvl, m, 0.0) for m in ms]
        xc = [_bdot(a, bb) for a, bb in zip(x, cm)]
        yield
        xcx = [_bdot(a, bb) for a, bb in zip(xc, x)]
        yield
        x = [a - bb for a, bb in zip(x, xcx)]
    rhs = [jnp.concatenate([vb_[g], kb_[g] * eg_[g]], axis=1) for g in range(n)]
    yield
    uw = [_bdot(x[g], rhs[g]) for g in range(n)]
    yield
    wq_ = [jnp.concatenate([uw[g][:, DV_A:DV_A + DK_A], q_[g] * eg_[g]], axis=0).astype(BF16)
           for g in range(n)]
    kd_ = [(k_[g] * ekd_[g]).astype(BF16) for g in range(n)]
    s = [s_ref[b, h] for h in range(H_A)]
    yield
    for c in range(nchunk):
        gi = [c * H_A + h for h in range(H_A)]
        ws = [jnp.dot(wq_[gi[h]], s[h].astype(BF16), preferred_element_type=F32) for h in range(H_A)]
        yield
        vn = [(uw[gi[h]][:, 0:DV_A] - ws[h][0:CHUNK]).astype(BF16) for h in range(H_A)]
        yield
        av = [jnp.dot(a_[gi[h]], vn[h], preferred_element_type=F32) for h in range(H_A)]
        kv = [lax.dot_general(kd_[gi[h]], vn[h], (((0,), (0,)), ((), ())), preferred_element_type=F32)
              for h in range(H_A)]
        yield
        for h in range(H_A):
            o_ref[b, c * CHUNK:(c + 1) * CHUNK, h * DV_A:(h + 1) * DV_A] = ws[h][CHUNK:2 * CHUNK] + av[h]
            s[h] = s[h] * gl_[gi[h]] + kv[h]
        yield
    for h in range(H_A):
        s_ref[b, h] = s[h]
        sout_ref[b, h] = s[h]


def _gdn_kernel(qkv_ref, ba_ref, prm_ref, o_ref, sout_ref, s_ref, *, tb, n_seq):
    j = pl.program_id(0)

    @pl.when(j == 0)
    def _():
        s_ref[...] = jnp.zeros(s_ref.shape, F32)

    prm = prm_ref[...]
    masks = _tri_masks()
    chains = [_gdn_sequence(b, qkv_ref, ba_ref, prm, o_ref, sout_ref, s_ref, masks, tb // CHUNK)
              for b in range(n_seq)]
    live = []
    while chains or live:
        if chains:
            live.append(chains.pop(0))
        for ch in list(live):
            if next(ch, StopIteration) is StopIteration:
                live.remove(ch)


def _prompt_gdn(qkv, ba, prm, n_seq, seq, tb):
    nb = seq // tb
    return pl.pallas_call(
        functools.partial(_gdn_kernel, tb=tb, n_seq=n_seq),
        grid=(nb,),
        in_specs=[
            pl.BlockSpec((n_seq, tb, QKV_A), lambda j: (0, j, 0)),
            pl.BlockSpec((n_seq, tb, LANES), lambda j: (0, j, 0)),
            pl.BlockSpec(prm.shape, lambda j: (0, 0)),
        ],
        out_specs=[
            pl.BlockSpec((n_seq, tb, H_A * DV_A), lambda j: (0, j, 0)),
            pl.BlockSpec((n_seq, H_A, DK_A, DV_A), lambda j: (0, 0, 0, 0)),
        ],
        out_shape=[
            jax.ShapeDtypeStruct((n_seq, seq, H_A * DV_A), F32),
            jax.ShapeDtypeStruct((n_seq, H_A, DK_A, DV_A), F32),
        ],
        scratch_shapes=[pltpu.VMEM((n_seq, H_A, DK_A, DV_A), F32)],
        compiler_params=_cparams(1, VMEM_MEDIUM_MIB),
        name="prompt_gdn",
    )(qkv, ba, prm)


def _merge_kernel(x_ref, oa_ref, ob_ref, nw_ref, w2_ref, onw_ref, sbw_ref, wb_ref, wo_ref, out_ref, *, ob_scale):
    x = x_ref[...]
    h = _rms(x, nw_ref[...]).astype(BF16)
    z = jnp.dot(h, w2_ref[:, P2_Z:P2_GA], preferred_element_type=F32)
    ob = ob_ref[...]
    sbw = sbw_ref[...]
    ob_n = [(_rms(ob[:, hd * DV_B:(hd + 1) * DV_B], sbw) * ob_scale).astype(BF16) for hd in range(H_B)]
    ga = jnp.dot(h, w2_ref[:, P2_GA:P2_GB], preferred_element_type=F32)
    oa = oa_ref[...]
    onw = onw_ref[...]
    oa_n = [(_rms(oa[:, hd * DV_A:(hd + 1) * DV_A], onw) * _silu(z[:, hd * DV_A:(hd + 1) * DV_A])).astype(BF16)
            for hd in range(H_A)]
    pb = jnp.dot(jnp.concatenate(ob_n, axis=1), wb_ref[1], preferred_element_type=F32)
    sga = _sigmoid(ga)
    gb = jnp.dot(h, w2_ref[:, P2_GB:P2_END], preferred_element_type=F32)
    pa = jnp.dot(jnp.concatenate(oa_n, axis=1), wb_ref[0], preferred_element_type=F32)
    mixed = (sga * pa + _sigmoid(gb) * pb).astype(BF16)
    out_ref[...] = x + jnp.dot(mixed, wo_ref[...], preferred_element_type=F32)


def _merge(x, oa, ob, nw, w2, onw, sbw, wb, wo, tm, ob_scale):
    r = x.shape[0]
    row = lambda w: pl.BlockSpec((tm, w), lambda i: (i, 0))
    full = lambda a: pl.BlockSpec(a.shape, lambda i: (0,) * a.ndim)
    return pl.pallas_call(
        functools.partial(_merge_kernel, ob_scale=ob_scale),
        grid=(r // tm,),
        in_specs=[row(D_MODEL), row(H_A * DV_A), row(H_B * DV_B), full(nw), full(w2), full(onw), full(sbw),
                  full(wb), full(wo)],
        out_specs=row(D_MODEL),
        out_shape=jax.ShapeDtypeStruct((r, D_MODEL), F32),
        compiler_params=_cparams(1, VMEM_LARGE_MIB),
        name="merge",
    )(x, oa, ob, nw, w2, onw, sbw, wb, wo)


FF_CHUNK = MXU_DIM
UP_AHEAD = 2
DOWN_GROUP = 4


def _ffn_kernel(x_ref, st_ref, nw_ref, wup_ref, cw_ref, cb_ref, wdn_ref, out_ref, stout_ref,
                carry_ref, ext_ref, *, tm, shift, halo):
    i = pl.program_id(1)

    @pl.when(i == 0)
    def _():
        carry_ref[...] = st_ref[0]

    x = x_ref[...]
    h = _rms(x, nw_ref[...]).astype(BF16)
    cw = cw_ref[...]
    cb = cb_ref[...]
    n_chunks = D_FF // FF_CHUNK

    def col(c, part):
        lo = part * D_FF + c * FF_CHUNK
        return slice(lo, lo + FF_CHUNK)

    def up_proj(c):
        return [jnp.dot(h, wup_ref[:, col(c, part)], preferred_element_type=F32) for part in range(2)]

    def gated(c, ups):
        ext = ext_ref.at[c % 2]
        parts = []
        for part in range(2):
            sl = col(c, part)
            es = slice(part * FF_CHUNK, (part + 1) * FF_CHUNK)
            ext[0:halo, es] = carry_ref[:, sl]
            ext[halo:halo + tm, es] = ups[part]
            u = cw[2:3, sl] * ups[part] + cb[:, sl]
            u = u + cw[1:2, sl] * ext[halo - shift:halo - shift + tm, es]
            u = u + cw[0:1, sl] * ext[halo - 2 * shift:halo - 2 * shift + tm, es]
            carry_ref[:, sl] = ext[tm:tm + halo, es]
            parts.append(u)
        return (_silu(parts[0]) * parts[1]).astype(BF16)

    acc = x
    ups, acts = {}, {}
    for s in range(n_chunks + UP_AHEAD + 1):
        if s < n_chunks:
            ups[s] = up_proj(s)
        c = s - UP_AHEAD
        if 0 <= c < n_chunks:
            acts[c] = gated(c, ups.pop(c))
        c = s - UP_AHEAD - 1
        if 0 <= c < n_chunks and (c % DOWN_GROUP == DOWN_GROUP - 1 or c == n_chunks - 1):
            c0 = c - c % DOWN_GROUP
            a = jnp.concatenate([acts.pop(i) for i in range(c0, c + 1)], axis=1)
            acc = acc + jnp.dot(a, wdn_ref[c0 * FF_CHUNK:(c + 1) * FF_CHUNK, :], preferred_element_type=F32)
    out_ref[...] = acc
    stout_ref[0] = carry_ref[...]


def _ffn(x, st, nw, wup, cw, cb, wdn, groups, tm, shift, halo):
    r = x.shape[0]
    tiles = r // groups // tm
    full = lambda a: pl.BlockSpec(a.shape, lambda g, i: (0,) * a.ndim)
    return pl.pallas_call(
        functools.partial(_ffn_kernel, tm=tm, shift=shift, halo=halo),
        grid=(groups, tiles),
        in_specs=[
            pl.BlockSpec((tm, D_MODEL), lambda g, i: (g * tiles + i, 0)),
            pl.BlockSpec((1, halo, 2 * D_FF), lambda g, i: (g, 0, 0)),
            full(nw), full(wup), full(cw), full(cb), full(wdn),
        ],
        out_specs=[
            pl.BlockSpec((tm, D_MODEL), lambda g, i: (g * tiles + i, 0)),
            pl.BlockSpec((1, halo, 2 * D_FF), lambda g, i: (g, 0, 0)),
        ],
        out_shape=[jax.ShapeDtypeStruct((r, D_MODEL), F32),
                   jax.ShapeDtypeStruct((groups, halo, 2 * D_FF), F32)],
        scratch_shapes=[pltpu.VMEM((halo, 2 * D_FF), F32), pltpu.VMEM((2, halo + tm, 2 * FF_CHUNK), F32)],
        compiler_params=_cparams(2, VMEM_LARGE_MIB),
        name="ffn",
    )(x, st, nw, wup, cw, cb, wdn)


def _sgdn_pre_kernel(qkv_ref, ba_ref, prm_ref, wq_ref, u_ref, kd_ref, a_ref, gl_ref, *, t_new, nb):
    prm = prm_ref[...]
    lane = lax.broadcasted_iota(jnp.int32, (nb, LANES), 1)

    ys, betas, gs = [], [], []
    for t in range(t_new):
        ys.append(qkv_ref[t])
        bg = ba_ref[t]
        betas.append(_sigmoid(bg))
        gs.append(-jnp.exp(prm[0:1]) * _softplus(bg + prm[1:2]))
    gcs = [gs[0]]
    for t in range(1, t_new):
        gcs.append(gcs[-1] + gs[t])

    a_out = [jnp.zeros((nb, LANES), F32) for _ in range(t_new)]
    for h in range(H_A):
        q, k, v, beta, gc = [], [], [], [], []
        for t in range(t_new):
            q.append(ys[t][:, h * DK_A:(h + 1) * DK_A])
            k.append(ys[t][:, H_A * DK_A + h * DK_A:H_A * DK_A + (h + 1) * DK_A])
            v.append(ys[t][:, 2 * H_A * DK_A + h * DV_A:2 * H_A * DK_A + (h + 1) * DV_A])
            beta.append(betas[t][:, h:h + 1])
            gc.append(gcs[t][:, H_A + h:H_A + h + 1])
        m = [[None] * t_new for _ in range(t_new)]
        for i in range(t_new):
            for jj in range(i + 1):
                dec = jnp.exp(gc[i] - gc[jj])
                if jj < i:
                    m[i][jj] = beta[i] * jnp.sum(k[i] * k[jj], axis=-1, keepdims=True) * dec
                aij = jnp.sum(q[i] * k[jj], axis=-1, keepdims=True) * dec
                a_out[i] = jnp.where(lane == h * SUBLANES + jj, aij, a_out[i])
        tm_ = [[None] * t_new for _ in range(t_new)]
        for i in range(t_new):
            for jj in range(i):
                acc = m[i][jj]
                for l in range(jj + 1, i):
                    acc = acc + m[i][l] * tm_[l][jj]
                tm_[i][jj] = -acc
        vb = [v[t] * beta[t] for t in range(t_new)]
        kbg = [k[t] * (beta[t] * jnp.exp(gc[t])) for t in range(t_new)]
        hs = slice(h * DK_A, (h + 1) * DK_A)
        for i in range(t_new):
            u = vb[i]
            w = kbg[i]
            for jj in range(i):
                u = u + tm_[i][jj] * vb[jj]
                w = w + tm_[i][jj] * kbg[jj]
            u_ref[i, :, hs] = u
            wq_ref[i, :, hs] = w
            wq_ref[t_new + i, :, hs] = q[i] * jnp.exp(gc[i])
            kd_ref[i, :, hs] = k[i] * jnp.exp(gc[t_new - 1] - gc[i])
        gl_ref[:, hs] = jnp.broadcast_to(jnp.exp(gc[t_new - 1]), (nb, DK_A))
    for i in range(t_new):
        a_ref[i] = a_out[i]


def _sgdn_pre(qkv_t, ba_t, prm, t_new, nb):
    wide = H_A * DK_A
    return pl.pallas_call(
        functools.partial(_sgdn_pre_kernel, t_new=t_new, nb=nb),
        out_shape=[
            jax.ShapeDtypeStruct((2 * t_new, nb, wide), F32),
            jax.ShapeDtypeStruct((t_new, nb, wide), F32),
            jax.ShapeDtypeStruct((t_new, nb, wide), F32),
            jax.ShapeDtypeStruct((t_new, nb, LANES), F32),
            jax.ShapeDtypeStruct((nb, wide), F32),
        ],
        compiler_params=_cparams(0, VMEM_MEDIUM_MIB),
        name="sample_gdn_pre",
    )(qkv_t, ba_t, prm)


SGDN_BB = SUBLANES


def _sgdn_state_kernel(wq_ref, u_ref, kd_ref, a_ref, gl_ref, s0_ref, *refs, t_new):
    o_ref, s1_ref = refs[-2:]
    rows = 2 * t_new
    rid = lax.broadcasted_iota(jnp.int32, (rows, DK_A), 0)
    zpad = jnp.zeros((rows - t_new, DK_A), F32)
    pairs = [(bi, h) for bi in range(SGDN_BB) for h in range(H_A)]
    hs = [slice(h * DK_A, (h + 1) * DK_A) for h in range(H_A)]
    r = {(bi, h): jnp.dot(wq_ref[:, bi, hs[h]].astype(BF16), s0_ref[bi, h].astype(BF16),
                          preferred_element_type=F32) for (bi, h) in pairs}
    v_new = {}
    for (bi, h) in pairs:
        u8 = jnp.concatenate([u_ref[:, bi, hs[h]], zpad], axis=0)
        v_new[bi, h] = jnp.where(rid < t_new, u8 - r[bi, h], 0.0)
    kv = {}
    for (bi, h) in pairs:
        kd8 = jnp.concatenate([kd_ref[:, bi, hs[h]], zpad], axis=0)
        kv[bi, h] = lax.dot_general(kd8.astype(BF16), v_new[bi, h].astype(BF16), (((0,), (0,)), ((), ())),
                                    preferred_element_type=F32)
    for (bi, h) in pairs:
        amat = a_ref[:, bi, :]
        o = r[bi, h][t_new:rows]
        for jj in range(t_new):
            col = amat[:, h * SUBLANES + jj:h * SUBLANES + jj + 1]
            o = o + col * v_new[bi, h][jj:jj + 1, :]
        o_ref[:, bi, hs[h]] = o
        s1_ref[bi, h] = s0_ref[bi, h] * gl_ref[bi:bi + 1, hs[h]] + kv[bi, h]


def _sgdn_state(wq, u, kd, a, gl, s0_all, li, t_new, nb, stacked=None):
    wide = H_A * DK_A
    bb = SGDN_BB
    state_spec = pl.BlockSpec((None, bb, H_A, DK_A, DV_A), lambda i: (li, i, 0, 0, 0))
    in_specs = [
        pl.BlockSpec((2 * t_new, bb, wide), lambda i: (0, i, 0)),
        pl.BlockSpec((t_new, bb, wide), lambda i: (0, i, 0)),
        pl.BlockSpec((t_new, bb, wide), lambda i: (0, i, 0)),
        pl.BlockSpec((t_new, bb, LANES), lambda i: (0, i, 0)),
        pl.BlockSpec((bb, wide), lambda i: (i, 0)),
        state_spec,
    ]
    args = [wq, u, kd, a, gl, s0_all]
    aliases = {}
    if stacked is not None:
        in_specs.append(pl.BlockSpec(memory_space=pl.ANY))
        args.append(stacked)
        aliases = {6: 1}
    return pl.pallas_call(
        functools.partial(_sgdn_state_kernel, t_new=t_new),
        grid=(nb // bb,),
        in_specs=in_specs,
        out_specs=[pl.BlockSpec((t_new, bb, wide), lambda i: (0, i, 0)), state_spec],
        out_shape=[jax.ShapeDtypeStruct((t_new, nb, wide), F32),
                   jax.ShapeDtypeStruct(s0_all.shape, F32)],
        input_output_aliases=aliases,
        compiler_params=_cparams(1, VMEM_SMALL_MIB),
        name="sample_gdn_state",
    )(*args)


NEW_ROWS = 2 * SUBLANES


def _sattn_kernel(pt_ref, lam_ref, q_ref, kn_ref, vn_ref, bias_ref, *rest, n_pages, t_new, lam_init):
    del pt_ref
    k_pages = rest[:n_pages]
    v_pages = rest[n_pages:2 * n_pages]
    o_ref = rest[2 * n_pages]
    rows = 2 * t_new
    q = q_ref[0]
    bias = bias_ref[...]
    lam = _lam_value(lam_ref[...], lam_init)
    r = lax.broadcasted_iota(jnp.int32, (rows, 2 * DH_B), 0)
    c = lax.broadcasted_iota(jnp.int32, (rows, 2 * DH_B), 1)
    map_mask = (c // DH_B) == (r // t_new)
    pad = jnp.zeros((PAGE - NEW_ROWS, LANES), BF16)
    nt = (((1,), (1,)), ((), ()))

    heads = range(H_B)
    lanes = [slice(h * LANES, (h + 1) * LANES) for h in heads]
    head_rows = [pl.ds(h, PAGE, stride=H_B) for h in heads]
    qx = [jnp.where(map_mask, jnp.concatenate([q[:, lanes[h]]] * 2, axis=0), 0.0).astype(BF16) for h in heads]
    bh = [bias[h * rows:(h + 1) * rows] for h in heads]

    groups = [list(range(p, min(p + MXU_DIM // PAGE, n_pages))) for p in range(0, n_pages, MXU_DIM // PAGE)]

    def head_tile(page_refs, grp, h):
        return jnp.concatenate([page_refs[p][0, 0, head_rows[h], :] for p in grp], axis=0).astype(BF16)

    s_parts = [[] for _ in heads]
    for grp in groups:
        for h in heads:
            s = lax.dot_general(qx[h], head_tile(k_pages, grp, h), nt, preferred_element_type=F32)
            if grp[-1] == n_pages - 1:
                zeros = [jnp.zeros((rows, PAGE), F32)] * (len(grp) - 1)
                s = s + jnp.concatenate(zeros + [bh[h][:, 0:PAGE]], axis=1)
            s_parts[h].append(s)
    for h in heads:
        kn = jnp.concatenate([kn_ref[0, :, lanes[h]], pad], axis=0)
        s_parts[h].append(lax.dot_general(qx[h], kn, nt, preferred_element_type=F32) + bh[h][:, PAGE:2 * PAGE])

    m = []
    for h in heads:
        mh = s_parts[h][0].max(axis=-1, keepdims=True)
        for s in s_parts[h][1:]:
            mh = jnp.maximum(mh, s.max(axis=-1, keepdims=True))
        m.append(mh)
    l = [jnp.zeros((rows, 1), F32) for _ in heads]
    acc = [jnp.zeros((rows, DV_B), F32) for _ in heads]
    for gi in range(len(groups) + 1):
        for h in heads:
            pr = jnp.exp2(s_parts[h][gi] - m[h])
            l[h] = l[h] + jnp.sum(pr, axis=-1, keepdims=True)
            if gi < len(groups):
                vv = head_tile(v_pages, groups[gi], h)
            else:
                vv = jnp.concatenate([vn_ref[0, :, lanes[h]], pad], axis=0)
            acc[h] = acc[h] + jnp.dot(pr.astype(BF16), vv, preferred_element_type=F32)
    for h in heads:
        a = acc[h] / l[h]
        o_ref[0, :, lanes[h]] = a[0:t_new] - lam * a[t_new:rows]


def _sample_attn(page_table, lamv, q_b, kn_b, vn_b, bias, ck, cv, li, t_new, lam_init):
    nb, n_pages = page_table.shape
    page_spec = lambda p: pl.BlockSpec((1, 1, PAGE * H_B, LANES), lambda b, pt, p=p: (li, pt[b, p], 0, 0))
    tok_spec = pl.BlockSpec((1, t_new, HB_W), lambda b, pt: (b, 0, 0))
    new_spec = pl.BlockSpec((1, NEW_ROWS, HB_W), lambda b, pt: (b, 0, 0))
    grid_spec = pltpu.PrefetchScalarGridSpec(
        num_scalar_prefetch=1,
        grid=(nb,),
        in_specs=[pl.BlockSpec(lamv.shape, lambda b, pt: (0, 0)), tok_spec, new_spec, new_spec,
                  pl.BlockSpec(bias.shape, lambda b, pt: (0, 0))]
                 + [page_spec(p) for p in range(n_pages)] * 2,
        out_specs=pl.BlockSpec((1, t_new, HB_W), lambda b, pt: (b, 0, 0)),
    )
    return pl.pallas_call(
        functools.partial(_sattn_kernel, n_pages=n_pages, t_new=t_new, lam_init=lam_init),
        grid_spec=grid_spec,
        out_shape=jax.ShapeDtypeStruct((nb, t_new, HB_W), F32),
        compiler_params=_cparams(1, VMEM_SMALL_MIB),
        name="sample_attn",
    )(page_table, lamv, q_b, kn_b, vn_b, bias, *([ck] * n_pages), *([cv] * n_pages))


def _pick(n, pref):
    return pref if n % pref == 0 else n


def _to_bmajor(a_t, t_new, nb):
    return jnp.swapaxes(a_t.reshape(t_new, nb, -1), 0, 1)


def _layer_weights(li, rel_bias, norm_mix, w_in, conv_a, a_log, dt_bias, onorm_a, qnorm_b, knorm_b,
                   lam_q1, lam_k1, lam_q2, lam_k2, subln_b, w_branch, w_o, norm_ffn, w_up, conv_f,
                   conv_f_b, w_down):
    w = w_in[li]
    o_z = QKV_A
    o_b = o_z + H_A * DV_A
    o_qb = o_b + 2 * H_A
    o_kb = o_qb + HB_W
    o_vb = o_kb + HB_W
    o_ga = o_vb + H_B * DV_B
    o_gb = o_ga + D_MODEL
    ba_cols = jnp.pad(w[:, o_b:o_qb], ((0, 0), (0, LANES - 2 * H_A)))
    w1 = jnp.concatenate([w[:, 0:o_z], w[:, o_qb:o_ga], ba_cols], axis=1).astype(BF16)
    w2 = jnp.concatenate([w[:, o_z:o_b], w[:, o_ga:]], axis=1).astype(BF16)
    prm = jnp.zeros((2, LANES), F32)
    prm = prm.at[0, H_A:2 * H_A].set(a_log[li]).at[1, H_A:2 * H_A].set(dt_bias[li])
    k_bound = 1.01 * math.sqrt(DH_B) * jnp.max(jnp.abs(knorm_b[li]))
    q_bound = 1.01 * Q_SCALE * math.sqrt(DH_B) * jnp.max(jnp.abs(qnorm_b[li]))
    rel = (rel_bias - rel_bias[NUM_BUCKETS - 1:NUM_BUCKETS]) * LOG2E
    spread = 2.0 * q_bound * k_bound + jnp.max(jnp.max(rel, axis=0) - jnp.min(rel, axis=0))
    scal = jnp.concatenate([jnp.where(spread <= SAFE_SPREAD_BITS, 1.0, 0.0).reshape(1), k_bound.reshape(1),
                            jnp.max(rel, axis=0), jnp.zeros((2,), F32)]).astype(F32)
    return dict(
        scal=scal,
        nw=norm_mix[li].reshape(1, D_MODEL), w1=w1, w2=w2,
        qw=jnp.tile(qnorm_b[li], 2 * H_B).reshape(1, HB_W),
        kw=jnp.tile(knorm_b[li], 2 * H_B).reshape(1, HB_W),
        cw_a=conv_a[li], prm=prm,
        onw=onorm_a[li].reshape(1, DV_A), sbw=subln_b[li].reshape(1, DV_B),
        lamv=jnp.stack([lam_q1[li], lam_k1[li], lam_q2[li], lam_k2[li]]),
        wb=w_branch[li].astype(BF16), wo=w_o[li].astype(BF16),
        nwf=norm_ffn[li].reshape(1, D_MODEL), wup=w_up[li].astype(BF16),
        cw_f=conv_f[li], cb_f=conv_f_b[li].reshape(1, 2 * D_FF), wdn=w_down[li].astype(BF16),
    )


def _group_ones():
    i = jnp.arange(MXU_DIM)
    return ((i[:, None] // DH_B) == (i[None, :] // DH_B)).astype(BF16)


def _prompt_layer(x, wt, bias_p, g, n_seq, seq, lam_init, li, depth, kv_stacked):
    tm = _pick(seq, ROW_TILE)
    blk = bias_p.shape[1]
    zero_st_a = jnp.zeros((n_seq, SUBLANES, QKV_A), F32)
    qkv, ba, qn, kn, kf, vf, vb, cst_a = _inproj(x, wt["nw"], wt["w1"], wt["qw"], wt["kw"], g, zero_st_a,
                                                 wt["cw_a"], n_seq, tm, 1, SUBLANES,
                                                 layer=li, depth=depth, stacked=kv_stacked)
    o_a, s_fin = _prompt_gdn(qkv.reshape(n_seq, seq, QKV_A), ba.reshape(n_seq, seq, LANES),
                             wt["prm"], n_seq, seq, _pick(seq, GDN_TIME_BLOCK))
    o_a = o_a.reshape(n_seq * seq, H_A * DV_A)
    o_b = _prompt_attn(wt["scal"], wt["lamv"], qn, kn, vb, bias_p, n_seq, seq, blk, lam_init)
    x = _merge(x, o_a, o_b, wt["nw"], wt["w2"], wt["onw"], wt["sbw"], wt["wb"], wt["wo"], tm, 1.0 - lam_init)
    zero_st = jnp.zeros((n_seq, SUBLANES, 2 * D_FF), F32)
    x, cst_f = _ffn(x, zero_st, wt["nwf"], wt["wup"], wt["cw_f"], wt["cb_f"], wt["wdn"],
                    n_seq, tm, 1, SUBLANES)
    return (x, kf, vf, s_fin, cst_a[:, SUBLANES - (CONV_A - 1):], cst_f[:, SUBLANES - (CONV_F - 1):])


def _sample_layer(x_t, wt, bias_s, g, page_table, ck, cv, li, s0_all, s1_stacked, cst_a, cst_f, nb, t_new,
                  lam_init):
    r = nb * t_new
    halo_a = (CONV_A - 1) * nb
    st_a_t = jnp.swapaxes(cst_a, 0, 1).reshape(1, halo_a, QKV_A)
    qkv, ba, qn, kn, kf, vf, vb, cst_a_t = _inproj(x_t, wt["nw"], wt["w1"], wt["qw"], wt["kw"], g, st_a_t,
                                                   wt["cw_a"], 1, r, nb, halo_a)
    cst_a_t = cst_a_t.reshape(CONV_A - 1, nb, QKV_A)
    wq, u, kd, a, gl = _sgdn_pre(qkv.reshape(t_new, nb, QKV_A), ba.reshape(t_new, nb, LANES), wt["prm"],
                                 t_new, nb)
    o_a_t, s1 = _sgdn_state(wq, u, kd, a, gl, s0_all, li, t_new, nb, stacked=s1_stacked)
    pad_new = lambda a: jnp.pad(_to_bmajor(a, t_new, nb), ((0, 0), (0, NEW_ROWS - t_new), (0, 0)))
    o_b = _sample_attn(page_table, wt["lamv"], _to_bmajor(qn, t_new, nb).astype(F32), pad_new(kn),
                       pad_new(vb), bias_s, ck, cv, li, t_new, lam_init)
    o_b_t = jnp.swapaxes(o_b, 0, 1).reshape(r, HB_W)
    x_t = _merge(x_t, o_a_t.reshape(r, H_A * DV_A), o_b_t, wt["nw"], wt["w2"], wt["onw"], wt["sbw"],
                 wt["wb"], wt["wo"], r, 1.0 - lam_init)
    halo = (CONV_F - 1) * nb
    st_f_t = jnp.swapaxes(cst_f, 0, 1).reshape(1, halo, 2 * D_FF)
    x_t, cst_f_t = _ffn(x_t, st_f_t, wt["nwf"], wt["wup"], wt["cw_f"], wt["cb_f"], wt["wdn"],
                        1, r, nb, halo)
    return (x_t, _to_bmajor(kf, t_new, nb).reshape(nb, t_new, H_B, 2 * DH_B),
            _to_bmajor(vf, t_new, nb).reshape(nb, t_new, H_B, DV_B), s1,
            jnp.swapaxes(cst_a_t, 0, 1), jnp.swapaxes(cst_f_t.reshape(CONV_F - 1, nb, 2 * D_FF), 0, 1))


def kernel(x_prompt, x_sample, cache_k, cache_v, state_delta, state_conv_a, state_conv_ffn, page_table,
           rel_bias, norm_mix, w_in, conv_a, a_log, dt_bias, onorm_a, qnorm_b, knorm_b, lam_q1, lam_k1,
           lam_q2, lam_k2, subln_b, w_branch, w_o, norm_ffn, w_up, conv_f, conv_f_b, w_down):
    n_seq, seq, _ = x_prompt.shape
    nb, t_new, _ = x_sample.shape
    depth = w_in.shape[0]
    n_pool = cache_k.shape[1]

    rb_flat = rel_bias.T.reshape(-1)
    bias_p = _prompt_bias(rb_flat, _pick(seq, ATTN_BLOCK))
    bias_s = _sample_bias(rb_flat, t_new)
    g = _group_ones()
    ck = cache_k.reshape(depth, n_pool, PAGE * H_B, 2 * DH_B)
    cv = cache_v.reshape(depth, n_pool, PAGE * H_B, DV_B)

    xp = x_prompt.reshape(n_seq * seq, D_MODEL)
    xs = jnp.swapaxes(x_sample, 0, 1).reshape(t_new * nb, D_MODEL)
    outs_p, outs_s = [], []
    kv_stacked, s1_stacked = None, None
    for li in range(depth):
        wt = _layer_weights(li, rel_bias, norm_mix, w_in, conv_a, a_log, dt_bias, onorm_a, qnorm_b, knorm_b,
                            lam_q1, lam_k1, lam_q2, lam_k2, subln_b, w_branch, w_o, norm_ffn, w_up,
                            conv_f, conv_f_b, w_down)
        lam_init = 0.8 - 0.6 * math.exp(-0.3 * li)
        xp, kf_all, vf_all, *rest_p = _prompt_layer(xp, wt, bias_p, g, n_seq, seq, lam_init, li, depth,
                                                    kv_stacked)
        kv_stacked = (kf_all, vf_all)
        xs, ks, vs, s1_stacked, *rest_s = _sample_layer(xs, wt, bias_s, g, page_table, ck, cv, li, state_delta,
                                                        s1_stacked, state_conv_a[li], state_conv_ffn[li], nb,
                                                        t_new, lam_init)
        outs_p.append(rest_p)
        outs_s.append([ks, vs] + rest_s)

    stack = lambda outs, i: jnp.stack([o[i] for o in outs])
    y_prompt = xp.reshape(n_seq, seq, D_MODEL)
    y_sample = jnp.swapaxes(xs.reshape(t_new, nb, D_MODEL), 0, 1)
    k_prompt = kv_stacked[0].reshape(depth, n_seq, seq, H_B, 2 * DH_B)
    v_prompt = kv_stacked[1].reshape(depth, n_seq, seq, H_B, DV_B)
    return (y_prompt, y_sample,
            k_prompt, v_prompt, stack(outs_p, 0), stack(outs_p, 1), stack(outs_p, 2),
            stack(outs_s, 0), stack(outs_s, 1), s1_stacked, stack(outs_s, 2), stack(outs_s, 3))
```

```python
import functools
import math

import jax
import jax.numpy as jnp
from jax import lax
from jax.experimental import pallas as pl
from jax.experimental.pallas import tpu as pltpu

F32 = jnp.float32
BF16 = jnp.bfloat16

D_MODEL = 1024
H_A, DK_A, DV_A, CONV_A, CHUNK = 4, 128, 128, 4, 64
QKV_A = 2 * H_A * DK_A + H_A * DV_A
H_B, DH_B, DV_B = 4, 64, 128
HB_W = H_B * 2 * DH_B
PAGE = 128
D_FF, CONV_F = 2816, 3
NUM_BUCKETS, MAX_DISTANCE = 32, 128
NEG_INF = -1e30
EPS = 1e-6
ATT_SCALE = DH_B ** -0.5
LOG2E = math.log2(math.e)
Q_SCALE = ATT_SCALE * LOG2E
SAFE_SPREAD_BITS = 100.0

V7X_VMEM_BYTES = 64 * 1024 * 1024
LANES = 128
SUBLANES = 8
MXU_DIM = 256
MIB = 1024 * 1024

ROW_TILE = 512
ATTN_BLOCK = 512
GDN_TIME_BLOCK = 8 * CHUNK
VMEM_LARGE_MIB = 56
VMEM_MEDIUM_MIB = 48
VMEM_SMALL_MIB = 40

P1_QKV = 0
P1_Q = P1_QKV + QKV_A
P1_K = P1_Q + HB_W
P1_V = P1_K + HB_W
P1_BA = P1_V + H_B * DV_B
P1_END = P1_BA + LANES
P2_Z = 0
P2_GA = P2_Z + H_A * DV_A
P2_GB = P2_GA + D_MODEL
P2_END = P2_GB + D_MODEL


def _cparams(n_axes, vmem_mib):
    return pltpu.CompilerParams(
        dimension_semantics=("arbitrary",) * n_axes,
        vmem_limit_bytes=vmem_mib * MIB,
    )


def _bdot(a, b):
    return jnp.dot(a.astype(BF16), b.astype(BF16), preferred_element_type=F32)


def _rms(x, w):
    return x * lax.rsqrt(jnp.mean(x * x, axis=-1, keepdims=True) + EPS) * w


def _sigmoid(x):
    return 1.0 / (1.0 + jnp.exp(-x))


def _silu(x):
    return x * _sigmoid(x)


def _softplus(x):
    return jnp.maximum(x, 0.0) + jnp.log1p(jnp.exp(-jnp.abs(x)))


def _inproj_kernel(x_ref, nw_ref, w_ref, qw_ref, kw_ref, g_ref, st_ref, cw_ref, *refs, shift, halo):
    n_out = 8
    qkv_ref, ba_ref, qn_ref, kn_ref, kf_ref, vf_ref, vb_ref, stout_ref = refs[-n_out - 2:-2]
    carry_ref, ext_ref = refs[-2:]
    h = _rms(x_ref[...], nw_ref[...]).astype(BF16)

    def proj(lo, hi):
        return jnp.dot(h, w_ref[:, lo:hi], preferred_element_type=F32)

    tm = x_ref.shape[0]
    head_rows = [pl.ds(hd, tm, stride=H_B) for hd in range(H_B)]

    @pl.when(pl.program_id(1) == 0)
    def _():
        carry_ref[...] = st_ref[0]

    cw = cw_ref[...]
    g = g_ref[...]
    seg = H_A * DK_A

    def delta_front(part, raw):
        cols = slice(part * seg, (part + 1) * seg)
        ext_ref[0:halo, cols] = carry_ref[:, cols]
        ext_ref[halo:halo + tm, cols] = raw
        y = cw[CONV_A - 1:CONV_A, cols] * raw
        for i in range(CONV_A - 1):
            back = (CONV_A - 1 - i) * shift
            y = y + cw[i:i + 1, cols] * ext_ref[halo - back:halo - back + tm, cols]
        new_carry = ext_ref[tm:tm + halo, cols]
        carry_ref[:, cols] = new_carry
        stout_ref[0, :, cols] = new_carry
        y = _silu(y)
        if part == 2:
            qkv_ref[:, cols] = y
            return
        for hd in range(H_A):
            sl = slice(hd * DK_A, (hd + 1) * DK_A)
            t = y[:, sl]
            t = t * lax.rsqrt(jnp.sum(t * t, axis=-1, keepdims=True) + EPS)
            qkv_ref[:, part * seg + hd * DK_A:part * seg + (hd + 1) * DK_A] = t * (DK_A ** -0.5) if part == 0 else t

    def qk_norm(y, w):
        sq = y * y
        hi = sq.astype(BF16)
        lo = (sq - hi.astype(F32)).astype(BF16)
        outs = []
        for c in range(HB_W // MXU_DIM):
            sl = slice(c * MXU_DIM, (c + 1) * MXU_DIM)
            ss = (jnp.dot(hi[:, sl], g, preferred_element_type=F32)
                  + jnp.dot(lo[:, sl], g, preferred_element_type=F32))
            outs.append(y[:, sl] * lax.rsqrt(ss * (1.0 / DH_B) + EPS) * w[:, sl])
        return outs

    def attn_q(y):
        qn = qk_norm(y, qw_ref[...])
        for c in range(HB_W // MXU_DIM):
            qn_ref[:, c * MXU_DIM:(c + 1) * MXU_DIM] = (qn[c] * Q_SCALE).astype(BF16)

    def attn_k(y):
        kn = qk_norm(y, kw_ref[...])
        for c in range(HB_W // MXU_DIM):
            kn_ref[:, c * MXU_DIM:(c + 1) * MXU_DIM] = kn[c].astype(BF16)
            for i in range(MXU_DIM // LANES):
                kf_ref[head_rows[c * (MXU_DIM // LANES) + i], :] = kn[c][:, i * LANES:(i + 1) * LANES]

    def attn_v(v):
        for hd in range(H_B):
            vf_ref[head_rows[hd], :] = v[:, hd * DV_B:(hd + 1) * DV_B]
        vb_ref[...] = v.astype(BF16)

    def store_ba(y):
        ba_ref[...] = y

    stages = [
        ((P1_QKV, P1_QKV + seg), functools.partial(delta_front, 0)),
        ((P1_QKV + seg, P1_QKV + 2 * seg), functools.partial(delta_front, 1)),
        ((P1_QKV + 2 * seg, P1_Q), functools.partial(delta_front, 2)),
        ((P1_Q, P1_K), attn_q),
        ((P1_K, P1_V), attn_k),
        ((P1_V, P1_BA), attn_v),
        ((P1_BA, P1_END), store_ba),
    ]
    pending = None
    for cols, post in stages:
        y = proj(*cols)
        if pending is not None:
            pending[1](pending[0])
        pending = (y, post)
    pending[1](pending[0])


def _inproj(x, nw, w1, qw, kw, g, st, cw, groups, tm, shift, halo, layer=None, depth=None, stacked=None):
    r = x.shape[0]
    tiles = r // groups // tm
    row = lambda w: pl.BlockSpec((tm, w), lambda gi, i: (gi * tiles + i, 0))
    full = lambda a: pl.BlockSpec(a.shape, lambda gi, i: (0,) * a.ndim)
    st_spec = pl.BlockSpec((1, halo, QKV_A), lambda gi, i: (gi, 0, 0))
    outs = [(1, QKV_A, F32), (1, LANES, F32), (1, HB_W, BF16), (1, HB_W, BF16), (H_B, LANES, F32),
            (H_B, LANES, F32), (1, HB_W, BF16)]
    out_specs = [pl.BlockSpec((tm * m, w), lambda gi, i: (gi * tiles + i, 0)) for m, w, _ in outs]
    out_shape = [jax.ShapeDtypeStruct((r * m, w), dt) for m, w, dt in outs]
    out_specs.append(st_spec)
    out_shape.append(jax.ShapeDtypeStruct((groups, halo, QKV_A), F32))
    in_specs = [row(D_MODEL), full(nw), full(w1), full(qw), full(kw), full(g), st_spec, full(cw)]
    args = [x, nw, w1, qw, kw, g, st, cw]
    aliases = {}
    if layer is not None:
        for o in (4, 5):
            m, w, dt = outs[o]
            out_specs[o] = pl.BlockSpec((None, tm * m, w), lambda gi, i: (layer, gi * tiles + i, 0))
            out_shape[o] = jax.ShapeDtypeStruct((depth, r * m, w), dt)
        if stacked is not None:
            in_specs += [pl.BlockSpec(memory_space=pl.ANY)] * 2
            args += list(stacked)
            aliases = {len(args) - 2: 4, len(args) - 1: 5}
    return pl.pallas_call(
        functools.partial(_inproj_kernel, shift=shift, halo=halo),
        grid=(groups, tiles),
        in_specs=in_specs,
        out_specs=out_specs,
        out_shape=out_shape,
        input_output_aliases=aliases,
        scratch_shapes=[pltpu.VMEM((halo, QKV_A), F32), pltpu.VMEM((halo + tm, QKV_A), F32)],
        compiler_params=_cparams(2, VMEM_LARGE_MIB),
        name="inproj",
    )(*args)


def _bucket_bias(d, table):
    n = jnp.maximum(d, 0)
    max_exact = NUM_BUCKETS // 2
    nf = jnp.maximum(n, 1).astype(F32)
    large = max_exact + (jnp.log(nf / max_exact) / math.log(MAX_DISTANCE / max_exact)
                         * (NUM_BUCKETS - max_exact)).astype(jnp.int32)
    large = jnp.minimum(large, NUM_BUCKETS - 1)
    bucket = jnp.where(n < max_exact, n, large)
    val = jnp.zeros(d.shape, F32)
    for b in range(NUM_BUCKETS):
        val = jnp.where(bucket == b, table(b), val)
    return val


def _prompt_bias_kernel(rb_ref, o_ref, *, blk):
    h = pl.program_id(0)
    table = lambda b: rb_ref[h * NUM_BUCKETS + b]
    far = table(NUM_BUCKETS - 1)
    i = lax.broadcasted_iota(jnp.int32, (blk, blk), 0)
    j = lax.broadcasted_iota(jnp.int32, (blk, blk), 1)
    d0 = i - j
    o_ref[0, :, 0:blk] = (_bucket_bias(d0 + blk, table) - far) * LOG2E
    o_ref[0, :, blk:2 * blk] = jnp.where(d0 >= 0, (_bucket_bias(d0, table) - far) * LOG2E, NEG_INF)


def _prompt_bias(rb_flat, blk):
    return pl.pallas_call(
        functools.partial(_prompt_bias_kernel, blk=blk),
        grid=(H_B,),
        in_specs=[pl.BlockSpec(memory_space=pltpu.SMEM)],
        out_specs=pl.BlockSpec((1, blk, 2 * blk), lambda h: (h, 0, 0)),
        out_shape=jax.ShapeDtypeStruct((H_B, blk, 2 * blk), F32),
        compiler_params=_cparams(1, VMEM_SMALL_MIB),
        name="prompt_bias",
    )(rb_flat)


def _sample_bias_kernel(rb_ref, o_ref, *, t_new):
    rows, cols = o_ref.shape
    r = lax.broadcasted_iota(jnp.int32, (rows, cols), 0)
    c = lax.broadcasted_iota(jnp.int32, (rows, cols), 1)
    t = r % t_new
    hd = r // (2 * t_new)
    is_new = c >= PAGE
    d = jnp.where(is_new, t - (c - PAGE), t + PAGE - c)
    val = jnp.zeros((rows, cols), F32)
    for h in range(H_B):
        table = lambda b, h=h: rb_ref[h * NUM_BUCKETS + b]
        vh = _bucket_bias(d, table) - table(NUM_BUCKETS - 1)
        val = jnp.where(hd == h, vh, val)
    visible = jnp.logical_and(d >= 0, jnp.logical_or(~is_new, (c - PAGE) < t_new))
    o_ref[...] = jnp.where(visible, val * LOG2E, NEG_INF)


def _sample_bias(rb_flat, t_new):
    rows = H_B * 2 * t_new
    return pl.pallas_call(
        functools.partial(_sample_bias_kernel, t_new=t_new),
        in_specs=[pl.BlockSpec(memory_space=pltpu.SMEM)],
        out_shape=jax.ShapeDtypeStruct((rows, 2 * PAGE), F32),
        name="sample_bias",
    )(rb_flat)


def _lam_value(lv, lam_init):
    s1 = jnp.sum(lv[0:1] * lv[1:2], axis=-1, keepdims=True)
    s2 = jnp.sum(lv[2:3] * lv[3:4], axis=-1, keepdims=True)
    return jnp.exp(s1) - jnp.exp(s2) + lam_init


FAR_BLOCKS = 4


def _causal_sweep(qi, blk, bias_ref, update):
    n_far = jnp.maximum(qi - 1, 0)
    span = FAR_BLOCKS * blk

    def far(kk, carry):
        update(pl.multiple_of(kk * span, span), FAR_BLOCKS, None)
        return carry

    n_trips = n_far // FAR_BLOCKS
    lax.fori_loop(0, n_trips, far, 0)
    done = n_trips * FAR_BLOCKS
    rem = n_far - done
    for r in range(FAR_BLOCKS):
        @pl.when(jnp.logical_and(qi >= 1, rem == r))
        def _(r=r):
            update(pl.multiple_of(done * blk, blk), r + 2, bias_ref[0])

    @pl.when(qi == 0)
    def _():
        update(0, 1, bias_ref[0, :, blk:2 * blk])


def _attn_kernel(scal_ref, lam_ref, q_ref, k_ref, v_ref, bias_ref, o_ref, m_ref, l_ref, acc_ref, accx_ref,
                 *, blk, nq, lam_init):
    hd = pl.program_id(1)
    lane = lax.broadcasted_iota(jnp.int32, (1, 2 * DH_B), 1)
    lam = _lam_value(lam_ref[...], lam_init)
    nt = (((1,), (1,)), ((), ()))

    def q_maps(qi):
        q = q_ref[pl.ds(pl.multiple_of(qi * blk, blk), blk), :]
        zero = jnp.zeros_like(q)
        return jnp.where(lane < DH_B, q, zero), jnp.where(lane >= DH_B, q, zero)

    def store_o(qi, val):
        o_ref[pl.ds(pl.multiple_of(qi * blk, blk), blk), :] = val

    @pl.when(scal_ref[0] > 0.5)
    def _():
        one_col = jnp.where(lane == 0, 1.0, 0.0).astype(BF16)
        ext_k = {n: jnp.broadcast_to(one_col, (n * blk, 2 * DH_B)) for n in range(1, FAR_BLOCKS + 2)}

        def q_block(qi, carry):
            qs = q_maps(qi)
            qe = []
            for mp in range(2):
                qf = qs[mp].astype(F32)
                shift = jnp.sqrt(jnp.sum(qf * qf, axis=-1, keepdims=True)) * scal_ref[1] + scal_ref[2 + hd]
                qe.append(jnp.concatenate([qs[mp], jnp.where(lane == 0, -shift, 0.0).astype(BF16)], axis=1))
            qe = jnp.concatenate(qe, axis=0)
            accx_ref[...] = jnp.zeros(accx_ref.shape, F32)

            def update(off, nblk, bias):
                rows = nblk * blk
                kb = jnp.concatenate([k_ref[pl.ds(off, rows), :], ext_k[nblk]], axis=1)
                vb = jnp.concatenate([v_ref[pl.ds(off, rows), :], ext_k[nblk]], axis=1)
                s = lax.dot_general(qe, kb, nt, preferred_element_type=F32)
                if bias is not None:
                    plain = rows - bias.shape[1]
                    tail = s[:, plain:] + jnp.concatenate([bias, bias], axis=0)
                    s = tail if plain == 0 else jnp.concatenate([s[:, :plain], tail], axis=1)
                accx_ref[...] += jnp.dot(jnp.exp2(s).astype(BF16), vb, preferred_element_type=F32)

            _causal_sweep(qi, blk, bias_ref, update)
            a1 = accx_ref[0:blk]
            a2 = accx_ref[blk:2 * blk]
            store_o(qi, a1[:, 0:DV_B] / a1[:, DV_B:DV_B + 1] - lam * (a2[:, 0:DV_B] / a2[:, DV_B:DV_B + 1]))
            return carry

        lax.fori_loop(0, nq, q_block, 0)

    @pl.when(scal_ref[0] <= 0.5)
    def _():
        def q_block(qi, carry):
            qs = q_maps(qi)
            m_ref[...] = jnp.full(m_ref.shape, NEG_INF, F32)
            l_ref[...] = jnp.zeros(l_ref.shape, F32)
            acc_ref[...] = jnp.zeros(acc_ref.shape, F32)

            def update(off, nblk, bias):
                plain = nblk if bias is None else nblk - bias.shape[1] // blk
                for i in range(nblk):
                    update_one(pl.multiple_of(off + i * blk, blk),
                               None if i < plain else bias[:, (i - plain) * blk:(i - plain + 1) * blk])

            def update_one(off, bias):
                kb = k_ref[pl.ds(off, blk), :]
                vb = v_ref[pl.ds(off, blk), :]
                for mp in range(2):
                    s = lax.dot_general(qs[mp], kb, nt, preferred_element_type=F32)
                    if bias is not None:
                        s = s + bias
                    m_old = m_ref[mp]
                    m_new = jnp.maximum(m_old, jnp.max(s, axis=-1, keepdims=True))
                    alpha = jnp.exp2(m_old - m_new)
                    p = jnp.exp2(s - jnp.concatenate([m_new] * (blk // LANES), axis=1))
                    l_ref[mp] = alpha * l_ref[mp] + jnp.sum(p, axis=-1, keepdims=True)
                    acc_ref[mp] = alpha * acc_ref[mp] + jnp.dot(p.astype(BF16), vb, preferred_element_type=F32)
                    m_ref[mp] = m_new

            _causal_sweep(qi, blk, bias_ref, update)
            store_o(qi, acc_ref[0] / l_ref[0] - lam * (acc_ref[1] / l_ref[1]))
            return carry

        lax.fori_loop(0, nq, q_block, 0)


def _prompt_attn(scal, lamv, qn, kn, vb, bias, n_seq, seq, blk, lam_init):
    nq = seq // blk
    r = n_seq * seq
    return pl.pallas_call(
        functools.partial(_attn_kernel, blk=blk, nq=nq, lam_init=lam_init),
        grid=(n_seq, H_B),
        in_specs=[
            pl.BlockSpec(memory_space=pltpu.SMEM),
            pl.BlockSpec(lamv.shape, lambda b, h: (0, 0)),
            pl.BlockSpec((seq, LANES), lambda b, h: (b, h)),
            pl.BlockSpec((seq, LANES), lambda b, h: (b, h)),
            pl.BlockSpec((seq, LANES), lambda b, h: (b, h)),
            pl.BlockSpec((1, blk, 2 * blk), lambda b, h: (h, 0, 0)),
        ],
        out_specs=pl.BlockSpec((seq, LANES), lambda b, h: (b, h)),
        out_shape=jax.ShapeDtypeStruct((r, HB_W), F32),
        scratch_shapes=[pltpu.VMEM((2, blk, LANES), F32), pltpu.VMEM((2, blk, LANES), F32),
                        pltpu.VMEM((2, blk, DV_B), F32), pltpu.VMEM((2 * blk, 2 * DV_B), F32)],
        compiler_params=_cparams(2, VMEM_LARGE_MIB),
        name="prompt_attn",
    )(scal, lamv, qn, kn, vb, bias)


def _tri_masks():
    i = lax.broadcasted_iota(jnp.int32, (CHUNK, CHUNK), 0)
    j = lax.broadcasted_iota(jnp.int32, (CHUNK, CHUNK), 1)
    incl = i >= j
    strict = i > j
    eye = (i == j).astype(F32)
    base = jnp.logical_and(strict, (i // SUBLANES) == (j // SUBLANES))
    levels = []
    s = SUBLANES
    while s < CHUNK:
        levels.append(jnp.logical_and((i // (2 * s)) == (j // (2 * s)), (i // s) > (j // s)))
        s *= 2
    return incl, strict, eye, base, levels


def _gdn_sequence(b, qkv_ref, ba_ref, prm, o_ref, sout_ref, s_ref, masks, nchunk):
    incl, strict, eye, base, levels = masks
    tril = incl.astype(F32)
    nt = (((1,), (1,)), ((), ()))
    y = qkv_ref[b]
    bg = ba_ref[b]
    beta_all = _sigmoid(bg)
    g_all = -jnp.exp(prm[0:1]) * _softplus(bg + prm[1:2])
    gcs = [jnp.dot(tril, g_all[c * CHUNK:(c + 1) * CHUNK], precision=lax.Precision.HIGHEST,
                   preferred_element_type=F32) for c in range(nchunk)]
    yield
    gct = [gc.T for gc in gcs]
    grp = [(c, h) for c in range(nchunk) for h in range(H_A)]
    q_, k_, kb_, vb_, dec_, eg_, ekd_, gl_ = [], [], [], [], [], [], [], []
    for (c, h) in grp:
        rs = slice(c * CHUNK, (c + 1) * CHUNK)
        q = y[rs, h * DK_A:(h + 1) * DK_A]
        k = y[rs, H_A * DK_A + h * DK_A:H_A * DK_A + (h + 1) * DK_A]
        v = y[rs, 2 * H_A * DK_A + h * DV_A:2 * H_A * DK_A + (h + 1) * DV_A]
        beta = beta_all[rs, h:h + 1]
        gcol = gcs[c][:, H_A + h:H_A + h + 1]
        grow = gct[c][H_A + h:H_A + h + 1, :]
        glast = gcs[c][CHUNK - 1:CHUNK, H_A + h:H_A + h + 1]
        dec_.append(jnp.where(incl, jnp.exp(jnp.where(incl, gcol - grow, 0.0)), 0.0))
        eg_.append(jnp.exp(gcol))
        ekd_.append(jnp.exp(glast - gcol))
        gl_.append(jnp.exp(glast))
        q_.append(q)
        k_.append(k)
        kb_.append(k * beta)
        vb_.append(v * beta)
    n = len(grp)
    yield
    kk_qk = [lax.dot_general(jnp.concatenate([kb_[g], q_[g]], axis=0).astype(BF16), k_[g].astype(BF16),
                             nt, preferred_element_type=F32) for g in range(n)]
    yield
    ms = [jnp.where(strict, kk_qk[g][0:CHUNK] * dec_[g], 0.0) for g in range(n)]
    a_ = [(kk_qk[g][CHUNK:2 * CHUNK] * dec_[g]).astype(BF16) for g in range(n)]
    d = [jnp.where(base, m, 0.0) for m in ms]
    yield
    d2 = [_bdot(a, a) for a in d]
    yield
    d4 = [_bdot(a, a) for a in d2]
    x = [_bdot(eye - a, eye + bb) for a, bb in zip(d, d2)]
    yield
    x = [_bdot(a, eye + bb) for a, bb in zip(x, d4)]
    yield
    for lvl in levels:
        cm = [jnp.where(lvl, m, 0.0) for m in ms]
        xc = [_bdot(a, bb) for a, bb in zip(x, cm)]
        yield
        xcx = [_bdot(a, bb) for a, bb in zip(xc, x)]
        yield
        x = [a - bb for a, bb in zip(x, xcx)]
    rhs = [jnp.concatenate([vb_[g], kb_[g] * eg_[g]], axis=1) for g in range(n)]
    yield
    uw = [_bdot(x[g], rhs[g]) for g in range(n)]
    yield
    wq_ = [jnp.concatenate([uw[g][:, DV_A:DV_A + DK_A], q_[g] * eg_[g]], axis=0).astype(BF16)
           for g in range(n)]
    kd_ = [(k_[g] * ekd_[g]).astype(BF16) for g in range(n)]
    s = [s_ref[b, h] for h in range(H_A)]
    yield
    for c in range(nchunk):
        gi = [c * H_A + h for h in range(H_A)]
        ws = [jnp.dot(wq_[gi[h]], s[h].astype(BF16), preferred_element_type=F32) for h in range(H_A)]
        yield
        vn = [(uw[gi[h]][:, 0:DV_A] - ws[h][0:CHUNK]).astype(BF16) for h in range(H_A)]
        yield
        av = [jnp.dot(a_[gi[h]], vn[h], preferred_element_type=F32) for h in range(H_A)]
        kv = [lax.dot_general(kd_[gi[h]], vn[h], (((0,), (0,)), ((), ())), preferred_element_type=F32)
              for h in range(H_A)]
        yield
        for h in range(H_A):
            o_ref[b, c * CHUNK:(c + 1) * CHUNK, h * DV_A:(h + 1) * DV_A] = ws[h][CHUNK:2 * CHUNK] + av[h]
            s[h] = s[h] * gl_[gi[h]] + kv[h]
        yield
    for h in range(H_A):
        s_ref[b, h] = s[h]
        sout_ref[b, h] = s[h]


def _gdn_kernel(qkv_ref, ba_ref, prm_ref, o_ref, sout_ref, s_ref, *, tb, n_seq):
    j = pl.program_id(0)

    @pl.when(j == 0)
    def _():
        s_ref[...] = jnp.zeros(s_ref.shape, F32)

    prm = prm_ref[...]
    masks = _tri_masks()
    chains = [_gdn_sequence(b, qkv_ref, ba_ref, prm, o_ref, sout_ref, s_ref, masks, tb // CHUNK)
              for b in range(n_seq)]
    live = []
    while chains or live:
        if chains:
            live.append(chains.pop(0))
        for ch in list(live):
            if next(ch, StopIteration) is StopIteration:
                live.remove(ch)


def _prompt_gdn(qkv, ba, prm, n_seq, seq, tb):
    nb = seq // tb
    return pl.pallas_call(
        functools.partial(_gdn_kernel, tb=tb, n_seq=n_seq),
        grid=(nb,),
        in_specs=[
            pl.BlockSpec((n_seq, tb, QKV_A), lambda j: (0, j, 0)),
            pl.BlockSpec((n_seq, tb, LANES), lambda j: (0, j, 0)),
            pl.BlockSpec(prm.shape, lambda j: (0, 0)),
        ],
        out_specs=[
            pl.BlockSpec((n_seq, tb, H_A * DV_A), lambda j: (0, j, 0)),
            pl.BlockSpec((n_seq, H_A, DK_A, DV_A), lambda j: (0, 0, 0, 0)),
        ],
        out_shape=[
            jax.ShapeDtypeStruct((n_seq, seq, H_A * DV_A), F32),
            jax.ShapeDtypeStruct((n_seq, H_A, DK_A, DV_A), F32),
        ],
        scratch_shapes=[pltpu.VMEM((n_seq, H_A, DK_A, DV_A), F32)],
        compiler_params=_cparams(1, VMEM_MEDIUM_MIB),
        name="prompt_gdn",
    )(qkv, ba, prm)


def _merge_kernel(x_ref, oa_ref, ob_ref, nw_ref, w2_ref, onw_ref, sbw_ref, wb_ref, wo_ref, out_ref, *, ob_scale):
    x = x_ref[...]
    h = _rms(x, nw_ref[...]).astype(BF16)
    z = jnp.dot(h, w2_ref[:, P2_Z:P2_GA], preferred_element_type=F32)
    ob = ob_ref[...]
    sbw = sbw_ref[...]
    ob_n = [(_rms(ob[:, hd * DV_B:(hd + 1) * DV_B], sbw) * ob_scale).astype(BF16) for hd in range(H_B)]
    ga = jnp.dot(h, w2_ref[:, P2_GA:P2_GB], preferred_element_type=F32)
    oa = oa_ref[...]
    onw = onw_ref[...]
    oa_n = [(_rms(oa[:, hd * DV_A:(hd + 1) * DV_A], onw) * _silu(z[:, hd * DV_A:(hd + 1) * DV_A])).astype(BF16)
            for hd in range(H_A)]
    pb = jnp.dot(jnp.concatenate(ob_n, axis=1), wb_ref[1], preferred_element_type=F32)
    sga = _sigmoid(ga)
    gb = jnp.dot(h, w2_ref[:, P2_GB:P2_END], preferred_element_type=F32)
    pa = jnp.dot(jnp.concatenate(oa_n, axis=1), wb_ref[0], preferred_element_type=F32)
    mixed = (sga * pa + _sigmoid(gb) * pb).astype(BF16)
    out_ref[...] = x + jnp.dot(mixed, wo_ref[...], preferred_element_type=F32)


def _merge(x, oa, ob, nw, w2, onw, sbw, wb, wo, tm, ob_scale):
    r = x.shape[0]
    row = lambda w: pl.BlockSpec((tm, w), lambda i: (i, 0))
    full = lambda a: pl.BlockSpec(a.shape, lambda i: (0,) * a.ndim)
    return pl.pallas_call(
        functools.partial(_merge_kernel, ob_scale=ob_scale),
        grid=(r // tm,),
        in_specs=[row(D_MODEL), row(H_A * DV_A), row(H_B * DV_B), full(nw), full(w2), full(onw), full(sbw),
                  full(wb), full(wo)],
        out_specs=row(D_MODEL),
        out_shape=jax.ShapeDtypeStruct((r, D_MODEL), F32),
        compiler_params=_cparams(1, VMEM_LARGE_MIB),
        name="merge",
    )(x, oa, ob, nw, w2, onw, sbw, wb, wo)


FF_CHUNK = MXU_DIM
UP_AHEAD = 4
DOWN_GROUP = 6


def _ffn_kernel(x_ref, st_ref, nw_ref, wup_ref, cw_ref, cb_ref, wdn_ref, out_ref, stout_ref,
                carry_ref, ext_ref, *, tm, shift, halo):
    i = pl.program_id(1)

    @pl.when(i == 0)
    def _():
        carry_ref[...] = st_ref[0]

    x = x_ref[...]
    h = _rms(x, nw_ref[...]).astype(BF16)
    cw = cw_ref[...]
    cb = cb_ref[...]
    n_chunks = D_FF // FF_CHUNK

    def col(c, part):
        lo = part * D_FF + c * FF_CHUNK
        return slice(lo, lo + FF_CHUNK)

    def up_proj(c):
        return [jnp.dot(h, wup_ref[:, col(c, part)], preferred_element_type=F32) for part in range(2)]

    def gated(c, ups):
        ext = ext_ref.at[c % 2]
        parts = []
        for part in range(2):
            sl = col(c, part)
            es = slice(part * FF_CHUNK, (part + 1) * FF_CHUNK)
            ext[0:halo, es] = carry_ref[:, sl]
            ext[halo:halo + tm, es] = ups[part]
            u = cw[2:3, sl] * ups[part] + cb[:, sl]
            u = u + cw[1:2, sl] * ext[halo - shift:halo - shift + tm, es]
            u = u + cw[0:1, sl] * ext[halo - 2 * shift:halo - 2 * shift + tm, es]
            carry_ref[:, sl] = ext[tm:tm + halo, es]
            parts.append(u)
        return (_silu(parts[0]) * parts[1]).astype(BF16)

    acc = x
    ups, acts = {}, {}
    for s in range(n_chunks + UP_AHEAD + 1):
        if s < n_chunks:
            ups[s] = up_proj(s)
        c = s - UP_AHEAD
        if 0 <= c < n_chunks:
            acts[c] = gated(c, ups.pop(c))
        c = s - UP_AHEAD - 1
        if 0 <= c < n_chunks and (c % DOWN_GROUP == DOWN_GROUP - 1 or c == n_chunks - 1):
            c0 = c - c % DOWN_GROUP
            a = jnp.concatenate([acts.pop(i) for i in range(c0, c + 1)], axis=1)
            acc = acc + jnp.dot(a, wdn_ref[c0 * FF_CHUNK:(c + 1) * FF_CHUNK, :], preferred_element_type=F32)
    out_ref[...] = acc
    stout_ref[0] = carry_ref[...]


def _ffn(x, st, nw, wup, cw, cb, wdn, groups, tm, shift, halo):
    r = x.shape[0]
    tiles = r // groups // tm
    full = lambda a: pl.BlockSpec(a.shape, lambda g, i: (0,) * a.ndim)
    return pl.pallas_call(
        functools.partial(_ffn_kernel, tm=tm, shift=shift, halo=halo),
        grid=(groups, tiles),
        in_specs=[
            pl.BlockSpec((tm, D_MODEL), lambda g, i: (g * tiles + i, 0)),
            pl.BlockSpec((1, halo, 2 * D_FF), lambda g, i: (g, 0, 0)),
            full(nw), full(wup), full(cw), full(cb), full(wdn),
        ],
        out_specs=[
            pl.BlockSpec((tm, D_MODEL), lambda g, i: (g * tiles + i, 0)),
            pl.BlockSpec((1, halo, 2 * D_FF), lambda g, i: (g, 0, 0)),
        ],
        out_shape=[jax.ShapeDtypeStruct((r, D_MODEL), F32),
                   jax.ShapeDtypeStruct((groups, halo, 2 * D_FF), F32)],
        scratch_shapes=[pltpu.VMEM((halo, 2 * D_FF), F32), pltpu.VMEM((2, halo + tm, 2 * FF_CHUNK), F32)],
        compiler_params=_cparams(2, VMEM_LARGE_MIB),
        name="ffn",
    )(x, st, nw, wup, cw, cb, wdn)


def _sgdn_pre_kernel(qkv_ref, ba_ref, prm_ref, wq_ref, u_ref, kd_ref, a_ref, gl_ref, *, t_new, nb):
    prm = prm_ref[...]
    lane = lax.broadcasted_iota(jnp.int32, (nb, LANES), 1)

    ys, betas, gs = [], [], []
    for t in range(t_new):
        ys.append(qkv_ref[t])
        bg = ba_ref[t]
        betas.append(_sigmoid(bg))
        gs.append(-jnp.exp(prm[0:1]) * _softplus(bg + prm[1:2]))
    gcs = [gs[0]]
    for t in range(1, t_new):
        gcs.append(gcs[-1] + gs[t])

    a_out = [jnp.zeros((nb, LANES), F32) for _ in range(t_new)]
    for h in range(H_A):
        q, k, v, beta, gc = [], [], [], [], []
        for t in range(t_new):
            q.append(ys[t][:, h * DK_A:(h + 1) * DK_A])
            k.append(ys[t][:, H_A * DK_A + h * DK_A:H_A * DK_A + (h + 1) * DK_A])
            v.append(ys[t][:, 2 * H_A * DK_A + h * DV_A:2 * H_A * DK_A + (h + 1) * DV_A])
            beta.append(betas[t][:, h:h + 1])
            gc.append(gcs[t][:, H_A + h:H_A + h + 1])
        m = [[None] * t_new for _ in range(t_new)]
        for i in range(t_new):
            for jj in range(i + 1):
                dec = jnp.exp(gc[i] - gc[jj])
                if jj < i:
                    m[i][jj] = beta[i] * jnp.sum(k[i] * k[jj], axis=-1, keepdims=True) * dec
                aij = jnp.sum(q[i] * k[jj], axis=-1, keepdims=True) * dec
                a_out[i] = jnp.where(lane == h * SUBLANES + jj, aij, a_out[i])
        tm_ = [[None] * t_new for _ in range(t_new)]
        for i in range(t_new):
            for jj in range(i):
                acc = m[i][jj]
                for l in range(jj + 1, i):
                    acc = acc + m[i][l] * tm_[l][jj]
                tm_[i][jj] = -acc
        vb = [v[t] * beta[t] for t in range(t_new)]
        kbg = [k[t] * (beta[t] * jnp.exp(gc[t])) for t in range(t_new)]
        hs = slice(h * DK_A, (h + 1) * DK_A)
        for i in range(t_new):
            u = vb[i]
            w = kbg[i]
            for jj in range(i):
                u = u + tm_[i][jj] * vb[jj]
                w = w + tm_[i][jj] * kbg[jj]
            u_ref[i, :, hs] = u
            wq_ref[i, :, hs] = w
            wq_ref[t_new + i, :, hs] = q[i] * jnp.exp(gc[i])
            kd_ref[i, :, hs] = k[i] * jnp.exp(gc[t_new - 1] - gc[i])
        gl_ref[:, hs] = jnp.broadcast_to(jnp.exp(gc[t_new - 1]), (nb, DK_A))
    for i in range(t_new):
        a_ref[i] = a_out[i]


def _sgdn_pre(qkv_t, ba_t, prm, t_new, nb):
    wide = H_A * DK_A
    return pl.pallas_call(
        functools.partial(_sgdn_pre_kernel, t_new=t_new, nb=nb),
        out_shape=[
            jax.ShapeDtypeStruct((2 * t_new, nb, wide), F32),
            jax.ShapeDtypeStruct((t_new, nb, wide), F32),
            jax.ShapeDtypeStruct((t_new, nb, wide), F32),
            jax.ShapeDtypeStruct((t_new, nb, LANES), F32),
            jax.ShapeDtypeStruct((nb, wide), F32),
        ],
        compiler_params=_cparams(0, VMEM_MEDIUM_MIB),
        name="sample_gdn_pre",
    )(qkv_t, ba_t, prm)


SGDN_BB = SUBLANES


def _sgdn_state_kernel(wq_ref, u_ref, kd_ref, a_ref, gl_ref, s0_ref, *refs, t_new):
    o_ref, s1_ref = refs[-2:]
    rows = 2 * t_new
    rid = lax.broadcasted_iota(jnp.int32, (rows, DK_A), 0)
    zpad = jnp.zeros((rows - t_new, DK_A), F32)
    pairs = [(bi, h) for bi in range(SGDN_BB) for h in range(H_A)]
    hs = [slice(h * DK_A, (h + 1) * DK_A) for h in range(H_A)]
    r = {(bi, h): jnp.dot(wq_ref[:, bi, hs[h]].astype(BF16), s0_ref[bi, h].astype(BF16),
                          preferred_element_type=F32) for (bi, h) in pairs}
    v_new = {}
    for (bi, h) in pairs:
        u8 = jnp.concatenate([u_ref[:, bi, hs[h]], zpad], axis=0)
        v_new[bi, h] = jnp.where(rid < t_new, u8 - r[bi, h], 0.0)
    kv = {}
    for (bi, h) in pairs:
        kd8 = jnp.concatenate([kd_ref[:, bi, hs[h]], zpad], axis=0)
        kv[bi, h] = lax.dot_general(kd8.astype(BF16), v_new[bi, h].astype(BF16), (((0,), (0,)), ((), ())),
                                    preferred_element_type=F32)
    for (bi, h) in pairs:
        amat = a_ref[:, bi, :]
        o = r[bi, h][t_new:rows]
        for jj in range(t_new):
            col = amat[:, h * SUBLANES + jj:h * SUBLANES + jj + 1]
            o = o + col * v_new[bi, h][jj:jj + 1, :]
        o_ref[:, bi, hs[h]] = o
        s1_ref[bi, h] = s0_ref[bi, h] * gl_ref[bi:bi + 1, hs[h]] + kv[bi, h]


def _sgdn_state(wq, u, kd, a, gl, s0_all, li, t_new, nb, stacked=None):
    wide = H_A * DK_A
    bb = SGDN_BB
    state_spec = pl.BlockSpec((None, bb, H_A, DK_A, DV_A), lambda i: (li, i, 0, 0, 0))
    in_specs = [
        pl.BlockSpec((2 * t_new, bb, wide), lambda i: (0, i, 0)),
        pl.BlockSpec((t_new, bb, wide), lambda i: (0, i, 0)),
        pl.BlockSpec((t_new, bb, wide), lambda i: (0, i, 0)),
        pl.BlockSpec((t_new, bb, LANES), lambda i: (0, i, 0)),
        pl.BlockSpec((bb, wide), lambda i: (i, 0)),
        state_spec,
    ]
    args = [wq, u, kd, a, gl, s0_all]
    aliases = {}
    if stacked is not None:
        in_specs.append(pl.BlockSpec(memory_space=pl.ANY))
        args.append(stacked)
        aliases = {6: 1}
    return pl.pallas_call(
        functools.partial(_sgdn_state_kernel, t_new=t_new),
        grid=(nb // bb,),
        in_specs=in_specs,
        out_specs=[pl.BlockSpec((t_new, bb, wide), lambda i: (0, i, 0)), state_spec],
        out_shape=[jax.ShapeDtypeStruct((t_new, nb, wide), F32),
                   jax.ShapeDtypeStruct(s0_all.shape, F32)],
        input_output_aliases=aliases,
        compiler_params=_cparams(1, VMEM_SMALL_MIB),
        name="sample_gdn_state",
    )(*args)


NEW_ROWS = 2 * SUBLANES


def _sattn_kernel(pt_ref, lam_ref, q_ref, kn_ref, vn_ref, bias_ref, *rest, n_pages, t_new, lam_init):
    del pt_ref
    k_pages = rest[:n_pages]
    v_pages = rest[n_pages:2 * n_pages]
    o_ref = rest[2 * n_pages]
    rows = 2 * t_new
    q = q_ref[0]
    bias = bias_ref[...]
    lam = _lam_value(lam_ref[...], lam_init)
    r = lax.broadcasted_iota(jnp.int32, (rows, 2 * DH_B), 0)
    c = lax.broadcasted_iota(jnp.int32, (rows, 2 * DH_B), 1)
    map_mask = (c // DH_B) == (r // t_new)
    pad = jnp.zeros((PAGE - NEW_ROWS, LANES), BF16)
    nt = (((1,), (1,)), ((), ()))

    heads = range(H_B)
    lanes = [slice(h * LANES, (h + 1) * LANES) for h in heads]
    head_rows = [pl.ds(h, PAGE, stride=H_B) for h in heads]
    qx = [jnp.where(map_mask, jnp.concatenate([q[:, lanes[h]]] * 2, axis=0), 0.0).astype(BF16) for h in heads]
    bh = [bias[h * rows:(h + 1) * rows] for h in heads]

    groups = [list(range(p, min(p + MXU_DIM // PAGE, n_pages))) for p in range(0, n_pages, MXU_DIM // PAGE)]

    def head_tile(page_refs, grp, h):
        return jnp.concatenate([page_refs[p][0, 0, head_rows[h], :] for p in grp], axis=0).astype(BF16)

    s_parts = [[] for _ in heads]
    for grp in groups:
        for h in heads:
            s = lax.dot_general(qx[h], head_tile(k_pages, grp, h), nt, preferred_element_type=F32)
            if grp[-1] == n_pages - 1:
                zeros = [jnp.zeros((rows, PAGE), F32)] * (len(grp) - 1)
                s = s + jnp.concatenate(zeros + [bh[h][:, 0:PAGE]], axis=1)
            s_parts[h].append(s)
    for h in heads:
        kn = jnp.concatenate([kn_ref[0, :, lanes[h]], pad], axis=0)
        s_parts[h].append(lax.dot_general(qx[h], kn, nt, preferred_element_type=F32) + bh[h][:, PAGE:2 * PAGE])

    m = []
    for h in heads:
        mh = s_parts[h][0].max(axis=-1, keepdims=True)
        for s in s_parts[h][1:]:
            mh = jnp.maximum(mh, s.max(axis=-1, keepdims=True))
        m.append(mh)
    l = [jnp.zeros((rows, 1), F32) for _ in heads]
    acc = [jnp.zeros((rows, DV_B), F32) for _ in heads]
    for gi in range(len(groups) + 1):
        for h in heads:
            pr = jnp.exp2(s_parts[h][gi] - m[h])
            l[h] = l[h] + jnp.sum(pr, axis=-1, keepdims=True)
            if gi < len(groups):
                vv = head_tile(v_pages, groups[gi], h)
            else:
                vv = jnp.concatenate([vn_ref[0, :, lanes[h]], pad], axis=0)
            acc[h] = acc[h] + jnp.dot(pr.astype(BF16), vv, preferred_element_type=F32)
    for h in heads:
        a = acc[h] / l[h]
        o_ref[0, :, lanes[h]] = a[0:t_new] - lam * a[t_new:rows]


def _sample_attn(page_table, lamv, q_b, kn_b, vn_b, bias, ck, cv, li, t_new, lam_init):
    nb, n_pages = page_table.shape
    page_spec = lambda p: pl.BlockSpec((1, 1, PAGE * H_B, LANES), lambda b, pt, p=p: (li, pt[b, p], 0, 0))
    tok_spec = pl.BlockSpec((1, t_new, HB_W), lambda b, pt: (b, 0, 0))
    new_spec = pl.BlockSpec((1, NEW_ROWS, HB_W), lambda b, pt: (b, 0, 0))
    grid_spec = pltpu.PrefetchScalarGridSpec(
        num_scalar_prefetch=1,
        grid=(nb,),
        in_specs=[pl.BlockSpec(lamv.shape, lambda b, pt: (0, 0)), tok_spec, new_spec, new_spec,
                  pl.BlockSpec(bias.shape, lambda b, pt: (0, 0))]
                 + [page_spec(p) for p in range(n_pages)] * 2,
        out_specs=pl.BlockSpec((1, t_new, HB_W), lambda b, pt: (b, 0, 0)),
    )
    return pl.pallas_call(
        functools.partial(_sattn_kernel, n_pages=n_pages, t_new=t_new, lam_init=lam_init),
        grid_spec=grid_spec,
        out_shape=jax.ShapeDtypeStruct((nb, t_new, HB_W), F32),
        compiler_params=_cparams(1, VMEM_SMALL_MIB),
        name="sample_attn",
    )(page_table, lamv, q_b, kn_b, vn_b, bias, *([ck] * n_pages), *([cv] * n_pages))


def _pick(n, pref):
    return pref if n % pref == 0 else n


def _to_bmajor(a_t, t_new, nb):
    return jnp.swapaxes(a_t.reshape(t_new, nb, -1), 0, 1)


def _layer_weights(li, rel_bias, norm_mix, w_in, conv_a, a_log, dt_bias, onorm_a, qnorm_b, knorm_b,
                   lam_q1, lam_k1, lam_q2, lam_k2, subln_b, w_branch, w_o, norm_ffn, w_up, conv_f,
                   conv_f_b, w_down):
    w = w_in[li]
    o_z = QKV_A
    o_b = o_z + H_A * DV_A
    o_qb = o_b + 2 * H_A
    o_kb = o_qb + HB_W
    o_vb = o_kb + HB_W
    o_ga = o_vb + H_B * DV_B
    o_gb = o_ga + D_MODEL
    ba_cols = jnp.pad(w[:, o_b:o_qb], ((0, 0), (0, LANES - 2 * H_A)))
    w1 = jnp.concatenate([w[:, 0:o_z], w[:, o_qb:o_ga], ba_cols], axis=1).astype(BF16)
    w2 = jnp.concatenate([w[:, o_z:o_b], w[:, o_ga:]], axis=1).astype(BF16)
    prm = jnp.zeros((2, LANES), F32)
    prm = prm.at[0, H_A:2 * H_A].set(a_log[li]).at[1, H_A:2 * H_A].set(dt_bias[li])
    k_bound = 1.01 * math.sqrt(DH_B) * jnp.max(jnp.abs(knorm_b[li]))
    q_bound = 1.01 * Q_SCALE * math.sqrt(DH_B) * jnp.max(jnp.abs(qnorm_b[li]))
    rel = (rel_bias - rel_bias[NUM_BUCKETS - 1:NUM_BUCKETS]) * LOG2E
    spread = 2.0 * q_bound * k_bound + jnp.max(jnp.max(rel, axis=0) - jnp.min(rel, axis=0))
    scal = jnp.concatenate([jnp.where(spread <= SAFE_SPREAD_BITS, 1.0, 0.0).reshape(1), k_bound.reshape(1),
                            jnp.max(rel, axis=0), jnp.zeros((2,), F32)]).astype(F32)
    return dict(
        scal=scal,
        nw=norm_mix[li].reshape(1, D_MODEL), w1=w1, w2=w2,
        qw=jnp.tile(qnorm_b[li], 2 * H_B).reshape(1, HB_W),
        kw=jnp.tile(knorm_b[li], 2 * H_B).reshape(1, HB_W),
        cw_a=conv_a[li], prm=prm,
        onw=onorm_a[li].reshape(1, DV_A), sbw=subln_b[li].reshape(1, DV_B),
        lamv=jnp.stack([lam_q1[li], lam_k1[li], lam_q2[li], lam_k2[li]]),
        wb=w_branch[li].astype(BF16), wo=w_o[li].astype(BF16),
        nwf=norm_ffn[li].reshape(1, D_MODEL), wup=w_up[li].astype(BF16),
        cw_f=conv_f[li], cb_f=conv_f_b[li].reshape(1, 2 * D_FF), wdn=w_down[li].astype(BF16),
    )


def _group_ones():
    i = jnp.arange(MXU_DIM)
    return ((i[:, None] // DH_B) == (i[None, :] // DH_B)).astype(BF16)


def _prompt_layer(x, wt, bias_p, g, n_seq, seq, lam_init, li, depth, kv_stacked):
    tm = _pick(seq, ROW_TILE)
    blk = bias_p.shape[1]
    zero_st_a = jnp.zeros((n_seq, SUBLANES, QKV_A), F32)
    qkv, ba, qn, kn, kf, vf, vb, cst_a = _inproj(x, wt["nw"], wt["w1"], wt["qw"], wt["kw"], g, zero_st_a,
                                                 wt["cw_a"], n_seq, tm, 1, SUBLANES,
                                                 layer=li, depth=depth, stacked=kv_stacked)
    o_a, s_fin = _prompt_gdn(qkv.reshape(n_seq, seq, QKV_A), ba.reshape(n_seq, seq, LANES),
                             wt["prm"], n_seq, seq, _pick(seq, GDN_TIME_BLOCK))
    o_a = o_a.reshape(n_seq * seq, H_A * DV_A)
    o_b = _prompt_attn(wt["scal"], wt["lamv"], qn, kn, vb, bias_p, n_seq, seq, blk, lam_init)
    x = _merge(x, o_a, o_b, wt["nw"], wt["w2"], wt["onw"], wt["sbw"], wt["wb"], wt["wo"], tm, 1.0 - lam_init)
    zero_st = jnp.zeros((n_seq, SUBLANES, 2 * D_FF), F32)
    x, cst_f = _ffn(x, zero_st, wt["nwf"], wt["wup"], wt["cw_f"], wt["cb_f"], wt["wdn"],
                    n_seq, tm, 1, SUBLANES)
    return (x, kf, vf, s_fin, cst_a[:, SUBLANES - (CONV_A - 1):], cst_f[:, SUBLANES - (CONV_F - 1):])


def _sample_layer(x_t, wt, bias_s, g, page_table, ck, cv, li, s0_all, s1_stacked, cst_a, cst_f, nb, t_new,
                  lam_init):
    r = nb * t_new
    halo_a = (CONV_A - 1) * nb
    st_a_t = jnp.swapaxes(cst_a, 0, 1).reshape(1, halo_a, QKV_A)
    qkv, ba, qn, kn, kf, vf, vb, cst_a_t = _inproj(x_t, wt["nw"], wt["w1"], wt["qw"], wt["kw"], g, st_a_t,
                                                   wt["cw_a"], 1, r, nb, halo_a)
    cst_a_t = cst_a_t.reshape(CONV_A - 1, nb, QKV_A)
    wq, u, kd, a, gl = _sgdn_pre(qkv.reshape(t_new, nb, QKV_A), ba.reshape(t_new, nb, LANES), wt["prm"],
                                 t_new, nb)
    o_a_t, s1 = _sgdn_state(wq, u, kd, a, gl, s0_all, li, t_new, nb, stacked=s1_stacked)
    pad_new = lambda a: jnp.pad(_to_bmajor(a, t_new, nb), ((0, 0), (0, NEW_ROWS - t_new), (0, 0)))
    o_b = _sample_attn(page_table, wt["lamv"], _to_bmajor(qn, t_new, nb).astype(F32), pad_new(kn),
                       pad_new(vb), bias_s, ck, cv, li, t_new, lam_init)
    o_b_t = jnp.swapaxes(o_b, 0, 1).reshape(r, HB_W)
    x_t = _merge(x_t, o_a_t.reshape(r, H_A * DV_A), o_b_t, wt["nw"], wt["w2"], wt["onw"], wt["sbw"],
                 wt["wb"], wt["wo"], r, 1.0 - lam_init)
    halo = (CONV_F - 1) * nb
    st_f_t = jnp.swapaxes(cst_f, 0, 1).reshape(1, halo, 2 * D_FF)
    x_t, cst_f_t = _ffn(x_t, st_f_t, wt["nwf"], wt["wup"], wt["cw_f"], wt["cb_f"], wt["wdn"],
                        1, r, nb, halo)
    return (x_t, _to_bmajor(kf, t_new, nb).reshape(nb, t_new, H_B, 2 * DH_B),
            _to_bmajor(vf, t_new, nb).reshape(nb, t_new, H_B, DV_B), s1,
            jnp.swapaxes(cst_a_t, 0, 1), jnp.swapaxes(cst_f_t.reshape(CONV_F - 1, nb, 2 * D_FF), 0, 1))


def kernel(x_prompt, x_sample, cache_k, cache_v, state_delta, state_conv_a, state_conv_ffn, page_table,
           rel_bias, norm_mix, w_in, conv_a, a_log, dt_bias, onorm_a, qnorm_b, knorm_b, lam_q1, lam_k1,
           lam_q2, lam_k2, subln_b, w_branch, w_o, norm_ffn, w_up, conv_f, conv_f_b, w_down):
    n_seq, seq, _ = x_prompt.shape
    nb, t_new, _ = x_sample.shape
    depth = w_in.shape[0]
    n_pool = cache_k.shape[1]

    rb_flat = rel_bias.T.reshape(-1)
    bias_p = _prompt_bias(rb_flat, _pick(seq, ATTN_BLOCK))
    bias_s = _sample_bias(rb_flat, t_new)
    g = _group_ones()
    ck = cache_k.reshape(depth, n_pool, PAGE * H_B, 2 * DH_B)
    cv = cache_v.reshape(depth, n_pool, PAGE * H_B, DV_B)

    xp = x_prompt.reshape(n_seq * seq, D_MODEL)
    xs = jnp.swapaxes(x_sample, 0, 1).reshape(t_new * nb, D_MODEL)
    outs_p, outs_s = [], []
    kv_stacked, s1_stacked = None, None
    for li in range(depth):
        wt = _layer_weights(li, rel_bias, norm_mix, w_in, conv_a, a_log, dt_bias, onorm_a, qnorm_b, knorm_b,
                            lam_q1, lam_k1, lam_q2, lam_k2, subln_b, w_branch, w_o, norm_ffn, w_up,
                            conv_f, conv_f_b, w_down)
        lam_init = 0.8 - 0.6 * math.exp(-0.3 * li)
        xp, kf_all, vf_all, *rest_p = _prompt_layer(xp, wt, bias_p, g, n_seq, seq, lam_init, li, depth,
                                                    kv_stacked)
        kv_stacked = (kf_all, vf_all)
        xs, ks, vs, s1_stacked, *rest_s = _sample_layer(xs, wt, bias_s, g, page_table, ck, cv, li, state_delta,
                                                        s1_stacked, state_conv_a[li], state_conv_ffn[li], nb,
                                                        t_new, lam_init)
        outs_p.append(rest_p)
        outs_s.append([ks, vs] + rest_s)

    stack = lambda outs, i: jnp.stack([o[i] for o in outs])
    y_prompt = xp.reshape(n_seq, seq, D_MODEL)
    y_sample = jnp.swapaxes(xs.reshape(t_new, nb, D_MODEL), 0, 1)
    k_prompt = kv_stacked[0].reshape(depth, n_seq, seq, H_B, 2 * DH_B)
    v_prompt = kv_stacked[1].reshape(depth, n_seq, seq, H_B, DV_B)
    return (y_prompt, y_sample,
            k_prompt, v_prompt, stack(outs_p, 0), stack(outs_p, 1), stack(outs_p, 2),
            stack(outs_s, 0), stack(outs_s, 1), s1_stacked, stack(outs_s, 2), stack(outs_s, 3))
```

```python
import functools
import math

import jax
import jax.numpy as jnp
from jax import lax
from jax.experimental import pallas as pl
from jax.experimental.pallas import tpu as pltpu

F32 = jnp.float32
BF16 = jnp.bfloat16

D_MODEL = 1024
H_A, DK_A, DV_A, CONV_A, CHUNK = 4, 128, 128, 4, 64
QKV_A = 2 * H_A * DK_A + H_A * DV_A
H_B, DH_B, DV_B = 4, 64, 128
HB_W = H_B * 2 * DH_B
PAGE = 128
D_FF, CONV_F = 2816, 3
NUM_BUCKETS, MAX_DISTANCE = 32, 128
NEG_INF = -1e30
EPS = 1e-6
ATT_SCALE = DH_B ** -0.5
LOG2E = math.log2(math.e)
Q_SCALE = ATT_SCALE * LOG2E
SAFE_SPREAD_BITS = 100.0

MIB = 1024 * 1024
V7X_VMEM_BYTES = 64 * MIB
LANES = 128
SUBLANES = 8
MXU_DIM = 256

ROW_TILE = 512
ATTN_BLOCK = 512
GDN_TIME_BLOCK = 8 * CHUNK
VMEM_LARGE_MIB = 56
VMEM_MEDIUM_MIB = 48
VMEM_SMALL_MIB = 40
assert VMEM_LARGE_MIB * MIB < V7X_VMEM_BYTES

P1_QKV = 0
P1_Q = P1_QKV + QKV_A
P1_K = P1_Q + HB_W
P1_V = P1_K + HB_W
P1_BA = P1_V + H_B * DV_B
P1_END = P1_BA + LANES
P2_Z = 0
P2_GA = P2_Z + H_A * DV_A
P2_GB = P2_GA + D_MODEL
P2_END = P2_GB + D_MODEL


def _cparams(n_axes, vmem_mib):
    return pltpu.CompilerParams(
        dimension_semantics=("arbitrary",) * n_axes,
        vmem_limit_bytes=vmem_mib * MIB,
    )


def _bdot(a, b):
    return jnp.dot(a.astype(BF16), b.astype(BF16), preferred_element_type=F32)


def _rms(x, w):
    return x * lax.rsqrt(jnp.mean(x * x, axis=-1, keepdims=True) + EPS) * w


def _sigmoid(x):
    return 1.0 / (1.0 + jnp.exp(-x))


def _silu(x):
    return x * _sigmoid(x)


def _softplus(x):
    return jnp.maximum(x, 0.0) + jnp.log1p(jnp.exp(-jnp.abs(x)))


def _inproj_kernel(x_ref, nw_ref, w_ref, qw_ref, kw_ref, g_ref, st_ref, cw_ref, *refs, shift, halo):
    n_out = 8
    qkv_ref, ba_ref, qn_ref, kn_ref, kf_ref, vf_ref, vb_ref, stout_ref = refs[-n_out - 2:-2]
    carry_ref, ext_ref = refs[-2:]
    h = _rms(x_ref[...], nw_ref[...]).astype(BF16)

    def proj(lo, hi):
        return jnp.dot(h, w_ref[:, lo:hi], preferred_element_type=F32)

    tm = x_ref.shape[0]
    head_rows = [pl.ds(hd, tm, stride=H_B) for hd in range(H_B)]

    @pl.when(pl.program_id(1) == 0)
    def _():
        carry_ref[...] = st_ref[0]

    cw = cw_ref[...]
    g = g_ref[...]
    seg = H_A * DK_A

    def delta_front(part, raw):
        cols = slice(part * seg, (part + 1) * seg)
        ext_ref[0:halo, cols] = carry_ref[:, cols]
        ext_ref[halo:halo + tm, cols] = raw
        y = cw[CONV_A - 1:CONV_A, cols] * raw
        for i in range(CONV_A - 1):
            back = (CONV_A - 1 - i) * shift
            y = y + cw[i:i + 1, cols] * ext_ref[halo - back:halo - back + tm, cols]
        new_carry = ext_ref[tm:tm + halo, cols]
        carry_ref[:, cols] = new_carry
        stout_ref[0, :, cols] = new_carry
        y = _silu(y)
        if part == 2:
            qkv_ref[:, cols] = y
            return
        for hd in range(H_A):
            sl = slice(hd * DK_A, (hd + 1) * DK_A)
            t = y[:, sl]
            t = t * lax.rsqrt(jnp.sum(t * t, axis=-1, keepdims=True) + EPS)
            qkv_ref[:, part * seg + hd * DK_A:part * seg + (hd + 1) * DK_A] = t * (DK_A ** -0.5) if part == 0 else t

    def qk_norm(y, w):
        sq = y * y
        hi = sq.astype(BF16)
        lo = (sq - hi.astype(F32)).astype(BF16)
        outs = []
        for c in range(HB_W // MXU_DIM):
            sl = slice(c * MXU_DIM, (c + 1) * MXU_DIM)
            ss = (jnp.dot(hi[:, sl], g, preferred_element_type=F32)
                  + jnp.dot(lo[:, sl], g, preferred_element_type=F32))
            outs.append(y[:, sl] * lax.rsqrt(ss * (1.0 / DH_B) + EPS) * w[:, sl])
        return outs

    def attn_q(y):
        qn = qk_norm(y, qw_ref[...])
        for c in range(HB_W // MXU_DIM):
            qn_ref[:, c * MXU_DIM:(c + 1) * MXU_DIM] = (qn[c] * Q_SCALE).astype(BF16)

    def attn_k(y):
        kn = qk_norm(y, kw_ref[...])
        for c in range(HB_W // MXU_DIM):
            kn_ref[:, c * MXU_DIM:(c + 1) * MXU_DIM] = kn[c].astype(BF16)
            for i in range(MXU_DIM // LANES):
                kf_ref[head_rows[c * (MXU_DIM // LANES) + i], :] = kn[c][:, i * LANES:(i + 1) * LANES]

    def attn_v(v):
        for hd in range(H_B):
            vf_ref[head_rows[hd], :] = v[:, hd * DV_B:(hd + 1) * DV_B]
        vb_ref[...] = v.astype(BF16)

    def store_ba(y):
        ba_ref[...] = y

    stages = [
        ((P1_QKV, P1_QKV + seg), functools.partial(delta_front, 0)),
        ((P1_QKV + seg, P1_QKV + 2 * seg), functools.partial(delta_front, 1)),
        ((P1_QKV + 2 * seg, P1_Q), functools.partial(delta_front, 2)),
        ((P1_Q, P1_K), attn_q),
        ((P1_K, P1_V), attn_k),
        ((P1_V, P1_BA), attn_v),
        ((P1_BA, P1_END), store_ba),
    ]
    pending = None
    for cols, post in stages:
        y = proj(*cols)
        if pending is not None:
            pending[1](pending[0])
        pending = (y, post)
    pending[1](pending[0])


def _inproj(x, nw, w1, qw, kw, g, st, cw, groups, tm, shift, halo, layer=None, depth=None, stacked=None):
    r = x.shape[0]
    tiles = r // groups // tm
    row = lambda w: pl.BlockSpec((tm, w), lambda gi, i: (gi * tiles + i, 0))
    full = lambda a: pl.BlockSpec(a.shape, lambda gi, i: (0,) * a.ndim)
    st_spec = pl.BlockSpec((1, halo, QKV_A), lambda gi, i: (gi, 0, 0))
    outs = [(1, QKV_A, F32), (1, LANES, F32), (1, HB_W, BF16), (1, HB_W, BF16), (H_B, LANES, F32),
            (H_B, LANES, F32), (1, HB_W, BF16)]
    out_specs = [pl.BlockSpec((tm * m, w), lambda gi, i: (gi * tiles + i, 0)) for m, w, _ in outs]
    out_shape = [jax.ShapeDtypeStruct((r * m, w), dt) for m, w, dt in outs]
    out_specs.append(st_spec)
    out_shape.append(jax.ShapeDtypeStruct((groups, halo, QKV_A), F32))
    in_specs = [row(D_MODEL), full(nw), full(w1), full(qw), full(kw), full(g), st_spec, full(cw)]
    args = [x, nw, w1, qw, kw, g, st, cw]
    aliases = {}
    if layer is not None:
        for o in (4, 5):
            m, w, dt = outs[o]
            out_specs[o] = pl.BlockSpec((None, tm * m, w), lambda gi, i: (layer, gi * tiles + i, 0))
            out_shape[o] = jax.ShapeDtypeStruct((depth, r * m, w), dt)
        if stacked is not None:
            in_specs += [pl.BlockSpec(memory_space=pl.ANY)] * 2
            args += list(stacked)
            aliases = {len(args) - 2: 4, len(args) - 1: 5}
    return pl.pallas_call(
        functools.partial(_inproj_kernel, shift=shift, halo=halo),
        grid=(groups, tiles),
        in_specs=in_specs,
        out_specs=out_specs,
        out_shape=out_shape,
        input_output_aliases=aliases,
        scratch_shapes=[pltpu.VMEM((halo, QKV_A), F32), pltpu.VMEM((halo + tm, QKV_A), F32)],
        compiler_params=_cparams(2, VMEM_LARGE_MIB),
        name="inproj",
    )(*args)


def _bucket_bias(d, table):
    n = jnp.maximum(d, 0)
    max_exact = NUM_BUCKETS // 2
    nf = jnp.maximum(n, 1).astype(F32)
    large = max_exact + (jnp.log(nf / max_exact) / math.log(MAX_DISTANCE / max_exact)
                         * (NUM_BUCKETS - max_exact)).astype(jnp.int32)
    large = jnp.minimum(large, NUM_BUCKETS - 1)
    bucket = jnp.where(n < max_exact, n, large)
    val = jnp.zeros(d.shape, F32)
    for b in range(NUM_BUCKETS):
        val = jnp.where(bucket == b, table(b), val)
    return val


def _prompt_bias_kernel(rb_ref, o_ref, *, blk):
    h = pl.program_id(0)
    table = lambda b: rb_ref[h * NUM_BUCKETS + b]
    far = table(NUM_BUCKETS - 1)
    i = lax.broadcasted_iota(jnp.int32, (blk, blk), 0)
    j = lax.broadcasted_iota(jnp.int32, (blk, blk), 1)
    d0 = i - j
    o_ref[0, :, 0:blk] = (_bucket_bias(d0 + blk, table) - far) * LOG2E
    o_ref[0, :, blk:2 * blk] = jnp.where(d0 >= 0, (_bucket_bias(d0, table) - far) * LOG2E, NEG_INF)


def _prompt_bias(rb_flat, blk):
    return pl.pallas_call(
        functools.partial(_prompt_bias_kernel, blk=blk),
        grid=(H_B,),
        in_specs=[pl.BlockSpec(memory_space=pltpu.SMEM)],
        out_specs=pl.BlockSpec((1, blk, 2 * blk), lambda h: (h, 0, 0)),
        out_shape=jax.ShapeDtypeStruct((H_B, blk, 2 * blk), F32),
        compiler_params=_cparams(1, VMEM_SMALL_MIB),
        name="prompt_bias",
    )(rb_flat)


def _sample_bias_kernel(rb_ref, o_ref, *, t_new):
    rows, cols = o_ref.shape
    r = lax.broadcasted_iota(jnp.int32, (rows, cols), 0)
    c = lax.broadcasted_iota(jnp.int32, (rows, cols), 1)
    t = r % t_new
    hd = r // (2 * t_new)
    is_new = c >= PAGE
    d = jnp.where(is_new, t - (c - PAGE), t + PAGE - c)
    val = jnp.zeros((rows, cols), F32)
    for h in range(H_B):
        table = lambda b, h=h: rb_ref[h * NUM_BUCKETS + b]
        vh = _bucket_bias(d, table) - table(NUM_BUCKETS - 1)
        val = jnp.where(hd == h, vh, val)
    visible = jnp.logical_and(d >= 0, jnp.logical_or(~is_new, (c - PAGE) < t_new))
    o_ref[...] = jnp.where(visible, val * LOG2E, NEG_INF)


def _sample_bias(rb_flat, t_new):
    rows = H_B * 2 * t_new
    return pl.pallas_call(
        functools.partial(_sample_bias_kernel, t_new=t_new),
        in_specs=[pl.BlockSpec(memory_space=pltpu.SMEM)],
        out_shape=jax.ShapeDtypeStruct((rows, 2 * PAGE), F32),
        name="sample_bias",
    )(rb_flat)


def _lam_value(lv, lam_init):
    s1 = jnp.sum(lv[0:1] * lv[1:2], axis=-1, keepdims=True)
    s2 = jnp.sum(lv[2:3] * lv[3:4], axis=-1, keepdims=True)
    return jnp.exp(s1) - jnp.exp(s2) + lam_init


FAR_BLOCKS = 4


def _causal_sweep(qi, blk, bias_ref, update):
    n_far = jnp.maximum(qi - 1, 0)
    span = FAR_BLOCKS * blk

    def far(kk, carry):
        update(pl.multiple_of(kk * span, span), FAR_BLOCKS, None)
        return carry

    n_trips = n_far // FAR_BLOCKS
    lax.fori_loop(0, n_trips, far, 0)
    done = n_trips * FAR_BLOCKS
    rem = n_far - done
    for r in range(FAR_BLOCKS):
        @pl.when(jnp.logical_and(qi >= 1, rem == r))
        def _(r=r):
            update(pl.multiple_of(done * blk, blk), r + 2, bias_ref[0])

    @pl.when(qi == 0)
    def _():
        update(0, 1, bias_ref[0, :, blk:2 * blk])


def _attn_kernel(scal_ref, lam_ref, q_ref, k_ref, v_ref, bias_ref, o_ref, m_ref, l_ref, acc_ref, accx_ref,
                 *, blk, nq, lam_init):
    hd = pl.program_id(1)
    lane = lax.broadcasted_iota(jnp.int32, (1, 2 * DH_B), 1)
    lam = _lam_value(lam_ref[...], lam_init)
    nt = (((1,), (1,)), ((), ()))

    def q_maps(qi):
        q = q_ref[pl.ds(pl.multiple_of(qi * blk, blk), blk), :]
        zero = jnp.zeros_like(q)
        return jnp.where(lane < DH_B, q, zero), jnp.where(lane >= DH_B, q, zero)

    def store_o(qi, val):
        o_ref[pl.ds(pl.multiple_of(qi * blk, blk), blk), :] = val

    @pl.when(scal_ref[0] > 0.5)
    def _():
        one_col = jnp.where(lane == 0, 1.0, 0.0).astype(BF16)
        ext_k = {n: jnp.broadcast_to(one_col, (n * blk, 2 * DH_B)) for n in range(1, FAR_BLOCKS + 2)}

        def q_block(qi, carry):
            qs = q_maps(qi)
            qe = []
            for mp in range(2):
                qf = qs[mp].astype(F32)
                shift = jnp.sqrt(jnp.sum(qf * qf, axis=-1, keepdims=True)) * scal_ref[1] + scal_ref[2 + hd]
                qe.append(jnp.concatenate([qs[mp], jnp.where(lane == 0, -shift, 0.0).astype(BF16)], axis=1))
            qe = jnp.concatenate(qe, axis=0)
            accx_ref[...] = jnp.zeros(accx_ref.shape, F32)

            def update(off, nblk, bias):
                rows = nblk * blk
                kb = jnp.concatenate([k_ref[pl.ds(off, rows), :], ext_k[nblk]], axis=1)
                vb = jnp.concatenate([v_ref[pl.ds(off, rows), :], ext_k[nblk]], axis=1)
                s = lax.dot_general(qe, kb, nt, preferred_element_type=F32)
                if bias is not None:
                    plain = rows - bias.shape[1]
                    tail = s[:, plain:] + jnp.concatenate([bias, bias], axis=0)
                    s = tail if plain == 0 else jnp.concatenate([s[:, :plain], tail], axis=1)
                accx_ref[...] += jnp.dot(jnp.exp2(s).astype(BF16), vb, preferred_element_type=F32)

            _causal_sweep(qi, blk, bias_ref, update)
            a1 = accx_ref[0:blk]
            a2 = accx_ref[blk:2 * blk]
            store_o(qi, a1[:, 0:DV_B] / a1[:, DV_B:DV_B + 1] - lam * (a2[:, 0:DV_B] / a2[:, DV_B:DV_B + 1]))
            return carry

        lax.fori_loop(0, nq, q_block, 0)

    @pl.when(scal_ref[0] <= 0.5)
    def _():
        def q_block(qi, carry):
            qs = q_maps(qi)
            m_ref[...] = jnp.full(m_ref.shape, NEG_INF, F32)
            l_ref[...] = jnp.zeros(l_ref.shape, F32)
            acc_ref[...] = jnp.zeros(acc_ref.shape, F32)

            def update(off, nblk, bias):
                plain = nblk if bias is None else nblk - bias.shape[1] // blk
                for i in range(nblk):
                    update_one(pl.multiple_of(off + i * blk, blk),
                               None if i < plain else bias[:, (i - plain) * blk:(i - plain + 1) * blk])

            def update_one(off, bias):
                kb = k_ref[pl.ds(off, blk), :]
                vb = v_ref[pl.ds(off, blk), :]
                for mp in range(2):
                    s = lax.dot_general(qs[mp], kb, nt, preferred_element_type=F32)
                    if bias is not None:
                        s = s + bias
                    m_old = m_ref[mp]
                    m_new = jnp.maximum(m_old, jnp.max(s, axis=-1, keepdims=True))
                    alpha = jnp.exp2(m_old - m_new)
                    p = jnp.exp2(s - jnp.concatenate([m_new] * (blk // LANES), axis=1))
                    l_ref[mp] = alpha * l_ref[mp] + jnp.sum(p, axis=-1, keepdims=True)
                    acc_ref[mp] = alpha * acc_ref[mp] + jnp.dot(p.astype(BF16), vb, preferred_element_type=F32)
                    m_ref[mp] = m_new

            _causal_sweep(qi, blk, bias_ref, update)
            store_o(qi, acc_ref[0] / l_ref[0] - lam * (acc_ref[1] / l_ref[1]))
            return carry

        lax.fori_loop(0, nq, q_block, 0)


def _prompt_attn(scal, lamv, qn, kn, vb, bias, n_seq, seq, blk, lam_init):
    nq = seq // blk
    r = n_seq * seq
    return pl.pallas_call(
        functools.partial(_attn_kernel, blk=blk, nq=nq, lam_init=lam_init),
        grid=(n_seq, H_B),
        in_specs=[
            pl.BlockSpec(memory_space=pltpu.SMEM),
            pl.BlockSpec(lamv.shape, lambda b, h: (0, 0)),
            pl.BlockSpec((seq, LANES), lambda b, h: (b, h)),
            pl.BlockSpec((seq, LANES), lambda b, h: (b, h)),
            pl.BlockSpec((seq, LANES), lambda b, h: (b, h)),
            pl.BlockSpec((1, blk, 2 * blk), lambda b, h: (h, 0, 0)),
        ],
        out_specs=pl.BlockSpec((seq, LANES), lambda b, h: (b, h)),
        out_shape=jax.ShapeDtypeStruct((r, HB_W), F32),
        scratch_shapes=[pltpu.VMEM((2, blk, LANES), F32), pltpu.VMEM((2, blk, LANES), F32),
                        pltpu.VMEM((2, blk, DV_B), F32), pltpu.VMEM((2 * blk, 2 * DV_B), F32)],
        compiler_params=_cparams(2, VMEM_LARGE_MIB),
        name="prompt_attn",
    )(scal, lamv, qn, kn, vb, bias)


def _tri_masks():
    i = lax.broadcasted_iota(jnp.int32, (CHUNK, CHUNK), 0)
    j = lax.broadcasted_iota(jnp.int32, (CHUNK, CHUNK), 1)
    incl = i >= j
    strict = i > j
    eye = (i == j).astype(F32)
    base = jnp.logical_and(strict, (i // SUBLANES) == (j // SUBLANES))
    levels = []
    s = SUBLANES
    while s < CHUNK:
        levels.append(jnp.logical_and((i // (2 * s)) == (j // (2 * s)), (i // s) > (j // s)))
        s *= 2
    return incl, strict, eye, base, levels


def _gdn_sequence(b, qkv_ref, ba_ref, prm, o_ref, sout_ref, s_ref, masks, nchunk):
    incl, strict, eye, base, levels = masks
    tril = incl.astype(F32)
    nt = (((1,), (1,)), ((), ()))
    y = qkv_ref[b]
    bg = ba_ref[b]
    beta_all = _sigmoid(bg)
    g_all = -jnp.exp(prm[0:1]) * _softplus(bg + prm[1:2])
    gcs = [jnp.dot(tril, g_all[c * CHUNK:(c + 1) * CHUNK], precision=lax.Precision.HIGHEST,
                   preferred_element_type=F32) for c in range(nchunk)]
    yield
    gct = [gc.T for gc in gcs]
    grp = [(c, h) for c in range(nchunk) for h in range(H_A)]
    q_, k_, kb_, vb_, dec_, eg_, ekd_, gl_ = [], [], [], [], [], [], [], []
    for (c, h) in grp:
        rs = slice(c * CHUNK, (c + 1) * CHUNK)
        q = y[rs, h * DK_A:(h + 1) * DK_A]
        k = y[rs, H_A * DK_A + h * DK_A:H_A * DK_A + (h + 1) * DK_A]
        v = y[rs, 2 * H_A * DK_A + h * DV_A:2 * H_A * DK_A + (h + 1) * DV_A]
        beta = beta_all[rs, h:h + 1]
        gcol = gcs[c][:, H_A + h:H_A + h + 1]
        grow = gct[c][H_A + h:H_A + h + 1, :]
        glast = gcs[c][CHUNK - 1:CHUNK, H_A + h:H_A + h + 1]
        dec_.append(jnp.where(incl, jnp.exp(jnp.where(incl, gcol - grow, 0.0)), 0.0))
        eg_.append(jnp.exp(gcol))
        ekd_.append(jnp.exp(glast - gcol))
        gl_.append(jnp.exp(glast))
        q_.append(q)
        k_.append(k)
        kb_.append(k * beta)
        vb_.append(v * beta)
    n = len(grp)
    yield
    kk_qk = [lax.dot_general(jnp.concatenate([kb_[g], q_[g]], axis=0).astype(BF16), k_[g].astype(BF16),
                             nt, preferred_element_type=F32) for g in range(n)]
    yield
    ms = [jnp.where(strict, kk_qk[g][0:CHUNK] * dec_[g], 0.0) for g in range(n)]
    a_ = [(kk_qk[g][CHUNK:2 * CHUNK] * dec_[g]).astype(BF16) for g in range(n)]
    d = [jnp.where(base, m, 0.0) for m in ms]
    yield
    d2 = [_bdot(a, a) for a in d]
    yield
    d4 = [_bdot(a, a) for a in d2]
    x = [_bdot(eye - a, eye + bb) for a, bb in zip(d, d2)]
    yield
    x = [_bdot(a, eye + bb) for a, bb in zip(x, d4)]
    yield
    for lvl in levels:
        cm = [jnp.where(lvl, m, 0.0) for m in ms]
        xc = [_bdot(a, bb) for a, bb in zip(x, cm)]
        yield
        xcx = [_bdot(a, bb) for a, bb in zip(xc, x)]
        yield
        x = [a - bb for a, bb in zip(x, xcx)]
    rhs = [jnp.concatenate([vb_[g], kb_[g] * eg_[g]], axis=1) for g in range(n)]
    yield
    uw = [_bdot(x[g], rhs[g]) for g in range(n)]
    yield
    wq_ = [jnp.concatenate([uw[g][:, DV_A:DV_A + DK_A], q_[g] * eg_[g]], axis=0).astype(BF16)
           for g in range(n)]
    kd_ = [(k_[g] * ekd_[g]).astype(BF16) for g in range(n)]
    s = [s_ref[b, h] for h in range(H_A)]
    yield
    for c in range(nchunk):
        gi = [c * H_A + h for h in range(H_A)]
        ws = [jnp.dot(wq_[gi[h]], s[h].astype(BF16), preferred_element_type=F32) for h in range(H_A)]
        yield
        vn = [(uw[gi[h]][:, 0:DV_A] - ws[h][0:CHUNK]).astype(BF16) for h in range(H_A)]
        yield
        av = [jnp.dot(a_[gi[h]], vn[h], preferred_element_type=F32) for h in range(H_A)]
        kv = [lax.dot_general(kd_[gi[h]], vn[h], (((0,), (0,)), ((), ())), preferred_element_type=F32)
              for h in range(H_A)]
        yield
        for h in range(H_A):
            o_ref[b, c * CHUNK:(c + 1) * CHUNK, h * DV_A:(h + 1) * DV_A] = ws[h][CHUNK:2 * CHUNK] + av[h]
            s[h] = s[h] * gl_[gi[h]] + kv[h]
        yield
    for h in range(H_A):
        s_ref[b, h] = s[h]
        sout_ref[b, h] = s[h]


def _gdn_kernel(qkv_ref, ba_ref, prm_ref, o_ref, sout_ref, s_ref, *, tb, n_seq):
    j = pl.program_id(0)

    @pl.when(j == 0)
    def _():
        s_ref[...] = jnp.zeros(s_ref.shape, F32)

    prm = prm_ref[...]
    masks = _tri_masks()
    chains = [_gdn_sequence(b, qkv_ref, ba_ref, prm, o_ref, sout_ref, s_ref, masks, tb // CHUNK)
              for b in range(n_seq)]
    live = []
    while chains or live:
        if chains:
            live.append(chains.pop(0))
        for ch in list(live):
            if next(ch, StopIteration) is StopIteration:
                live.remove(ch)


def _prompt_gdn(qkv, ba, prm, n_seq, seq, tb):
    nb = seq // tb
    return pl.pallas_call(
        functools.partial(_gdn_kernel, tb=tb, n_seq=n_seq),
        grid=(nb,),
        in_specs=[
            pl.BlockSpec((n_seq, tb, QKV_A), lambda j: (0, j, 0)),
            pl.BlockSpec((n_seq, tb, LANES), lambda j: (0, j, 0)),
            pl.BlockSpec(prm.shape, lambda j: (0, 0)),
        ],
        out_specs=[
            pl.BlockSpec((n_seq, tb, H_A * DV_A), lambda j: (0, j, 0)),
            pl.BlockSpec((n_seq, H_A, DK_A, DV_A), lambda j: (0, 0, 0, 0)),
        ],
        out_shape=[
            jax.ShapeDtypeStruct((n_seq, seq, H_A * DV_A), F32),
            jax.ShapeDtypeStruct((n_seq, H_A, DK_A, DV_A), F32),
        ],
        scratch_shapes=[pltpu.VMEM((n_seq, H_A, DK_A, DV_A), F32)],
        compiler_params=_cparams(1, VMEM_MEDIUM_MIB),
        name="prompt_gdn",
    )(qkv, ba, prm)


def _merge_kernel(x_ref, oa_ref, ob_ref, nw_ref, w2_ref, onw_ref, sbw_ref, wb_ref, wo_ref, out_ref, *, ob_scale):
    x = x_ref[...]
    h = _rms(x, nw_ref[...]).astype(BF16)
    z = jnp.dot(h, w2_ref[:, P2_Z:P2_GA], preferred_element_type=F32)
    ob = ob_ref[...]
    sbw = sbw_ref[...]
    ob_n = [(_rms(ob[:, hd * DV_B:(hd + 1) * DV_B], sbw) * ob_scale).astype(BF16) for hd in range(H_B)]
    ga = jnp.dot(h, w2_ref[:, P2_GA:P2_GB], preferred_element_type=F32)
    oa = oa_ref[...]
    onw = onw_ref[...]
    oa_n = [(_rms(oa[:, hd * DV_A:(hd + 1) * DV_A], onw) * _silu(z[:, hd * DV_A:(hd + 1) * DV_A])).astype(BF16)
            for hd in range(H_A)]
    pb = jnp.dot(jnp.concatenate(ob_n, axis=1), wb_ref[1], preferred_element_type=F32)
    sga = _sigmoid(ga)
    gb = jnp.dot(h, w2_ref[:, P2_GB:P2_END], preferred_element_type=F32)
    pa = jnp.dot(jnp.concatenate(oa_n, axis=1), wb_ref[0], preferred_element_type=F32)
    mixed = (sga * pa + _sigmoid(gb) * pb).astype(BF16)
    out_ref[...] = x + jnp.dot(mixed, wo_ref[...], preferred_element_type=F32)


def _merge(x, oa, ob, nw, w2, onw, sbw, wb, wo, tm, ob_scale):
    r = x.shape[0]
    row = lambda w: pl.BlockSpec((tm, w), lambda i: (i, 0))
    full = lambda a: pl.BlockSpec(a.shape, lambda i: (0,) * a.ndim)
    return pl.pallas_call(
        functools.partial(_merge_kernel, ob_scale=ob_scale),
        grid=(r // tm,),
        in_specs=[row(D_MODEL), row(H_A * DV_A), row(H_B * DV_B), full(nw), full(w2), full(onw), full(sbw),
                  full(wb), full(wo)],
        out_specs=row(D_MODEL),
        out_shape=jax.ShapeDtypeStruct((r, D_MODEL), F32),
        compiler_params=_cparams(1, VMEM_LARGE_MIB),
        name="merge",
    )(x, oa, ob, nw, w2, onw, sbw, wb, wo)


FF_CHUNK = MXU_DIM
UP_AHEAD = 4
DOWN_GROUP = 6


def _ffn_kernel(x_ref, st_ref, nw_ref, wup_ref, cw_ref, cb_ref, wdn_ref, out_ref, stout_ref,
                carry_ref, ext_ref, *, tm, shift, halo):
    i = pl.program_id(1)

    @pl.when(i == 0)
    def _():
        carry_ref[...] = st_ref[0]

    x = x_ref[...]
    h = _rms(x, nw_ref[...]).astype(BF16)
    cw = cw_ref[...]
    cb = cb_ref[...]
    n_chunks = D_FF // FF_CHUNK

    def col(c, part):
        lo = part * D_FF + c * FF_CHUNK
        return slice(lo, lo + FF_CHUNK)

    def up_proj(c):
        return [jnp.dot(h, wup_ref[:, col(c, part)], preferred_element_type=F32) for part in range(2)]

    def gated(c, ups):
        ext = ext_ref.at[c % 2]
        parts = []
        for part in range(2):
            sl = col(c, part)
            es = slice(part * FF_CHUNK, (part + 1) * FF_CHUNK)
            ext[0:halo, es] = carry_ref[:, sl]
            ext[halo:halo + tm, es] = ups[part]
            u = cw[2:3, sl] * ups[part] + cb[:, sl]
            u = u + cw[1:2, sl] * ext[halo - shift:halo - shift + tm, es]
            u = u + cw[0:1, sl] * ext[halo - 2 * shift:halo - 2 * shift + tm, es]
            carry_ref[:, sl] = ext[tm:tm + halo, es]
            parts.append(u)
        return (_silu(parts[0]) * parts[1]).astype(BF16)

    acc = x
    ups, acts = {}, {}
    for s in range(n_chunks + UP_AHEAD + 1):
        if s < n_chunks:
            ups[s] = up_proj(s)
        c = s - UP_AHEAD
        if 0 <= c < n_chunks:
            acts[c] = gated(c, ups.pop(c))
        c = s - UP_AHEAD - 1
        if 0 <= c < n_chunks and (c % DOWN_GROUP == DOWN_GROUP - 1 or c == n_chunks - 1):
            c0 = c - c % DOWN_GROUP
            a = jnp.concatenate([acts.pop(i) for i in range(c0, c + 1)], axis=1)
            acc = acc + jnp.dot(a, wdn_ref[c0 * FF_CHUNK:(c + 1) * FF_CHUNK, :], preferred_element_type=F32)
    out_ref[...] = acc
    stout_ref[0] = carry_ref[...]


def _ffn(x, st, nw, wup, cw, cb, wdn, groups, tm, shift, halo):
    r = x.shape[0]
    tiles = r // groups // tm
    full = lambda a: pl.BlockSpec(a.shape, lambda g, i: (0,) * a.ndim)
    return pl.pallas_call(
        functools.partial(_ffn_kernel, tm=tm, shift=shift, halo=halo),
        grid=(groups, tiles),
        in_specs=[
            pl.BlockSpec((tm, D_MODEL), lambda g, i: (g * tiles + i, 0)),
            pl.BlockSpec((1, halo, 2 * D_FF), lambda g, i: (g, 0, 0)),
            full(nw), full(wup), full(cw), full(cb), full(wdn),
        ],
        out_specs=[
            pl.BlockSpec((tm, D_MODEL), lambda g, i: (g * tiles + i, 0)),
            pl.BlockSpec((1, halo, 2 * D_FF), lambda g, i: (g, 0, 0)),
        ],
        out_shape=[jax.ShapeDtypeStruct((r, D_MODEL), F32),
                   jax.ShapeDtypeStruct((groups, halo, 2 * D_FF), F32)],
        scratch_shapes=[pltpu.VMEM((halo, 2 * D_FF), F32), pltpu.VMEM((2, halo + tm, 2 * FF_CHUNK), F32)],
        compiler_params=_cparams(2, VMEM_LARGE_MIB),
        name="ffn",
    )(x, st, nw, wup, cw, cb, wdn)


def _sgdn_pre_kernel(qkv_ref, ba_ref, prm_ref, wq_ref, u_ref, kd_ref, a_ref, gl_ref, *, t_new, nb):
    prm = prm_ref[...]
    lane = lax.broadcasted_iota(jnp.int32, (nb, LANES), 1)

    ys, betas, gs = [], [], []
    for t in range(t_new):
        ys.append(qkv_ref[t])
        bg = ba_ref[t]
        betas.append(_sigmoid(bg))
        gs.append(-jnp.exp(prm[0:1]) * _softplus(bg + prm[1:2]))
    gcs = [gs[0]]
    for t in range(1, t_new):
        gcs.append(gcs[-1] + gs[t])

    a_out = [jnp.zeros((nb, LANES), F32) for _ in range(t_new)]
    for h in range(H_A):
        q, k, v, beta, gc = [], [], [], [], []
        for t in range(t_new):
            q.append(ys[t][:, h * DK_A:(h + 1) * DK_A])
            k.append(ys[t][:, H_A * DK_A + h * DK_A:H_A * DK_A + (h + 1) * DK_A])
            v.append(ys[t][:, 2 * H_A * DK_A + h * DV_A:2 * H_A * DK_A + (h + 1) * DV_A])
            beta.append(betas[t][:, h:h + 1])
            gc.append(gcs[t][:, H_A + h:H_A + h + 1])
        m = [[None] * t_new for _ in range(t_new)]
        for i in range(t_new):
            for jj in range(i + 1):
                dec = jnp.exp(gc[i] - gc[jj])
                if jj < i:
                    m[i][jj] = beta[i] * jnp.sum(k[i] * k[jj], axis=-1, keepdims=True) * dec
                aij = jnp.sum(q[i] * k[jj], axis=-1, keepdims=True) * dec
                a_out[i] = jnp.where(lane == h * SUBLANES + jj, aij, a_out[i])
        tm_ = [[None] * t_new for _ in range(t_new)]
        for i in range(t_new):
            for jj in range(i):
                acc = m[i][jj]
                for l in range(jj + 1, i):
                    acc = acc + m[i][l] * tm_[l][jj]
                tm_[i][jj] = -acc
        vb = [v[t] * beta[t] for t in range(t_new)]
        kbg = [k[t] * (beta[t] * jnp.exp(gc[t])) for t in range(t_new)]
        hs = slice(h * DK_A, (h + 1) * DK_A)
        for i in range(t_new):
            u = vb[i]
            w = kbg[i]
            for jj in range(i):
                u = u + tm_[i][jj] * vb[jj]
                w = w + tm_[i][jj] * kbg[jj]
            u_ref[i, :, hs] = u
            wq_ref[i, :, hs] = w
            wq_ref[t_new + i, :, hs] = q[i] * jnp.exp(gc[i])
            kd_ref[i, :, hs] = k[i] * jnp.exp(gc[t_new - 1] - gc[i])
        gl_ref[:, hs] = jnp.broadcast_to(jnp.exp(gc[t_new - 1]), (nb, DK_A))
    for i in range(t_new):
        a_ref[i] = a_out[i]


def _sgdn_pre(qkv_t, ba_t, prm, t_new, nb):
    wide = H_A * DK_A
    return pl.pallas_call(
        functools.partial(_sgdn_pre_kernel, t_new=t_new, nb=nb),
        out_shape=[
            jax.ShapeDtypeStruct((2 * t_new, nb, wide), F32),
            jax.ShapeDtypeStruct((t_new, nb, wide), F32),
            jax.ShapeDtypeStruct((t_new, nb, wide), F32),
            jax.ShapeDtypeStruct((t_new, nb, LANES), F32),
            jax.ShapeDtypeStruct((nb, wide), F32),
        ],
        compiler_params=_cparams(0, VMEM_MEDIUM_MIB),
        name="sample_gdn_pre",
    )(qkv_t, ba_t, prm)


SGDN_BB = SUBLANES


def _sgdn_state_kernel(wq_ref, u_ref, kd_ref, a_ref, gl_ref, s0_ref, *refs, t_new):
    o_ref, s1_ref = refs[-2:]
    rows = 2 * t_new
    rid = lax.broadcasted_iota(jnp.int32, (rows, DK_A), 0)
    zpad = jnp.zeros((rows - t_new, DK_A), F32)
    pairs = [(bi, h) for bi in range(SGDN_BB) for h in range(H_A)]
    hs = [slice(h * DK_A, (h + 1) * DK_A) for h in range(H_A)]
    r = {(bi, h): jnp.dot(wq_ref[:, bi, hs[h]].astype(BF16), s0_ref[bi, h].astype(BF16),
                          preferred_element_type=F32) for (bi, h) in pairs}
    v_new = {}
    for (bi, h) in pairs:
        u8 = jnp.concatenate([u_ref[:, bi, hs[h]], zpad], axis=0)
        v_new[bi, h] = jnp.where(rid < t_new, u8 - r[bi, h], 0.0)
    kv = {}
    for (bi, h) in pairs:
        kd8 = jnp.concatenate([kd_ref[:, bi, hs[h]], zpad], axis=0)
        kv[bi, h] = lax.dot_general(kd8.astype(BF16), v_new[bi, h].astype(BF16), (((0,), (0,)), ((), ())),
                                    preferred_element_type=F32)
    for (bi, h) in pairs:
        amat = a_ref[:, bi, :]
        o = r[bi, h][t_new:rows]
        for jj in range(t_new):
            col = amat[:, h * SUBLANES + jj:h * SUBLANES + jj + 1]
            o = o + col * v_new[bi, h][jj:jj + 1, :]
        o_ref[:, bi, hs[h]] = o
        s1_ref[bi, h] = s0_ref[bi, h] * gl_ref[bi:bi + 1, hs[h]] + kv[bi, h]


def _sgdn_state(wq, u, kd, a, gl, s0_all, li, t_new, nb, stacked=None):
    wide = H_A * DK_A
    bb = SGDN_BB
    state_spec = pl.BlockSpec((None, bb, H_A, DK_A, DV_A), lambda i: (li, i, 0, 0, 0))
    in_specs = [
        pl.BlockSpec((2 * t_new, bb, wide), lambda i: (0, i, 0)),
        pl.BlockSpec((t_new, bb, wide), lambda i: (0, i, 0)),
        pl.BlockSpec((t_new, bb, wide), lambda i: (0, i, 0)),
        pl.BlockSpec((t_new, bb, LANES), lambda i: (0, i, 0)),
        pl.BlockSpec((bb, wide), lambda i: (i, 0)),
        state_spec,
    ]
    args = [wq, u, kd, a, gl, s0_all]
    aliases = {}
    if stacked is not None:
        in_specs.append(pl.BlockSpec(memory_space=pl.ANY))
        args.append(stacked)
        aliases = {6: 1}
    return pl.pallas_call(
        functools.partial(_sgdn_state_kernel, t_new=t_new),
        grid=(nb // bb,),
        in_specs=in_specs,
        out_specs=[pl.BlockSpec((t_new, bb, wide), lambda i: (0, i, 0)), state_spec],
        out_shape=[jax.ShapeDtypeStruct((t_new, nb, wide), F32),
                   jax.ShapeDtypeStruct(s0_all.shape, F32)],
        input_output_aliases=aliases,
        compiler_params=_cparams(1, VMEM_SMALL_MIB),
        name="sample_gdn_state",
    )(*args)


NEW_ROWS = 2 * SUBLANES


def _sattn_kernel(pt_ref, lam_ref, q_ref, kn_ref, vn_ref, bias_ref, *rest, n_pages, t_new, lam_init):
    del pt_ref
    k_pages = rest[:n_pages]
    v_pages = rest[n_pages:2 * n_pages]
    o_ref = rest[2 * n_pages]
    rows = 2 * t_new
    q = q_ref[0]
    bias = bias_ref[...]
    lam = _lam_value(lam_ref[...], lam_init)
    r = lax.broadcasted_iota(jnp.int32, (rows, 2 * DH_B), 0)
    c = lax.broadcasted_iota(jnp.int32, (rows, 2 * DH_B), 1)
    map_mask = (c // DH_B) == (r // t_new)
    pad = jnp.zeros((PAGE - NEW_ROWS, LANES), BF16)
    nt = (((1,), (1,)), ((), ()))

    heads = range(H_B)
    lanes = [slice(h * LANES, (h + 1) * LANES) for h in heads]
    head_rows = [pl.ds(h, PAGE, stride=H_B) for h in heads]
    qx = [jnp.where(map_mask, jnp.concatenate([q[:, lanes[h]]] * 2, axis=0), 0.0).astype(BF16) for h in heads]
    bh = [bias[h * rows:(h + 1) * rows] for h in heads]

    groups = [list(range(p, min(p + MXU_DIM // PAGE, n_pages))) for p in range(0, n_pages, MXU_DIM // PAGE)]

    def head_tile(page_refs, grp, h):
        return jnp.concatenate([page_refs[p][0, 0, head_rows[h], :] for p in grp], axis=0).astype(BF16)

    s_parts = [[] for _ in heads]
    for grp in groups:
        for h in heads:
            s = lax.dot_general(qx[h], head_tile(k_pages, grp, h), nt, preferred_element_type=F32)
            if grp[-1] == n_pages - 1:
                zeros = [jnp.zeros((rows, PAGE), F32)] * (len(grp) - 1)
                s = s + jnp.concatenate(zeros + [bh[h][:, 0:PAGE]], axis=1)
            s_parts[h].append(s)
    for h in heads:
        kn = jnp.concatenate([kn_ref[0, :, lanes[h]], pad], axis=0)
        s_parts[h].append(lax.dot_general(qx[h], kn, nt, preferred_element_type=F32) + bh[h][:, PAGE:2 * PAGE])

    m = []
    for h in heads:
        mh = s_parts[h][0].max(axis=-1, keepdims=True)
        for s in s_parts[h][1:]:
            mh = jnp.maximum(mh, s.max(axis=-1, keepdims=True))
        m.append(mh)
    l = [jnp.zeros((rows, 1), F32) for _ in heads]
    acc = [jnp.zeros((rows, DV_B), F32) for _ in heads]
    for gi in range(len(groups) + 1):
        for h in heads:
            pr = jnp.exp2(s_parts[h][gi] - m[h])
            l[h] = l[h] + jnp.sum(pr, axis=-1, keepdims=True)
            if gi < len(groups):
                vv = head_tile(v_pages, groups[gi], h)
            else:
                vv = jnp.concatenate([vn_ref[0, :, lanes[h]], pad], axis=0)
            acc[h] = acc[h] + jnp.dot(pr.astype(BF16), vv, preferred_element_type=F32)
    for h in heads:
        a = acc[h] / l[h]
        o_ref[0, :, lanes[h]] = a[0:t_new] - lam * a[t_new:rows]


def _sample_attn(page_table, lamv, q_b, kn_b, vn_b, bias, ck, cv, li, t_new, lam_init):
    nb, n_pages = page_table.shape
    page_spec = lambda p: pl.BlockSpec((1, 1, PAGE * H_B, LANES), lambda b, pt, p=p: (li, pt[b, p], 0, 0))
    tok_spec = pl.BlockSpec((1, t_new, HB_W), lambda b, pt: (b, 0, 0))
    new_spec = pl.BlockSpec((1, NEW_ROWS, HB_W), lambda b, pt: (b, 0, 0))
    grid_spec = pltpu.PrefetchScalarGridSpec(
        num_scalar_prefetch=1,
        grid=(nb,),
        in_specs=[pl.BlockSpec(lamv.shape, lambda b, pt: (0, 0)), tok_spec, new_spec, new_spec,
                  pl.BlockSpec(bias.shape, lambda b, pt: (0, 0))]
                 + [page_spec(p) for p in range(n_pages)] * 2,
        out_specs=pl.BlockSpec((1, t_new, HB_W), lambda b, pt: (b, 0, 0)),
    )
    return pl.pallas_call(
        functools.partial(_sattn_kernel, n_pages=n_pages, t_new=t_new, lam_init=lam_init),
        grid_spec=grid_spec,
        out_shape=jax.ShapeDtypeStruct((nb, t_new, HB_W), F32),
        compiler_params=_cparams(1, VMEM_SMALL_MIB),
        name="sample_attn",
    )(page_table, lamv, q_b, kn_b, vn_b, bias, *([ck] * n_pages), *([cv] * n_pages))


def _pick(n, pref):
    return pref if n % pref == 0 else n


def _to_bmajor(a_t, t_new, nb):
    return jnp.swapaxes(a_t.reshape(t_new, nb, -1), 0, 1)


def _layer_weights(li, rel_bias, norm_mix, w_in, conv_a, a_log, dt_bias, onorm_a, qnorm_b, knorm_b,
                   lam_q1, lam_k1, lam_q2, lam_k2, subln_b, w_branch, w_o, norm_ffn, w_up, conv_f,
                   conv_f_b, w_down):
    w = w_in[li]
    o_z = QKV_A
    o_b = o_z + H_A * DV_A
    o_qb = o_b + 2 * H_A
    o_kb = o_qb + HB_W
    o_vb = o_kb + HB_W
    o_ga = o_vb + H_B * DV_B
    o_gb = o_ga + D_MODEL
    ba_cols = jnp.pad(w[:, o_b:o_qb], ((0, 0), (0, LANES - 2 * H_A)))
    w1 = jnp.concatenate([w[:, 0:o_z], w[:, o_qb:o_ga], ba_cols], axis=1).astype(BF16)
    w2 = jnp.concatenate([w[:, o_z:o_b], w[:, o_ga:]], axis=1).astype(BF16)
    prm = jnp.zeros((2, LANES), F32)
    prm = prm.at[0, H_A:2 * H_A].set(a_log[li]).at[1, H_A:2 * H_A].set(dt_bias[li])
    k_bound = 1.01 * math.sqrt(DH_B) * jnp.max(jnp.abs(knorm_b[li]))
    q_bound = 1.01 * Q_SCALE * math.sqrt(DH_B) * jnp.max(jnp.abs(qnorm_b[li]))
    rel = (rel_bias - rel_bias[NUM_BUCKETS - 1:NUM_BUCKETS]) * LOG2E
    spread = 2.0 * q_bound * k_bound + jnp.max(jnp.max(rel, axis=0) - jnp.min(rel, axis=0))
    scal = jnp.concatenate([jnp.where(spread <= SAFE_SPREAD_BITS, 1.0, 0.0).reshape(1), k_bound.reshape(1),
                            jnp.max(rel, axis=0), jnp.zeros((2,), F32)]).astype(F32)
    return dict(
        scal=scal,
        nw=norm_mix[li].reshape(1, D_MODEL), w1=w1, w2=w2,
        qw=jnp.tile(qnorm_b[li], 2 * H_B).reshape(1, HB_W),
        kw=jnp.tile(knorm_b[li], 2 * H_B).reshape(1, HB_W),
        cw_a=conv_a[li], prm=prm,
        onw=onorm_a[li].reshape(1, DV_A), sbw=subln_b[li].reshape(1, DV_B),
        lamv=jnp.stack([lam_q1[li], lam_k1[li], lam_q2[li], lam_k2[li]]),
        wb=w_branch[li].astype(BF16), wo=w_o[li].astype(BF16),
        nwf=norm_ffn[li].reshape(1, D_MODEL), wup=w_up[li].astype(BF16),
        cw_f=conv_f[li], cb_f=conv_f_b[li].reshape(1, 2 * D_FF), wdn=w_down[li].astype(BF16),
    )


def _group_ones():
    i = jnp.arange(MXU_DIM)
    return ((i[:, None] // DH_B) == (i[None, :] // DH_B)).astype(BF16)


def _prompt_layer(x, wt, bias_p, g, n_seq, seq, lam_init, li, depth, kv_stacked):
    tm = _pick(seq, ROW_TILE)
    blk = bias_p.shape[1]
    zero_st_a = jnp.zeros((n_seq, SUBLANES, QKV_A), F32)
    qkv, ba, qn, kn, kf, vf, vb, cst_a = _inproj(x, wt["nw"], wt["w1"], wt["qw"], wt["kw"], g, zero_st_a,
                                                 wt["cw_a"], n_seq, tm, 1, SUBLANES,
                                                 layer=li, depth=depth, stacked=kv_stacked)
    o_a, s_fin = _prompt_gdn(qkv.reshape(n_seq, seq, QKV_A), ba.reshape(n_seq, seq, LANES),
                             wt["prm"], n_seq, seq, _pick(seq, GDN_TIME_BLOCK))
    o_a = o_a.reshape(n_seq * seq, H_A * DV_A)
    o_b = _prompt_attn(wt["scal"], wt["lamv"], qn, kn, vb, bias_p, n_seq, seq, blk, lam_init)
    x = _merge(x, o_a, o_b, wt["nw"], wt["w2"], wt["onw"], wt["sbw"], wt["wb"], wt["wo"], tm, 1.0 - lam_init)
    zero_st = jnp.zeros((n_seq, SUBLANES, 2 * D_FF), F32)
    x, cst_f = _ffn(x, zero_st, wt["nwf"], wt["wup"], wt["cw_f"], wt["cb_f"], wt["wdn"],
                    n_seq, tm, 1, SUBLANES)
    return (x, kf, vf, s_fin, cst_a[:, SUBLANES - (CONV_A - 1):], cst_f[:, SUBLANES - (CONV_F - 1):])


def _sample_layer(x_t, wt, bias_s, g, page_table, ck, cv, li, s0_all, s1_stacked, cst_a, cst_f, nb, t_new,
                  lam_init):
    r = nb * t_new
    halo_a = (CONV_A - 1) * nb
    st_a_t = jnp.swapaxes(cst_a, 0, 1).reshape(1, halo_a, QKV_A)
    qkv, ba, qn, kn, kf, vf, vb, cst_a_t = _inproj(x_t, wt["nw"], wt["w1"], wt["qw"], wt["kw"], g, st_a_t,
                                                   wt["cw_a"], 1, r, nb, halo_a)
    cst_a_t = cst_a_t.reshape(CONV_A - 1, nb, QKV_A)
    wq, u, kd, a, gl = _sgdn_pre(qkv.reshape(t_new, nb, QKV_A), ba.reshape(t_new, nb, LANES), wt["prm"],
                                 t_new, nb)
    o_a_t, s1 = _sgdn_state(wq, u, kd, a, gl, s0_all, li, t_new, nb, stacked=s1_stacked)
    pad_new = lambda a: jnp.pad(_to_bmajor(a, t_new, nb), ((0, 0), (0, NEW_ROWS - t_new), (0, 0)))
    o_b = _sample_attn(page_table, wt["lamv"], _to_bmajor(qn, t_new, nb).astype(F32), pad_new(kn),
                       pad_new(vb), bias_s, ck, cv, li, t_new, lam_init)
    o_b_t = jnp.swapaxes(o_b, 0, 1).reshape(r, HB_W)
    x_t = _merge(x_t, o_a_t.reshape(r, H_A * DV_A), o_b_t, wt["nw"], wt["w2"], wt["onw"], wt["sbw"],
                 wt["wb"], wt["wo"], r, 1.0 - lam_init)
    halo = (CONV_F - 1) * nb
    st_f_t = jnp.swapaxes(cst_f, 0, 1).reshape(1, halo, 2 * D_FF)
    x_t, cst_f_t = _ffn(x_t, st_f_t, wt["nwf"], wt["wup"], wt["cw_f"], wt["cb_f"], wt["wdn"],
                        1, r, nb, halo)
    return (x_t, _to_bmajor(kf, t_new, nb).reshape(nb, t_new, H_B, 2 * DH_B),
            _to_bmajor(vf, t_new, nb).reshape(nb, t_new, H_B, DV_B), s1,
            jnp.swapaxes(cst_a_t, 0, 1), jnp.swapaxes(cst_f_t.reshape(CONV_F - 1, nb, 2 * D_FF), 0, 1))


def kernel(x_prompt, x_sample, cache_k, cache_v, state_delta, state_conv_a, state_conv_ffn, page_table,
           rel_bias, norm_mix, w_in, conv_a, a_log, dt_bias, onorm_a, qnorm_b, knorm_b, lam_q1, lam_k1,
           lam_q2, lam_k2, subln_b, w_branch, w_o, norm_ffn, w_up, conv_f, conv_f_b, w_down):
    n_seq, seq, _ = x_prompt.shape
    nb, t_new, _ = x_sample.shape
    depth = w_in.shape[0]
    n_pool = cache_k.shape[1]

    rb_flat = rel_bias.T.reshape(-1)
    bias_p = _prompt_bias(rb_flat, _pick(seq, ATTN_BLOCK))
    bias_s = _sample_bias(rb_flat, t_new)
    g = _group_ones()
    ck = cache_k.reshape(depth, n_pool, PAGE * H_B, 2 * DH_B)
    cv = cache_v.reshape(depth, n_pool, PAGE * H_B, DV_B)

    xp = x_prompt.reshape(n_seq * seq, D_MODEL)
    xs = jnp.swapaxes(x_sample, 0, 1).reshape(t_new * nb, D_MODEL)
    outs_p, outs_s = [], []
    kv_stacked, s1_stacked = None, None
    for li in range(depth):
        wt = _layer_weights(li, rel_bias, norm_mix, w_in, conv_a, a_log, dt_bias, onorm_a, qnorm_b, knorm_b,
                            lam_q1, lam_k1, lam_q2, lam_k2, subln_b, w_branch, w_o, norm_ffn, w_up,
                            conv_f, conv_f_b, w_down)
        lam_init = 0.8 - 0.6 * math.exp(-0.3 * li)
        xp, kf_all, vf_all, *rest_p = _prompt_layer(xp, wt, bias_p, g, n_seq, seq, lam_init, li, depth,
                                                    kv_stacked)
        kv_stacked = (kf_all, vf_all)
        xs, ks, vs, s1_stacked, *rest_s = _sample_layer(xs, wt, bias_s, g, page_table, ck, cv, li, state_delta,
                                                        s1_stacked, state_conv_a[li], state_conv_ffn[li], nb,
                                                        t_new, lam_init)
        outs_p.append(rest_p)
        outs_s.append([ks, vs] + rest_s)

    stack = lambda outs, i: jnp.stack([o[i] for o in outs])
    y_prompt = xp.reshape(n_seq, seq, D_MODEL)
    y_sample = jnp.swapaxes(xs.reshape(t_new, nb, D_MODEL), 0, 1)
    k_prompt = kv_stacked[0].reshape(depth, n_seq, seq, H_B, 2 * DH_B)
    v_prompt = kv_stacked[1].reshape(depth, n_seq, seq, H_B, DV_B)
    return (y_prompt, y_sample,
            k_prompt, v_prompt, stack(outs_p, 0), stack(outs_p, 1), stack(outs_p, 2),
            stack(outs_s, 0), stack(outs_s, 1), s1_stacked, stack(outs_s, 2), stack(outs_s, 3))
```
